```python
import math
import jax
import jax.numpy as jnp
from jax import lax
import numpy as np

D_MODEL = 2048
BATCH = 4
SEQ = 2048
DEPTH = 2
DEC_BATCH = 128
DEC_SEQ = 4
PAST_LEN = 2048
PAGE_SIZE = 128

CONV_WIDTH = 512
CONV_K = 3
HEAD_DIM = 64
HEADS_PER_GROUP = 4
ATTN_GROUPS = ((128, 1), (512, 4), (2048, 16))
N_ATTN_GROUPS = 3
ATTN_QKV = N_ATTN_GROUPS * HEADS_PER_GROUP * HEAD_DIM
ATTN_OUT = HEADS_PER_GROUP * HEAD_DIM
ROT_DIM = HEAD_DIM // 4
ROPE_THETA = 500000.0
SSM_WIDTH = 768
SSM_CH_PER_GROUP = 16
SSM_GROUPS = SSM_WIDTH // SSM_CH_PER_GROUP
SSM_STATE = 64
N_BRANCH = 3
IN_WIDTH = 3 * CONV_WIDTH + 3 * ATTN_QKV + SSM_WIDTH + N_BRANCH * D_MODEL
D_FF = 5632
N_EXPERTS = 8
TOP_K = 2
D_FF_EXPERT = D_FF // TOP_K
MOE_BLOCK = 128
N_DENSE = (DEPTH + 1) // 2
N_MOE = DEPTH // 2
EPS = 1e-6

kernel_name = 'hybrid_conv_dilattn_s5_decode_step'


def rmsnorm(x, g):
    xf = x.astype(jnp.float32)
    y = xf * lax.rsqrt(jnp.mean(xf * xf, axis=-1, keepdims=True) + EPS)
    return (y * g.astype(jnp.float32)).astype(x.dtype)


def swiglu(x, w1, w3, w2):
    return (jax.nn.silu(x @ w1) * (x @ w3)) @ w2


def rope_partial(x, pos):
    half = ROT_DIM // 2
    inv = ROPE_THETA ** (-jnp.arange(half, dtype=jnp.float32) * 2.0 / ROT_DIM)
    ang = pos.astype(jnp.float32)[:, None] * inv[None, :]
    ang = ang.reshape((ang.shape[0],) + (1,) * (x.ndim - 3) + (half,))
    cos, sin = jnp.cos(ang), jnp.sin(ang)
    xr = x[..., :ROT_DIM].astype(jnp.float32)
    x1, x2 = xr[..., :half], xr[..., half:]
    rot = jnp.concatenate([x1 * cos - x2 * sin, x1 * sin + x2 * cos], axis=-1)
    return jnp.concatenate([rot.astype(x.dtype), x[..., ROT_DIM:]], axis=-1)


def short_conv(u, buf, w):
    L = u.shape[1]
    ucat = jnp.concatenate([buf.astype(u.dtype), u], axis=1)
    y = w[0] * ucat[:, 0:L]
    for j in range(1, CONV_K):
        y = y + w[j] * ucat[:, j:j + L]
    return y, ucat[:, L:]


def banded_attention(q, k, v, band):
    N, L, H, hd = q.shape
    blk = band
    nb = -(-L // blk)
    Lp = nb * blk
    pe = Lp - L
    qb = jnp.pad(q, ((0, 0), (0, pe), (0, 0), (0, 0))).reshape(N, nb, blk, H, hd)
    kp = jnp.pad(k, ((0, 0), (blk, pe), (0, 0), (0, 0)))
    vp = jnp.pad(v, ((0, 0), (blk, pe), (0, 0), (0, 0)))

    def blocks(t):
        return jnp.concatenate([t[:, :Lp].reshape(N, nb, blk, H, hd),
                                t[:, blk:].reshape(N, nb, blk, H, hd)], axis=2)

    kb, vb = blocks(kp), blocks(vp)
    scores = jnp.einsum('nbqhd,nbkhd->nbhqk', qb, kb, preferred_element_type=jnp.float32) * (hd ** -0.5)
    qi = jnp.arange(blk)[:, None]
    kj = jnp.arange(2 * blk)[None, :]
    dist = qi + blk - kj
    kpos = jnp.arange(nb)[:, None, None] * blk - blk + kj[None]
    valid = (dist >= 0) & (dist <= band) & (kpos >= 0)
    scores = jnp.where(valid[None, :, None], scores, -jnp.inf)
    lse = jax.nn.logsumexp(scores, axis=-1)
    p = jnp.exp(scores - lse[..., None])
    out = jnp.einsum('nbhqk,nbkhd->nbqhd', p.astype(v.dtype), vb).reshape(N, Lp, H, hd)[:, :L]
    lse = lse.transpose(0, 1, 3, 2).reshape(N, Lp, H)[:, :L]
    return out, lse


def strided_window_attention_prompt(q, k, v, window, dilation):
    Bn, S, H, hd = q.shape
    L = S // dilation

    def to_sub(t):
        return t.reshape(Bn, L, dilation, H, hd).transpose(0, 2, 1, 3, 4).reshape(Bn * dilation, L, H, hd)

    out, lse = banded_attention(to_sub(q), to_sub(k), to_sub(v), window // dilation)
    out = out.reshape(Bn, dilation, L, H, hd).transpose(0, 2, 1, 3, 4).reshape(Bn, S, H, hd)
    lse = lse.reshape(Bn, dilation, L, H).transpose(0, 2, 1, 3).reshape(Bn, S, H)
    return out, lse


def strided_window_attention_cached(q, k_new, v_new, kv_buf, window, dilation):
    Bn, T, H, hd = q.shape
    wbuf = kv_buf.shape[1]
    band = window // dilation
    kcat = jnp.concatenate([kv_buf[:, :, 0].astype(k_new.dtype), k_new], axis=1)
    vcat = jnp.concatenate([kv_buf[:, :, 1].astype(v_new.dtype), v_new], axis=1)
    idx = wbuf + jnp.arange(T)[:, None] - dilation * jnp.arange(band + 1)[None, :]
    valid = idx >= 0
    idx = jnp.maximum(idx, 0)
    kg = kcat[:, idx]
    vg = vcat[:, idx]
    scores = jnp.einsum('bthd,btjhd->bhtj', q, kg, preferred_element_type=jnp.float32) * (hd ** -0.5)
    scores = jnp.where(valid[None, None], scores, -jnp.inf)
    lse = jax.nn.logsumexp(scores, axis=-1)
    p = jnp.exp(scores - lse[..., None])
    out = jnp.einsum('bhtj,btjhd->bthd', p.astype(vg.dtype), vg)
    return out, lse.transpose(0, 2, 1)


def s5_scan(u, s0, lam_re, lam_im, log_dt, b_re, b_im, c_re, c_im, d_skip):
    Bn, L, _ = u.shape
    f32 = jnp.float32
    lam = lax.complex(lam_re.astype(f32), lam_im.astype(f32))
    dt = jnp.exp(log_dt.astype(f32))[:, None]
    lam_bar = jnp.exp(lam * dt)
    b_bar = ((lam_bar - 1.0) / lam)[:, :, None] * lax.complex(b_re.astype(f32), b_im.astype(f32))
    c = lax.complex(c_re.astype(f32), c_im.astype(f32))
    uf = u.astype(f32)
    ug = uf.reshape(Bn, L, SSM_GROUPS, SSM_CH_PER_GROUP).astype(jnp.complex64)
    bu = jnp.einsum('gpc,blgc->blgp', b_bar, ug)
    s0c = lax.complex(s0[..., 0].astype(f32), s0[..., 1].astype(f32))
    bu = bu.at[:, 0].add(lam_bar * s0c)
    a = jnp.broadcast_to(lam_bar, bu.shape)

    def combine(e1, e2):
        a1, b1 = e1
        a2, b2 = e2
        return a2 * a1, a2 * b1 + b2

    _, s = lax.associative_scan(combine, (a, bu), axis=1)
    y = jnp.einsum('gcp,blgp->blgc', c, s).real.reshape(Bn, L, SSM_WIDTH) + d_skip.astype(f32) * uf
    s_last = s[:, -1]
    new_state = jnp.stack([s_last.real, s_last.imag], axis=-1)
    return y.astype(u.dtype), new_state.astype(s0.dtype)


def mixer(h, pos, conv_buf, kv_bufs, ssm_s0, lp, prompt):
    Bn, L, _ = h.shape
    z = h @ lp['w_in']
    o1 = CONV_WIDTH
    o2 = 2 * CONV_WIDTH
    o3 = 3 * CONV_WIDTH
    o4 = o3 + ATTN_QKV
    o5 = o4 + ATTN_QKV
    o6 = o5 + ATTN_QKV
    o7 = o6 + SSM_WIDTH
    conv_h, conv_b, conv_c, q, k, v, ssm_u, gates = jnp.split(z, [o1, o2, o3, o4, o5, o6, o7], axis=-1)
    y_conv, conv_new = short_conv(conv_c * conv_h, conv_buf, lp['conv_w'])
    y_conv = conv_b * y_conv
    def heads(t):
        return t.reshape(Bn, L, N_ATTN_GROUPS, HEADS_PER_GROUP, HEAD_DIM)
    q = rope_partial(heads(q), pos)
    k = rope_partial(heads(k), pos)
    v = heads(v)
    outs, lses, kv_new = [], [], []
    for g in range(N_ATTN_GROUPS):
        window, dilation = ATTN_GROUPS[g]
        qg, kg, vg = q[:, :, g], k[:, :, g], v[:, :, g]
        if prompt:
            o, s = strided_window_attention_prompt(qg, kg, vg, window, dilation)
            keep = min(window, L)
            kv_new.append(jnp.stack([kg[:, L - keep:], vg[:, L - keep:]], axis=2))
        else:
            o, s = strided_window_attention_cached(qg, kg, vg, kv_bufs[g], window, dilation)
            kv_new.append(jnp.stack([kg, vg], axis=2))
        outs.append(o)
        lses.append(s)
    wts = jax.nn.softmax(jnp.stack(lses, axis=0), axis=0)
    y_attn = jnp.einsum('gblh,gblhd->blhd', wts.astype(h.dtype), jnp.stack(outs, axis=0)).reshape(Bn, L, ATTN_OUT)
    y_ssm, ssm_new = s5_scan(ssm_u, ssm_s0, lp['lam_re'], lp['lam_im'], lp['log_dt'], lp['b_re'], lp['b_im'],
                             lp['c_re'], lp['c_im'], lp['d'])
    zg = jax.nn.gelu(y_ssm)
    y_ssm = zg * jax.nn.sigmoid(zg @ lp['w_glu'] + lp['b_glu'])
    gt = jax.nn.sigmoid(gates + lp['gate_bias']).reshape(Bn, L, N_BRANCH, D_MODEL)
    merged = (gt[:, :, 0] * (y_conv @ lp['proj_conv'])
              + gt[:, :, 1] * (y_attn @ lp['proj_attn'])
              + gt[:, :, 2] * (y_ssm @ lp['proj_ssm']))
    return merged @ lp['w_out'], kv_new, conv_new, ssm_new


def moe_swiglu(x, w_router, w1, w3, w2):
    T, D = x.shape
    logits = jnp.einsum('td,de->te', x, w_router, preferred_element_type=jnp.float32)
    top_val, top_idx = lax.top_k(logits, TOP_K)
    gates = jax.nn.softmax(top_val, axis=-1).astype(x.dtype)
    tk = T * TOP_K
    n_blocks = -(-(tk + N_EXPERTS * (MOE_BLOCK - 1)) // MOE_BLOCK)
    n_rows = n_blocks * MOE_BLOCK
    flat_e = top_idx.reshape(tk)
    flat_tok = jnp.repeat(jnp.arange(T, dtype=jnp.int32), TOP_K)
    flat_g = gates.reshape(tk)
    order = jnp.argsort(flat_e)
    se = flat_e[order]
    counts = jnp.bincount(flat_e, length=N_EXPERTS)
    padded = (counts + MOE_BLOCK - 1) // MOE_BLOCK * MOE_BLOCK
    start = jnp.cumsum(counts) - counts
    ends_p = jnp.cumsum(padded)
    pstart = ends_p - padded
    dest = pstart[se] + jnp.arange(tk) - start[se]
    row_tok = jnp.full((n_rows,), T, jnp.int32).at[dest].set(flat_tok[order])
    row_g = jnp.zeros((n_rows,), x.dtype).at[dest].set(flat_g[order])
    block_e = jnp.minimum(jnp.searchsorted(ends_p, jnp.arange(n_blocks) * MOE_BLOCK, side='right'), N_EXPERTS - 1)
    xb = jnp.concatenate([x, jnp.zeros((1, D), x.dtype)], axis=0)[row_tok].reshape(n_blocks, MOE_BLOCK, D)

    def expert_block(args):
        xi, e = args
        return swiglu(xi, w1[e], w3[e], w2[e])

    yb = lax.map(expert_block, (xb, block_e)).reshape(n_rows, D)
    return jax.ops.segment_sum(yb * row_g[:, None], row_tok, num_segments=T + 1)[:T]


def setup_inputs(seed: int = 0) -> dict:
    key = jax.random.key(seed)
    ks = iter(jax.random.split(key, 48))
    f32 = jnp.float32

    def nrm(shape, scale):
        return scale * jax.random.normal(next(ks), shape, f32)

    inp = {}
    inp['x_prompt'] = nrm((BATCH, SEQ, D_MODEL), 1.0)
    inp['x_sample'] = nrm((DEC_BATCH, DEC_SEQ, D_MODEL), 1.0)
    inp['cache_kv_w128'] = nrm((DEPTH, DEC_BATCH, min(128, PAST_LEN), 2, HEADS_PER_GROUP, HEAD_DIM), 1.0)
    inp['cache_kv_w512'] = nrm((DEPTH, DEC_BATCH, min(512, PAST_LEN), 2, HEADS_PER_GROUP, HEAD_DIM), 1.0)
    inp['cache_kv_w2048'] = nrm((DEPTH, DEC_BATCH, min(2048, PAST_LEN), 2, HEADS_PER_GROUP, HEAD_DIM), 1.0)
    inp['state_conv'] = nrm((DEPTH, DEC_BATCH, CONV_K - 1, CONV_WIDTH), 1.0)
    inp['state_ssm'] = nrm((DEPTH, DEC_BATCH, SSM_GROUPS, SSM_STATE, 2), 0.5)
    inp['norm_mix'] = 1.0 + nrm((DEPTH, D_MODEL), 0.02)
    inp['w_in'] = nrm((DEPTH, D_MODEL, IN_WIDTH), D_MODEL ** -0.5)
    inp['gate_bias'] = nrm((DEPTH, N_BRANCH * D_MODEL), 0.02)
    inp['conv_w'] = nrm((DEPTH, CONV_K, CONV_WIDTH), CONV_K ** -0.5)
    inp['ssm_lam_re'] = -0.5 + nrm((DEPTH, SSM_GROUPS, SSM_STATE), 0.01)
    inp['ssm_lam_im'] = math.pi * jnp.arange(SSM_STATE, dtype=f32) + nrm((DEPTH, SSM_GROUPS, SSM_STATE), 0.01)
    inp['ssm_log_dt'] = jax.random.uniform(next(ks), (DEPTH, SSM_GROUPS), f32, math.log(1e-3), math.log(1e-1))
    inp['ssm_b_re'] = nrm((DEPTH, SSM_GROUPS, SSM_STATE, SSM_CH_PER_GROUP), (2 * SSM_CH_PER_GROUP) ** -0.5)
    inp['ssm_b_im'] = nrm((DEPTH, SSM_GROUPS, SSM_STATE, SSM_CH_PER_GROUP), (2 * SSM_CH_PER_GROUP) ** -0.5)
    inp['ssm_c_re'] = nrm((DEPTH, SSM_GROUPS, SSM_CH_PER_GROUP, SSM_STATE), (2 * SSM_STATE) ** -0.5)
    inp['ssm_c_im'] = nrm((DEPTH, SSM_GROUPS, SSM_CH_PER_GROUP, SSM_STATE), (2 * SSM_STATE) ** -0.5)
    inp['ssm_d'] = nrm((DEPTH, SSM_WIDTH), 0.5)
    inp['ssm_w_glu'] = nrm((DEPTH, SSM_WIDTH, SSM_WIDTH), SSM_WIDTH ** -0.5)
    inp['ssm_b_glu'] = nrm((DEPTH, SSM_WIDTH), 0.02)
    inp['proj_conv'] = nrm((DEPTH, CONV_WIDTH, D_MODEL), CONV_WIDTH ** -0.5)
    inp['proj_attn'] = nrm((DEPTH, ATTN_OUT, D_MODEL), ATTN_OUT ** -0.5)
    inp['proj_ssm'] = nrm((DEPTH, SSM_WIDTH, D_MODEL), SSM_WIDTH ** -0.5)
    inp['w_out'] = nrm((DEPTH, D_MODEL, D_MODEL), D_MODEL ** -0.5)
    inp['norm_ffn'] = 1.0 + nrm((DEPTH, D_MODEL), 0.02)
    inp['ffn_w1'] = nrm((N_DENSE, D_MODEL, D_FF), D_MODEL ** -0.5)
    inp['ffn_w3'] = nrm((N_DENSE, D_MODEL, D_FF), D_MODEL ** -0.5)
    inp['ffn_w2'] = nrm((N_DENSE, D_FF, D_MODEL), D_FF ** -0.5)
    inp['router_w'] = nrm((N_MOE, D_MODEL, N_EXPERTS), D_MODEL ** -0.5)
    inp['moe_w1'] = nrm((N_MOE, N_EXPERTS, D_MODEL, D_FF_EXPERT), D_MODEL ** -0.5)
    inp['moe_w3'] = nrm((N_MOE, N_EXPERTS, D_MODEL, D_FF_EXPERT), D_MODEL ** -0.5)
    inp['moe_w2'] = nrm((N_MOE, N_EXPERTS, D_FF_EXPERT, D_MODEL), D_FF_EXPERT ** -0.5)
    inp['norm_final'] = 1.0 + nrm((D_MODEL,), 0.02)
    return inp


def reference(x_prompt, x_sample, cache_kv_w128, cache_kv_w512, cache_kv_w2048, state_conv, state_ssm,
              norm_mix, w_in, gate_bias, conv_w, ssm_lam_re, ssm_lam_im, ssm_log_dt, ssm_b_re, ssm_b_im,
              ssm_c_re, ssm_c_im, ssm_d, ssm_w_glu, ssm_b_glu, proj_conv, proj_attn, proj_ssm, w_out,
              norm_ffn, ffn_w1, ffn_w3, ffn_w2, router_w, moe_w1, moe_w3, moe_w2, norm_final):
    xp, xs = x_prompt, x_sample
    bp = xp.shape[0]
    pos_p = jnp.arange(xp.shape[1], dtype=jnp.int32)
    pos_s = PAST_LEN + jnp.arange(xs.shape[1], dtype=jnp.int32)
    caches = (cache_kv_w128, cache_kv_w512, cache_kv_w2048)
    kvp_all = [[], [], []]
    kvs_all = [[], [], []]
    conv_p_all, conv_s_all, ssm_p_all, ssm_s_all = [], [], [], []
    for l in range(DEPTH):
        lp = {'w_in': w_in[l], 'gate_bias': gate_bias[l], 'conv_w': conv_w[l],
              'lam_re': ssm_lam_re[l], 'lam_im': ssm_lam_im[l], 'log_dt': ssm_log_dt[l],
              'b_re': ssm_b_re[l], 'b_im': ssm_b_im[l], 'c_re': ssm_c_re[l], 'c_im': ssm_c_im[l],
              'd': ssm_d[l], 'w_glu': ssm_w_glu[l], 'b_glu': ssm_b_glu[l],
              'proj_conv': proj_conv[l], 'proj_attn': proj_attn[l], 'proj_ssm': proj_ssm[l], 'w_out': w_out[l]}
        zero_conv = jnp.zeros((bp, CONV_K - 1, CONV_WIDTH), xp.dtype)
        zero_ssm = jnp.zeros((bp, SSM_GROUPS, SSM_STATE, 2), xp.dtype)
        mp, kvp, cp, sp = mixer(rmsnorm(xp, norm_mix[l]), pos_p, zero_conv, None, zero_ssm, lp, True)
        ms, kvs, cs, ss = mixer(rmsnorm(xs, norm_mix[l]), pos_s, state_conv[l], [c[l] for c in caches],
                                state_ssm[l], lp, False)
        xp = xp + mp
        xs = xs + ms
        for g in range(N_ATTN_GROUPS):
            kvp_all[g].append(kvp[g])
            kvs_all[g].append(kvs[g])
        conv_p_all.append(cp)
        conv_s_all.append(cs)
        ssm_p_all.append(sp)
        ssm_s_all.append(ss)
        i = l // 2
        hp = rmsnorm(xp, norm_ffn[l])
        hs = rmsnorm(xs, norm_ffn[l])
        if l % 2 == 0:
            xp = xp + swiglu(hp, ffn_w1[i], ffn_w3[i], ffn_w2[i])
            xs = xs + swiglu(hs, ffn_w1[i], ffn_w3[i], ffn_w2[i])
        else:
            xp = xp + moe_swiglu(hp.reshape(-1, D_MODEL), router_w[i], moe_w1[i], moe_w3[i], moe_w2[i]).reshape(xp.shape)
            xs = xs + moe_swiglu(hs.reshape(-1, D_MODEL), router_w[i], moe_w1[i], moe_w3[i], moe_w2[i]).reshape(xs.shape)
    y_prompt = rmsnorm(xp, norm_final)
    y_sample = rmsnorm(xs, norm_final)
    kv128_p = jnp.stack(kvp_all[0])
    kv512_p = jnp.stack(kvp_all[1])
    kv2048_p = jnp.stack(kvp_all[2])
    conv_p = jnp.stack(conv_p_all)
    ssm_p = jnp.stack(ssm_p_all)
    kv128_s = jnp.stack(kvs_all[0])
    kv512_s = jnp.stack(kvs_all[1])
    kv2048_s = jnp.stack(kvs_all[2])
    conv_s = jnp.stack(conv_s_all)
    ssm_s = jnp.stack(ssm_s_all)
    return (y_prompt, y_sample, kv128_p, kv512_p, kv2048_p, conv_p, ssm_p, kv128_s, kv512_s, kv2048_s, conv_s, ssm_s)
```

```python
import functools
import math

import jax
import jax.numpy as jnp
import numpy as np
from jax import lax
from jax.experimental import pallas as pl
from jax.experimental.pallas import tpu as pltpu

D_MODEL = 2048
BATCH = 4
SEQ = 2048
DEPTH = 2
DEC_BATCH = 128
DEC_SEQ = 4
PAST_LEN = 2048
CONV_WIDTH = 512
CONV_K = 3
HEAD_DIM = 64
HEADS_PER_GROUP = 4
ATTN_GROUPS = ((128, 1), (512, 4), (2048, 16))
N_ATTN_GROUPS = 3
ATTN_QKV = N_ATTN_GROUPS * HEADS_PER_GROUP * HEAD_DIM
ATTN_OUT = HEADS_PER_GROUP * HEAD_DIM
ROT_DIM = HEAD_DIM // 4
ROPE_THETA = 500000.0
SSM_WIDTH = 768
SSM_CH_PER_GROUP = 16
SSM_GROUPS = SSM_WIDTH // SSM_CH_PER_GROUP
SSM_STATE = 64
N_BRANCH = 3
D_FF = 5632
N_EXPERTS = 8
TOP_K = 2
D_FF_EXPERT = D_FF // TOP_K
EPS = 1e-6

V7X_LANES = 128
V7X_VMEM_BYTES = 64 * 1024 * 1024
MIB = 1024 * 1024

P_ROWS = BATCH * SEQ
S_ROWS = DEC_SEQ * DEC_BATCH
ROWS = P_ROWS + S_ROWS
ZA_WIDTH = 3 * CONV_WIDTH + 3 * ATTN_QKV + SSM_WIDTH
GATE_WIDTH = N_BRANCH * D_MODEL
QK_WIDTH = 2 * ATTN_QKV
OFF_Q = 3 * CONV_WIDTH
OFF_V = OFF_Q + 2 * ATTN_QKV
OFF_U = OFF_V + ATTN_QKV
BAND = 128
SSM_CHUNK = 16
SSM_GB = 16
N_GB = SSM_GROUPS // SSM_GB
GB_CH = SSM_GB * SSM_CH_PER_GROUP
GB_ST = SSM_GB * SSM_STATE
ST_WIDTH = N_GB * 2 * GB_ST
ROW_TILE = 512
MIX_TILE = 256
MOE_BLOCK = 256
N_ASSIGN = ROWS * TOP_K
MOE_BLOCKS = -(-(N_ASSIGN + N_EXPERTS * (MOE_BLOCK - 1)) // MOE_BLOCK)
MOE_ROWS = MOE_BLOCKS * MOE_BLOCK
ROUTER_PAD = V7X_LANES
SAMPLE_SEQ_TILE = 8

assert ROWS % ROW_TILE == 0 and P_ROWS % ROW_TILE == 0
assert ROWS % MIX_TILE == 0 and P_ROWS % MIX_TILE == 0


def _params(semantics, vmem_bytes):
    return pltpu.CompilerParams(dimension_semantics=semantics,
                                vmem_limit_bytes=min(int(vmem_bytes), V7X_VMEM_BYTES - 4 * MIB))


def _rms(x, g):
    y = x * lax.rsqrt(jnp.mean(x * x, axis=-1, keepdims=True) + EPS)
    return y * g


def _bdot(a, b):
    return jnp.dot(a.astype(jnp.bfloat16), b.astype(jnp.bfloat16), preferred_element_type=jnp.float32)


def _norm_kernel(x_ref, g_ref, h_ref):
    h_ref[...] = _rms(x_ref[...], g_ref[...]).astype(h_ref.dtype)


def _norm(x, g):
    return pl.pallas_call(
        _norm_kernel,
        grid=(ROWS // ROW_TILE,),
        in_specs=[pl.BlockSpec((ROW_TILE, D_MODEL), lambda i: (i, 0)),
                  pl.BlockSpec((1, D_MODEL), lambda i: (0, 0))],
        out_specs=pl.BlockSpec((ROW_TILE, D_MODEL), lambda i: (i, 0)),
        out_shape=jax.ShapeDtypeStruct((ROWS, D_MODEL), jnp.bfloat16),
        compiler_params=_params(("parallel",), 24 * MIB),
        name="norm",
    )(x, g.reshape(1, D_MODEL))


def _linear_kernel(a_ref, w_ref, o_ref):
    o_ref[...] = jnp.dot(a_ref[...], w_ref[...], preferred_element_type=jnp.float32).astype(o_ref.dtype)


def _linear(a, w, out_dtype, bn, name):
    m, k = a.shape
    n = w.shape[1]
    return pl.pallas_call(
        _linear_kernel,
        grid=(n // bn, m // ROW_TILE),
        in_specs=[pl.BlockSpec((ROW_TILE, k), lambda j, i: (i, 0)),
                  pl.BlockSpec((k, bn), lambda j, i: (0, j))],
        out_specs=pl.BlockSpec((ROW_TILE, bn), lambda j, i: (i, j)),
        out_shape=jax.ShapeDtypeStruct((m, n), out_dtype),
        compiler_params=_params(("parallel", "parallel"), 40 * MIB),
        name=name,
    )(a, w)


def _rope_tables():
    half = ROT_DIM // 2
    inv = ROPE_THETA ** (-jnp.arange(half, dtype=jnp.float32) * 2.0 / ROT_DIM)
    pos_p = jnp.arange(SEQ, dtype=jnp.int32)
    pos_s = jnp.repeat(PAST_LEN + jnp.arange(DEC_SEQ, dtype=jnp.int32), DEC_BATCH)
    pos = jnp.concatenate([pos_p, pos_s]).astype(jnp.float32)
    ang = pos[:, None] * inv[None, :]
    cos, sin = jnp.cos(ang), jnp.sin(ang)
    ones = jnp.ones((pos.shape[0], HEAD_DIM - ROT_DIM), jnp.float32)
    zeros = jnp.zeros_like(ones)
    zh = jnp.zeros_like(sin)
    cos_t = jnp.concatenate([cos, cos, ones], axis=1)
    sin_hi = jnp.concatenate([-sin, zh, zeros], axis=1)
    sin_lo = jnp.concatenate([zh, sin, zeros], axis=1)
    reps = V7X_LANES // HEAD_DIM
    return tuple(jnp.tile(t, (1, reps)) for t in (cos_t, sin_hi, sin_lo))


def _rope_kernel(x_ref, c_ref, sh_ref, sl_ref, o_ref):
    x = x_ref[...]
    half = ROT_DIM // 2
    reps = QK_WIDTH // V7X_LANES
    c = jnp.tile(c_ref[...], (1, reps))
    sh = jnp.tile(sh_ref[...], (1, reps))
    sl = jnp.tile(sl_ref[...], (1, reps))
    x_up = pltpu.roll(x, QK_WIDTH - half, axis=1)
    x_dn = pltpu.roll(x, half, axis=1)
    o_ref[...] = x * c + x_up * sh + x_dn * sl


def _rope(za, tables):
    n_p = P_ROWS // ROW_TILE
    per_seq = SEQ // ROW_TILE
    tmap = lambda i: (jnp.where(i < n_p, i % per_seq, per_seq), 0)
    tspec = pl.BlockSpec((ROW_TILE, V7X_LANES), tmap)
    return pl.pallas_call(
        _rope_kernel,
        grid=(ROWS // ROW_TILE,),
        in_specs=[pl.BlockSpec((ROW_TILE, QK_WIDTH), lambda i: (i, OFF_Q // QK_WIDTH)), tspec, tspec, tspec],
        out_specs=pl.BlockSpec((ROW_TILE, QK_WIDTH), lambda i: (i, 0)),
        out_shape=jax.ShapeDtypeStruct((ROWS, QK_WIDTH), jnp.float32),
        compiler_params=_params(("parallel",), 40 * MIB),
        name="rope",
    )(za, *tables)


def _conv_prompt_kernel(h_ref, b_ref, c_ref, w_ref, y_ref, st_ref):
    u = c_ref[...] * h_ref[...]
    w = w_ref[...]
    row = lax.broadcasted_iota(jnp.int32, u.shape, 0)
    u1 = jnp.where(row >= 1, pltpu.roll(u, 1, axis=0), 0.0)
    u2 = jnp.where(row >= 2, pltpu.roll(u, 2, axis=0), 0.0)
    y = w[0:1] * u2 + w[1:2] * u1 + w[2:3] * u
    y_ref[...] = (b_ref[...] * y).astype(y_ref.dtype)
    st_ref[...] = u[SEQ - (CONV_K - 1):, :]


def _conv_prompt(za, w):
    col = lambda c: pl.BlockSpec((SEQ, CONV_WIDTH), lambda b, c=c: (b, c))
    return pl.pallas_call(
        _conv_prompt_kernel,
        grid=(BATCH,),
        in_specs=[col(0), col(1), col(2), pl.BlockSpec((CONV_K, CONV_WIDTH), lambda b: (0, 0))],
        out_specs=[pl.BlockSpec((SEQ, CONV_WIDTH), lambda b: (b, 0)),
                   pl.BlockSpec((None, CONV_K - 1, CONV_WIDTH), lambda b: (b, 0, 0))],
        out_shape=[jax.ShapeDtypeStruct((P_ROWS, CONV_WIDTH), jnp.bfloat16),
                   jax.ShapeDtypeStruct((BATCH, CONV_K - 1, CONV_WIDTH), jnp.float32)],
        compiler_params=_params(("parallel",), 48 * MIB),
        name="conv_prompt",
    )(za, za, za, w)


def _conv_sample_kernel(h_ref, b_ref, c_ref, w_ref, s_ref, y_ref, st_ref):
    u = c_ref[...] * h_ref[...]
    w = w_ref[...]
    s0, s1 = s_ref[0], s_ref[1]
    keep = S_ROWS - DEC_BATCH
    u1 = jnp.concatenate([s1, u[:keep]], axis=0)
    u2 = jnp.concatenate([s0, s1, u[:keep - DEC_BATCH]], axis=0)
    y = w[0:1] * u2 + w[1:2] * u1 + w[2:3] * u
    y_ref[...] = (b_ref[...] * y).astype(y_ref.dtype)
    st_ref[0] = u[S_ROWS - 2 * DEC_BATCH:S_ROWS - DEC_BATCH]
    st_ref[1] = u[S_ROWS - DEC_BATCH:]


def _conv_sample(za, w, state):
    blk = P_ROWS // S_ROWS
    col = lambda c: pl.BlockSpec((S_ROWS, CONV_WIDTH), lambda i, c=c: (blk, c))
    st_spec = pl.BlockSpec((CONV_K - 1, DEC_BATCH, CONV_WIDTH), lambda i: (0, 0, 0))
    return pl.pallas_call(
        _conv_sample_kernel,
        grid=(1,),
        in_specs=[col(0), col(1), col(2), pl.BlockSpec((CONV_K, CONV_WIDTH), lambda i: (0, 0)), st_spec],
        out_specs=[pl.BlockSpec((S_ROWS, CONV_WIDTH), lambda i: (0, 0)), st_spec],
        out_shape=[jax.ShapeDtypeStruct((S_ROWS, CONV_WIDTH), jnp.bfloat16),
                   jax.ShapeDtypeStruct((CONV_K - 1, DEC_BATCH, CONV_WIDTH), jnp.float32)],
        compiler_params=_params(("arbitrary",), 24 * MIB),
        name="conv_sample",
    )(za, za, za, w, state)


def _attn_prompt_kernel(q_ref, kc_ref, kp_ref, vc_ref, vp_ref, o_ref, l_ref):
    blk = pl.program_id(2)
    q = q_ref[...].astype(jnp.bfloat16)
    k = jnp.concatenate([kp_ref[...], kc_ref[...]], axis=0).astype(jnp.bfloat16)
    v = jnp.concatenate([vp_ref[...], vc_ref[...]], axis=0).astype(jnp.bfloat16)
    qi = lax.broadcasted_iota(jnp.int32, (BAND, 2 * BAND), 0)
    kj = lax.broadcasted_iota(jnp.int32, (BAND, 2 * BAND), 1)
    dist = qi + BAND - kj
    valid = (dist >= 0) & (dist <= BAND) & ((kj >= BAND) | (blk > 0))
    outs, lses = [], []
    for h in range(HEADS_PER_GROUP):
        sl = slice(h * HEAD_DIM, (h + 1) * HEAD_DIM)
        s = lax.dot_general(q[:, sl], k[:, sl], (((1,), (1,)), ((), ())),
                            preferred_element_type=jnp.float32) * (HEAD_DIM ** -0.5)
        s = jnp.where(valid, s, -jnp.inf)
        m = jnp.max(s, axis=-1, keepdims=True)
        p = jnp.exp(s - m)
        den = jnp.sum(p, axis=-1, keepdims=True)
        o = jnp.dot(p.astype(jnp.bfloat16), v[:, sl], preferred_element_type=jnp.float32) / den
        outs.append(o)
        lses.append(jnp.broadcast_to(m + jnp.log(den), (BAND, HEAD_DIM)))
    o_ref[...] = jnp.concatenate(outs, axis=-1)
    l_ref[...] = jnp.concatenate(lses, axis=-1)


def _attn_prompt(qk, za, g):
    d = ATTN_GROUPS[g][1]
    nb = SEQ // d // BAND
    qk_v = qk.reshape(ROWS // d, d * QK_WIDTH)
    za_v = za.reshape(ROWS // d, d * ZA_WIDTH)
    qcols, zcols = QK_WIDTH // ATTN_OUT, ZA_WIDTH // ATTN_OUT
    koff, voff = ATTN_QKV // ATTN_OUT, OFF_V // ATTN_OUT
    blk = (BAND, ATTN_OUT)
    cur = lambda b, r, j: b * nb + j
    prev = lambda b, r, j: b * nb + jnp.maximum(j - 1, 0)
    in_specs = [
        pl.BlockSpec(blk, lambda b, r, j: (cur(b, r, j), r * qcols + g)),
        pl.BlockSpec(blk, lambda b, r, j: (cur(b, r, j), r * qcols + koff + g)),
        pl.BlockSpec(blk, lambda b, r, j: (prev(b, r, j), r * qcols + koff + g)),
        pl.BlockSpec(blk, lambda b, r, j: (cur(b, r, j), r * zcols + voff + g)),
        pl.BlockSpec(blk, lambda b, r, j: (prev(b, r, j), r * zcols + voff + g)),
    ]
    out_spec = pl.BlockSpec(blk, lambda b, r, j: (cur(b, r, j), r))
    shape = jax.ShapeDtypeStruct((P_ROWS // d, d * ATTN_OUT), jnp.float32)
    o, l = pl.pallas_call(
        _attn_prompt_kernel,
        grid=(BATCH, d, nb),
        in_specs=in_specs,
        out_specs=[out_spec, out_spec],
        out_shape=[shape, shape],
        compiler_params=_params(("parallel", "parallel", "parallel"), 24 * MIB),
        name=f"attn_prompt_g{g}",
    )(qk_v, qk_v, qk_v, za_v, za_v)
    return o.reshape(P_ROWS, ATTN_OUT), l.reshape(P_ROWS, ATTN_OUT)


def _head_segments():
    c = np.arange(ATTN_OUT)[:, None] // HEAD_DIM
    h = np.arange(V7X_LANES)[None, :]
    return (c == h).astype(np.float32)


def _attn_sample_kernel(q_ref, kn_ref, vn_ref, c_ref, seg_ref, segt_ref, o_ref, l_ref, *, dil):
    nseq = SAMPLE_SEQ_TILE
    seg = seg_ref[...]
    segt = segt_ref[...]
    scale = HEAD_DIM ** -0.5
    kn = kn_ref[...]
    vn = vn_ref[...]
    row = lax.broadcasted_iota(jnp.int32, (1, BAND, 1), 1)
    for t in range(DEC_SEQ):
        qt = q_ref[t]
        off = (t % dil) * 2 * ATTN_OUT if dil > 1 else 0
        kc = c_ref[:, :, off:off + ATTN_OUT]
        vc = c_ref[:, :, off + ATTN_OUT:off + 2 * ATTN_OUT]
        sc = _bdot((kc * qt[:, None, :]).reshape(nseq * BAND, ATTN_OUT), seg)
        sc = sc.reshape(nseq, BAND, V7X_LANES) * scale
        sn = _bdot((kn * qt[None, :, :]).reshape(DEC_SEQ * nseq, ATTN_OUT), seg)
        sn = sn.reshape(DEC_SEQ, nseq, V7X_LANES) * scale
        if dil == 1:
            sc = jnp.where(row >= t, sc, -jnp.inf)
            new_ok = [s <= t for s in range(DEC_SEQ)]
        else:
            new_ok = [s == t for s in range(DEC_SEQ)]
        m = jnp.max(sc, axis=1)
        for s in range(DEC_SEQ):
            if new_ok[s]:
                m = jnp.maximum(m, sn[s])
        pc = jnp.exp(sc - m[:, None, :])
        pe = _bdot(pc.reshape(nseq * BAND, V7X_LANES), segt).reshape(nseq, BAND, ATTN_OUT)
        acc = jnp.sum(pe * vc, axis=1)
        den = jnp.sum(pe, axis=1)
        for s in range(DEC_SEQ):
            if new_ok[s]:
                pn = _bdot(jnp.exp(sn[s] - m), segt)
                acc = acc + pn * vn[s]
                den = den + pn
        me = jnp.concatenate(
            [jnp.broadcast_to(m[:, h:h + 1], (nseq, HEAD_DIM)) for h in range(HEADS_PER_GROUP)], axis=-1)
        o_ref[t] = acc / den
        l_ref[t] = me + jnp.log(den)


def _attn_sample(qk, za, cache_l, layer, g):
    d = ATTN_GROUPS[g][1]
    res = min(d, DEC_SEQ)
    width = res * 2 * ATTN_OUT
    cache_v = cache_l.reshape(DEPTH, DEC_BATCH, BAND, d * 2 * ATTN_OUT)
    qk_v = qk.reshape(ROWS // DEC_BATCH, DEC_BATCH, QK_WIDTH)
    za_v = za.reshape(ROWS // DEC_BATCH, DEC_BATCH, ZA_WIDTH)
    tblk = P_ROWS // S_ROWS
    nb = (DEC_SEQ, SAMPLE_SEQ_TILE, ATTN_OUT)
    koff, voff = ATTN_QKV // ATTN_OUT, OFF_V // ATTN_OUT
    seg = jnp.asarray(_head_segments())
    out_spec = pl.BlockSpec(nb, lambda i: (0, i, 0))
    shape = jax.ShapeDtypeStruct((DEC_SEQ, DEC_BATCH, ATTN_OUT), jnp.float32)
    o, l = pl.pallas_call(
        functools.partial(_attn_sample_kernel, dil=d),
        grid=(DEC_BATCH // SAMPLE_SEQ_TILE,),
        in_specs=[pl.BlockSpec(nb, lambda i: (tblk, i, g)),
                  pl.BlockSpec(nb, lambda i: (tblk, i, koff + g)),
                  pl.BlockSpec(nb, lambda i: (tblk, i, voff + g)),
                  pl.BlockSpec((None, SAMPLE_SEQ_TILE, BAND, width), lambda i: (layer, i, 0, 0)),
                  pl.BlockSpec((ATTN_OUT, V7X_LANES), lambda i: (0, 0)),
                  pl.BlockSpec((V7X_LANES, ATTN_OUT), lambda i: (0, 0))],
        out_specs=[out_spec, out_spec],
        out_shape=[shape, shape],
        compiler_params=_params(("parallel",), 48 * MIB),
        name=f"attn_sample_g{g}",
    )(qk_v, qk_v, za_v, cache_v, seg, seg.T)
    return o.reshape(S_ROWS, ATTN_OUT), l.reshape(S_ROWS, ATTN_OUT)


def _ssm_tables(lam_re, lam_im, log_dt, b_re, b_im, c_re, c_im, d_skip):
    dt = jnp.exp(log_dt)[:, None]
    er = jnp.exp(lam_re * dt)
    lbr, lbi = er * jnp.cos(lam_im * dt), er * jnp.sin(lam_im * dt)
    den = lam_re * lam_re + lam_im * lam_im
    nr, ni = lbr - 1.0, lbi
    qr = (nr * lam_re + ni * lam_im) / den
    qi = (ni * lam_re - nr * lam_im) / den
    bbr = qr[:, :, None] * b_re - qi[:, :, None] * b_im
    bbi = qr[:, :, None] * b_im + qi[:, :, None] * b_re
    eye = jnp.eye(SSM_GB, dtype=jnp.float32)

    def in_block(b):
        b = b.reshape(N_GB, SSM_GB, SSM_STATE, SSM_CH_PER_GROUP)
        return jnp.einsum('ngpc,gh->ngchp', b, eye).reshape(N_GB, GB_CH, GB_ST)

    def out_block(c):
        c = c.reshape(N_GB, SSM_GB, SSM_CH_PER_GROUP, SSM_STATE)
        return jnp.einsum('ngcp,gh->ngphc', c, eye).reshape(N_GB, GB_ST, GB_CH)

    b_blk = jnp.concatenate([in_block(bbr), in_block(bbi)], axis=2).astype(jnp.bfloat16)
    c_blk = jnp.concatenate([out_block(c_re), out_block(-c_im)], axis=1).astype(jnp.bfloat16)

    def pack(re, im):
        re = re.reshape(N_GB, 1, GB_ST)
        im = im.reshape(N_GB, 1, GB_ST)
        return jnp.concatenate([re, re], axis=2), jnp.concatenate([-im, im], axis=2)

    pr, pi = lbr, lbi
    for _ in range(int(math.log2(SSM_CHUNK))):
        pr, pi = pr * pr - pi * pi, 2.0 * pr * pi
    return dict(b=b_blk, c=c_blk, lam=pack(lbr, lbi), lam_chunk=pack(pr, pi), d=d_skip.reshape(1, SSM_WIDTH))


def _cstep(s, a, bsw):
    swapped = jnp.concatenate([s[:, GB_ST:], s[:, :GB_ST]], axis=1)
    return a * s + bsw * swapped


def _ssm_kernel(u_ref, s0_ref, b_ref, c_ref, a_ref, bsw_ref, d_ref, *refs, steps, emit_y):
    if emit_y:
        y_ref, sout_ref, s_ref = refs
    else:
        sout_ref, s_ref = refs
    t = pl.program_id(1)

    @pl.when(t == 0)
    def _():
        s_ref[...] = s0_ref[...]

    u = u_ref[...]
    s = _cstep(s_ref[...], a_ref[...], bsw_ref[...]) + _bdot(u, b_ref[...])
    s_ref[...] = s
    if emit_y:
        y_ref[...] = _bdot(s, c_ref[...]) + d_ref[...] * u

    @pl.when(t == steps - 1)
    def _():
        sout_ref[...] = s


def _ssm_pass(u_arr, u_map, y_map, rows, steps, s0, tabs, y_shape, name):
    emit_y = y_shape is not None
    st_spec = pl.BlockSpec((rows, 2 * GB_ST), lambda gb, t: (0, gb))
    coef = pl.BlockSpec((None, 1, 2 * GB_ST), lambda gb, t: (gb, 0, 0))
    in_specs = [pl.BlockSpec((rows, GB_CH), u_map), st_spec,
                pl.BlockSpec((None, GB_CH, 2 * GB_ST), lambda gb, t: (gb, 0, 0)),
                pl.BlockSpec((None, 2 * GB_ST, GB_CH), lambda gb, t: (gb, 0, 0)),
                coef, coef,
                pl.BlockSpec((1, GB_CH), lambda gb, t: (0, gb))]
    st_shape = jax.ShapeDtypeStruct((rows, ST_WIDTH), jnp.float32)
    out_specs, out_shape = [st_spec], [st_shape]
    if emit_y:
        out_specs, out_shape = [pl.BlockSpec((rows, GB_CH), y_map), st_spec], [y_shape, st_shape]
    return pl.pallas_call(
        functools.partial(_ssm_kernel, steps=steps, emit_y=emit_y),
        grid=(N_GB, steps),
        in_specs=in_specs,
        out_specs=out_specs,
        out_shape=out_shape,
        scratch_shapes=[pltpu.VMEM((rows, 2 * GB_ST), jnp.float32)],
        compiler_params=_params(("parallel", "arbitrary"), 48 * MIB),
        name=name,
    )(u_arr, s0, tabs['b'], tabs['c'], tabs['lam'][0], tabs['lam'][1], tabs['d'])


def _ssm_carry_kernel(e_ref, a_ref, bsw_ref, sin_ref, fin_ref):
    a = a_ref[...]
    bsw = bsw_ref[...]
    n_chunks = SEQ // SSM_CHUNK

    def body(n, states):
        new = []
        for b in range(BATCH):
            row = b * n_chunks + n
            sin_ref[pl.ds(row, 1), :] = states[b]
            new.append(_cstep(states[b], a, bsw) + e_ref[pl.ds(row, 1), :])
        return tuple(new)

    zero = jnp.zeros((1, 2 * GB_ST), jnp.float32)
    final = lax.fori_loop(0, n_chunks, body, (zero,) * BATCH)
    for b in range(BATCH):
        fin_ref[pl.ds(b, 1), :] = final[b]


def _ssm_carry(e, tabs):
    n_rows = P_ROWS // SSM_CHUNK
    st_spec = lambda rows: pl.BlockSpec((rows, 2 * GB_ST), lambda gb: (0, gb))
    coef = pl.BlockSpec((None, 1, 2 * GB_ST), lambda gb: (gb, 0, 0))
    return pl.pallas_call(
        _ssm_carry_kernel,
        grid=(N_GB,),
        in_specs=[st_spec(n_rows), coef, coef],
        out_specs=[st_spec(n_rows), st_spec(BATCH)],
        out_shape=[jax.ShapeDtypeStruct((n_rows, ST_WIDTH), jnp.float32),
                   jax.ShapeDtypeStruct((BATCH, ST_WIDTH), jnp.float32)],
        compiler_params=_params(("parallel",), 32 * MIB),
        name="ssm_carry",
    )(e, tabs['lam_chunk'][0], tabs['lam_chunk'][1])


def _pack_state(s):
    b = s.shape[0]
    s = s.reshape(b, N_GB, SSM_GB * SSM_STATE, 2)
    return jnp.transpose(s, (0, 1, 3, 2)).reshape(b, ST_WIDTH)


def _unpack_state(s):
    b = s.shape[0]
    s = s.reshape(b, N_GB, 2, SSM_GB * SSM_STATE)
    return jnp.transpose(s, (0, 1, 3, 2)).reshape(b, SSM_GROUPS, SSM_STATE, 2)


def _ssm(za, state_s, tabs):
    n_rows = P_ROWS // SSM_CHUNK
    zcols = ZA_WIDTH // GB_CH
    ucol = OFF_U // GB_CH
    za_chunks = za.reshape(ROWS // SSM_CHUNK, SSM_CHUNK * ZA_WIDTH)
    ycols = SSM_WIDTH // GB_CH
    p_map = lambda gb, t: (0, t * zcols + ucol + gb)
    zero = jnp.zeros((n_rows, ST_WIDTH), jnp.float32)
    (e,) = _ssm_pass(za_chunks, p_map, None, n_rows, SSM_CHUNK, zero, tabs, None, "ssm_prompt_local")
    s_in, s_fin = _ssm_carry(e, tabs)
    y_shape = jax.ShapeDtypeStruct((n_rows, SSM_CHUNK * SSM_WIDTH), jnp.float32)
    y_p, _ = _ssm_pass(za_chunks, p_map, lambda gb, t: (0, t * ycols + gb), n_rows, SSM_CHUNK,
                       s_in, tabs, y_shape, "ssm_prompt")
    tblk = P_ROWS // DEC_BATCH
    ys_shape = jax.ShapeDtypeStruct((S_ROWS, SSM_WIDTH), jnp.float32)
    y_s, s_new = _ssm_pass(za, lambda gb, t: (tblk + t, ucol + gb), lambda gb, t: (t, gb), DEC_BATCH, DEC_SEQ,
                           _pack_state(state_s), tabs, ys_shape, "ssm_sample")
    return y_p.reshape(P_ROWS, SSM_WIDTH), y_s, _unpack_state(s_fin), _unpack_state(s_new)


def _mix_kernel(x_ref, gate_ref, gb_ref, cp_ref, cs_ref, ap_refs, as_refs, sp_ref, ss_ref,
                wglu_ref, bglu_ref, pc_ref, pa_ref, ps_ref, wo_ref, gn_ref, xo_ref, ho_ref):
    is_p = pl.program_id(0) < P_ROWS // MIX_TILE
    pick = lambda p_ref, s_ref: jnp.where(is_p, p_ref[...], s_ref[...])
    y_conv = pick(cp_ref, cs_ref)
    o = [pick(ap_refs[2 * g], as_refs[2 * g]) for g in range(N_ATTN_GROUPS)]
    lse = [pick(ap_refs[2 * g + 1], as_refs[2 * g + 1]) for g in range(N_ATTN_GROUPS)]
    m = jnp.maximum(jnp.maximum(lse[0], lse[1]), lse[2])
    w = [jnp.exp(l - m) for l in lse]
    y_attn = (w[0] * o[0] + w[1] * o[1] + w[2] * o[2]) / (w[0] + w[1] + w[2])
    zg = jax.nn.gelu(pick(sp_ref, ss_ref))
    y_ssm = zg * jax.nn.sigmoid(_bdot(zg, wglu_ref[...]) + bglu_ref[...])
    gt = jax.nn.sigmoid(gate_ref[...].astype(jnp.float32) + gb_ref[...])
    merged = (gt[:, :D_MODEL] * _bdot(y_conv, pc_ref[...])
              + gt[:, D_MODEL:2 * D_MODEL] * _bdot(y_attn, pa_ref[...])
              + gt[:, 2 * D_MODEL:] * _bdot(y_ssm, ps_ref[...]))
    x = x_ref[...] + _bdot(merged, wo_ref[...])
    xo_ref[...] = x
    ho_ref[...] = _rms(x, gn_ref[...]).astype(ho_ref.dtype)


def _mix(x, gates, conv_p, conv_s, attn_p, attn_s, ssm_p, ssm_s, lp, h_dtype):
    n_p = P_ROWS // MIX_TILE
    p_map = lambda i: (jnp.minimum(i, n_p - 1), 0)
    s_map = lambda i: (jnp.maximum(i - n_p, 0), 0)
    row_map = lambda i: (i, 0)
    const = lambda i: (0, 0)
    rows = lambda width, imap: pl.BlockSpec((MIX_TILE, width), imap)
    full = lambda a: pl.BlockSpec(a.shape, const, pipeline_mode=pl.Buffered(1))
    n_attn = 2 * N_ATTN_GROUPS
    weights = [lp['w_glu'], lp['b_glu'], lp['proj_conv'], lp['proj_attn'], lp['proj_ssm'], lp['w_out'],
               lp['norm_next']]
    in_specs = ([rows(D_MODEL, row_map), rows(GATE_WIDTH, row_map), full(lp['gate_bias']),
                 rows(CONV_WIDTH, p_map), rows(CONV_WIDTH, s_map)]
                + [rows(ATTN_OUT, p_map)] * n_attn + [rows(ATTN_OUT, s_map)] * n_attn
                + [rows(SSM_WIDTH, p_map), rows(SSM_WIDTH, s_map)]
                + [full(w) for w in weights])

    def body(*refs):
        x_ref, gate_ref, gb_ref, cp_ref, cs_ref = refs[:5]
        ap_refs = refs[5:5 + n_attn]
        as_refs = refs[5 + n_attn:5 + 2 * n_attn]
        rest = refs[5 + 2 * n_attn:]
        _mix_kernel(x_ref, gate_ref, gb_ref, cp_ref, cs_ref, ap_refs, as_refs, *rest)

    return pl.pallas_call(
        body,
        grid=(ROWS // MIX_TILE,),
        in_specs=in_specs,
        out_specs=[rows(D_MODEL, row_map), rows(D_MODEL, row_map)],
        out_shape=[jax.ShapeDtypeStruct((ROWS, D_MODEL), jnp.float32),
                   jax.ShapeDtypeStruct((ROWS, D_MODEL), h_dtype)],
        compiler_params=_params(("parallel",), 56 * MIB),
        name="mix",
    )(x, gates, lp['gate_bias'], conv_p, conv_s, *attn_p, *attn_s, ssm_p, ssm_s, *weights)


def _ffn_up_kernel(be_ref, x_ref, w1_ref, w3_ref, h_ref):
    x = x_ref[...].astype(jnp.bfloat16)
    a = jnp.dot(x, w1_ref[...], preferred_element_type=jnp.float32)
    b = jnp.dot(x, w3_ref[...], preferred_element_type=jnp.float32)
    h_ref[...] = (jax.nn.silu(a) * b).astype(h_ref.dtype)


def _ffn_up(block_e, x, w1, w3, bm, bf, name):
    m = x.shape[0]
    f = w1.shape[2]
    wspec = pl.BlockSpec((None, D_MODEL, bf), lambda j, i, be: (be[i], 0, j))
    grid_spec = pltpu.PrefetchScalarGridSpec(
        num_scalar_prefetch=1,
        grid=(f // bf, m // bm),
        in_specs=[pl.BlockSpec((bm, D_MODEL), lambda j, i, be: (i, 0)), wspec, wspec],
        out_specs=pl.BlockSpec((bm, bf), lambda j, i, be: (i, j)),
    )
    return pl.pallas_call(
        _ffn_up_kernel,
        grid_spec=grid_spec,
        out_shape=jax.ShapeDtypeStruct((m, f), jnp.bfloat16),
        compiler_params=_params(("parallel", "arbitrary"), 48 * MIB),
        name=name,
    )(block_e, x, w1, w3)


def _ffn_down_dense_kernel(h_ref, w_ref, x_ref, g_ref, xo_ref, ho_ref, acc_ref, *, n_k):
    k = pl.program_id(1)

    @pl.when(k == 0)
    def _():
        acc_ref[...] = x_ref[...]

    acc_ref[...] += jnp.dot(h_ref[...], w_ref[...], preferred_element_type=jnp.float32)

    @pl.when(k == n_k - 1)
    def _():
        x = acc_ref[...]
        xo_ref[...] = x
        ho_ref[...] = _rms(x, g_ref[...]).astype(ho_ref.dtype)


def _ffn_down_dense(h, w2, x, g_next, bk):
    n_k = D_FF // bk
    row = pl.BlockSpec((ROW_TILE, D_MODEL), lambda i, k: (i, 0))
    return pl.pallas_call(
        functools.partial(_ffn_down_dense_kernel, n_k=n_k),
        grid=(ROWS // ROW_TILE, n_k),
        in_specs=[pl.BlockSpec((ROW_TILE, bk), lambda i, k: (i, k)),
                  pl.BlockSpec((bk, D_MODEL), lambda i, k: (k, 0)),
                  row, pl.BlockSpec((1, D_MODEL), lambda i, k: (0, 0))],
        out_specs=[row, row],
        out_shape=[jax.ShapeDtypeStruct((ROWS, D_MODEL), jnp.float32),
                   jax.ShapeDtypeStruct((ROWS, D_MODEL), jnp.bfloat16)],
        scratch_shapes=[pltpu.VMEM((ROW_TILE, D_MODEL), jnp.float32)],
        compiler_params=_params(("parallel", "arbitrary"), 48 * MIB),
        name="ffn_down",
    )(h, w2, x, g_next.reshape(1, D_MODEL))


def _moe_down_kernel(be_ref, h_ref, w_ref, g_ref, y_ref):
    y = jnp.dot(h_ref[...], w_ref[...], preferred_element_type=jnp.float32)
    y_ref[...] = y * g_ref[...]


def _moe_down(block_e, h, w2, row_gate):
    grid_spec = pltpu.PrefetchScalarGridSpec(
        num_scalar_prefetch=1,
        grid=(MOE_BLOCKS,),
        in_specs=[pl.BlockSpec((MOE_BLOCK, D_FF_EXPERT), lambda i, be: (i, 0)),
                  pl.BlockSpec((None, D_FF_EXPERT, D_MODEL), lambda i, be: (be[i], 0, 0)),
                  pl.BlockSpec((MOE_BLOCK, 1), lambda i, be: (i, 0))],
        out_specs=pl.BlockSpec((MOE_BLOCK, D_MODEL), lambda i, be: (i, 0)),
    )
    return pl.pallas_call(
        _moe_down_kernel,
        grid_spec=grid_spec,
        out_shape=jax.ShapeDtypeStruct((MOE_ROWS, D_MODEL), jnp.float32),
        compiler_params=_params(("arbitrary",), 48 * MIB),
        name="moe_down",
    )(block_e, h, w2, row_gate)


def _router_kernel(x_ref, g_ref, w_ref, hs_ref, r_ref):
    hs = _rms(x_ref[...], g_ref[...])
    hs_ref[...] = hs
    logits = jnp.dot(hs, w_ref[...], preferred_element_type=jnp.float32, precision=lax.Precision.HIGHEST)
    lane = lax.broadcasted_iota(jnp.int32, logits.shape, 1)
    logits = jnp.where(lane < N_EXPERTS, logits, -jnp.inf)
    m1 = jnp.max(logits, axis=-1, keepdims=True)
    i1 = jnp.min(jnp.where(logits == m1, lane, ROUTER_PAD), axis=-1, keepdims=True)
    rest = jnp.where(lane == i1, -jnp.inf, logits)
    m2 = jnp.max(rest, axis=-1, keepdims=True)
    i2 = jnp.min(jnp.where(rest == m2, lane, ROUTER_PAD), axis=-1, keepdims=True)
    e2 = jnp.exp(m2 - m1)
    g1 = 1.0 / (1.0 + e2)
    g2 = e2 / (1.0 + e2)
    out = jnp.where(lane == 0, i1.astype(jnp.float32),
                    jnp.where(lane == 1, i2.astype(jnp.float32),
                              jnp.where(lane == 2, g1, jnp.where(lane == 3, g2, 0.0))))
    r_ref[...] = out


def _router(x, g, w_router):
    w = jnp.zeros((D_MODEL, ROUTER_PAD), jnp.float32).at[:, :N_EXPERTS].set(w_router)
    row = pl.BlockSpec((ROW_TILE, D_MODEL), lambda i: (i, 0))
    return pl.pallas_call(
        _router_kernel,
        grid=(ROWS // ROW_TILE,),
        in_specs=[row, pl.BlockSpec((1, D_MODEL), lambda i: (0, 0)),
                  pl.BlockSpec((D_MODEL, ROUTER_PAD), lambda i: (0, 0))],
        out_specs=[row, pl.BlockSpec((ROW_TILE, ROUTER_PAD), lambda i: (i, 0))],
        out_shape=[jax.ShapeDtypeStruct((ROWS, D_MODEL), jnp.float32),
                   jax.ShapeDtypeStruct((ROWS, ROUTER_PAD), jnp.float32)],
        compiler_params=_params(("parallel",), 32 * MIB),
        name="router",
    )(x, g.reshape(1, D_MODEL), w)


def _gather_kernel(idx_ref, src_ref, dst_ref, sem, *, rows_per_step):
    base = pl.program_id(0) * rows_per_step

    def copy(r):
        return pltpu.make_async_copy(src_ref.at[pl.ds(idx_ref[base + r], 1), :],
                                     dst_ref.at[pl.ds(base + r, 1), :], sem)

    def start(r, c):
        copy(r).start()
        return c

    def wait(r, c):
        copy(r).wait()
        return c

    lax.fori_loop(0, rows_per_step, start, 0)
    lax.fori_loop(0, rows_per_step, wait, 0)


def _gather_rows(idx, src, n_out, rows_per_step, name):
    grid_spec = pltpu.PrefetchScalarGridSpec(
        num_scalar_prefetch=1,
        grid=(n_out // rows_per_step,),
        in_specs=[pl.BlockSpec(memory_space=pl.ANY)],
        out_specs=pl.BlockSpec(memory_space=pl.ANY),
        scratch_shapes=[pltpu.SemaphoreType.DMA(())],
    )
    return pl.pallas_call(
        functools.partial(_gather_kernel, rows_per_step=rows_per_step),
        grid_spec=grid_spec,
        out_shape=jax.ShapeDtypeStruct((n_out, src.shape[1]), src.dtype),
        compiler_params=_params(("arbitrary",), 16 * MIB),
        name=name,
    )(idx, src)


def _combine_kernel(x_ref, y0_ref, y1_ref, g_ref, o_ref):
    o_ref[...] = _rms(x_ref[...] + y0_ref[...] + y1_ref[...], g_ref[...])


def _combine(x, yg, g_final):
    row = lambda off: pl.BlockSpec((ROW_TILE, D_MODEL), lambda i, off=off: (i + off, 0))
    return pl.pallas_call(
        _combine_kernel,
        grid=(ROWS // ROW_TILE,),
        in_specs=[row(0), row(0), row(ROWS // ROW_TILE), pl.BlockSpec((1, D_MODEL), lambda i: (0, 0))],
        out_specs=row(0),
        out_shape=jax.ShapeDtypeStruct((ROWS, D_MODEL), jnp.float32),
        compiler_params=_params(("parallel",), 40 * MIB),
        name="combine",
    )(x, yg, yg, g_final.reshape(1, D_MODEL))


def _dispatch_plan(route):
    top_idx = route[:, 0:TOP_K].astype(jnp.int32)
    gates = route[:, TOP_K:2 * TOP_K]
    flat_e = top_idx.reshape(N_ASSIGN)
    flat_tok = jnp.repeat(jnp.arange(ROWS, dtype=jnp.int32), TOP_K)
    order = jnp.argsort(flat_e)
    se = flat_e[order]
    counts = jnp.bincount(flat_e, length=N_EXPERTS)
    padded = (counts + MOE_BLOCK - 1) // MOE_BLOCK * MOE_BLOCK
    start = jnp.cumsum(counts) - counts
    ends_p = jnp.cumsum(padded)
    dest = (ends_p - padded)[se] + jnp.arange(N_ASSIGN, dtype=jnp.int32) - start[se]
    row_tok = jnp.zeros((MOE_ROWS,), jnp.int32).at[dest].set(flat_tok[order])
    row_gate = jnp.zeros((MOE_ROWS,), jnp.float32).at[dest].set(gates.reshape(N_ASSIGN)[order])
    block_e = jnp.minimum(jnp.searchsorted(ends_p, jnp.arange(MOE_BLOCKS) * MOE_BLOCK, side='right'),
                          N_EXPERTS - 1).astype(jnp.int32)
    pos = jnp.zeros((N_ASSIGN,), jnp.int32).at[order].set(dest.astype(jnp.int32))
    pos = pos.reshape(ROWS, TOP_K).T.reshape(N_ASSIGN)
    return row_tok, row_gate.reshape(MOE_ROWS, 1), block_e, pos


def kernel(x_prompt, x_sample, cache_kv_w128, cache_kv_w512, cache_kv_w2048, state_conv, state_ssm, norm_mix, w_in, gate_bias, conv_w, ssm_lam_re, ssm_lam_im, ssm_log_dt, ssm_b_re, ssm_b_im, ssm_c_re, ssm_c_im, ssm_d, ssm_w_glu, ssm_b_glu, proj_conv, proj_attn, proj_ssm, w_out, norm_ffn, ffn_w1, ffn_w3, ffn_w2, router_w, moe_w1, moe_w3, moe_w2, norm_final):
    bf = jnp.bfloat16
    caches = (cache_kv_w128, cache_kv_w512, cache_kv_w2048)
    x = jnp.concatenate([x_prompt.reshape(P_ROWS, D_MODEL),
                         jnp.transpose(x_sample, (1, 0, 2)).reshape(S_ROWS, D_MODEL)], axis=0)
    tables = _rope_tables()
    h = _norm(x, norm_mix[0])
    kv_p = [[] for _ in range(N_ATTN_GROUPS)]
    kv_s = [[] for _ in range(N_ATTN_GROUPS)]
    conv_p_all, conv_s_all, ssm_p_all, ssm_s_all = [], [], [], []
    y = None
    for l in range(DEPTH):
        w_in_l = w_in[l].astype(bf)
        za = _linear(h, w_in_l[:, :ZA_WIDTH], jnp.float32, ZA_WIDTH // 3, f"in_proj_{l}")
        gates = _linear(h, w_in_l[:, ZA_WIDTH:], bf, GATE_WIDTH // 4, f"gate_proj_{l}")
        qk = _rope(za, tables)
        conv_p, cst_p = _conv_prompt(za, conv_w[l])
        conv_s, cst_s = _conv_sample(za, conv_w[l], jnp.transpose(state_conv[l], (1, 0, 2)))
        attn_p, attn_s = [], []
        for g in range(N_ATTN_GROUPS):
            attn_p += list(_attn_prompt(qk, za, g))
            attn_s += list(_attn_sample(qk, za, caches[g], l, g))
        tabs = _ssm_tables(ssm_lam_re[l], ssm_lam_im[l], ssm_log_dt[l], ssm_b_re[l], ssm_b_im[l],
                           ssm_c_re[l], ssm_c_im[l], ssm_d[l])
        ssm_p, ssm_s, sst_p, sst_s = _ssm(za, state_ssm[l], tabs)
        moe_layer = l % 2 == 1
        lp = dict(gate_bias=gate_bias[l].reshape(1, GATE_WIDTH), w_glu=ssm_w_glu[l].astype(bf),
                  b_glu=ssm_b_glu[l].reshape(1, SSM_WIDTH), proj_conv=proj_conv[l].astype(bf),
                  proj_attn=proj_attn[l].astype(bf), proj_ssm=proj_ssm[l].astype(bf),
                  w_out=w_out[l].astype(bf), norm_next=norm_ffn[l].reshape(1, D_MODEL))
        x, h = _mix(x, gates, conv_p, conv_s, attn_p, attn_s, ssm_p, ssm_s, lp, bf)
        i = l // 2
        if not moe_layer:
            zero_e = jnp.zeros((ROWS // ROW_TILE,), jnp.int32)
            hid = _ffn_up(zero_e, h, ffn_w1[i].astype(bf)[None], ffn_w3[i].astype(bf)[None],
                          ROW_TILE, 512, "ffn_up")
            x, h = _ffn_down_dense(hid, ffn_w2[i].astype(bf), x, norm_mix[l + 1], D_FF // 4)
        else:
            hs, route = _router(x, norm_ffn[l], router_w[i])
            row_tok, row_gate, block_e, pos = _dispatch_plan(route)
            xs = _gather_rows(row_tok, hs, MOE_ROWS, MOE_BLOCK, "moe_dispatch")
            hid = _ffn_up(block_e, xs, moe_w1[i].astype(bf), moe_w3[i].astype(bf),
                          MOE_BLOCK, D_FF_EXPERT // 2, "moe_up")
            yb = _moe_down(block_e, hid, moe_w2[i].astype(bf), row_gate)
            yg = _gather_rows(pos, yb, N_ASSIGN, ROW_TILE, "moe_combine_gather")
            y = _combine(x, yg, norm_final)
        k_off = ATTN_QKV
        for g in range(N_ATTN_GROUPS):
            keep = min(ATTN_GROUPS[g][0], SEQ)
            cols = slice(g * ATTN_OUT, (g + 1) * ATTN_OUT)
            kp = qk[:P_ROWS, k_off:][:, cols].reshape(BATCH, SEQ, HEADS_PER_GROUP, HEAD_DIM)[:, SEQ - keep:]
            vp = za[:P_ROWS, OFF_V:OFF_V + ATTN_QKV][:, cols].reshape(BATCH, SEQ, HEADS_PER_GROUP, HEAD_DIM)[:, SEQ - keep:]
            kv_p[g].append(jnp.stack([kp, vp], axis=2))
            ks = qk[P_ROWS:, k_off:][:, cols].reshape(DEC_SEQ, DEC_BATCH, HEADS_PER_GROUP, HEAD_DIM)
            vs = za[P_ROWS:, OFF_V:OFF_V + ATTN_QKV][:, cols].reshape(DEC_SEQ, DEC_BATCH, HEADS_PER_GROUP, HEAD_DIM)
            kv_s[g].append(jnp.transpose(jnp.stack([ks, vs], axis=2), (1, 0, 2, 3, 4)))
        conv_p_all.append(cst_p)
        conv_s_all.append(jnp.transpose(cst_s, (1, 0, 2)))
        ssm_p_all.append(sst_p)
        ssm_s_all.append(sst_s)
    y_prompt = y[:P_ROWS].reshape(BATCH, SEQ, D_MODEL)
    y_sample = jnp.transpose(y[P_ROWS:].reshape(DEC_SEQ, DEC_BATCH, D_MODEL), (1, 0, 2))
    return (y_prompt, y_sample, jnp.stack(kv_p[0]), jnp.stack(kv_p[1]), jnp.stack(kv_p[2]),
            jnp.stack(conv_p_all), jnp.stack(ssm_p_all), jnp.stack(kv_s[0]), jnp.stack(kv_s[1]),
            jnp.stack(kv_s[2]), jnp.stack(conv_s_all), jnp.stack(ssm_s_all))
```

```python
import functools
import math

import jax
import jax.numpy as jnp
import numpy as np
from jax import lax
from jax.experimental import pallas as pl
from jax.experimental.pallas import tpu as pltpu

D_MODEL = 2048
BATCH = 4
SEQ = 2048
DEPTH = 2
DEC_BATCH = 128
DEC_SEQ = 4
PAST_LEN = 2048
CONV_WIDTH = 512
CONV_K = 3
HEAD_DIM = 64
HEADS_PER_GROUP = 4
ATTN_GROUPS = ((128, 1), (512, 4), (2048, 16))
N_ATTN_GROUPS = 3
ATTN_QKV = N_ATTN_GROUPS * HEADS_PER_GROUP * HEAD_DIM
ATTN_OUT = HEADS_PER_GROUP * HEAD_DIM
ROT_DIM = HEAD_DIM // 4
ROPE_THETA = 500000.0
SSM_WIDTH = 768
SSM_CH_PER_GROUP = 16
SSM_GROUPS = SSM_WIDTH // SSM_CH_PER_GROUP
SSM_STATE = 64
N_BRANCH = 3
D_FF = 5632
N_EXPERTS = 8
TOP_K = 2
D_FF_EXPERT = D_FF // TOP_K
EPS = 1e-6

V7X_LANES = 128
V7X_VMEM_BYTES = 64 * 1024 * 1024
MIB = 1024 * 1024

P_ROWS = BATCH * SEQ
S_ROWS = DEC_SEQ * DEC_BATCH
ROWS = P_ROWS + S_ROWS
ZA_WIDTH = 3 * CONV_WIDTH + 3 * ATTN_QKV + SSM_WIDTH
GATE_WIDTH = N_BRANCH * D_MODEL
QK_WIDTH = 2 * ATTN_QKV
OFF_Q = 3 * CONV_WIDTH
OFF_V = OFF_Q + 2 * ATTN_QKV
OFF_U = OFF_V + ATTN_QKV
BAND = 128
SSM_CHUNK = 16
SSM_GB = 16
N_GB = SSM_GROUPS // SSM_GB
GB_CH = SSM_GB * SSM_CH_PER_GROUP
GB_ST = SSM_GB * SSM_STATE
ST_WIDTH = N_GB * 2 * GB_ST
ROW_TILE = 512
MIX_TILE = 256
MOE_BLOCK = 256
N_ASSIGN = ROWS * TOP_K
MOE_BLOCKS = -(-(N_ASSIGN + N_EXPERTS * (MOE_BLOCK - 1)) // MOE_BLOCK)
MOE_ROWS = MOE_BLOCKS * MOE_BLOCK
ROUTER_PAD = V7X_LANES
SAMPLE_SEQ_TILE = 8
SAMPLE_CACHE_TILE_BYTES = 8 * MIB
SSM_BATCH_SPLIT = 2
HALVES = ATTN_OUT // V7X_LANES
assert DEPTH == 2 and HALVES == 2 and GB_CH == HALVES * V7X_LANES

assert ROWS % ROW_TILE == 0 and P_ROWS % ROW_TILE == 0
assert ROWS % MIX_TILE == 0 and P_ROWS % MIX_TILE == 0


def _params(semantics, vmem_bytes):
    return pltpu.CompilerParams(dimension_semantics=semantics,
                                vmem_limit_bytes=min(int(vmem_bytes), V7X_VMEM_BYTES - 4 * MIB))


def _rms(x, g):
    y = x * lax.rsqrt(jnp.mean(x * x, axis=-1, keepdims=True) + EPS)
    return y * g


def _bdot(a, b):
    return jnp.dot(a.astype(jnp.bfloat16), b.astype(jnp.bfloat16), preferred_element_type=jnp.float32)


def _norm_kernel(x_ref, g_ref, h_ref):
    h_ref[...] = _rms(x_ref[...], g_ref[...]).astype(h_ref.dtype)


def _norm(x, g):
    return pl.pallas_call(
        _norm_kernel,
        grid=(ROWS // ROW_TILE,),
        in_specs=[pl.BlockSpec((ROW_TILE, D_MODEL), lambda i: (i, 0)),
                  pl.BlockSpec((1, D_MODEL), lambda i: (0, 0))],
        out_specs=pl.BlockSpec((ROW_TILE, D_MODEL), lambda i: (i, 0)),
        out_shape=jax.ShapeDtypeStruct((ROWS, D_MODEL), jnp.bfloat16),
        compiler_params=_params(("parallel",), 24 * MIB),
        name="norm",
    )(x, g.reshape(1, D_MODEL))


def _linear_kernel(a_ref, w_ref, o_ref):
    o_ref[...] = jnp.dot(a_ref[...], w_ref[...], preferred_element_type=jnp.float32).astype(o_ref.dtype)


def _linear(a, w, out_dtype, bn, name):
    m, k = a.shape
    n = w.shape[1]
    return pl.pallas_call(
        _linear_kernel,
        grid=(n // bn, m // ROW_TILE),
        in_specs=[pl.BlockSpec((ROW_TILE, k), lambda j, i: (i, 0)),
                  pl.BlockSpec((k, bn), lambda j, i: (0, j))],
        out_specs=pl.BlockSpec((ROW_TILE, bn), lambda j, i: (i, j)),
        out_shape=jax.ShapeDtypeStruct((m, n), out_dtype),
        compiler_params=_params(("parallel", "parallel"), 40 * MIB),
        name=name,
    )(a, w)


def _rope_tables():
    half = ROT_DIM // 2
    inv = ROPE_THETA ** (-jnp.arange(half, dtype=jnp.float32) * 2.0 / ROT_DIM)
    pos_p = jnp.arange(SEQ, dtype=jnp.int32)
    pos_s = jnp.repeat(PAST_LEN + jnp.arange(DEC_SEQ, dtype=jnp.int32), DEC_BATCH)
    pos = jnp.concatenate([pos_p, pos_s]).astype(jnp.float32)
    ang = pos[:, None] * inv[None, :]
    cos, sin = jnp.cos(ang), jnp.sin(ang)
    ones = jnp.ones((pos.shape[0], HEAD_DIM - ROT_DIM), jnp.float32)
    zeros = jnp.zeros_like(ones)
    zh = jnp.zeros_like(sin)
    cos_t = jnp.concatenate([cos, cos, ones], axis=1)
    sin_hi = jnp.concatenate([-sin, zh, zeros], axis=1)
    sin_lo = jnp.concatenate([zh, sin, zeros], axis=1)
    reps = V7X_LANES // HEAD_DIM
    return tuple(jnp.tile(t, (1, reps)) for t in (cos_t, sin_hi, sin_lo))


def _rope_kernel(x_ref, c_ref, sh_ref, sl_ref, o_ref):
    x = x_ref[...]
    half = ROT_DIM // 2
    reps = QK_WIDTH // V7X_LANES
    c = jnp.tile(c_ref[...], (1, reps))
    sh = jnp.tile(sh_ref[...], (1, reps))
    sl = jnp.tile(sl_ref[...], (1, reps))
    x_up = pltpu.roll(x, QK_WIDTH - half, axis=1)
    x_dn = pltpu.roll(x, half, axis=1)
    o_ref[...] = x * c + x_up * sh + x_dn * sl


def _rope(za, tables):
    n_p = P_ROWS // ROW_TILE
    per_seq = SEQ // ROW_TILE
    tmap = lambda i: (jnp.where(i < n_p, i % per_seq, per_seq), 0)
    tspec = pl.BlockSpec((ROW_TILE, V7X_LANES), tmap)
    return pl.pallas_call(
        _rope_kernel,
        grid=(ROWS // ROW_TILE,),
        in_specs=[pl.BlockSpec((ROW_TILE, QK_WIDTH), lambda i: (i, OFF_Q // QK_WIDTH)), tspec, tspec, tspec],
        out_specs=pl.BlockSpec((ROW_TILE, QK_WIDTH), lambda i: (i, 0)),
        out_shape=jax.ShapeDtypeStruct((ROWS, QK_WIDTH), jnp.float32),
        compiler_params=_params(("parallel",), 40 * MIB),
        name="rope",
    )(za, *tables)


def _conv_prompt_kernel(h_ref, b_ref, c_ref, w_ref, y_ref, st_ref):
    u = c_ref[...] * h_ref[...]
    w = w_ref[...]
    row = lax.broadcasted_iota(jnp.int32, u.shape, 0)
    u1 = jnp.where(row >= 1, pltpu.roll(u, 1, axis=0), 0.0)
    u2 = jnp.where(row >= 2, pltpu.roll(u, 2, axis=0), 0.0)
    y = w[0:1] * u2 + w[1:2] * u1 + w[2:3] * u
    y_ref[...] = (b_ref[...] * y).astype(y_ref.dtype)
    st_ref[...] = u[SEQ - (CONV_K - 1):, :]


def _conv_prompt(za, w):
    col = lambda c: pl.BlockSpec((SEQ, CONV_WIDTH), lambda b, c=c: (b, c))
    return pl.pallas_call(
        _conv_prompt_kernel,
        grid=(BATCH,),
        in_specs=[col(0), col(1), col(2), pl.BlockSpec((CONV_K, CONV_WIDTH), lambda b: (0, 0))],
        out_specs=[pl.BlockSpec((SEQ, CONV_WIDTH), lambda b: (b, 0)),
                   pl.BlockSpec((None, CONV_K - 1, CONV_WIDTH), lambda b: (b, 0, 0))],
        out_shape=[jax.ShapeDtypeStruct((P_ROWS, CONV_WIDTH), jnp.bfloat16),
                   jax.ShapeDtypeStruct((BATCH, CONV_K - 1, CONV_WIDTH), jnp.float32)],
        compiler_params=_params(("parallel",), 48 * MIB),
        name="conv_prompt",
    )(za, za, za, w)


def _conv_sample_kernel(h_ref, b_ref, c_ref, w_ref, s_ref, y_ref, st_ref):
    u = c_ref[...] * h_ref[...]
    w = w_ref[...]
    s0, s1 = s_ref[0], s_ref[1]
    keep = S_ROWS - DEC_BATCH
    u1 = jnp.concatenate([s1, u[:keep]], axis=0)
    u2 = jnp.concatenate([s0, s1, u[:keep - DEC_BATCH]], axis=0)
    y = w[0:1] * u2 + w[1:2] * u1 + w[2:3] * u
    y_ref[...] = (b_ref[...] * y).astype(y_ref.dtype)
    st_ref[0] = u[S_ROWS - 2 * DEC_BATCH:S_ROWS - DEC_BATCH]
    st_ref[1] = u[S_ROWS - DEC_BATCH:]


def _conv_sample(za, w, state):
    blk = P_ROWS // S_ROWS
    col = lambda c: pl.BlockSpec((S_ROWS, CONV_WIDTH), lambda i, c=c: (blk, c))
    st_spec = pl.BlockSpec((CONV_K - 1, DEC_BATCH, CONV_WIDTH), lambda i: (0, 0, 0))
    return pl.pallas_call(
        _conv_sample_kernel,
        grid=(1,),
        in_specs=[col(0), col(1), col(2), pl.BlockSpec((CONV_K, CONV_WIDTH), lambda i: (0, 0)), st_spec],
        out_specs=[pl.BlockSpec((S_ROWS, CONV_WIDTH), lambda i: (0, 0)), st_spec],
        out_shape=[jax.ShapeDtypeStruct((S_ROWS, CONV_WIDTH), jnp.bfloat16),
                   jax.ShapeDtypeStruct((CONV_K - 1, DEC_BATCH, CONV_WIDTH), jnp.float32)],
        compiler_params=_params(("arbitrary",), 24 * MIB),
        name="conv_sample",
    )(za, za, za, w, state)


def _attn_prompt_kernel(*refs, dil, has_prev):
    n_in = (5 if has_prev else 3) * HALVES
    q_refs, kc_refs, vc_refs = (refs[i * HALVES:(i + 1) * HALVES] for i in range(3))
    if has_prev:
        kp_refs, vp_refs = (refs[i * HALVES:(i + 1) * HALVES] for i in range(3, 5))
    o_ref, l_ref = refs[n_in:]
    n_keys = (2 if has_prev else 1) * BAND
    qi = lax.broadcasted_iota(jnp.int32, (BAND, n_keys), 0)
    kj = lax.broadcasted_iota(jnp.int32, (BAND, n_keys), 1)
    if has_prev:
        dist = qi + BAND - kj
        valid = (dist >= 0) & (dist <= BAND) & ((kj >= BAND) | (pl.program_id(1) > 0))
    else:
        valid = kj <= qi
    heads_per_half = V7X_LANES // HEAD_DIM

    def residue(r, carry):
        rows = pl.ds(r, BAND, stride=dil) if dil > 1 else slice(None)
        for half in range(HALVES):
            q = q_refs[half][rows, :].astype(jnp.bfloat16)
            k = kc_refs[half][rows, :]
            v = vc_refs[half][rows, :]
            if has_prev:
                k = jnp.concatenate([kp_refs[half][rows, :], k], axis=0)
                v = jnp.concatenate([vp_refs[half][rows, :], v], axis=0)
            k = k.astype(jnp.bfloat16)
            v = v.astype(jnp.bfloat16)
            outs, lses = [], []
            for h in range(heads_per_half):
                sl = slice(h * HEAD_DIM, (h + 1) * HEAD_DIM)
                s = lax.dot_general(q[:, sl], k[:, sl], (((1,), (1,)), ((), ())),
                                    preferred_element_type=jnp.float32) * (HEAD_DIM ** -0.5)
                s = jnp.where(valid, s, -jnp.inf)
                m = jnp.max(s, axis=-1, keepdims=True)
                p = jnp.exp(s - m)
                den = jnp.sum(p, axis=-1, keepdims=True)
                outs.append(jnp.dot(p.astype(jnp.bfloat16), v[:, sl], preferred_element_type=jnp.float32) / den)
                lses.append(jnp.broadcast_to(m + jnp.log(den), (BAND, HEAD_DIM)))
            o_ref[half, rows, :] = jnp.concatenate(outs, axis=-1)
            l_ref[half, rows, :] = jnp.concatenate(lses, axis=-1)
        return carry

    if dil == 1:
        residue(0, 0)
    else:
        lax.fori_loop(0, dil, residue, 0)


def _attn_prompt(qk, za, g):
    d = ATTN_GROUPS[g][1]
    rows = BAND * d
    nb = SEQ // rows
    has_prev = nb > 1
    qcol, kcol, vcol = (off // V7X_LANES + g * HALVES for off in (0, ATTN_QKV, OFF_V))
    blk = (rows, V7X_LANES)
    cur = lambda b, j: b * nb + j
    prev = lambda b, j: b * nb + jnp.maximum(j - 1, 0)
    spec = lambda row, col: [pl.BlockSpec(blk, lambda b, j, h=h: (row(b, j), col + h)) for h in range(HALVES)]
    in_specs = spec(cur, qcol) + spec(cur, kcol) + spec(cur, vcol)
    args = [qk] * (2 * HALVES) + [za] * HALVES
    if has_prev:
        in_specs += spec(prev, kcol) + spec(prev, vcol)
        args += [qk] * HALVES + [za] * HALVES
    out_spec = pl.BlockSpec((HALVES, rows, V7X_LANES), lambda b, j: (0, cur(b, j), 0))
    shape = jax.ShapeDtypeStruct((HALVES, P_ROWS, V7X_LANES), jnp.float32)
    return pl.pallas_call(
        functools.partial(_attn_prompt_kernel, dil=d, has_prev=has_prev),
        grid=(BATCH, nb),
        in_specs=in_specs,
        out_specs=[out_spec, out_spec],
        out_shape=[shape, shape],
        compiler_params=_params(("parallel", "parallel"), 40 * MIB),
        name=f"attn_prompt_g{g}",
    )(*args)


def _attn_sample_kernel(q_ref, kn_ref, vn_ref, c_ref, o_ref, l_ref, *, dil, nseq):
    width = c_ref.shape[-1]
    scale = HEAD_DIM ** -0.5
    nt = (((1,), (1,)), ((), ()))
    t_c = lax.broadcasted_iota(jnp.int32, (DEC_SEQ, width), 0)
    w_c = lax.broadcasted_iota(jnp.int32, (DEC_SEQ, width), 1)
    ok_c = (w_c >= t_c) & (w_c % dil == t_c % dil)
    t_n = lax.broadcasted_iota(jnp.int32, (DEC_SEQ, DEC_SEQ), 0)
    s_n = lax.broadcasted_iota(jnp.int32, (DEC_SEQ, DEC_SEQ), 1)
    ok_n = (s_n <= t_n) & (s_n % dil == t_n % dil)

    def one_seq(s, carry):
        q = q_ref[s].astype(jnp.bfloat16)
        kn = kn_ref[s].astype(jnp.bfloat16)
        vn = vn_ref[s].astype(jnp.bfloat16)
        outs, lses = [], []
        for h in range(HEADS_PER_GROUP):
            sl = slice(h * HEAD_DIM, (h + 1) * HEAD_DIM)
            k_t = c_ref[s, 0, h].astype(jnp.bfloat16)
            v_t = c_ref[s, 1, h].astype(jnp.bfloat16)
            sc = jnp.dot(q[:, sl], k_t, preferred_element_type=jnp.float32) * scale
            sc = jnp.where(ok_c, sc, -jnp.inf)
            sn = lax.dot_general(q[:, sl], kn[:, sl], nt, preferred_element_type=jnp.float32) * scale
            sn = jnp.where(ok_n, sn, -jnp.inf)
            m = jnp.maximum(jnp.max(sc, axis=-1, keepdims=True), jnp.max(sn, axis=-1, keepdims=True))
            pc = jnp.exp(sc - m)
            pn = jnp.exp(sn - m)
            den = jnp.sum(pc, axis=-1, keepdims=True) + jnp.sum(pn, axis=-1, keepdims=True)
            o = (lax.dot_general(pc.astype(jnp.bfloat16), v_t, nt, preferred_element_type=jnp.float32)
                 + jnp.dot(pn.astype(jnp.bfloat16), vn[:, sl], preferred_element_type=jnp.float32))
            outs.append(o / den)
            lses.append(jnp.broadcast_to(m + jnp.log(den), (DEC_SEQ, HEAD_DIM)))
        o_ref[s] = jnp.concatenate(outs, axis=-1)
        l_ref[s] = jnp.concatenate(lses, axis=-1)
        return carry

    lax.fori_loop(0, nseq, one_seq, 0)


def _attn_sample(qk_s, v_s, cache_t, layer, g):
    d = ATTN_GROUPS[g][1]
    width = cache_t.shape[-1]
    seq_bytes = 2 * ATTN_OUT * width * 4
    nseq = max(1, min(SAMPLE_SEQ_TILE, SAMPLE_CACHE_TILE_BYTES // seq_bytes))
    blk = (nseq, DEC_SEQ, ATTN_OUT)
    koff = ATTN_QKV // ATTN_OUT
    out_spec = pl.BlockSpec(blk, lambda i: (i, 0, 0))
    shape = jax.ShapeDtypeStruct((DEC_BATCH, DEC_SEQ, ATTN_OUT), jnp.float32)
    o, l = pl.pallas_call(
        functools.partial(_attn_sample_kernel, dil=d, nseq=nseq),
        grid=(DEC_BATCH // nseq,),
        in_specs=[pl.BlockSpec(blk, lambda i: (i, 0, g)),
                  pl.BlockSpec(blk, lambda i: (i, 0, koff + g)),
                  pl.BlockSpec(blk, lambda i: (i, 0, g)),
                  pl.BlockSpec((None, nseq, 2, HEADS_PER_GROUP, HEAD_DIM, width),
                               lambda i: (layer, i, 0, 0, 0, 0))],
        out_specs=[out_spec, out_spec],
        out_shape=[shape, shape],
        compiler_params=_params(("parallel",), 4 * SAMPLE_CACHE_TILE_BYTES),
        name=f"attn_sample_g{g}",
    )(qk_s, qk_s, v_s, cache_t)
    to_rows = lambda a: jnp.transpose(a, (1, 0, 2)).reshape(S_ROWS, ATTN_OUT)
    return to_rows(o), to_rows(l)


def _ssm_tables(lam_re, lam_im, log_dt, b_re, b_im, c_re, c_im, d_skip):
    dt = jnp.exp(log_dt)[:, None]
    er = jnp.exp(lam_re * dt)
    lbr, lbi = er * jnp.cos(lam_im * dt), er * jnp.sin(lam_im * dt)
    den = lam_re * lam_re + lam_im * lam_im
    nr, ni = lbr - 1.0, lbi
    qr = (nr * lam_re + ni * lam_im) / den
    qi = (ni * lam_re - nr * lam_im) / den
    bbr = qr[:, :, None] * b_re - qi[:, :, None] * b_im
    bbi = qr[:, :, None] * b_im + qi[:, :, None] * b_re
    eye = jnp.eye(SSM_GB, dtype=jnp.float32)

    def in_block(b):
        b = b.reshape(N_GB, SSM_GB, SSM_STATE, SSM_CH_PER_GROUP)
        return jnp.einsum('ngpc,gh->ngchp', b, eye).reshape(N_GB, GB_CH, GB_ST)

    def out_block(c):
        c = c.reshape(N_GB, SSM_GB, SSM_CH_PER_GROUP, SSM_STATE)
        return jnp.einsum('ngcp,gh->ngphc', c, eye).reshape(N_GB, GB_ST, GB_CH)

    b_blk = jnp.concatenate([in_block(bbr), in_block(bbi)], axis=2).astype(jnp.bfloat16)
    c_blk = jnp.concatenate([out_block(c_re), out_block(-c_im)], axis=1).astype(jnp.bfloat16)

    def pack(re, im):
        re = re.reshape(N_GB, 1, GB_ST)
        im = im.reshape(N_GB, 1, GB_ST)
        return jnp.concatenate([re, re], axis=2), jnp.concatenate([-im, im], axis=2)

    pr, pi = lbr, lbi
    for _ in range(int(math.log2(SSM_CHUNK))):
        pr, pi = pr * pr - pi * pi, 2.0 * pr * pi
    return dict(b=b_blk, c=c_blk, lam=pack(lbr, lbi), lam_chunk=pack(pr, pi), d=d_skip.reshape(1, SSM_WIDTH))


def _cstep(s, a, bsw):
    swapped = jnp.concatenate([s[:, GB_ST:], s[:, :GB_ST]], axis=1)
    return a * s + bsw * swapped


def _ssm_kernel(u_ref, s0_ref, b_ref, c_ref, a_ref, bsw_ref, d_ref, *refs, steps, emit_y):
    if emit_y:
        y_ref, sout_ref, s_ref = refs
    else:
        sout_ref, s_ref = refs
    t = pl.program_id(1)

    @pl.when(t == 0)
    def _():
        s_ref[...] = s0_ref[...]

    u = u_ref[...]
    s = _cstep(s_ref[...], a_ref[...], bsw_ref[...]) + _bdot(u, b_ref[...])
    s_ref[...] = s
    if emit_y:
        y_ref[...] = _bdot(s, c_ref[...]) + d_ref[...] * u

    @pl.when(t == steps - 1)
    def _():
        sout_ref[...] = s


def _ssm_pass(u_arr, u_map, y_map, rows, steps, s0, tabs, y_shape, name):
    emit_y = y_shape is not None
    st_spec = pl.BlockSpec((rows, 2 * GB_ST), lambda gb, t: (0, gb))
    coef = pl.BlockSpec((None, 1, 2 * GB_ST), lambda gb, t: (gb, 0, 0))
    in_specs = [pl.BlockSpec((rows, GB_CH), u_map), st_spec,
                pl.BlockSpec((None, GB_CH, 2 * GB_ST), lambda gb, t: (gb, 0, 0)),
                pl.BlockSpec((None, 2 * GB_ST, GB_CH), lambda gb, t: (gb, 0, 0)),
                coef, coef,
                pl.BlockSpec((1, GB_CH), lambda gb, t: (0, gb))]
    st_shape = jax.ShapeDtypeStruct((rows, ST_WIDTH), jnp.float32)
    out_specs, out_shape = [st_spec], [st_shape]
    if emit_y:
        out_specs, out_shape = [pl.BlockSpec((rows, GB_CH), y_map), st_spec], [y_shape, st_shape]
    return pl.pallas_call(
        functools.partial(_ssm_kernel, steps=steps, emit_y=emit_y),
        grid=(N_GB, steps),
        in_specs=in_specs,
        out_specs=out_specs,
        out_shape=out_shape,
        scratch_shapes=[pltpu.VMEM((rows, 2 * GB_ST), jnp.float32)],
        compiler_params=_params(("parallel", "arbitrary"), 48 * MIB),
        name=name,
    )(u_arr, s0, tabs['b'], tabs['c'], tabs['lam'][0], tabs['lam'][1], tabs['d'])


def _ssm_prompt_kernel(*refs, emit_y):
    u_refs = refs[:HALVES]
    s0_ref, b_ref, c_ref, a_ref, bsw_ref, d_ref = refs[HALVES:HALVES + 6]
    if emit_y:
        y_ref, sout_ref, s_ref = refs[HALVES + 6:]
    else:
        sout_ref, s_ref = refs[HALVES + 6:]
    t = pl.program_id(2)
    rows = pl.ds(t, s_ref.shape[0], stride=SSM_CHUNK)

    @pl.when(t == 0)
    def _():
        s_ref[...] = s0_ref[...]

    u = jnp.concatenate([r[rows, :] for r in u_refs], axis=-1)
    s = _cstep(s_ref[...], a_ref[...], bsw_ref[...]) + _bdot(u, b_ref[...])
    s_ref[...] = s
    if emit_y:
        y = _bdot(s, c_ref[...]) + d_ref[...] * u
        for half in range(HALVES):
            y_ref[half, rows, :] = y[:, half * V7X_LANES:(half + 1) * V7X_LANES]

    @pl.when(t == SSM_CHUNK - 1)
    def _():
        sout_ref[...] = s


def _ssm_prompt_pass(za, s0, tabs, emit_y, name):
    n_rows = P_ROWS // SSM_CHUNK // SSM_BATCH_SPLIT
    nat_rows = P_ROWS // SSM_BATCH_SPLIT
    ucol = OFF_U // V7X_LANES
    u_specs = [pl.BlockSpec((nat_rows, V7X_LANES), lambda gb, p, t, h=h: (p, ucol + gb * HALVES + h))
               for h in range(HALVES)]
    st_spec = pl.BlockSpec((n_rows, 2 * GB_ST), lambda gb, p, t: (p, gb))
    coef = pl.BlockSpec((None, 1, 2 * GB_ST), lambda gb, p, t: (gb, 0, 0))
    in_specs = u_specs + [st_spec,
                          pl.BlockSpec((None, GB_CH, 2 * GB_ST), lambda gb, p, t: (gb, 0, 0)),
                          pl.BlockSpec((None, 2 * GB_ST, GB_CH), lambda gb, p, t: (gb, 0, 0)),
                          coef, coef,
                          pl.BlockSpec((1, GB_CH), lambda gb, p, t: (0, gb))]
    st_shape = jax.ShapeDtypeStruct((P_ROWS // SSM_CHUNK, ST_WIDTH), jnp.float32)
    out_specs, out_shape = [st_spec], [st_shape]
    if emit_y:
        out_specs = [pl.BlockSpec((HALVES, nat_rows, V7X_LANES), lambda gb, p, t: (gb, p, 0)), st_spec]
        out_shape = [jax.ShapeDtypeStruct((N_GB * HALVES, P_ROWS, V7X_LANES), jnp.float32), st_shape]
    return pl.pallas_call(
        functools.partial(_ssm_prompt_kernel, emit_y=emit_y),
        grid=(N_GB, SSM_BATCH_SPLIT, SSM_CHUNK),
        in_specs=in_specs,
        out_specs=out_specs,
        out_shape=out_shape,
        scratch_shapes=[pltpu.VMEM((n_rows, 2 * GB_ST), jnp.float32)],
        compiler_params=_params(("parallel", "parallel", "arbitrary"), 48 * MIB),
        name=name,
    )(*([za] * HALVES), s0, tabs['b'], tabs['c'], tabs['lam'][0], tabs['lam'][1], tabs['d'])


def _ssm_carry_kernel(e_ref, a_ref, bsw_ref, sin_ref, fin_ref):
    a = a_ref[...]
    bsw = bsw_ref[...]
    n_chunks = SEQ // SSM_CHUNK

    def body(n, states):
        new = []
        for b in range(BATCH):
            row = b * n_chunks + n
            sin_ref[pl.ds(row, 1), :] = states[b]
            new.append(_cstep(states[b], a, bsw) + e_ref[pl.ds(row, 1), :])
        return tuple(new)

    zero = jnp.zeros((1, 2 * GB_ST), jnp.float32)
    final = lax.fori_loop(0, n_chunks, body, (zero,) * BATCH)
    for b in range(BATCH):
        fin_ref[pl.ds(b, 1), :] = final[b]


def _ssm_carry(e, tabs):
    n_rows = P_ROWS // SSM_CHUNK
    st_spec = lambda rows: pl.BlockSpec((rows, 2 * GB_ST), lambda gb: (0, gb))
    coef = pl.BlockSpec((None, 1, 2 * GB_ST), lambda gb: (gb, 0, 0))
    return pl.pallas_call(
        _ssm_carry_kernel,
        grid=(N_GB,),
        in_specs=[st_spec(n_rows), coef, coef],
        out_specs=[st_spec(n_rows), st_spec(BATCH)],
        out_shape=[jax.ShapeDtypeStruct((n_rows, ST_WIDTH), jnp.float32),
                   jax.ShapeDtypeStruct((BATCH, ST_WIDTH), jnp.float32)],
        compiler_params=_params(("parallel",), 32 * MIB),
        name="ssm_carry",
    )(e, tabs['lam_chunk'][0], tabs['lam_chunk'][1])


def _pack_state(s):
    b = s.shape[0]
    s = s.reshape(b, N_GB, SSM_GB * SSM_STATE, 2)
    return jnp.transpose(s, (0, 1, 3, 2)).reshape(b, ST_WIDTH)


def _unpack_state(s):
    b = s.shape[0]
    s = s.reshape(b, N_GB, 2, SSM_GB * SSM_STATE)
    return jnp.transpose(s, (0, 1, 3, 2)).reshape(b, SSM_GROUPS, SSM_STATE, 2)


def _ssm(za, state_s, tabs):
    ucol = OFF_U // GB_CH
    zero = jnp.zeros((P_ROWS // SSM_CHUNK, ST_WIDTH), jnp.float32)
    (e,) = _ssm_prompt_pass(za, zero, tabs, False, "ssm_prompt_local")
    s_in, s_fin = _ssm_carry(e, tabs)
    y_p, _ = _ssm_prompt_pass(za, s_in, tabs, True, "ssm_prompt")
    tblk = P_ROWS // DEC_BATCH
    ys_shape = jax.ShapeDtypeStruct((S_ROWS, SSM_WIDTH), jnp.float32)
    y_s, s_new = _ssm_pass(za, lambda gb, t: (tblk + t, ucol + gb), lambda gb, t: (t, gb), DEC_BATCH, DEC_SEQ,
                           _pack_state(state_s), tabs, ys_shape, "ssm_sample")
    return y_p, y_s, _unpack_state(s_fin), _unpack_state(s_new)


def _mix_kernel(x_ref, gate_ref, gb_ref, cp_ref, cs_ref, ap_refs, as_refs, sp_ref, ss_ref,
                wglu_ref, bglu_ref, pc_ref, pa_ref, ps_ref, wo_ref, gn_ref, xo_ref, ho_ref):
    is_p = pl.program_id(0) < P_ROWS // MIX_TILE

    def pick(p_ref, s_ref):
        p = p_ref[...]
        if p.ndim == 3:
            p = jnp.concatenate([p[i] for i in range(p.shape[0])], axis=-1)
        return jnp.where(is_p, p, s_ref[...])

    y_conv = pick(cp_ref, cs_ref)
    o = [pick(ap_refs[2 * g], as_refs[2 * g]) for g in range(N_ATTN_GROUPS)]
    lse = [pick(ap_refs[2 * g + 1], as_refs[2 * g + 1]) for g in range(N_ATTN_GROUPS)]
    m = jnp.maximum(jnp.maximum(lse[0], lse[1]), lse[2])
    w = [jnp.exp(l - m) for l in lse]
    y_attn = (w[0] * o[0] + w[1] * o[1] + w[2] * o[2]) / (w[0] + w[1] + w[2])
    zg = jax.nn.gelu(pick(sp_ref, ss_ref))
    y_ssm = zg * jax.nn.sigmoid(_bdot(zg, wglu_ref[...]) + bglu_ref[...])
    gt = jax.nn.sigmoid(gate_ref[...].astype(jnp.float32) + gb_ref[...])
    merged = (gt[:, :D_MODEL] * _bdot(y_conv, pc_ref[...])
              + gt[:, D_MODEL:2 * D_MODEL] * _bdot(y_attn, pa_ref[...])
              + gt[:, 2 * D_MODEL:] * _bdot(y_ssm, ps_ref[...]))
    x = x_ref[...] + _bdot(merged, wo_ref[...])
    xo_ref[...] = x
    ho_ref[...] = _rms(x, gn_ref[...]).astype(ho_ref.dtype)


def _mix(x, gates, conv_p, conv_s, attn_p, attn_s, ssm_p, ssm_s, lp, h_dtype):
    n_p = P_ROWS // MIX_TILE
    p_map = lambda i: (jnp.minimum(i, n_p - 1), 0)
    s_map = lambda i: (jnp.maximum(i - n_p, 0), 0)
    row_map = lambda i: (i, 0)
    const = lambda i: (0, 0)
    rows = lambda width, imap: pl.BlockSpec((MIX_TILE, width), imap)
    slabs = lambda n: pl.BlockSpec((n, MIX_TILE, V7X_LANES), lambda i: (0, jnp.minimum(i, n_p - 1), 0))
    full = lambda a: pl.BlockSpec(a.shape, const, pipeline_mode=pl.Buffered(1))
    n_attn = 2 * N_ATTN_GROUPS
    weights = [lp['w_glu'], lp['b_glu'], lp['proj_conv'], lp['proj_attn'], lp['proj_ssm'], lp['w_out'],
               lp['norm_next']]
    in_specs = ([rows(D_MODEL, row_map), rows(GATE_WIDTH, row_map), full(lp['gate_bias']),
                 rows(CONV_WIDTH, p_map), rows(CONV_WIDTH, s_map)]
                + [slabs(HALVES)] * n_attn + [rows(ATTN_OUT, s_map)] * n_attn
                + [slabs(N_GB * HALVES), rows(SSM_WIDTH, s_map)]
                + [full(w) for w in weights])

    def body(*refs):
        x_ref, gate_ref, gb_ref, cp_ref, cs_ref = refs[:5]
        ap_refs = refs[5:5 + n_attn]
        as_refs = refs[5 + n_attn:5 + 2 * n_attn]
        rest = refs[5 + 2 * n_attn:]
        _mix_kernel(x_ref, gate_ref, gb_ref, cp_ref, cs_ref, ap_refs, as_refs, *rest)

    return pl.pallas_call(
        body,
        grid=(ROWS // MIX_TILE,),
        in_specs=in_specs,
        out_specs=[rows(D_MODEL, row_map), rows(D_MODEL, row_map)],
        out_shape=[jax.ShapeDtypeStruct((ROWS, D_MODEL), jnp.float32),
                   jax.ShapeDtypeStruct((ROWS, D_MODEL), h_dtype)],
        compiler_params=_params(("parallel",), 56 * MIB),
        name="mix",
    )(x, gates, lp['gate_bias'], conv_p, conv_s, *attn_p, *attn_s, ssm_p, ssm_s, *weights)


def _ffn_up_kernel(be_ref, x_ref, w1_ref, w3_ref, h_ref):
    x = x_ref[...].astype(jnp.bfloat16)
    a = jnp.dot(x, w1_ref[...], preferred_element_type=jnp.float32)
    b = jnp.dot(x, w3_ref[...], preferred_element_type=jnp.float32)
    h_ref[...] = (jax.nn.silu(a) * b).astype(h_ref.dtype)


def _ffn_up(block_e, x, w1, w3, bm, bf, name):
    m = x.shape[0]
    f = w1.shape[2]
    wspec = pl.BlockSpec((None, D_MODEL, bf), lambda j, i, be: (be[i], 0, j))
    grid_spec = pltpu.PrefetchScalarGridSpec(
        num_scalar_prefetch=1,
        grid=(f // bf, m // bm),
        in_specs=[pl.BlockSpec((bm, D_MODEL), lambda j, i, be: (i, 0)), wspec, wspec],
        out_specs=pl.BlockSpec((bm, bf), lambda j, i, be: (i, j)),
    )
    return pl.pallas_call(
        _ffn_up_kernel,
        grid_spec=grid_spec,
        out_shape=jax.ShapeDtypeStruct((m, f), jnp.bfloat16),
        compiler_params=_params(("parallel", "arbitrary"), 48 * MIB),
        name=name,
    )(block_e, x, w1, w3)


def _ffn_down_dense_kernel(h_ref, w_ref, x_ref, g_ref, xo_ref, ho_ref, acc_ref, *, n_k):
    k = pl.program_id(1)

    @pl.when(k == 0)
    def _():
        acc_ref[...] = x_ref[...]

    acc_ref[...] += jnp.dot(h_ref[...], w_ref[...], preferred_element_type=jnp.float32)

    @pl.when(k == n_k - 1)
    def _():
        x = acc_ref[...]
        xo_ref[...] = x
        ho_ref[...] = _rms(x, g_ref[...]).astype(ho_ref.dtype)


def _ffn_down_dense(h, w2, x, g_next, bk):
    n_k = D_FF // bk
    row = pl.BlockSpec((ROW_TILE, D_MODEL), lambda i, k: (i, 0))
    return pl.pallas_call(
        functools.partial(_ffn_down_dense_kernel, n_k=n_k),
        grid=(ROWS // ROW_TILE, n_k),
        in_specs=[pl.BlockSpec((ROW_TILE, bk), lambda i, k: (i, k)),
                  pl.BlockSpec((bk, D_MODEL), lambda i, k: (k, 0)),
                  row, pl.BlockSpec((1, D_MODEL), lambda i, k: (0, 0))],
        out_specs=[row, row],
        out_shape=[jax.ShapeDtypeStruct((ROWS, D_MODEL), jnp.float32),
                   jax.ShapeDtypeStruct((ROWS, D_MODEL), jnp.bfloat16)],
        scratch_shapes=[pltpu.VMEM((ROW_TILE, D_MODEL), jnp.float32)],
        compiler_params=_params(("parallel", "arbitrary"), 48 * MIB),
        name="ffn_down",
    )(h, w2, x, g_next.reshape(1, D_MODEL))


def _moe_down_kernel(be_ref, h_ref, w_ref, g_ref, y_ref):
    y = jnp.dot(h_ref[...], w_ref[...], preferred_element_type=jnp.float32)
    y_ref[...] = y * g_ref[...]


def _moe_down(block_e, h, w2, row_gate):
    grid_spec = pltpu.PrefetchScalarGridSpec(
        num_scalar_prefetch=1,
        grid=(MOE_BLOCKS,),
        in_specs=[pl.BlockSpec((MOE_BLOCK, D_FF_EXPERT), lambda i, be: (i, 0)),
                  pl.BlockSpec((None, D_FF_EXPERT, D_MODEL), lambda i, be: (be[i], 0, 0)),
                  pl.BlockSpec((MOE_BLOCK, 1), lambda i, be: (i, 0))],
        out_specs=pl.BlockSpec((MOE_BLOCK, D_MODEL), lambda i, be: (i, 0)),
    )
    return pl.pallas_call(
        _moe_down_kernel,
        grid_spec=grid_spec,
        out_shape=jax.ShapeDtypeStruct((MOE_ROWS, D_MODEL), jnp.float32),
        compiler_params=_params(("arbitrary",), 48 * MIB),
        name="moe_down",
    )(block_e, h, w2, row_gate)


def _router_kernel(x_ref, g_ref, w_ref, hs_ref, r_ref):
    hs = _rms(x_ref[...], g_ref[...])
    hs_ref[...] = hs
    logits = jnp.dot(hs, w_ref[...], preferred_element_type=jnp.float32, precision=lax.Precision.HIGHEST)
    lane = lax.broadcasted_iota(jnp.int32, logits.shape, 1)
    logits = jnp.where(lane < N_EXPERTS, logits, -jnp.inf)
    m1 = jnp.max(logits, axis=-1, keepdims=True)
    i1 = jnp.min(jnp.where(logits == m1, lane, ROUTER_PAD), axis=-1, keepdims=True)
    rest = jnp.where(lane == i1, -jnp.inf, logits)
    m2 = jnp.max(rest, axis=-1, keepdims=True)
    i2 = jnp.min(jnp.where(rest == m2, lane, ROUTER_PAD), axis=-1, keepdims=True)
    e2 = jnp.exp(m2 - m1)
    g1 = 1.0 / (1.0 + e2)
    g2 = e2 / (1.0 + e2)
    out = jnp.where(lane == 0, i1.astype(jnp.float32),
                    jnp.where(lane == 1, i2.astype(jnp.float32),
                              jnp.where(lane == 2, g1, jnp.where(lane == 3, g2, 0.0))))
    r_ref[...] = out


def _router(x, g, w_router):
    w = jnp.zeros((D_MODEL, ROUTER_PAD), jnp.float32).at[:, :N_EXPERTS].set(w_router)
    row = pl.BlockSpec((ROW_TILE, D_MODEL), lambda i: (i, 0))
    return pl.pallas_call(
        _router_kernel,
        grid=(ROWS // ROW_TILE,),
        in_specs=[row, pl.BlockSpec((1, D_MODEL), lambda i: (0, 0)),
                  pl.BlockSpec((D_MODEL, ROUTER_PAD), lambda i: (0, 0))],
        out_specs=[row, pl.BlockSpec((ROW_TILE, ROUTER_PAD), lambda i: (i, 0))],
        out_shape=[jax.ShapeDtypeStruct((ROWS, D_MODEL), jnp.float32),
                   jax.ShapeDtypeStruct((ROWS, ROUTER_PAD), jnp.float32)],
        compiler_params=_params(("parallel",), 32 * MIB),
        name="router",
    )(x, g.reshape(1, D_MODEL), w)


def _gather_step(idx_ref, src_ref, buf, sem, rows):
    i = pl.program_id(0)
    slot = i % 2

    def copy(step, slot_, r):
        return pltpu.make_async_copy(src_ref.at[pl.ds(idx_ref[step * rows + r], 1), :],
                                     buf.at[slot_, pl.ds(r, 1), :], sem.at[slot_])

    def start_all(step, slot_):
        def body(r, c):
            copy(step, slot_, r).start()
            return c
        lax.fori_loop(0, rows, body, 0, unroll=8)

    @pl.when(i == 0)
    def _():
        start_all(0, 0)

    @pl.when(i + 1 < pl.num_programs(0))
    def _():
        start_all(i + 1, 1 - slot)

    def wait_body(r, c):
        copy(i, slot, r).wait()
        return c
    lax.fori_loop(0, rows, wait_body, 0, unroll=8)
    return slot


def _dispatch_kernel(tok_ref, hs_ref, o_ref, buf, sem):
    slot = _gather_step(tok_ref, hs_ref, buf, sem, MOE_BLOCK)
    o_ref[...] = buf[slot].astype(o_ref.dtype)


def _dispatch(row_tok, hs):
    grid_spec = pltpu.PrefetchScalarGridSpec(
        num_scalar_prefetch=1,
        grid=(MOE_BLOCKS,),
        in_specs=[pl.BlockSpec(memory_space=pl.ANY)],
        out_specs=pl.BlockSpec((MOE_BLOCK, D_MODEL), lambda i, tok: (i, 0)),
        scratch_shapes=[pltpu.VMEM((2, MOE_BLOCK, D_MODEL), jnp.float32), pltpu.SemaphoreType.DMA((2,))],
    )
    return pl.pallas_call(
        _dispatch_kernel,
        grid_spec=grid_spec,
        out_shape=jax.ShapeDtypeStruct((MOE_ROWS, D_MODEL), jnp.bfloat16),
        compiler_params=_params(("arbitrary",), 16 * MIB),
        name="moe_dispatch",
    )(row_tok, hs)


def _combine_kernel(pos_ref, x_ref, y_ref, g_ref, o_ref, buf, sem):
    slot = _gather_step(pos_ref, y_ref, buf, sem, TOP_K * MIX_TILE)
    y = buf[slot]
    o_ref[...] = _rms(x_ref[...] + y[:MIX_TILE] + y[MIX_TILE:], g_ref[...])


def _combine(pos, x, yb, g_final):
    rows = TOP_K * MIX_TILE
    row = pl.BlockSpec((MIX_TILE, D_MODEL), lambda i, pos: (i, 0))
    grid_spec = pltpu.PrefetchScalarGridSpec(
        num_scalar_prefetch=1,
        grid=(ROWS // MIX_TILE,),
        in_specs=[row, pl.BlockSpec(memory_space=pl.ANY), pl.BlockSpec((1, D_MODEL), lambda i, pos: (0, 0))],
        out_specs=row,
        scratch_shapes=[pltpu.VMEM((2, rows, D_MODEL), jnp.float32), pltpu.SemaphoreType.DMA((2,))],
    )
    return pl.pallas_call(
        _combine_kernel,
        grid_spec=grid_spec,
        out_shape=jax.ShapeDtypeStruct((ROWS, D_MODEL), jnp.float32),
        compiler_params=_params(("arbitrary",), 32 * MIB),
        name="moe_combine",
    )(pos, x, yb, g_final.reshape(1, D_MODEL))


def _dispatch_plan(route):
    top_idx = route[:, 0:TOP_K].astype(jnp.int32)
    gates = route[:, TOP_K:2 * TOP_K]
    flat_e = top_idx.reshape(N_ASSIGN)
    flat_tok = jnp.repeat(jnp.arange(ROWS, dtype=jnp.int32), TOP_K)
    order = jnp.argsort(flat_e)
    se = flat_e[order]
    counts = jnp.bincount(flat_e, length=N_EXPERTS)
    padded = (counts + MOE_BLOCK - 1) // MOE_BLOCK * MOE_BLOCK
    start = jnp.cumsum(counts) - counts
    ends_p = jnp.cumsum(padded)
    dest = (ends_p - padded)[se] + jnp.arange(N_ASSIGN, dtype=jnp.int32) - start[se]
    row_tok = jnp.zeros((MOE_ROWS,), jnp.int32).at[dest].set(flat_tok[order])
    row_gate = jnp.zeros((MOE_ROWS,), jnp.float32).at[dest].set(gates.reshape(N_ASSIGN)[order])
    block_e = jnp.minimum(jnp.searchsorted(ends_p, jnp.arange(MOE_BLOCKS) * MOE_BLOCK, side='right'),
                          N_EXPERTS - 1).astype(jnp.int32)
    pos = jnp.zeros((N_ASSIGN,), jnp.int32).at[order].set(dest.astype(jnp.int32))
    pos = jnp.transpose(pos.reshape(ROWS // MIX_TILE, MIX_TILE, TOP_K), (0, 2, 1)).reshape(N_ASSIGN)
    return row_tok, row_gate.reshape(MOE_ROWS, 1), block_e, pos


def kernel(x_prompt, x_sample, cache_kv_w128, cache_kv_w512, cache_kv_w2048, state_conv, state_ssm, norm_mix, w_in, gate_bias, conv_w, ssm_lam_re, ssm_lam_im, ssm_log_dt, ssm_b_re, ssm_b_im, ssm_c_re, ssm_c_im, ssm_d, ssm_w_glu, ssm_b_glu, proj_conv, proj_attn, proj_ssm, w_out, norm_ffn, ffn_w1, ffn_w3, ffn_w2, router_w, moe_w1, moe_w3, moe_w2, norm_final):
    bf = jnp.bfloat16
    caches = tuple(jnp.transpose(c, (0, 1, 3, 4, 5, 2)) for c in (cache_kv_w128, cache_kv_w512, cache_kv_w2048))
    x = jnp.concatenate([x_prompt.reshape(P_ROWS, D_MODEL),
                         jnp.transpose(x_sample, (1, 0, 2)).reshape(S_ROWS, D_MODEL)], axis=0)
    tables = _rope_tables()
    h = _norm(x, norm_mix[0])
    kv_p = [[] for _ in range(N_ATTN_GROUPS)]
    kv_s = [[] for _ in range(N_ATTN_GROUPS)]
    conv_p_all, conv_s_all, ssm_p_all, ssm_s_all = [], [], [], []
    y = None
    for l in range(DEPTH):
        w_in_l = w_in[l].astype(bf)
        za = _linear(h, w_in_l[:, :ZA_WIDTH], jnp.float32, ZA_WIDTH // 3, f"in_proj_{l}")
        gates = _linear(h, w_in_l[:, ZA_WIDTH:], bf, GATE_WIDTH // 4, f"gate_proj_{l}")
        qk = _rope(za, tables)
        conv_p, cst_p = _conv_prompt(za, conv_w[l])
        conv_s, cst_s = _conv_sample(za, conv_w[l], jnp.transpose(state_conv[l], (1, 0, 2)))
        by_seq = lambda a: jnp.transpose(a.reshape(DEC_SEQ, DEC_BATCH, a.shape[-1]), (1, 0, 2))
        qk_s = by_seq(qk[P_ROWS:])
        v_s = by_seq(za[P_ROWS:, OFF_V:OFF_V + ATTN_QKV])
        attn_p, attn_s = [], []
        for g in range(N_ATTN_GROUPS):
            attn_p += list(_attn_prompt(qk, za, g))
            attn_s += list(_attn_sample(qk_s, v_s, caches[g], l, g))
        tabs = _ssm_tables(ssm_lam_re[l], ssm_lam_im[l], ssm_log_dt[l], ssm_b_re[l], ssm_b_im[l],
                           ssm_c_re[l], ssm_c_im[l], ssm_d[l])
        ssm_p, ssm_s, sst_p, sst_s = _ssm(za, state_ssm[l], tabs)
        moe_layer = l % 2 == 1
        lp = dict(gate_bias=gate_bias[l].reshape(1, GATE_WIDTH), w_glu=ssm_w_glu[l].astype(bf),
                  b_glu=ssm_b_glu[l].reshape(1, SSM_WIDTH), proj_conv=proj_conv[l].astype(bf),
                  proj_attn=proj_attn[l].astype(bf), proj_ssm=proj_ssm[l].astype(bf),
                  w_out=w_out[l].astype(bf), norm_next=norm_ffn[l].reshape(1, D_MODEL))
        x, h = _mix(x, gates, conv_p, conv_s, attn_p, attn_s, ssm_p, ssm_s, lp, bf)
        i = l // 2
        if not moe_layer:
            zero_e = jnp.zeros((ROWS // ROW_TILE,), jnp.int32)
            hid = _ffn_up(zero_e, h, ffn_w1[i].astype(bf)[None], ffn_w3[i].astype(bf)[None],
                          ROW_TILE, 512, "ffn_up")
            x, h = _ffn_down_dense(hid, ffn_w2[i].astype(bf), x, norm_mix[l + 1], D_FF // 4)
        else:
            hs, route = _router(x, norm_ffn[l], router_w[i])
            row_tok, row_gate, block_e, pos = _dispatch_plan(route)
            xs = _dispatch(row_tok, hs)
            hid = _ffn_up(block_e, xs, moe_w1[i].astype(bf), moe_w3[i].astype(bf),
                          MOE_BLOCK, D_FF_EXPERT // 2, "moe_up")
            yb = _moe_down(block_e, hid, moe_w2[i].astype(bf), row_gate)
            y = _combine(pos, x, yb, norm_final)
        k_off = ATTN_QKV
        for g in range(N_ATTN_GROUPS):
            keep = min(ATTN_GROUPS[g][0], SEQ)
            cols = slice(g * ATTN_OUT, (g + 1) * ATTN_OUT)
            kp = qk[:P_ROWS, k_off:][:, cols].reshape(BATCH, SEQ, HEADS_PER_GROUP, HEAD_DIM)[:, SEQ - keep:]
            vp = za[:P_ROWS, OFF_V:OFF_V + ATTN_QKV][:, cols].reshape(BATCH, SEQ, HEADS_PER_GROUP, HEAD_DIM)[:, SEQ - keep:]
            kv_p[g].append(jnp.stack([kp, vp], axis=2))
            ks = qk_s[:, :, k_off:][:, :, cols].reshape(DEC_BATCH, DEC_SEQ, HEADS_PER_GROUP, HEAD_DIM)
            vs = v_s[:, :, cols].reshape(DEC_BATCH, DEC_SEQ, HEADS_PER_GROUP, HEAD_DIM)
            kv_s[g].append(jnp.stack([ks, vs], axis=2))
        conv_p_all.append(cst_p)
        conv_s_all.append(jnp.transpose(cst_s, (1, 0, 2)))
        ssm_p_all.append(sst_p)
        ssm_s_all.append(sst_s)
    y_prompt = y[:P_ROWS].reshape(BATCH, SEQ, D_MODEL)
    y_sample = jnp.transpose(y[P_ROWS:].reshape(DEC_SEQ, DEC_BATCH, D_MODEL), (1, 0, 2))
    return (y_prompt, y_sample, jnp.stack(kv_p[0]), jnp.stack(kv_p[1]), jnp.stack(kv_p[2]),
            jnp.stack(conv_p_all), jnp.stack(ssm_p_all), jnp.stack(kv_s[0]), jnp.stack(kv_s[1]),
            jnp.stack(kv_s[2]), jnp.stack(conv_s_all), jnp.stack(ssm_s_all))
```

```python
import functools
import math

import jax
import jax.numpy as jnp
import numpy as np
from jax import lax
from jax.experimental import pallas as pl
from jax.experimental.pallas import tpu as pltpu

D_MODEL = 2048
BATCH = 4
SEQ = 2048
DEPTH = 2
DEC_BATCH = 128
DEC_SEQ = 4
PAST_LEN = 2048
CONV_WIDTH = 512
CONV_K = 3
HEAD_DIM = 64
HEADS_PER_GROUP = 4
ATTN_GROUPS = ((128, 1), (512, 4), (2048, 16))
N_ATTN_GROUPS = 3
ATTN_QKV = N_ATTN_GROUPS * HEADS_PER_GROUP * HEAD_DIM
ATTN_OUT = HEADS_PER_GROUP * HEAD_DIM
ROT_DIM = HEAD_DIM // 4
ROPE_THETA = 500000.0
SSM_WIDTH = 768
SSM_CH_PER_GROUP = 16
SSM_GROUPS = SSM_WIDTH // SSM_CH_PER_GROUP
SSM_STATE = 64
N_BRANCH = 3
D_FF = 5632
N_EXPERTS = 8
TOP_K = 2
D_FF_EXPERT = D_FF // TOP_K
EPS = 1e-6

V7X_LANES = 128
V7X_VMEM_BYTES = 64 * 1024 * 1024
DMA_PRIORITIES = 2
MIB = 1024 * 1024

P_ROWS = BATCH * SEQ
S_ROWS = DEC_SEQ * DEC_BATCH
ROWS = P_ROWS + S_ROWS
ZA_WIDTH = 3 * CONV_WIDTH + 3 * ATTN_QKV + SSM_WIDTH
GATE_WIDTH = N_BRANCH * D_MODEL
QK_WIDTH = 2 * ATTN_QKV
OFF_Q = 3 * CONV_WIDTH
OFF_V = OFF_Q + 2 * ATTN_QKV
OFF_U = OFF_V + ATTN_QKV
BAND = 128
ATTN_DENSE_BLOCKS = 4
ATTN_RESIDUE_UNROLL = 4
SSM_CHUNK = 16
SSM_GB = 16
N_GB = SSM_GROUPS // SSM_GB
GB_CH = SSM_GB * SSM_CH_PER_GROUP
GB_ST = SSM_GB * SSM_STATE
ST_WIDTH = N_GB * 2 * GB_ST
ROW_TILE = 512
MIX_TILE = 256
IN_TILE = 1536
MOE_BLOCK = 256
N_ASSIGN = ROWS * TOP_K
MOE_BLOCKS = -(-(N_ASSIGN + N_EXPERTS * (MOE_BLOCK - 1)) // MOE_BLOCK)
MOE_ROWS = MOE_BLOCKS * MOE_BLOCK
ROUTER_PAD = V7X_LANES
SAMPLE_SEQ_TILE = 8
SAMPLE_CACHE_TILE_BYTES = 8 * MIB
SSM_BATCH_SPLIT = 2
HALVES = ATTN_OUT // V7X_LANES
assert DEPTH == 2 and HALVES == 2 and GB_CH == HALVES * V7X_LANES

assert ROWS % ROW_TILE == 0 and P_ROWS % ROW_TILE == 0
assert ROWS % MIX_TILE == 0 and P_ROWS % MIX_TILE == 0


def _params(semantics, vmem_bytes):
    return pltpu.CompilerParams(dimension_semantics=semantics,
                                vmem_limit_bytes=min(int(vmem_bytes), V7X_VMEM_BYTES - 4 * MIB))


def _rms(x, g):
    y = x * lax.rsqrt(jnp.mean(x * x, axis=-1, keepdims=True) + EPS)
    return y * g


def _bdot(a, b):
    return jnp.dot(a.astype(jnp.bfloat16), b.astype(jnp.bfloat16), preferred_element_type=jnp.float32)


def _sigmoid(x):
    return 0.5 * jnp.tanh(0.5 * x) + 0.5


def _norm_kernel(x_ref, g_ref, h_ref):
    h_ref[...] = _rms(x_ref[...], g_ref[...]).astype(h_ref.dtype)


def _norm(x, g):
    return pl.pallas_call(
        _norm_kernel,
        grid=(ROWS // ROW_TILE,),
        in_specs=[pl.BlockSpec((ROW_TILE, D_MODEL), lambda i: (i, 0)),
                  pl.BlockSpec((1, D_MODEL), lambda i: (0, 0))],
        out_specs=pl.BlockSpec((ROW_TILE, D_MODEL), lambda i: (i, 0)),
        out_shape=jax.ShapeDtypeStruct((ROWS, D_MODEL), jnp.bfloat16),
        compiler_params=_params(("parallel",), 24 * MIB),
        name="norm",
    )(x, g.reshape(1, D_MODEL))


def _linear_kernel(a_ref, w_ref, o_ref, wb_ref):
    @pl.when(pl.program_id(1) == 0)
    def _():
        wb_ref[...] = w_ref[...].astype(wb_ref.dtype)

    o_ref[...] = jnp.dot(a_ref[...], wb_ref[...], preferred_element_type=jnp.float32).astype(o_ref.dtype)


def _linear(a, w, layer, col0, n, out_dtype, bn, name):
    m, k = a.shape
    jb = col0 // bn
    return pl.pallas_call(
        _linear_kernel,
        grid=(n // bn, m // ROW_TILE),
        in_specs=[pl.BlockSpec((ROW_TILE, k), lambda j, i: (i, 0)),
                  pl.BlockSpec((None, k, bn), lambda j, i: (layer, 0, jb + j))],
        out_specs=pl.BlockSpec((ROW_TILE, bn), lambda j, i: (i, j)),
        out_shape=jax.ShapeDtypeStruct((m, n), out_dtype),
        scratch_shapes=[pltpu.VMEM((k, bn), jnp.bfloat16)],
        compiler_params=_params(("parallel", "arbitrary"), 56 * MIB),
        name=name,
    )(a, w)


def _rope_tables():
    half = ROT_DIM // 2
    inv = ROPE_THETA ** (-jnp.arange(half, dtype=jnp.float32) * 2.0 / ROT_DIM)
    pos_p = jnp.arange(SEQ, dtype=jnp.int32)
    pos_s = jnp.repeat(PAST_LEN + jnp.arange(DEC_SEQ, dtype=jnp.int32), DEC_BATCH)
    pos = jnp.concatenate([pos_p, pos_s]).astype(jnp.float32)
    ang = pos[:, None] * inv[None, :]
    cos, sin = jnp.cos(ang), jnp.sin(ang)
    ones = jnp.ones((pos.shape[0], HEAD_DIM - ROT_DIM), jnp.float32)
    zeros = jnp.zeros_like(ones)
    zh = jnp.zeros_like(sin)
    cos_t = jnp.concatenate([cos, cos, ones], axis=1)
    sin_hi = jnp.concatenate([-sin, zh, zeros], axis=1)
    sin_lo = jnp.concatenate([zh, sin, zeros], axis=1)
    reps = V7X_LANES // HEAD_DIM
    return tuple(jnp.tile(t, (1, reps)) for t in (cos_t, sin_hi, sin_lo))


def _rope_kernel(x_ref, c_ref, sh_ref, sl_ref, o_ref):
    x = x_ref[...]
    half = ROT_DIM // 2
    reps = QK_WIDTH // V7X_LANES
    c = jnp.tile(c_ref[...], (1, reps))
    sh = jnp.tile(sh_ref[...], (1, reps))
    sl = jnp.tile(sl_ref[...], (1, reps))
    x_up = pltpu.roll(x, QK_WIDTH - half, axis=1)
    x_dn = pltpu.roll(x, half, axis=1)
    o_ref[...] = x * c + x_up * sh + x_dn * sl


def _rope(za, tables):
    n_p = P_ROWS // ROW_TILE
    per_seq = SEQ // ROW_TILE
    tmap = lambda i: (jnp.where(i < n_p, i % per_seq, per_seq), 0)
    tspec = pl.BlockSpec((ROW_TILE, V7X_LANES), tmap)
    return pl.pallas_call(
        _rope_kernel,
        grid=(ROWS // ROW_TILE,),
        in_specs=[pl.BlockSpec((ROW_TILE, QK_WIDTH), lambda i: (i, OFF_Q // QK_WIDTH)), tspec, tspec, tspec],
        out_specs=pl.BlockSpec((ROW_TILE, QK_WIDTH), lambda i: (i, 0)),
        out_shape=jax.ShapeDtypeStruct((ROWS, QK_WIDTH), jnp.float32),
        compiler_params=_params(("parallel",), 40 * MIB),
        name="rope",
    )(za, *tables)


def _conv_prompt_kernel(h_ref, b_ref, c_ref, w_ref, y_ref, st_ref):
    u = c_ref[...] * h_ref[...]
    w = w_ref[...]
    row = lax.broadcasted_iota(jnp.int32, u.shape, 0)
    u1 = jnp.where(row >= 1, pltpu.roll(u, 1, axis=0), 0.0)
    u2 = jnp.where(row >= 2, pltpu.roll(u, 2, axis=0), 0.0)
    y = w[0:1] * u2 + w[1:2] * u1 + w[2:3] * u
    y_ref[...] = (b_ref[...] * y).astype(y_ref.dtype)
    st_ref[...] = u[SEQ - (CONV_K - 1):, :]


def _conv_prompt(za, w):
    col = lambda c: pl.BlockSpec((SEQ, CONV_WIDTH), lambda b, c=c: (b, c))
    return pl.pallas_call(
        _conv_prompt_kernel,
        grid=(BATCH,),
        in_specs=[col(0), col(1), col(2), pl.BlockSpec((CONV_K, CONV_WIDTH), lambda b: (0, 0))],
        out_specs=[pl.BlockSpec((SEQ, CONV_WIDTH), lambda b: (b, 0)),
                   pl.BlockSpec((None, CONV_K - 1, CONV_WIDTH), lambda b: (b, 0, 0))],
        out_shape=[jax.ShapeDtypeStruct((P_ROWS, CONV_WIDTH), jnp.bfloat16),
                   jax.ShapeDtypeStruct((BATCH, CONV_K - 1, CONV_WIDTH), jnp.float32)],
        compiler_params=_params(("parallel",), 48 * MIB),
        name="conv_prompt",
    )(za, za, za, w)


def _conv_sample_kernel(h_ref, b_ref, c_ref, w_ref, s_ref, y_ref, st_ref):
    u = c_ref[...] * h_ref[...]
    w = w_ref[...]
    s0, s1 = s_ref[0], s_ref[1]
    keep = S_ROWS - DEC_BATCH
    u1 = jnp.concatenate([s1, u[:keep]], axis=0)
    u2 = jnp.concatenate([s0, s1, u[:keep - DEC_BATCH]], axis=0)
    y = w[0:1] * u2 + w[1:2] * u1 + w[2:3] * u
    y_ref[...] = (b_ref[...] * y).astype(y_ref.dtype)
    st_ref[0] = u[S_ROWS - 2 * DEC_BATCH:S_ROWS - DEC_BATCH]
    st_ref[1] = u[S_ROWS - DEC_BATCH:]


def _conv_sample(za, w, state):
    blk = P_ROWS // S_ROWS
    col = lambda c: pl.BlockSpec((S_ROWS, CONV_WIDTH), lambda i, c=c: (blk, c))
    st_spec = pl.BlockSpec((CONV_K - 1, DEC_BATCH, CONV_WIDTH), lambda i: (0, 0, 0))
    return pl.pallas_call(
        _conv_sample_kernel,
        grid=(1,),
        in_specs=[col(0), col(1), col(2), pl.BlockSpec((CONV_K, CONV_WIDTH), lambda i: (0, 0)), st_spec],
        out_specs=[pl.BlockSpec((S_ROWS, CONV_WIDTH), lambda i: (0, 0)), st_spec],
        out_shape=[jax.ShapeDtypeStruct((S_ROWS, CONV_WIDTH), jnp.bfloat16),
                   jax.ShapeDtypeStruct((CONV_K - 1, DEC_BATCH, CONV_WIDTH), jnp.float32)],
        compiler_params=_params(("arbitrary",), 24 * MIB),
        name="conv_sample",
    )(za, za, za, w, state)


def _attn_prompt_kernel(*refs, dil, has_prev, sub):
    n_in = (5 if has_prev else 3) * HALVES
    q_refs, kc_refs, vc_refs = (refs[i * HALVES:(i + 1) * HALVES] for i in range(3))
    if has_prev:
        kp_refs, vp_refs = (refs[i * HALVES:(i + 1) * HALVES] for i in range(3, 5))
    o_ref, l_ref = refs[n_in:]
    n_keys = (2 if has_prev else 1) * BAND
    qi = lax.broadcasted_iota(jnp.int32, (BAND, n_keys), 0)
    kj = lax.broadcasted_iota(jnp.int32, (BAND, n_keys), 1)
    if has_prev:
        dist = qi + BAND - kj
        inner_ok = (dist >= 0) & (dist <= BAND)
        first_ok = inner_ok & ((kj >= BAND) | (pl.program_id(1) > 0))
    else:
        first_ok = kj <= qi
    heads_per_half = V7X_LANES // HEAD_DIM

    def attend(q, k, v, valid):
        q, k, v = (t.astype(jnp.bfloat16) for t in (q, k, v))
        outs, lses = [], []
        for h in range(heads_per_half):
            sl = slice(h * HEAD_DIM, (h + 1) * HEAD_DIM)
            s = lax.dot_general(q[:, sl], k[:, sl], (((1,), (1,)), ((), ())),
                                preferred_element_type=jnp.float32) * (HEAD_DIM ** -0.5)
            s = jnp.where(valid, s, -jnp.inf)
            m = jnp.max(s, axis=-1, keepdims=True)
            p = jnp.exp(s - m)
            den = jnp.sum(p, axis=-1, keepdims=True)
            outs.append(jnp.dot(p.astype(jnp.bfloat16), v[:, sl], preferred_element_type=jnp.float32) / den)
            lses.append(jnp.broadcast_to(m + jnp.log(den), (BAND, HEAD_DIM)))
        return jnp.concatenate(outs, axis=-1), jnp.concatenate(lses, axis=-1)

    if dil == 1:
        for half in range(HALVES):
            q, kc, vc = q_refs[half][...], kc_refs[half][...], vc_refs[half][...]
            kp, vp = kp_refs[half][...], vp_refs[half][...]
            for s in range(sub):
                lo, hi = (s - 1) * BAND, (s + 1) * BAND
                k = jnp.concatenate([kp, kc[:BAND]], axis=0) if s == 0 else kc[lo:hi]
                v = jnp.concatenate([vp, vc[:BAND]], axis=0) if s == 0 else vc[lo:hi]
                o, l = attend(q[s * BAND:hi], k, v, first_ok if s == 0 else inner_ok)
                o_ref[half, s * BAND:hi, :] = o
                l_ref[half, s * BAND:hi, :] = l
        return

    def residue(r, carry):
        rows = pl.ds(r, BAND, stride=dil)
        for half in range(HALVES):
            k = kc_refs[half][rows, :]
            v = vc_refs[half][rows, :]
            if has_prev:
                k = jnp.concatenate([kp_refs[half][rows, :], k], axis=0)
                v = jnp.concatenate([vp_refs[half][rows, :], v], axis=0)
            o, l = attend(q_refs[half][rows, :], k, v, first_ok)
            o_ref[half, rows, :] = o
            l_ref[half, rows, :] = l
        return carry

    lax.fori_loop(0, dil, residue, 0, unroll=min(dil, ATTN_RESIDUE_UNROLL))


def _attn_prompt(qk, za, g):
    d = ATTN_GROUPS[g][1]
    sub = ATTN_DENSE_BLOCKS if d == 1 else 1
    rows = BAND * d * sub
    nb = SEQ // rows
    has_prev = nb > 1
    prev_rows = BAND if d == 1 else rows
    qcol, kcol, vcol = (off // V7X_LANES + g * HALVES for off in (0, ATTN_QKV, OFF_V))
    cur = lambda b, j: b * nb + j
    prev = lambda b, j: b * (SEQ // prev_rows) + jnp.maximum(j * (rows // prev_rows) - 1, 0)
    spec = lambda n, row, col: [pl.BlockSpec((n, V7X_LANES), lambda b, j, h=h: (row(b, j), col + h))
                                for h in range(HALVES)]
    in_specs = spec(rows, cur, qcol) + spec(rows, cur, kcol) + spec(rows, cur, vcol)
    args = [qk] * (2 * HALVES) + [za] * HALVES
    if has_prev:
        in_specs += spec(prev_rows, prev, kcol) + spec(prev_rows, prev, vcol)
        args += [qk] * HALVES + [za] * HALVES
    out_spec = pl.BlockSpec((HALVES, rows, V7X_LANES), lambda b, j: (0, cur(b, j), 0))
    shape = jax.ShapeDtypeStruct((HALVES, P_ROWS, V7X_LANES), jnp.float32)
    return pl.pallas_call(
        functools.partial(_attn_prompt_kernel, dil=d, has_prev=has_prev, sub=sub),
        grid=(BATCH, nb),
        in_specs=in_specs,
        out_specs=[out_spec, out_spec],
        out_shape=[shape, shape],
        compiler_params=_params(("parallel", "parallel"), 40 * MIB),
        name=f"attn_prompt_g{g}",
    )(*args)


def _attn_sample_kernel(q_ref, kn_ref, vn_ref, c_ref, o_ref, l_ref, *, dil, nseq):
    width = c_ref.shape[-1]
    scale = HEAD_DIM ** -0.5
    nt = (((1,), (1,)), ((), ()))
    t_c = lax.broadcasted_iota(jnp.int32, (DEC_SEQ, width), 0)
    w_c = lax.broadcasted_iota(jnp.int32, (DEC_SEQ, width), 1)
    ok_c = (w_c >= t_c) & (w_c % dil == t_c % dil)
    t_n = lax.broadcasted_iota(jnp.int32, (DEC_SEQ, DEC_SEQ), 0)
    s_n = lax.broadcasted_iota(jnp.int32, (DEC_SEQ, DEC_SEQ), 1)
    ok_n = (s_n <= t_n) & (s_n % dil == t_n % dil)

    def one_seq(s, carry):
        q = q_ref[s].astype(jnp.bfloat16)
        kn = kn_ref[s].astype(jnp.bfloat16)
        vn = vn_ref[s].astype(jnp.bfloat16)
        outs, lses = [], []
        for h in range(HEADS_PER_GROUP):
            sl = slice(h * HEAD_DIM, (h + 1) * HEAD_DIM)
            k_t = c_ref[s, 0, h].astype(jnp.bfloat16)
            v_t = c_ref[s, 1, h].astype(jnp.bfloat16)
            sc = jnp.dot(q[:, sl], k_t, preferred_element_type=jnp.float32) * scale
            sc = jnp.where(ok_c, sc, -jnp.inf)
            sn = lax.dot_general(q[:, sl], kn[:, sl], nt, preferred_element_type=jnp.float32) * scale
            sn = jnp.where(ok_n, sn, -jnp.inf)
            m = jnp.maximum(jnp.max(sc, axis=-1, keepdims=True), jnp.max(sn, axis=-1, keepdims=True))
            pc = jnp.exp(sc - m)
            pn = jnp.exp(sn - m)
            den = jnp.sum(pc, axis=-1, keepdims=True) + jnp.sum(pn, axis=-1, keepdims=True)
            o = (lax.dot_general(pc.astype(jnp.bfloat16), v_t, nt, preferred_element_type=jnp.float32)
                 + jnp.dot(pn.astype(jnp.bfloat16), vn[:, sl], preferred_element_type=jnp.float32))
            outs.append(o / den)
            lses.append(jnp.broadcast_to(m + jnp.log(den), (DEC_SEQ, HEAD_DIM)))
        o_ref[s] = jnp.concatenate(outs, axis=-1)
        l_ref[s] = jnp.concatenate(lses, axis=-1)
        return carry

    lax.fori_loop(0, nseq, one_seq, 0, unroll=True)


def _attn_sample(qk_s, v_s, cache_t, layer, g):
    d = ATTN_GROUPS[g][1]
    width = cache_t.shape[-1]
    seq_bytes = 2 * ATTN_OUT * width * 4
    nseq = max(1, min(SAMPLE_SEQ_TILE, SAMPLE_CACHE_TILE_BYTES // seq_bytes))
    blk = (nseq, DEC_SEQ, ATTN_OUT)
    koff = ATTN_QKV // ATTN_OUT
    out_spec = pl.BlockSpec(blk, lambda i: (i, 0, 0))
    shape = jax.ShapeDtypeStruct((DEC_BATCH, DEC_SEQ, ATTN_OUT), jnp.float32)
    o, l = pl.pallas_call(
        functools.partial(_attn_sample_kernel, dil=d, nseq=nseq),
        grid=(DEC_BATCH // nseq,),
        in_specs=[pl.BlockSpec(blk, lambda i: (i, 0, g)),
                  pl.BlockSpec(blk, lambda i: (i, 0, koff + g)),
                  pl.BlockSpec(blk, lambda i: (i, 0, g)),
                  pl.BlockSpec((None, nseq, 2, HEADS_PER_GROUP, HEAD_DIM, width),
                               lambda i: (layer, i, 0, 0, 0, 0))],
        out_specs=[out_spec, out_spec],
        out_shape=[shape, shape],
        compiler_params=_params(("parallel",), 4 * SAMPLE_CACHE_TILE_BYTES),
        name=f"attn_sample_g{g}",
    )(qk_s, qk_s, v_s, cache_t)
    to_rows = lambda a: jnp.transpose(a, (1, 0, 2)).reshape(S_ROWS, ATTN_OUT)
    return to_rows(o), to_rows(l)


def _ssm_tables(lam_re, lam_im, log_dt, b_re, b_im, c_re, c_im, d_skip):
    dt = jnp.exp(log_dt)[:, None]
    er = jnp.exp(lam_re * dt)
    lbr, lbi = er * jnp.cos(lam_im * dt), er * jnp.sin(lam_im * dt)
    den = lam_re * lam_re + lam_im * lam_im
    nr, ni = lbr - 1.0, lbi
    qr = (nr * lam_re + ni * lam_im) / den
    qi = (ni * lam_re - nr * lam_im) / den
    bbr = qr[:, :, None] * b_re - qi[:, :, None] * b_im
    bbi = qr[:, :, None] * b_im + qi[:, :, None] * b_re
    eye = jnp.eye(SSM_GB, dtype=jnp.float32)

    def in_block(b):
        b = b.reshape(N_GB, SSM_GB, SSM_STATE, SSM_CH_PER_GROUP)
        return jnp.einsum('ngpc,gh->ngchp', b, eye).reshape(N_GB, GB_CH, GB_ST)

    def out_block(c):
        c = c.reshape(N_GB, SSM_GB, SSM_CH_PER_GROUP, SSM_STATE)
        return jnp.einsum('ngcp,gh->ngphc', c, eye).reshape(N_GB, GB_ST, GB_CH)

    b_blk = jnp.concatenate([in_block(bbr), in_block(bbi)], axis=2).astype(jnp.bfloat16)
    c_blk = jnp.concatenate([out_block(c_re), out_block(-c_im)], axis=1).astype(jnp.bfloat16)

    def pack(re, im):
        re = re.reshape(N_GB, 1, GB_ST)
        im = im.reshape(N_GB, 1, GB_ST)
        return jnp.concatenate([re, re], axis=2), jnp.concatenate([-im, im], axis=2)

    pr, pi = lbr, lbi
    for _ in range(int(math.log2(SSM_CHUNK))):
        pr, pi = pr * pr - pi * pi, 2.0 * pr * pi
    return dict(b=b_blk, c=c_blk, lam=pack(lbr, lbi), lam_chunk=pack(pr, pi), d=d_skip.reshape(1, SSM_WIDTH))


def _cstep(s, a, bsw):
    swapped = jnp.concatenate([s[:, GB_ST:], s[:, :GB_ST]], axis=1)
    return a * s + bsw * swapped


def _ssm_kernel(u_ref, s0_ref, b_ref, c_ref, a_ref, bsw_ref, d_ref, *refs, steps, emit_y):
    if emit_y:
        y_ref, sout_ref, s_ref = refs
    else:
        sout_ref, s_ref = refs
    t = pl.program_id(1)

    @pl.when(t == 0)
    def _():
        s_ref[...] = s0_ref[...]

    u = u_ref[...]
    s = _cstep(s_ref[...], a_ref[...], bsw_ref[...]) + _bdot(u, b_ref[...])
    s_ref[...] = s
    if emit_y:
        y_ref[...] = _bdot(s, c_ref[...]) + d_ref[...] * u

    @pl.when(t == steps - 1)
    def _():
        sout_ref[...] = s


def _ssm_pass(u_arr, u_map, y_map, rows, steps, s0, tabs, y_shape, name):
    emit_y = y_shape is not None
    st_spec = pl.BlockSpec((rows, 2 * GB_ST), lambda gb, t: (0, gb))
    coef = pl.BlockSpec((None, 1, 2 * GB_ST), lambda gb, t: (gb, 0, 0))
    in_specs = [pl.BlockSpec((rows, GB_CH), u_map), st_spec,
                pl.BlockSpec((None, GB_CH, 2 * GB_ST), lambda gb, t: (gb, 0, 0)),
                pl.BlockSpec((None, 2 * GB_ST, GB_CH), lambda gb, t: (gb, 0, 0)),
                coef, coef,
                pl.BlockSpec((1, GB_CH), lambda gb, t: (0, gb))]
    st_shape = jax.ShapeDtypeStruct((rows, ST_WIDTH), jnp.float32)
    out_specs, out_shape = [st_spec], [st_shape]
    if emit_y:
        out_specs, out_shape = [pl.BlockSpec((rows, GB_CH), y_map), st_spec], [y_shape, st_shape]
    return pl.pallas_call(
        functools.partial(_ssm_kernel, steps=steps, emit_y=emit_y),
        grid=(N_GB, steps),
        in_specs=in_specs,
        out_specs=out_specs,
        out_shape=out_shape,
        scratch_shapes=[pltpu.VMEM((rows, 2 * GB_ST), jnp.float32)],
        compiler_params=_params(("parallel", "arbitrary"), 48 * MIB),
        name=name,
    )(u_arr, s0, tabs['b'], tabs['c'], tabs['lam'][0], tabs['lam'][1], tabs['d'])


def _ssm_prompt_kernel(*refs, emit_y):
    u_refs = refs[:HALVES]
    s0_ref, b_ref, c_ref, a_ref, bsw_ref, d_ref = refs[HALVES:HALVES + 6]
    if emit_y:
        y_ref, sout_ref, s_ref = refs[HALVES + 6:]
    else:
        sout_ref, s_ref = refs[HALVES + 6:]
    t = pl.program_id(2)
    rows = pl.ds(t, s_ref.shape[0], stride=SSM_CHUNK)

    @pl.when(t == 0)
    def _():
        s_ref[...] = s0_ref[...]

    u = jnp.concatenate([r[rows, :] for r in u_refs], axis=-1)
    s = _cstep(s_ref[...], a_ref[...], bsw_ref[...]) + _bdot(u, b_ref[...])
    s_ref[...] = s
    if emit_y:
        y = _bdot(s, c_ref[...]) + d_ref[...] * u
        for half in range(HALVES):
            y_ref[half, rows, :] = y[:, half * V7X_LANES:(half + 1) * V7X_LANES]

    @pl.when(t == SSM_CHUNK - 1)
    def _():
        sout_ref[...] = s


def _ssm_prompt_pass(za, s0, tabs, emit_y, name):
    n_rows = P_ROWS // SSM_CHUNK // SSM_BATCH_SPLIT
    nat_rows = P_ROWS // SSM_BATCH_SPLIT
    ucol = OFF_U // V7X_LANES
    u_specs = [pl.BlockSpec((nat_rows, V7X_LANES), lambda gb, p, t, h=h: (p, ucol + gb * HALVES + h))
               for h in range(HALVES)]
    st_spec = pl.BlockSpec((n_rows, 2 * GB_ST), lambda gb, p, t: (p, gb))
    coef = pl.BlockSpec((None, 1, 2 * GB_ST), lambda gb, p, t: (gb, 0, 0))
    in_specs = u_specs + [st_spec,
                          pl.BlockSpec((None, GB_CH, 2 * GB_ST), lambda gb, p, t: (gb, 0, 0)),
                          pl.BlockSpec((None, 2 * GB_ST, GB_CH), lambda gb, p, t: (gb, 0, 0)),
                          coef, coef,
                          pl.BlockSpec((1, GB_CH), lambda gb, p, t: (0, gb))]
    st_shape = jax.ShapeDtypeStruct((P_ROWS // SSM_CHUNK, ST_WIDTH), jnp.float32)
    out_specs, out_shape = [st_spec], [st_shape]
    if emit_y:
        out_specs = [pl.BlockSpec((HALVES, nat_rows, V7X_LANES), lambda gb, p, t: (gb, p, 0)), st_spec]
        out_shape = [jax.ShapeDtypeStruct((N_GB * HALVES, P_ROWS, V7X_LANES), jnp.float32), st_shape]
    return pl.pallas_call(
        functools.partial(_ssm_prompt_kernel, emit_y=emit_y),
        grid=(N_GB, SSM_BATCH_SPLIT, SSM_CHUNK),
        in_specs=in_specs,
        out_specs=out_specs,
        out_shape=out_shape,
        scratch_shapes=[pltpu.VMEM((n_rows, 2 * GB_ST), jnp.float32)],
        compiler_params=_params(("parallel", "parallel", "arbitrary"), 48 * MIB),
        name=name,
    )(*([za] * HALVES), s0, tabs['b'], tabs['c'], tabs['lam'][0], tabs['lam'][1], tabs['d'])


def _ssm_carry_kernel(e_ref, a_ref, bsw_ref, sin_ref, fin_ref):
    a = a_ref[...]
    bsw = bsw_ref[...]
    n_chunks = SEQ // SSM_CHUNK

    def body(n, states):
        new = []
        for b in range(BATCH):
            row = b * n_chunks + n
            sin_ref[pl.ds(row, 1), :] = states[b]
            new.append(_cstep(states[b], a, bsw) + e_ref[pl.ds(row, 1), :])
        return tuple(new)

    zero = jnp.zeros((1, 2 * GB_ST), jnp.float32)
    final = lax.fori_loop(0, n_chunks, body, (zero,) * BATCH)
    for b in range(BATCH):
        fin_ref[pl.ds(b, 1), :] = final[b]


def _ssm_carry(e, tabs):
    n_rows = P_ROWS // SSM_CHUNK
    st_spec = lambda rows: pl.BlockSpec((rows, 2 * GB_ST), lambda gb: (0, gb))
    coef = pl.BlockSpec((None, 1, 2 * GB_ST), lambda gb: (gb, 0, 0))
    return pl.pallas_call(
        _ssm_carry_kernel,
        grid=(N_GB,),
        in_specs=[st_spec(n_rows), coef, coef],
        out_specs=[st_spec(n_rows), st_spec(BATCH)],
        out_shape=[jax.ShapeDtypeStruct((n_rows, ST_WIDTH), jnp.float32),
                   jax.ShapeDtypeStruct((BATCH, ST_WIDTH), jnp.float32)],
        compiler_params=_params(("parallel",), 32 * MIB),
        name="ssm_carry",
    )(e, tabs['lam_chunk'][0], tabs['lam_chunk'][1])


def _pack_state(s):
    b = s.shape[0]
    s = s.reshape(b, N_GB, SSM_GB * SSM_STATE, 2)
    return jnp.transpose(s, (0, 1, 3, 2)).reshape(b, ST_WIDTH)


def _unpack_state(s):
    b = s.shape[0]
    s = s.reshape(b, N_GB, 2, SSM_GB * SSM_STATE)
    return jnp.transpose(s, (0, 1, 3, 2)).reshape(b, SSM_GROUPS, SSM_STATE, 2)


def _ssm(za, state_s, tabs):
    ucol = OFF_U // GB_CH
    zero = jnp.zeros((P_ROWS // SSM_CHUNK, ST_WIDTH), jnp.float32)
    (e,) = _ssm_prompt_pass(za, zero, tabs, False, "ssm_prompt_local")
    s_in, s_fin = _ssm_carry(e, tabs)
    y_p, _ = _ssm_prompt_pass(za, s_in, tabs, True, "ssm_prompt")
    tblk = P_ROWS // DEC_BATCH
    ys_shape = jax.ShapeDtypeStruct((S_ROWS, SSM_WIDTH), jnp.float32)
    y_s, s_new = _ssm_pass(za, lambda gb, t: (tblk + t, ucol + gb), lambda gb, t: (t, gb), DEC_BATCH, DEC_SEQ,
                           _pack_state(state_s), tabs, ys_shape, "ssm_sample")
    return y_p, y_s, _unpack_state(s_fin), _unpack_state(s_new)


def _mix_kernel(x_ref, gate_ref, gb_ref, cp_ref, cs_ref, ap_refs, as_refs, sp_ref, ss_ref,
                wglu_ref, bglu_ref, pc_ref, pa_ref, ps_ref, wo_ref, gn_ref, xo_ref, ho_ref):
    is_p = pl.program_id(0) < P_ROWS // MIX_TILE

    def pick(p_ref, s_ref):
        p = p_ref[...]
        if p.ndim == 3:
            p = jnp.concatenate([p[i] for i in range(p.shape[0])], axis=-1)
        return jnp.where(is_p, p, s_ref[...])

    y_conv = pick(cp_ref, cs_ref)
    o = [pick(ap_refs[2 * g], as_refs[2 * g]) for g in range(N_ATTN_GROUPS)]
    lse = [pick(ap_refs[2 * g + 1], as_refs[2 * g + 1]) for g in range(N_ATTN_GROUPS)]
    m = jnp.maximum(jnp.maximum(lse[0], lse[1]), lse[2])
    w = [jnp.exp(l - m) for l in lse]
    y_attn = (w[0] * o[0] + w[1] * o[1] + w[2] * o[2]) / (w[0] + w[1] + w[2])
    zg = jax.nn.gelu(pick(sp_ref, ss_ref))
    y_ssm = zg * _sigmoid(_bdot(zg, wglu_ref[...]) + bglu_ref[...])
    gt = _sigmoid(gate_ref[...].astype(jnp.float32) + gb_ref[...])
    merged = (gt[:, :D_MODEL] * _bdot(y_conv, pc_ref[...])
              + gt[:, D_MODEL:2 * D_MODEL] * _bdot(y_attn, pa_ref[...])
              + gt[:, 2 * D_MODEL:] * _bdot(y_ssm, ps_ref[...]))
    x = x_ref[...] + _bdot(merged, wo_ref[...])
    xo_ref[...] = x
    ho_ref[...] = _rms(x, gn_ref[...]).astype(ho_ref.dtype)


def _mix(x, gates, conv_p, conv_s, attn_p, attn_s, ssm_p, ssm_s, lp, h_dtype):
    n_p = P_ROWS // MIX_TILE
    p_map = lambda i: (jnp.minimum(i, n_p - 1), 0)
    s_map = lambda i: (jnp.maximum(i - n_p, 0), 0)
    row_map = lambda i: (i, 0)
    const = lambda i: (0, 0)
    rows = lambda width, imap: pl.BlockSpec((MIX_TILE, width), imap)
    slabs = lambda n: pl.BlockSpec((n, MIX_TILE, V7X_LANES), lambda i: (0, jnp.minimum(i, n_p - 1), 0))
    full = lambda a: pl.BlockSpec(a.shape, const, pipeline_mode=pl.Buffered(1))
    n_attn = 2 * N_ATTN_GROUPS
    weights = [lp['w_glu'], lp['b_glu'], lp['proj_conv'], lp['proj_attn'], lp['proj_ssm'], lp['w_out'],
               lp['norm_next']]
    in_specs = ([rows(D_MODEL, row_map), rows(GATE_WIDTH, row_map), full(lp['gate_bias']),
                 rows(CONV_WIDTH, p_map), rows(CONV_WIDTH, s_map)]
                + [slabs(HALVES)] * n_attn + [rows(ATTN_OUT, s_map)] * n_attn
                + [slabs(N_GB * HALVES), rows(SSM_WIDTH, s_map)]
                + [full(w) for w in weights])

    def body(*refs):
        x_ref, gate_ref, gb_ref, cp_ref, cs_ref = refs[:5]
        ap_refs = refs[5:5 + n_attn]
        as_refs = refs[5 + n_attn:5 + 2 * n_attn]
        rest = refs[5 + 2 * n_attn:]
        _mix_kernel(x_ref, gate_ref, gb_ref, cp_ref, cs_ref, ap_refs, as_refs, *rest)

    return pl.pallas_call(
        body,
        grid=(ROWS // MIX_TILE,),
        in_specs=in_specs,
        out_specs=[rows(D_MODEL, row_map), rows(D_MODEL, row_map)],
        out_shape=[jax.ShapeDtypeStruct((ROWS, D_MODEL), jnp.float32),
                   jax.ShapeDtypeStruct((ROWS, D_MODEL), h_dtype)],
        compiler_params=_params(("parallel",), 56 * MIB),
        name="mix",
    )(x, gates, lp['gate_bias'], conv_p, conv_s, *attn_p, *attn_s, ssm_p, ssm_s, *weights)


def _ffn_up_kernel(be_ref, x_ref, w1_ref, w3_ref, h_ref, *scratch):
    if scratch:
        w1b_ref, w3b_ref = scratch

        @pl.when(pl.program_id(1) == 0)
        def _():
            w1b_ref[...] = w1_ref[...].astype(w1b_ref.dtype)
            w3b_ref[...] = w3_ref[...].astype(w3b_ref.dtype)
    else:
        w1b_ref, w3b_ref = w1_ref, w3_ref
    x = x_ref[...].astype(jnp.bfloat16)
    a = jnp.dot(x, w1b_ref[...], preferred_element_type=jnp.float32)
    b = jnp.dot(x, w3b_ref[...], preferred_element_type=jnp.float32)
    h_ref[...] = (a * _sigmoid(a) * b).astype(h_ref.dtype)


def _ffn_up(block_e, x, w1, w3, bm, bf, name):
    m = x.shape[0]
    e, _, f = w1.shape
    cast = w1.dtype != jnp.bfloat16
    assert not cast or e == 1
    wspec = pl.BlockSpec((None, D_MODEL, bf), lambda j, i, be: (be[i], 0, j))
    grid_spec = pltpu.PrefetchScalarGridSpec(
        num_scalar_prefetch=1,
        grid=(f // bf, m // bm),
        in_specs=[pl.BlockSpec((bm, D_MODEL), lambda j, i, be: (i, 0)), wspec, wspec],
        out_specs=pl.BlockSpec((bm, bf), lambda j, i, be: (i, j)),
        scratch_shapes=[pltpu.VMEM((D_MODEL, bf), jnp.bfloat16)] * 2 if cast else [],
    )
    return pl.pallas_call(
        _ffn_up_kernel,
        grid_spec=grid_spec,
        out_shape=jax.ShapeDtypeStruct((m, f), jnp.bfloat16),
        compiler_params=_params(("parallel", "arbitrary"), 48 * MIB),
        name=name,
    )(block_e, x, w1, w3)


def _ffn_down_dense_kernel(h_ref, w_ref, x_ref, g_ref, xo_ref, ho_ref, acc_ref, *, n_k):
    k = pl.program_id(1)

    @pl.when(k == 0)
    def _():
        acc_ref[...] = x_ref[...]

    acc_ref[...] += jnp.dot(h_ref[...], w_ref[...], preferred_element_type=jnp.float32)

    @pl.when(k == n_k - 1)
    def _():
        x = acc_ref[...]
        xo_ref[...] = x
        ho_ref[...] = _rms(x, g_ref[...]).astype(ho_ref.dtype)


def _ffn_down_dense(h, w2, x, g_next, bk):
    n_k = D_FF // bk
    row = pl.BlockSpec((ROW_TILE, D_MODEL), lambda i, k: (i, 0))
    return pl.pallas_call(
        functools.partial(_ffn_down_dense_kernel, n_k=n_k),
        grid=(ROWS // ROW_TILE, n_k),
        in_specs=[pl.BlockSpec((ROW_TILE, bk), lambda i, k: (i, k)),
                  pl.BlockSpec((bk, D_MODEL), lambda i, k: (k, 0)),
                  row, pl.BlockSpec((1, D_MODEL), lambda i, k: (0, 0))],
        out_specs=[row, row],
        out_shape=[jax.ShapeDtypeStruct((ROWS, D_MODEL), jnp.float32),
                   jax.ShapeDtypeStruct((ROWS, D_MODEL), jnp.bfloat16)],
        scratch_shapes=[pltpu.VMEM((ROW_TILE, D_MODEL), jnp.float32)],
        compiler_params=_params(("parallel", "arbitrary"), 48 * MIB),
        name="ffn_down",
    )(h, w2, x, g_next.reshape(1, D_MODEL))


def _moe_down_kernel(be_ref, h_ref, w_ref, g_ref, y_ref):
    y = jnp.dot(h_ref[...], w_ref[...], preferred_element_type=jnp.float32)
    y_ref[...] = y * g_ref[...]


def _moe_down(block_e, h, w2, row_gate):
    grid_spec = pltpu.PrefetchScalarGridSpec(
        num_scalar_prefetch=1,
        grid=(MOE_BLOCKS,),
        in_specs=[pl.BlockSpec((MOE_BLOCK, D_FF_EXPERT), lambda i, be: (i, 0)),
                  pl.BlockSpec((None, D_FF_EXPERT, D_MODEL), lambda i, be: (be[i], 0, 0)),
                  pl.BlockSpec((MOE_BLOCK, 1), lambda i, be: (i, 0))],
        out_specs=pl.BlockSpec((MOE_BLOCK, D_MODEL), lambda i, be: (i, 0)),
    )
    return pl.pallas_call(
        _moe_down_kernel,
        grid_spec=grid_spec,
        out_shape=jax.ShapeDtypeStruct((MOE_ROWS, D_MODEL), jnp.float32),
        compiler_params=_params(("arbitrary",), 48 * MIB),
        name="moe_down",
    )(block_e, h, w2, row_gate)


def _router_kernel(x_ref, g_ref, w_ref, hs_ref, r_ref):
    hs = _rms(x_ref[...], g_ref[...])
    hs_ref[...] = hs
    logits = jnp.dot(hs, w_ref[...], preferred_element_type=jnp.float32, precision=lax.Precision.HIGHEST)
    lane = lax.broadcasted_iota(jnp.int32, logits.shape, 1)
    logits = jnp.where(lane < N_EXPERTS, logits, -jnp.inf)
    m1 = jnp.max(logits, axis=-1, keepdims=True)
    i1 = jnp.min(jnp.where(logits == m1, lane, ROUTER_PAD), axis=-1, keepdims=True)
    rest = jnp.where(lane == i1, -jnp.inf, logits)
    m2 = jnp.max(rest, axis=-1, keepdims=True)
    i2 = jnp.min(jnp.where(rest == m2, lane, ROUTER_PAD), axis=-1, keepdims=True)
    e2 = jnp.exp(m2 - m1)
    g1 = 1.0 / (1.0 + e2)
    g2 = e2 / (1.0 + e2)
    out = jnp.where(lane == 0, i1.astype(jnp.float32),
                    jnp.where(lane == 1, i2.astype(jnp.float32),
                              jnp.where(lane == 2, g1, jnp.where(lane == 3, g2, 0.0))))
    r_ref[...] = out


def _router(x, g, w_router):
    w = jnp.zeros((D_MODEL, ROUTER_PAD), jnp.float32).at[:, :N_EXPERTS].set(w_router)
    row = pl.BlockSpec((ROW_TILE, D_MODEL), lambda i: (i, 0))
    return pl.pallas_call(
        _router_kernel,
        grid=(ROWS // ROW_TILE,),
        in_specs=[row, pl.BlockSpec((1, D_MODEL), lambda i: (0, 0)),
                  pl.BlockSpec((D_MODEL, ROUTER_PAD), lambda i: (0, 0))],
        out_specs=[row, pl.BlockSpec((ROW_TILE, ROUTER_PAD), lambda i: (i, 0))],
        out_shape=[jax.ShapeDtypeStruct((ROWS, D_MODEL), jnp.float32),
                   jax.ShapeDtypeStruct((ROWS, ROUTER_PAD), jnp.float32)],
        compiler_params=_params(("parallel",), 32 * MIB),
        name="router",
    )(x, g.reshape(1, D_MODEL), w)


def _gather_step(idx_ref, src_ref, buf, sem, rows):
    i = pl.program_id(0)
    slot = i % 2

    def copy(step, slot_, r):
        return pltpu.make_async_copy(src_ref.at[pl.ds(idx_ref[step * rows + r], 1), :],
                                     buf.at[slot_, pl.ds(r, 1), :], sem.at[slot_])

    def start_all(step, slot_):
        def body(r2, c):
            for p in range(DMA_PRIORITIES):
                copy(step, slot_, r2 * DMA_PRIORITIES + p).start(priority=p)
            return c
        lax.fori_loop(0, rows // DMA_PRIORITIES, body, 0, unroll=4)

    @pl.when(i == 0)
    def _():
        start_all(0, 0)

    @pl.when(i + 1 < pl.num_programs(0))
    def _():
        start_all(i + 1, 1 - slot)

    def wait_body(r, c):
        copy(i, slot, r).wait()
        return c
    lax.fori_loop(0, rows, wait_body, 0, unroll=8)
    return slot


def _dispatch_kernel(tok_ref, hs_ref, o_ref, buf, sem):
    slot = _gather_step(tok_ref, hs_ref, buf, sem, MOE_BLOCK)
    o_ref[...] = buf[slot].astype(o_ref.dtype)


def _dispatch(row_tok, hs):
    grid_spec = pltpu.PrefetchScalarGridSpec(
        num_scalar_prefetch=1,
        grid=(MOE_BLOCKS,),
        in_specs=[pl.BlockSpec(memory_space=pl.ANY)],
        out_specs=pl.BlockSpec((MOE_BLOCK, D_MODEL), lambda i, tok: (i, 0)),
        scratch_shapes=[pltpu.VMEM((2, MOE_BLOCK, D_MODEL), jnp.float32), pltpu.SemaphoreType.DMA((2,))],
    )
    return pl.pallas_call(
        _dispatch_kernel,
        grid_spec=grid_spec,
        out_shape=jax.ShapeDtypeStruct((MOE_ROWS, D_MODEL), jnp.bfloat16),
        compiler_params=_params(("arbitrary",), 16 * MIB),
        name="moe_dispatch",
    )(row_tok, hs)


def _combine_kernel(pos_ref, x_ref, y_ref, g_ref, o_ref, buf, sem):
    slot = _gather_step(pos_ref, y_ref, buf, sem, TOP_K * MIX_TILE)
    y = buf[slot]
    o_ref[...] = _rms(x_ref[...] + y[:MIX_TILE] + y[MIX_TILE:], g_ref[...])


def _combine(pos, x, yb, g_final):
    rows = TOP_K * MIX_TILE
    row = pl.BlockSpec((MIX_TILE, D_MODEL), lambda i, pos: (i, 0))
    grid_spec = pltpu.PrefetchScalarGridSpec(
        num_scalar_prefetch=1,
        grid=(ROWS // MIX_TILE,),
        in_specs=[row, pl.BlockSpec(memory_space=pl.ANY), pl.BlockSpec((1, D_MODEL), lambda i, pos: (0, 0))],
        out_specs=row,
        scratch_shapes=[pltpu.VMEM((2, rows, D_MODEL), jnp.float32), pltpu.SemaphoreType.DMA((2,))],
    )
    return pl.pallas_call(
        _combine_kernel,
        grid_spec=grid_spec,
        out_shape=jax.ShapeDtypeStruct((ROWS, D_MODEL), jnp.float32),
        compiler_params=_params(("arbitrary",), 32 * MIB),
        name="moe_combine",
    )(pos, x, yb, g_final.reshape(1, D_MODEL))


def _dispatch_plan(route):
    top_idx = route[:, 0:TOP_K].astype(jnp.int32)
    gates = route[:, TOP_K:2 * TOP_K]
    flat_e = top_idx.reshape(N_ASSIGN)
    order = jnp.argsort(flat_e).astype(jnp.int32)
    rank = jnp.argsort(order).astype(jnp.int32)
    counts = jnp.sum((flat_e[:, None] == jnp.arange(N_EXPERTS)[None, :]).astype(jnp.int32), axis=0)
    padded = (counts + MOE_BLOCK - 1) // MOE_BLOCK * MOE_BLOCK
    start = jnp.cumsum(counts) - counts
    ends_p = jnp.cumsum(padded)
    pstart = ends_p - padded
    block_e = jnp.minimum(jnp.searchsorted(ends_p, jnp.arange(MOE_BLOCKS) * MOE_BLOCK, side='right'),
                          N_EXPERTS - 1).astype(jnp.int32)
    row = jnp.arange(MOE_ROWS, dtype=jnp.int32)
    row_e = jnp.repeat(block_e, MOE_BLOCK)
    off = row - pstart[row_e]
    real = off < counts[row_e]
    assign = order[jnp.clip(start[row_e] + off, 0, N_ASSIGN - 1)]
    row_tok = jnp.where(real, assign // TOP_K, 0).astype(jnp.int32)
    row_gate = jnp.where(real, gates.reshape(N_ASSIGN)[assign], 0.0)
    pos = (pstart[flat_e] + rank - start[flat_e]).astype(jnp.int32)
    pos = jnp.transpose(pos.reshape(ROWS // MIX_TILE, MIX_TILE, TOP_K), (0, 2, 1)).reshape(N_ASSIGN)
    return row_tok, row_gate.reshape(MOE_ROWS, 1), block_e, pos


def kernel(x_prompt, x_sample, cache_kv_w128, cache_kv_w512, cache_kv_w2048, state_conv, state_ssm, norm_mix, w_in, gate_bias, conv_w, ssm_lam_re, ssm_lam_im, ssm_log_dt, ssm_b_re, ssm_b_im, ssm_c_re, ssm_c_im, ssm_d, ssm_w_glu, ssm_b_glu, proj_conv, proj_attn, proj_ssm, w_out, norm_ffn, ffn_w1, ffn_w3, ffn_w2, router_w, moe_w1, moe_w3, moe_w2, norm_final):
    bf = jnp.bfloat16
    caches = tuple(jnp.transpose(c, (0, 1, 3, 4, 5, 2)) for c in (cache_kv_w128, cache_kv_w512, cache_kv_w2048))
    x = jnp.concatenate([x_prompt.reshape(P_ROWS, D_MODEL),
                         jnp.transpose(x_sample, (1, 0, 2)).reshape(S_ROWS, D_MODEL)], axis=0)
    tables = _rope_tables()
    h = _norm(x, norm_mix[0])
    kv_p = [[] for _ in range(N_ATTN_GROUPS)]
    kv_s = [[] for _ in range(N_ATTN_GROUPS)]
    conv_p_all, conv_s_all, ssm_p_all, ssm_s_all = [], [], [], []
    y = None
    for l in range(DEPTH):
        za = _linear(h, w_in, l, 0, ZA_WIDTH, jnp.float32, IN_TILE, f"in_proj_{l}")
        gates = _linear(h, w_in, l, ZA_WIDTH, GATE_WIDTH, bf, IN_TILE, f"gate_proj_{l}")
        qk = _rope(za, tables)
        conv_p, cst_p = _conv_prompt(za, conv_w[l])
        conv_s, cst_s = _conv_sample(za, conv_w[l], jnp.transpose(state_conv[l], (1, 0, 2)))
        by_seq = lambda a: jnp.transpose(a.reshape(DEC_SEQ, DEC_BATCH, a.shape[-1]), (1, 0, 2))
        qk_s = by_seq(qk[P_ROWS:])
        v_s = by_seq(za[P_ROWS:, OFF_V:OFF_V + ATTN_QKV])
        attn_p, attn_s = [], []
        for g in range(N_ATTN_GROUPS):
            attn_p += list(_attn_prompt(qk, za, g))
            attn_s += list(_attn_sample(qk_s, v_s, caches[g], l, g))
        tabs = _ssm_tables(ssm_lam_re[l], ssm_lam_im[l], ssm_log_dt[l], ssm_b_re[l], ssm_b_im[l],
                           ssm_c_re[l], ssm_c_im[l], ssm_d[l])
        ssm_p, ssm_s, sst_p, sst_s = _ssm(za, state_ssm[l], tabs)
        moe_layer = l % 2 == 1
        lp = dict(gate_bias=gate_bias[l].reshape(1, GATE_WIDTH), w_glu=ssm_w_glu[l].astype(bf),
                  b_glu=ssm_b_glu[l].reshape(1, SSM_WIDTH), proj_conv=proj_conv[l].astype(bf),
                  proj_attn=proj_attn[l].astype(bf), proj_ssm=proj_ssm[l].astype(bf),
                  w_out=w_out[l].astype(bf), norm_next=norm_ffn[l].reshape(1, D_MODEL))
        x, h = _mix(x, gates, conv_p, conv_s, attn_p, attn_s, ssm_p, ssm_s, lp, bf)
        i = l // 2
        if not moe_layer:
            zero_e = jnp.zeros((ROWS // ROW_TILE,), jnp.int32)
            hid = _ffn_up(zero_e, h, ffn_w1[i:i + 1], ffn_w3[i:i + 1], ROW_TILE, 512, "ffn_up")
            x, h = _ffn_down_dense(hid, ffn_w2[i].astype(bf), x, norm_mix[l + 1], D_FF // 4)
        else:
            hs, route = _router(x, norm_ffn[l], router_w[i])
            row_tok, row_gate, block_e, pos = _dispatch_plan(route)
            xs = _dispatch(row_tok, hs)
            hid = _ffn_up(block_e, xs, moe_w1[i].astype(bf), moe_w3[i].astype(bf),
                          MOE_BLOCK, D_FF_EXPERT // 2, "moe_up")
            yb = _moe_down(block_e, hid, moe_w2[i].astype(bf), row_gate)
            y = _combine(pos, x, yb, norm_final)
        k_off = ATTN_QKV
        for g in range(N_ATTN_GROUPS):
            keep = min(ATTN_GROUPS[g][0], SEQ)
            cols = slice(g * ATTN_OUT, (g + 1) * ATTN_OUT)
            kp = qk[:P_ROWS, k_off:][:, cols].reshape(BATCH, SEQ, HEADS_PER_GROUP, HEAD_DIM)[:, SEQ - keep:]
            vp = za[:P_ROWS, OFF_V:OFF_V + ATTN_QKV][:, cols].reshape(BATCH, SEQ, HEADS_PER_GROUP, HEAD_DIM)[:, SEQ - keep:]
            kv_p[g].append(jnp.stack([kp, vp], axis=2))
            ks = qk_s[:, :, k_off:][:, :, cols].reshape(DEC_BATCH, DEC_SEQ, HEADS_PER_GROUP, HEAD_DIM)
            vs = v_s[:, :, cols].reshape(DEC_BATCH, DEC_SEQ, HEADS_PER_GROUP, HEAD_DIM)
            kv_s[g].append(jnp.stack([ks, vs], axis=2))
        conv_p_all.append(cst_p)
        conv_s_all.append(jnp.transpose(cst_s, (1, 0, 2)))
        ssm_p_all.append(sst_p)
        ssm_s_all.append(sst_s)
    y_prompt = y[:P_ROWS].reshape(BATCH, SEQ, D_MODEL)
    y_sample = jnp.transpose(y[P_ROWS:].reshape(DEC_SEQ, DEC_BATCH, D_MODEL), (1, 0, 2))
    return (y_prompt, y_sample, jnp.stack(kv_p[0]), jnp.stack(kv_p[1]), jnp.stack(kv_p[2]),
            jnp.stack(conv_p_all), jnp.stack(ssm_p_all), jnp.stack(kv_s[0]), jnp.stack(kv_s[1]),
            jnp.stack(kv_s[2]), jnp.stack(conv_s_all), jnp.stack(ssm_s_all))
```

```python
import functools
import math

import jax
import jax.numpy as jnp
import numpy as np
from jax import lax
from jax.experimental import pallas as pl
from jax.experimental.pallas import tpu as pltpu

D_MODEL = 2048
BATCH = 4
SEQ = 2048
DEPTH = 2
DEC_BATCH = 128
DEC_SEQ = 4
PAST_LEN = 2048
CONV_WIDTH = 512
CONV_K = 3
HEAD_DIM = 64
HEADS_PER_GROUP = 4
ATTN_GROUPS = ((128, 1), (512, 4), (2048, 16))
N_ATTN_GROUPS = 3
ATTN_QKV = N_ATTN_GROUPS * HEADS_PER_GROUP * HEAD_DIM
ATTN_OUT = HEADS_PER_GROUP * HEAD_DIM
ROT_DIM = HEAD_DIM // 4
ROPE_THETA = 500000.0
SSM_WIDTH = 768
SSM_CH_PER_GROUP = 16
SSM_GROUPS = SSM_WIDTH // SSM_CH_PER_GROUP
SSM_STATE = 64
N_BRANCH = 3
D_FF = 5632
N_EXPERTS = 8
TOP_K = 2
D_FF_EXPERT = D_FF // TOP_K
EPS = 1e-6

V7X_LANES = 128
V7X_VMEM_BYTES = 64 * 1024 * 1024
MIB = 1024 * 1024

P_ROWS = BATCH * SEQ
S_ROWS = DEC_SEQ * DEC_BATCH
ROWS = P_ROWS + S_ROWS
ZA_WIDTH = 3 * CONV_WIDTH + 3 * ATTN_QKV + SSM_WIDTH
GATE_WIDTH = N_BRANCH * D_MODEL
QK_WIDTH = 2 * ATTN_QKV
OFF_Q = 3 * CONV_WIDTH
OFF_V = OFF_Q + 2 * ATTN_QKV
OFF_U = OFF_V + ATTN_QKV
BAND = 128
ATTN_DENSE_BLOCKS = 4
ATTN_RESIDUE_UNROLL = 4
SSM_CHUNK = 16
SSM_GB = 16
N_GB = SSM_GROUPS // SSM_GB
GB_CH = SSM_GB * SSM_CH_PER_GROUP
GB_ST = SSM_GB * SSM_STATE
ST_WIDTH = N_GB * 2 * GB_ST
ROW_TILE = 512
MIX_TILE = 256
IN_TILE = 1536
MOE_BLOCK = 256
N_ASSIGN = ROWS * TOP_K
MOE_BLOCKS = -(-(N_ASSIGN + N_EXPERTS * (MOE_BLOCK - 1)) // MOE_BLOCK)
MOE_ROWS = MOE_BLOCKS * MOE_BLOCK
ROUTER_PAD = V7X_LANES
SAMPLE_SEQ_TILE = 8
SAMPLE_CACHE_TILE_BYTES = 8 * MIB
SSM_BATCH_SPLIT = 2
HALVES = ATTN_OUT // V7X_LANES
assert DEPTH == 2 and HALVES == 2 and GB_CH == HALVES * V7X_LANES

assert ROWS % ROW_TILE == 0 and P_ROWS % ROW_TILE == 0
assert ROWS % MIX_TILE == 0 and P_ROWS % MIX_TILE == 0


def _params(semantics, vmem_bytes):
    return pltpu.CompilerParams(dimension_semantics=semantics,
                                vmem_limit_bytes=min(int(vmem_bytes), V7X_VMEM_BYTES - 4 * MIB))


def _rms(x, g):
    y = x * lax.rsqrt(jnp.mean(x * x, axis=-1, keepdims=True) + EPS)
    return y * g


def _bdot(a, b):
    return jnp.dot(a.astype(jnp.bfloat16), b.astype(jnp.bfloat16), preferred_element_type=jnp.float32)


def _sigmoid(x):
    return 0.5 * jnp.tanh(0.5 * x) + 0.5


def _norm_kernel(xp_ref, xs_ref, g_ref, x_ref, h_ref):
    x = jnp.where(pl.program_id(0) < P_ROWS // ROW_TILE, xp_ref[...], xs_ref[...])
    x_ref[...] = x
    h_ref[...] = _rms(x, g_ref[...]).astype(h_ref.dtype)


def _norm(x_p, x_s, g):
    n_p = P_ROWS // ROW_TILE
    row = pl.BlockSpec((ROW_TILE, D_MODEL), lambda i: (i, 0))
    return pl.pallas_call(
        _norm_kernel,
        grid=(ROWS // ROW_TILE,),
        in_specs=[pl.BlockSpec((ROW_TILE, D_MODEL), lambda i: (jnp.minimum(i, n_p - 1), 0)),
                  pl.BlockSpec((ROW_TILE, D_MODEL), lambda i: (jnp.maximum(i - n_p, 0), 0)),
                  pl.BlockSpec((1, D_MODEL), lambda i: (0, 0))],
        out_specs=[row, row],
        out_shape=[jax.ShapeDtypeStruct((ROWS, D_MODEL), jnp.float32),
                   jax.ShapeDtypeStruct((ROWS, D_MODEL), jnp.bfloat16)],
        compiler_params=_params(("parallel",), 40 * MIB),
        name="norm",
    )(x_p, x_s, g.reshape(1, D_MODEL))


def _linear_kernel(a_ref, w_ref, o_ref, wb_ref):
    @pl.when(pl.program_id(1) == 0)
    def _():
        wb_ref[...] = w_ref[...].astype(wb_ref.dtype)

    o_ref[...] = jnp.dot(a_ref[...], wb_ref[...], preferred_element_type=jnp.float32).astype(o_ref.dtype)


def _linear(a, w, layer, col0, n, out_dtype, bn, name):
    m, k = a.shape
    jb = col0 // bn
    return pl.pallas_call(
        _linear_kernel,
        grid=(n // bn, m // ROW_TILE),
        in_specs=[pl.BlockSpec((ROW_TILE, k), lambda j, i: (i, 0)),
                  pl.BlockSpec((None, k, bn), lambda j, i: (layer, 0, jb + j))],
        out_specs=pl.BlockSpec((ROW_TILE, bn), lambda j, i: (i, j)),
        out_shape=jax.ShapeDtypeStruct((m, n), out_dtype),
        scratch_shapes=[pltpu.VMEM((k, bn), jnp.bfloat16)],
        compiler_params=_params(("parallel", "arbitrary"), 56 * MIB),
        name=name,
    )(a, w)


def _rope_tables():
    half = ROT_DIM // 2
    inv = ROPE_THETA ** (-jnp.arange(half, dtype=jnp.float32) * 2.0 / ROT_DIM)
    pos_p = jnp.arange(SEQ, dtype=jnp.int32)
    pos_s = jnp.repeat(PAST_LEN + jnp.arange(DEC_SEQ, dtype=jnp.int32), DEC_BATCH)
    pos = jnp.concatenate([pos_p, pos_s]).astype(jnp.float32)
    ang = pos[:, None] * inv[None, :]
    cos, sin = jnp.cos(ang), jnp.sin(ang)
    ones = jnp.ones((pos.shape[0], HEAD_DIM - ROT_DIM), jnp.float32)
    zeros = jnp.zeros_like(ones)
    zh = jnp.zeros_like(sin)
    cos_t = jnp.concatenate([cos, cos, ones], axis=1)
    sin_hi = jnp.concatenate([-sin, zh, zeros], axis=1)
    sin_lo = jnp.concatenate([zh, sin, zeros], axis=1)
    reps = V7X_LANES // HEAD_DIM
    return tuple(jnp.tile(t, (1, reps)) for t in (cos_t, sin_hi, sin_lo))


def _rope_kernel(x_ref, c_ref, sh_ref, sl_ref, o_ref):
    x = x_ref[...]
    half = ROT_DIM // 2
    reps = QK_WIDTH // V7X_LANES
    c = jnp.tile(c_ref[...], (1, reps))
    sh = jnp.tile(sh_ref[...], (1, reps))
    sl = jnp.tile(sl_ref[...], (1, reps))
    x_up = pltpu.roll(x, QK_WIDTH - half, axis=1)
    x_dn = pltpu.roll(x, half, axis=1)
    o_ref[...] = x * c + x_up * sh + x_dn * sl


def _rope(za, tables):
    n_p = P_ROWS // ROW_TILE
    per_seq = SEQ // ROW_TILE
    tmap = lambda i: (jnp.where(i < n_p, i % per_seq, per_seq), 0)
    tspec = pl.BlockSpec((ROW_TILE, V7X_LANES), tmap)
    return pl.pallas_call(
        _rope_kernel,
        grid=(ROWS // ROW_TILE,),
        in_specs=[pl.BlockSpec((ROW_TILE, QK_WIDTH), lambda i: (i, OFF_Q // QK_WIDTH)), tspec, tspec, tspec],
        out_specs=pl.BlockSpec((ROW_TILE, QK_WIDTH), lambda i: (i, 0)),
        out_shape=jax.ShapeDtypeStruct((ROWS, QK_WIDTH), jnp.float32),
        compiler_params=_params(("parallel",), 40 * MIB),
        name="rope",
    )(za, *tables)


def _kv_prompt_kernel(k_ref, v_ref, *o_refs, keeps):
    last = pl.program_id(1) == SEQ // ROW_TILE - 1
    k, v = k_ref[...], v_ref[...]

    def write(g, lo):
        cols = slice(g * ATTN_OUT, (g + 1) * ATTN_OUT)
        o_refs[g][0] = k[lo:, cols].T
        o_refs[g][1] = v[lo:, cols].T

    for g, keep in enumerate(keeps):
        if keep == SEQ:
            write(g, 0)
        else:
            pl.when(last)(functools.partial(write, g, ROW_TILE - min(keep, ROW_TILE)))


def _kv_prompt(qk, za):
    keeps = tuple(min(w, SEQ) for w, _ in ATTN_GROUPS)
    assert all(k == SEQ or k <= ROW_TILE for k in keeps)
    n_t = SEQ // ROW_TILE
    out_specs, out_shape = [], []
    for keep in keeps:
        blk = min(keep, ROW_TILE)
        imap = (lambda b, j: (b, 0, 0, j)) if keep == SEQ else (lambda b, j: (b, 0, 0, 0))
        out_specs.append(pl.BlockSpec((None, 2, ATTN_OUT, blk), imap))
        out_shape.append(jax.ShapeDtypeStruct((BATCH, 2, ATTN_OUT, keep), jnp.float32))
    return pl.pallas_call(
        functools.partial(_kv_prompt_kernel, keeps=keeps),
        grid=(BATCH, n_t),
        in_specs=[pl.BlockSpec((ROW_TILE, ATTN_QKV), lambda b, j: (b * n_t + j, 1)),
                  pl.BlockSpec((ROW_TILE, ATTN_QKV), lambda b, j: (b * n_t + j, OFF_V // ATTN_QKV))],
        out_specs=out_specs,
        out_shape=out_shape,
        compiler_params=_params(("parallel", "arbitrary"), 32 * MIB),
        name="kv_prompt",
    )(qk, za)


def _kv_sample_kernel(k_ref, v_ref, *o_refs):
    for t in range(DEC_SEQ):
        rows = slice(t * DEC_BATCH, (t + 1) * DEC_BATCH)
        for g in range(N_ATTN_GROUPS):
            cols = slice(g * ATTN_OUT, (g + 1) * ATTN_OUT)
            o_refs[g][t, 0] = k_ref[rows, cols].T
            o_refs[g][t, 1] = v_ref[rows, cols].T


def _kv_sample(qk, za):
    blk = P_ROWS // S_ROWS
    shape = (DEC_SEQ, 2, ATTN_OUT, DEC_BATCH)
    return pl.pallas_call(
        _kv_sample_kernel,
        grid=(1,),
        in_specs=[pl.BlockSpec((S_ROWS, ATTN_QKV), lambda i: (blk, 1)),
                  pl.BlockSpec((S_ROWS, ATTN_QKV), lambda i: (blk, OFF_V // ATTN_QKV))],
        out_specs=[pl.BlockSpec(shape, lambda i: (0, 0, 0, 0))] * N_ATTN_GROUPS,
        out_shape=[jax.ShapeDtypeStruct(shape, jnp.float32)] * N_ATTN_GROUPS,
        compiler_params=_params(("arbitrary",), 32 * MIB),
        name="kv_sample",
    )(qk, za)


def _conv_prompt_kernel(h_ref, b_ref, c_ref, w_ref, y_ref, st_ref):
    u = c_ref[...] * h_ref[...]
    w = w_ref[...]
    row = lax.broadcasted_iota(jnp.int32, u.shape, 0)
    u1 = jnp.where(row >= 1, pltpu.roll(u, 1, axis=0), 0.0)
    u2 = jnp.where(row >= 2, pltpu.roll(u, 2, axis=0), 0.0)
    y = w[0:1] * u2 + w[1:2] * u1 + w[2:3] * u
    y_ref[...] = (b_ref[...] * y).astype(y_ref.dtype)
    st_ref[...] = u[SEQ - (CONV_K - 1):, :]


def _conv_prompt(za, w):
    col = lambda c: pl.BlockSpec((SEQ, CONV_WIDTH), lambda b, c=c: (b, c))
    return pl.pallas_call(
        _conv_prompt_kernel,
        grid=(BATCH,),
        in_specs=[col(0), col(1), col(2), pl.BlockSpec((CONV_K, CONV_WIDTH), lambda b: (0, 0))],
        out_specs=[pl.BlockSpec((SEQ, CONV_WIDTH), lambda b: (b, 0)),
                   pl.BlockSpec((None, CONV_K - 1, CONV_WIDTH), lambda b: (b, 0, 0))],
        out_shape=[jax.ShapeDtypeStruct((P_ROWS, CONV_WIDTH), jnp.bfloat16),
                   jax.ShapeDtypeStruct((BATCH, CONV_K - 1, CONV_WIDTH), jnp.float32)],
        compiler_params=_params(("parallel",), 48 * MIB),
        name="conv_prompt",
    )(za, za, za, w)


def _conv_sample_kernel(h_ref, b_ref, c_ref, w_ref, s_ref, y_ref, st_ref):
    u = c_ref[...] * h_ref[...]
    w = w_ref[...]
    s0, s1 = s_ref[0], s_ref[1]
    keep = S_ROWS - DEC_BATCH
    u1 = jnp.concatenate([s1, u[:keep]], axis=0)
    u2 = jnp.concatenate([s0, s1, u[:keep - DEC_BATCH]], axis=0)
    y = w[0:1] * u2 + w[1:2] * u1 + w[2:3] * u
    y_ref[...] = (b_ref[...] * y).astype(y_ref.dtype)
    st_ref[0] = u[S_ROWS - 2 * DEC_BATCH:S_ROWS - DEC_BATCH]
    st_ref[1] = u[S_ROWS - DEC_BATCH:]


def _conv_sample(za, w, state):
    blk = P_ROWS // S_ROWS
    col = lambda c: pl.BlockSpec((S_ROWS, CONV_WIDTH), lambda i, c=c: (blk, c))
    st_spec = pl.BlockSpec((CONV_K - 1, DEC_BATCH, CONV_WIDTH), lambda i: (0, 0, 0))
    return pl.pallas_call(
        _conv_sample_kernel,
        grid=(1,),
        in_specs=[col(0), col(1), col(2), pl.BlockSpec((CONV_K, CONV_WIDTH), lambda i: (0, 0)), st_spec],
        out_specs=[pl.BlockSpec((S_ROWS, CONV_WIDTH), lambda i: (0, 0)), st_spec],
        out_shape=[jax.ShapeDtypeStruct((S_ROWS, CONV_WIDTH), jnp.bfloat16),
                   jax.ShapeDtypeStruct((CONV_K - 1, DEC_BATCH, CONV_WIDTH), jnp.float32)],
        compiler_params=_params(("arbitrary",), 24 * MIB),
        name="conv_sample",
    )(za, za, za, w, state)


def _attn_prompt_kernel(*refs, dil, has_prev, sub):
    n_in = (5 if has_prev else 3) * HALVES
    q_refs, kc_refs, vc_refs = (refs[i * HALVES:(i + 1) * HALVES] for i in range(3))
    if has_prev:
        kp_refs, vp_refs = (refs[i * HALVES:(i + 1) * HALVES] for i in range(3, 5))
    o_ref, l_ref = refs[n_in:]
    n_keys = (2 if has_prev else 1) * BAND
    qi = lax.broadcasted_iota(jnp.int32, (BAND, n_keys), 0)
    kj = lax.broadcasted_iota(jnp.int32, (BAND, n_keys), 1)
    if has_prev:
        dist = qi + BAND - kj
        inner_ok = (dist >= 0) & (dist <= BAND)
        first_ok = inner_ok & ((kj >= BAND) | (pl.program_id(1) > 0))
    else:
        first_ok = kj <= qi
    heads_per_half = V7X_LANES // HEAD_DIM

    def attend(q, k, v, valid):
        q, k, v = (t.astype(jnp.bfloat16) for t in (q, k, v))
        outs, lses = [], []
        for h in range(heads_per_half):
            sl = slice(h * HEAD_DIM, (h + 1) * HEAD_DIM)
            s = lax.dot_general(q[:, sl], k[:, sl], (((1,), (1,)), ((), ())),
                                preferred_element_type=jnp.float32) * (HEAD_DIM ** -0.5)
            s = jnp.where(valid, s, -jnp.inf)
            m = jnp.max(s, axis=-1, keepdims=True)
            p = jnp.exp(s - m)
            den = jnp.sum(p, axis=-1, keepdims=True)
            outs.append(jnp.dot(p.astype(jnp.bfloat16), v[:, sl], preferred_element_type=jnp.float32) / den)
            lses.append(jnp.broadcast_to(m + jnp.log(den), (BAND, HEAD_DIM)))
        return jnp.concatenate(outs, axis=-1), jnp.concatenate(lses, axis=-1)

    if dil == 1:
        for half in range(HALVES):
            q, kc, vc = q_refs[half][...], kc_refs[half][...], vc_refs[half][...]
            kp, vp = kp_refs[half][...], vp_refs[half][...]
            for s in range(sub):
                lo, hi = (s - 1) * BAND, (s + 1) * BAND
                k = jnp.concatenate([kp, kc[:BAND]], axis=0) if s == 0 else kc[lo:hi]
                v = jnp.concatenate([vp, vc[:BAND]], axis=0) if s == 0 else vc[lo:hi]
                o, l = attend(q[s * BAND:hi], k, v, first_ok if s == 0 else inner_ok)
                o_ref[half, s * BAND:hi, :] = o
                l_ref[half, s * BAND:hi, :] = l
        return

    def residue(r, carry):
        rows = pl.ds(r, BAND, stride=dil)
        for half in range(HALVES):
            k = kc_refs[half][rows, :]
            v = vc_refs[half][rows, :]
            if has_prev:
                k = jnp.concatenate([kp_refs[half][rows, :], k], axis=0)
                v = jnp.concatenate([vp_refs[half][rows, :], v], axis=0)
            o, l = attend(q_refs[half][rows, :], k, v, first_ok)
            o_ref[half, rows, :] = o
            l_ref[half, rows, :] = l
        return carry

    lax.fori_loop(0, dil, residue, 0, unroll=min(dil, ATTN_RESIDUE_UNROLL))


def _attn_prompt(qk, za, g):
    d = ATTN_GROUPS[g][1]
    sub = ATTN_DENSE_BLOCKS if d == 1 else 1
    rows = BAND * d * sub
    nb = SEQ // rows
    has_prev = nb > 1
    prev_rows = BAND if d == 1 else rows
    qcol, kcol, vcol = (off // V7X_LANES + g * HALVES for off in (0, ATTN_QKV, OFF_V))
    cur = lambda b, j: b * nb + j
    prev = lambda b, j: b * (SEQ // prev_rows) + jnp.maximum(j * (rows // prev_rows) - 1, 0)
    spec = lambda n, row, col: [pl.BlockSpec((n, V7X_LANES), lambda b, j, h=h: (row(b, j), col + h))
                                for h in range(HALVES)]
    in_specs = spec(rows, cur, qcol) + spec(rows, cur, kcol) + spec(rows, cur, vcol)
    args = [qk] * (2 * HALVES) + [za] * HALVES
    if has_prev:
        in_specs += spec(prev_rows, prev, kcol) + spec(prev_rows, prev, vcol)
        args += [qk] * HALVES + [za] * HALVES
    out_spec = pl.BlockSpec((HALVES, rows, V7X_LANES), lambda b, j: (0, cur(b, j), 0))
    shape = jax.ShapeDtypeStruct((HALVES, P_ROWS, V7X_LANES), jnp.float32)
    return pl.pallas_call(
        functools.partial(_attn_prompt_kernel, dil=d, has_prev=has_prev, sub=sub),
        grid=(BATCH, nb),
        in_specs=in_specs,
        out_specs=[out_spec, out_spec],
        out_shape=[shape, shape],
        compiler_params=_params(("parallel", "parallel"), 40 * MIB),
        name=f"attn_prompt_g{g}",
    )(*args)


def _attn_sample_kernel(q_ref, kn_ref, vn_ref, c_ref, o_ref, l_ref, *, dil, nseq):
    width = c_ref.shape[-1]
    scale = HEAD_DIM ** -0.5
    nt = (((1,), (1,)), ((), ()))
    t_c = lax.broadcasted_iota(jnp.int32, (DEC_SEQ, width), 0)
    w_c = lax.broadcasted_iota(jnp.int32, (DEC_SEQ, width), 1)
    ok_c = (w_c >= t_c) & (w_c % dil == t_c % dil)
    t_n = lax.broadcasted_iota(jnp.int32, (DEC_SEQ, DEC_SEQ), 0)
    s_n = lax.broadcasted_iota(jnp.int32, (DEC_SEQ, DEC_SEQ), 1)
    ok_n = (s_n <= t_n) & (s_n % dil == t_n % dil)

    def one_seq(s, carry):
        q = q_ref[s].astype(jnp.bfloat16)
        kn = kn_ref[s].astype(jnp.bfloat16)
        vn = vn_ref[s].astype(jnp.bfloat16)
        outs, lses = [], []
        for h in range(HEADS_PER_GROUP):
            sl = slice(h * HEAD_DIM, (h + 1) * HEAD_DIM)
            k_t = c_ref[s, 0, h].astype(jnp.bfloat16)
            v_t = c_ref[s, 1, h].astype(jnp.bfloat16)
            sc = jnp.dot(q[:, sl], k_t, preferred_element_type=jnp.float32) * scale
            sc = jnp.where(ok_c, sc, -jnp.inf)
            sn = lax.dot_general(q[:, sl], kn[:, sl], nt, preferred_element_type=jnp.float32) * scale
            sn = jnp.where(ok_n, sn, -jnp.inf)
            m = jnp.maximum(jnp.max(sc, axis=-1, keepdims=True), jnp.max(sn, axis=-1, keepdims=True))
            pc = jnp.exp(sc - m)
            pn = jnp.exp(sn - m)
            den = jnp.sum(pc, axis=-1, keepdims=True) + jnp.sum(pn, axis=-1, keepdims=True)
            o = (lax.dot_general(pc.astype(jnp.bfloat16), v_t, nt, preferred_element_type=jnp.float32)
                 + jnp.dot(pn.astype(jnp.bfloat16), vn[:, sl], preferred_element_type=jnp.float32))
            outs.append(o / den)
            lses.append(jnp.broadcast_to(m + jnp.log(den), (DEC_SEQ, HEAD_DIM)))
        o_ref[s] = jnp.concatenate(outs, axis=-1)
        l_ref[s] = jnp.concatenate(lses, axis=-1)
        return carry

    lax.fori_loop(0, nseq, one_seq, 0, unroll=True)


def _attn_sample(qk_s, v_s, cache_t, layer, g):
    d = ATTN_GROUPS[g][1]
    width = cache_t.shape[-1]
    seq_bytes = 2 * ATTN_OUT * width * 4
    nseq = max(1, min(SAMPLE_SEQ_TILE, SAMPLE_CACHE_TILE_BYTES // seq_bytes))
    blk = (nseq, DEC_SEQ, ATTN_OUT)
    koff = ATTN_QKV // ATTN_OUT
    out_spec = pl.BlockSpec(blk, lambda i: (i, 0, 0))
    shape = jax.ShapeDtypeStruct((DEC_BATCH, DEC_SEQ, ATTN_OUT), jnp.float32)
    o, l = pl.pallas_call(
        functools.partial(_attn_sample_kernel, dil=d, nseq=nseq),
        grid=(DEC_BATCH // nseq,),
        in_specs=[pl.BlockSpec(blk, lambda i: (i, 0, g)),
                  pl.BlockSpec(blk, lambda i: (i, 0, koff + g)),
                  pl.BlockSpec(blk, lambda i: (i, 0, g)),
                  pl.BlockSpec((None, nseq, 2, HEADS_PER_GROUP, HEAD_DIM, width),
                               lambda i: (layer, i, 0, 0, 0, 0))],
        out_specs=[out_spec, out_spec],
        out_shape=[shape, shape],
        compiler_params=_params(("parallel",), 4 * SAMPLE_CACHE_TILE_BYTES),
        name=f"attn_sample_g{g}",
    )(qk_s, qk_s, v_s, cache_t)
    to_rows = lambda a: jnp.transpose(a, (1, 0, 2)).reshape(S_ROWS, ATTN_OUT)
    return to_rows(o), to_rows(l)


def _ssm_tables(lam_re, lam_im, log_dt, b_re, b_im, c_re, c_im, d_skip):
    dt = jnp.exp(log_dt)[:, None]
    er = jnp.exp(lam_re * dt)
    lbr, lbi = er * jnp.cos(lam_im * dt), er * jnp.sin(lam_im * dt)
    den = lam_re * lam_re + lam_im * lam_im
    nr, ni = lbr - 1.0, lbi
    qr = (nr * lam_re + ni * lam_im) / den
    qi = (ni * lam_re - nr * lam_im) / den
    bbr = qr[:, :, None] * b_re - qi[:, :, None] * b_im
    bbi = qr[:, :, None] * b_im + qi[:, :, None] * b_re
    eye = jnp.eye(SSM_GB, dtype=jnp.float32)

    def in_block(b):
        b = b.reshape(N_GB, SSM_GB, SSM_STATE, SSM_CH_PER_GROUP)
        return jnp.einsum('ngpc,gh->ngchp', b, eye).reshape(N_GB, GB_CH, GB_ST)

    def out_block(c):
        c = c.reshape(N_GB, SSM_GB, SSM_CH_PER_GROUP, SSM_STATE)
        return jnp.einsum('ngcp,gh->ngphc', c, eye).reshape(N_GB, GB_ST, GB_CH)

    b_blk = jnp.concatenate([in_block(bbr), in_block(bbi)], axis=2).astype(jnp.bfloat16)
    c_blk = jnp.concatenate([out_block(c_re), out_block(-c_im)], axis=1).astype(jnp.bfloat16)

    def pack(re, im):
        re = re.reshape(N_GB, 1, GB_ST)
        im = im.reshape(N_GB, 1, GB_ST)
        return jnp.concatenate([re, re], axis=2), jnp.concatenate([-im, im], axis=2)

    pr, pi = lbr, lbi
    for _ in range(int(math.log2(SSM_CHUNK))):
        pr, pi = pr * pr - pi * pi, 2.0 * pr * pi
    return dict(b=b_blk, c=c_blk, lam=pack(lbr, lbi), lam_chunk=pack(pr, pi), d=d_skip.reshape(1, SSM_WIDTH))


def _cstep(s, a, bsw):
    swapped = jnp.concatenate([s[:, GB_ST:], s[:, :GB_ST]], axis=1)
    return a * s + bsw * swapped


def _ssm_kernel(u_ref, s0_ref, b_ref, c_ref, a_ref, bsw_ref, d_ref, *refs, steps, emit_y):
    if emit_y:
        y_ref, sout_ref, s_ref = refs
    else:
        sout_ref, s_ref = refs
    t = pl.program_id(1)

    @pl.when(t == 0)
    def _():
        s_ref[...] = s0_ref[...]

    u = u_ref[...]
    s = _cstep(s_ref[...], a_ref[...], bsw_ref[...]) + _bdot(u, b_ref[...])
    s_ref[...] = s
    if emit_y:
        y_ref[...] = _bdot(s, c_ref[...]) + d_ref[...] * u

    @pl.when(t == steps - 1)
    def _():
        sout_ref[...] = s


def _ssm_pass(u_arr, u_map, y_map, rows, steps, s0, tabs, y_shape, name):
    emit_y = y_shape is not None
    st_spec = pl.BlockSpec((rows, 2 * GB_ST), lambda gb, t: (0, gb))
    coef = pl.BlockSpec((None, 1, 2 * GB_ST), lambda gb, t: (gb, 0, 0))
    in_specs = [pl.BlockSpec((rows, GB_CH), u_map), st_spec,
                pl.BlockSpec((None, GB_CH, 2 * GB_ST), lambda gb, t: (gb, 0, 0)),
                pl.BlockSpec((None, 2 * GB_ST, GB_CH), lambda gb, t: (gb, 0, 0)),
                coef, coef,
                pl.BlockSpec((1, GB_CH), lambda gb, t: (0, gb))]
    st_shape = jax.ShapeDtypeStruct((rows, ST_WIDTH), jnp.float32)
    out_specs, out_shape = [st_spec], [st_shape]
    if emit_y:
        out_specs, out_shape = [pl.BlockSpec((rows, GB_CH), y_map), st_spec], [y_shape, st_shape]
    return pl.pallas_call(
        functools.partial(_ssm_kernel, steps=steps, emit_y=emit_y),
        grid=(N_GB, steps),
        in_specs=in_specs,
        out_specs=out_specs,
        out_shape=out_shape,
        scratch_shapes=[pltpu.VMEM((rows, 2 * GB_ST), jnp.float32)],
        compiler_params=_params(("parallel", "arbitrary"), 48 * MIB),
        name=name,
    )(u_arr, s0, tabs['b'], tabs['c'], tabs['lam'][0], tabs['lam'][1], tabs['d'])


def _ssm_prompt_kernel(*refs, emit_y):
    u_refs = refs[:HALVES]
    s0_ref, b_ref, c_ref, a_ref, bsw_ref, d_ref = refs[HALVES:HALVES + 6]
    if emit_y:
        y_ref, sout_ref, s_ref = refs[HALVES + 6:]
    else:
        sout_ref, s_ref = refs[HALVES + 6:]
    t = pl.program_id(2)
    rows = pl.ds(t, s_ref.shape[0], stride=SSM_CHUNK)

    @pl.when(t == 0)
    def _():
        s_ref[...] = s0_ref[...]

    u = jnp.concatenate([r[rows, :] for r in u_refs], axis=-1)
    s = _cstep(s_ref[...], a_ref[...], bsw_ref[...]) + _bdot(u, b_ref[...])
    s_ref[...] = s
    if emit_y:
        y = _bdot(s, c_ref[...]) + d_ref[...] * u
        for half in range(HALVES):
            y_ref[half, rows, :] = y[:, half * V7X_LANES:(half + 1) * V7X_LANES]

    @pl.when(t == SSM_CHUNK - 1)
    def _():
        sout_ref[...] = s


def _ssm_prompt_pass(za, s0, tabs, emit_y, name):
    n_rows = P_ROWS // SSM_CHUNK // SSM_BATCH_SPLIT
    nat_rows = P_ROWS // SSM_BATCH_SPLIT
    ucol = OFF_U // V7X_LANES
    u_specs = [pl.BlockSpec((nat_rows, V7X_LANES), lambda gb, p, t, h=h: (p, ucol + gb * HALVES + h))
               for h in range(HALVES)]
    st_spec = pl.BlockSpec((n_rows, 2 * GB_ST), lambda gb, p, t: (p, gb))
    coef = pl.BlockSpec((None, 1, 2 * GB_ST), lambda gb, p, t: (gb, 0, 0))
    in_specs = u_specs + [st_spec,
                          pl.BlockSpec((None, GB_CH, 2 * GB_ST), lambda gb, p, t: (gb, 0, 0)),
                          pl.BlockSpec((None, 2 * GB_ST, GB_CH), lambda gb, p, t: (gb, 0, 0)),
                          coef, coef,
                          pl.BlockSpec((1, GB_CH), lambda gb, p, t: (0, gb))]
    st_shape = jax.ShapeDtypeStruct((P_ROWS // SSM_CHUNK, ST_WIDTH), jnp.float32)
    out_specs, out_shape = [st_spec], [st_shape]
    if emit_y:
        out_specs = [pl.BlockSpec((HALVES, nat_rows, V7X_LANES), lambda gb, p, t: (gb, p, 0)), st_spec]
        out_shape = [jax.ShapeDtypeStruct((N_GB * HALVES, P_ROWS, V7X_LANES), jnp.float32), st_shape]
    return pl.pallas_call(
        functools.partial(_ssm_prompt_kernel, emit_y=emit_y),
        grid=(N_GB, SSM_BATCH_SPLIT, SSM_CHUNK),
        in_specs=in_specs,
        out_specs=out_specs,
        out_shape=out_shape,
        scratch_shapes=[pltpu.VMEM((n_rows, 2 * GB_ST), jnp.float32)],
        compiler_params=_params(("parallel", "parallel", "arbitrary"), 48 * MIB),
        name=name,
    )(*([za] * HALVES), s0, tabs['b'], tabs['c'], tabs['lam'][0], tabs['lam'][1], tabs['d'])


def _ssm_carry_kernel(e_ref, a_ref, bsw_ref, sin_ref, fin_ref):
    a = a_ref[...]
    bsw = bsw_ref[...]
    n_chunks = SEQ // SSM_CHUNK

    def body(n, states):
        new = []
        for b in range(BATCH):
            row = b * n_chunks + n
            sin_ref[pl.ds(row, 1), :] = states[b]
            new.append(_cstep(states[b], a, bsw) + e_ref[pl.ds(row, 1), :])
        return tuple(new)

    zero = jnp.zeros((1, 2 * GB_ST), jnp.float32)
    final = lax.fori_loop(0, n_chunks, body, (zero,) * BATCH)
    for b in range(BATCH):
        fin_ref[pl.ds(b, 1), :] = final[b]


def _ssm_carry(e, tabs):
    n_rows = P_ROWS // SSM_CHUNK
    st_spec = lambda rows: pl.BlockSpec((rows, 2 * GB_ST), lambda gb: (0, gb))
    coef = pl.BlockSpec((None, 1, 2 * GB_ST), lambda gb: (gb, 0, 0))
    return pl.pallas_call(
        _ssm_carry_kernel,
        grid=(N_GB,),
        in_specs=[st_spec(n_rows), coef, coef],
        out_specs=[st_spec(n_rows), st_spec(BATCH)],
        out_shape=[jax.ShapeDtypeStruct((n_rows, ST_WIDTH), jnp.float32),
                   jax.ShapeDtypeStruct((BATCH, ST_WIDTH), jnp.float32)],
        compiler_params=_params(("parallel",), 32 * MIB),
        name="ssm_carry",
    )(e, tabs['lam_chunk'][0], tabs['lam_chunk'][1])


def _pack_state(s):
    b = s.shape[0]
    s = s.reshape(b, N_GB, SSM_GB * SSM_STATE, 2)
    return jnp.transpose(s, (0, 1, 3, 2)).reshape(b, ST_WIDTH)


def _unpack_state(s):
    b = s.shape[0]
    s = s.reshape(b, N_GB, 2, SSM_GB * SSM_STATE)
    return jnp.transpose(s, (0, 1, 3, 2)).reshape(b, SSM_GROUPS, SSM_STATE, 2)


def _ssm(za, state_s, tabs):
    ucol = OFF_U // GB_CH
    zero = jnp.zeros((P_ROWS // SSM_CHUNK, ST_WIDTH), jnp.float32)
    (e,) = _ssm_prompt_pass(za, zero, tabs, False, "ssm_prompt_local")
    s_in, s_fin = _ssm_carry(e, tabs)
    y_p, _ = _ssm_prompt_pass(za, s_in, tabs, True, "ssm_prompt")
    tblk = P_ROWS // DEC_BATCH
    ys_shape = jax.ShapeDtypeStruct((S_ROWS, SSM_WIDTH), jnp.float32)
    y_s, s_new = _ssm_pass(za, lambda gb, t: (tblk + t, ucol + gb), lambda gb, t: (t, gb), DEC_BATCH, DEC_SEQ,
                           _pack_state(state_s), tabs, ys_shape, "ssm_sample")
    return y_p, y_s, _unpack_state(s_fin), _unpack_state(s_new)


def _mix_kernel(x_ref, gate_ref, gb_ref, cp_ref, cs_ref, ap_refs, as_refs, sp_ref, ss_ref,
                wglu_ref, bglu_ref, pc_ref, pa_ref, ps_ref, wo_ref, gn_ref, xo_ref, ho_ref):
    is_p = pl.program_id(0) < P_ROWS // MIX_TILE

    def pick(p_ref, s_ref):
        p = p_ref[...]
        if p.ndim == 3:
            p = jnp.concatenate([p[i] for i in range(p.shape[0])], axis=-1)
        return jnp.where(is_p, p, s_ref[...])

    y_conv = pick(cp_ref, cs_ref)
    o = [pick(ap_refs[2 * g], as_refs[2 * g]) for g in range(N_ATTN_GROUPS)]
    lse = [pick(ap_refs[2 * g + 1], as_refs[2 * g + 1]) for g in range(N_ATTN_GROUPS)]
    m = jnp.maximum(jnp.maximum(lse[0], lse[1]), lse[2])
    w = [jnp.exp(l - m) for l in lse]
    y_attn = (w[0] * o[0] + w[1] * o[1] + w[2] * o[2]) / (w[0] + w[1] + w[2])
    zg = jax.nn.gelu(pick(sp_ref, ss_ref))
    y_ssm = zg * _sigmoid(_bdot(zg, wglu_ref[...]) + bglu_ref[...])
    gt = _sigmoid(gate_ref[...].astype(jnp.float32) + gb_ref[...])
    merged = (gt[:, :D_MODEL] * _bdot(y_conv, pc_ref[...])
              + gt[:, D_MODEL:2 * D_MODEL] * _bdot(y_attn, pa_ref[...])
              + gt[:, 2 * D_MODEL:] * _bdot(y_ssm, ps_ref[...]))
    x = x_ref[...] + _bdot(merged, wo_ref[...])
    xo_ref[...] = x
    ho_ref[...] = _rms(x, gn_ref[...]).astype(ho_ref.dtype)


def _mix(x, gates, conv_p, conv_s, attn_p, attn_s, ssm_p, ssm_s, lp, h_dtype):
    n_p = P_ROWS // MIX_TILE
    p_map = lambda i: (jnp.minimum(i, n_p - 1), 0)
    s_map = lambda i: (jnp.maximum(i - n_p, 0), 0)
    row_map = lambda i: (i, 0)
    const = lambda i: (0, 0)
    rows = lambda width, imap: pl.BlockSpec((MIX_TILE, width), imap)
    slabs = lambda n: pl.BlockSpec((n, MIX_TILE, V7X_LANES), lambda i: (0, jnp.minimum(i, n_p - 1), 0))
    full = lambda a: pl.BlockSpec(a.shape, const, pipeline_mode=pl.Buffered(1))
    n_attn = 2 * N_ATTN_GROUPS
    weights = [lp['w_glu'], lp['b_glu'], lp['proj_conv'], lp['proj_attn'], lp['proj_ssm'], lp['w_out'],
               lp['norm_next']]
    in_specs = ([rows(D_MODEL, row_map), rows(GATE_WIDTH, row_map), full(lp['gate_bias']),
                 rows(CONV_WIDTH, p_map), rows(CONV_WIDTH, s_map)]
                + [slabs(HALVES)] * n_attn + [rows(ATTN_OUT, s_map)] * n_attn
                + [slabs(N_GB * HALVES), rows(SSM_WIDTH, s_map)]
                + [full(w) for w in weights])

    def body(*refs):
        x_ref, gate_ref, gb_ref, cp_ref, cs_ref = refs[:5]
        ap_refs = refs[5:5 + n_attn]
        as_refs = refs[5 + n_attn:5 + 2 * n_attn]
        rest = refs[5 + 2 * n_attn:]
        _mix_kernel(x_ref, gate_ref, gb_ref, cp_ref, cs_ref, ap_refs, as_refs, *rest)

    return pl.pallas_call(
        body,
        grid=(ROWS // MIX_TILE,),
        in_specs=in_specs,
        out_specs=[rows(D_MODEL, row_map), rows(D_MODEL, row_map)],
        out_shape=[jax.ShapeDtypeStruct((ROWS, D_MODEL), jnp.float32),
                   jax.ShapeDtypeStruct((ROWS, D_MODEL), h_dtype)],
        compiler_params=_params(("parallel",), 56 * MIB),
        name="mix",
    )(x, gates, lp['gate_bias'], conv_p, conv_s, *attn_p, *attn_s, ssm_p, ssm_s, *weights)


def _ffn_up_kernel(be_ref, x_ref, w1_ref, w3_ref, h_ref, *scratch):
    if scratch:
        w1b_ref, w3b_ref = scratch

        @pl.when(pl.program_id(1) == 0)
        def _():
            w1b_ref[...] = w1_ref[...].astype(w1b_ref.dtype)
            w3b_ref[...] = w3_ref[...].astype(w3b_ref.dtype)
    else:
        w1b_ref, w3b_ref = w1_ref, w3_ref
    x = x_ref[...].astype(jnp.bfloat16)
    a = jnp.dot(x, w1b_ref[...], preferred_element_type=jnp.float32)
    b = jnp.dot(x, w3b_ref[...], preferred_element_type=jnp.float32)
    h_ref[...] = (a * _sigmoid(a) * b).astype(h_ref.dtype)


def _ffn_up(block_e, x, w1, w3, bm, bf, name):
    m = x.shape[0]
    e, _, f = w1.shape
    cast = w1.dtype != jnp.bfloat16
    assert not cast or e == 1
    wspec = pl.BlockSpec((None, D_MODEL, bf), lambda j, i, be: (be[i], 0, j))
    grid_spec = pltpu.PrefetchScalarGridSpec(
        num_scalar_prefetch=1,
        grid=(f // bf, m // bm),
        in_specs=[pl.BlockSpec((bm, D_MODEL), lambda j, i, be: (i, 0)), wspec, wspec],
        out_specs=pl.BlockSpec((bm, bf), lambda j, i, be: (i, j)),
        scratch_shapes=[pltpu.VMEM((D_MODEL, bf), jnp.bfloat16)] * 2 if cast else [],
    )
    return pl.pallas_call(
        _ffn_up_kernel,
        grid_spec=grid_spec,
        out_shape=jax.ShapeDtypeStruct((m, f), jnp.bfloat16),
        compiler_params=_params(("parallel", "arbitrary"), 48 * MIB),
        name=name,
    )(block_e, x, w1, w3)


def _ffn_down_dense_kernel(h_ref, w_ref, x_ref, g_ref, xo_ref, ho_ref, acc_ref, *, n_k):
    k = pl.program_id(1)

    @pl.when(k == 0)
    def _():
        acc_ref[...] = x_ref[...]

    acc_ref[...] += jnp.dot(h_ref[...], w_ref[...], preferred_element_type=jnp.float32)

    @pl.when(k == n_k - 1)
    def _():
        x = acc_ref[...]
        xo_ref[...] = x
        ho_ref[...] = _rms(x, g_ref[...]).astype(ho_ref.dtype)


def _ffn_down_dense(h, w2, x, g_next, bk):
    n_k = D_FF // bk
    row = pl.BlockSpec((ROW_TILE, D_MODEL), lambda i, k: (i, 0))
    return pl.pallas_call(
        functools.partial(_ffn_down_dense_kernel, n_k=n_k),
        grid=(ROWS // ROW_TILE, n_k),
        in_specs=[pl.BlockSpec((ROW_TILE, bk), lambda i, k: (i, k)),
                  pl.BlockSpec((bk, D_MODEL), lambda i, k: (k, 0)),
                  row, pl.BlockSpec((1, D_MODEL), lambda i, k: (0, 0))],
        out_specs=[row, row],
        out_shape=[jax.ShapeDtypeStruct((ROWS, D_MODEL), jnp.float32),
                   jax.ShapeDtypeStruct((ROWS, D_MODEL), jnp.bfloat16)],
        scratch_shapes=[pltpu.VMEM((ROW_TILE, D_MODEL), jnp.float32)],
        compiler_params=_params(("parallel", "arbitrary"), 48 * MIB),
        name="ffn_down",
    )(h, w2, x, g_next.reshape(1, D_MODEL))


def _moe_down_kernel(be_ref, h_ref, w_ref, g_ref, y_ref):
    y = jnp.dot(h_ref[...], w_ref[...], preferred_element_type=jnp.float32)
    y_ref[...] = y * g_ref[...]


def _moe_down(block_e, h, w2, row_gate):
    grid_spec = pltpu.PrefetchScalarGridSpec(
        num_scalar_prefetch=1,
        grid=(MOE_BLOCKS,),
        in_specs=[pl.BlockSpec((MOE_BLOCK, D_FF_EXPERT), lambda i, be: (i, 0)),
                  pl.BlockSpec((None, D_FF_EXPERT, D_MODEL), lambda i, be: (be[i], 0, 0)),
                  pl.BlockSpec((MOE_BLOCK, 1), lambda i, be: (i, 0))],
        out_specs=pl.BlockSpec((MOE_BLOCK, D_MODEL), lambda i, be: (i, 0)),
    )
    return pl.pallas_call(
        _moe_down_kernel,
        grid_spec=grid_spec,
        out_shape=jax.ShapeDtypeStruct((MOE_ROWS, D_MODEL), jnp.float32),
        compiler_params=_params(("arbitrary",), 48 * MIB),
        name="moe_down",
    )(block_e, h, w2, row_gate)


def _router_kernel(x_ref, g_ref, w_ref, hs_ref, r_ref):
    hs = _rms(x_ref[...], g_ref[...])
    hs_ref[...] = hs
    logits = jnp.dot(hs, w_ref[...], preferred_element_type=jnp.float32, precision=lax.Precision.HIGHEST)
    lane = lax.broadcasted_iota(jnp.int32, logits.shape, 1)
    logits = jnp.where(lane < N_EXPERTS, logits, -jnp.inf)
    m1 = jnp.max(logits, axis=-1, keepdims=True)
    i1 = jnp.min(jnp.where(logits == m1, lane, ROUTER_PAD), axis=-1, keepdims=True)
    rest = jnp.where(lane == i1, -jnp.inf, logits)
    m2 = jnp.max(rest, axis=-1, keepdims=True)
    i2 = jnp.min(jnp.where(rest == m2, lane, ROUTER_PAD), axis=-1, keepdims=True)
    e2 = jnp.exp(m2 - m1)
    g1 = 1.0 / (1.0 + e2)
    g2 = e2 / (1.0 + e2)
    out = jnp.where(lane == 0, i1.astype(jnp.float32),
                    jnp.where(lane == 1, i2.astype(jnp.float32),
                              jnp.where(lane == 2, g1, jnp.where(lane == 3, g2, 0.0))))
    r_ref[...] = out


def _router(x, g, w_router):
    w = jnp.zeros((D_MODEL, ROUTER_PAD), jnp.float32).at[:, :N_EXPERTS].set(w_router)
    row = pl.BlockSpec((ROW_TILE, D_MODEL), lambda i: (i, 0))
    return pl.pallas_call(
        _router_kernel,
        grid=(ROWS // ROW_TILE,),
        in_specs=[row, pl.BlockSpec((1, D_MODEL), lambda i: (0, 0)),
                  pl.BlockSpec((D_MODEL, ROUTER_PAD), lambda i: (0, 0))],
        out_specs=[row, pl.BlockSpec((ROW_TILE, ROUTER_PAD), lambda i: (i, 0))],
        out_shape=[jax.ShapeDtypeStruct((ROWS, D_MODEL), jnp.float32),
                   jax.ShapeDtypeStruct((ROWS, ROUTER_PAD), jnp.float32)],
        compiler_params=_params(("parallel",), 32 * MIB),
        name="router",
    )(x, g.reshape(1, D_MODEL), w)


def _gather_step(idx_ref, src_ref, buf, sem, rows):
    i = pl.program_id(0)
    slot = i % 2

    def copy(step, slot_, r):
        return pltpu.make_async_copy(src_ref.at[pl.ds(idx_ref[step * rows + r], 1), :],
                                     buf.at[slot_, pl.ds(r, 1), :], sem.at[slot_])

    def start_all(step, slot_):
        def body(r, c):
            copy(step, slot_, r).start()
            return c
        lax.fori_loop(0, rows, body, 0, unroll=8)

    @pl.when(i == 0)
    def _():
        start_all(0, 0)

    @pl.when(i + 1 < pl.num_programs(0))
    def _():
        start_all(i + 1, 1 - slot)

    def wait_body(r, c):
        copy(i, slot, r).wait()
        return c
    lax.fori_loop(0, rows, wait_body, 0, unroll=8)
    return slot


def _dispatch_kernel(tok_ref, hs_ref, o_ref, buf, sem):
    slot = _gather_step(tok_ref, hs_ref, buf, sem, MOE_BLOCK)
    o_ref[...] = buf[slot].astype(o_ref.dtype)


def _dispatch(row_tok, hs):
    grid_spec = pltpu.PrefetchScalarGridSpec(
        num_scalar_prefetch=1,
        grid=(MOE_BLOCKS,),
        in_specs=[pl.BlockSpec(memory_space=pl.ANY)],
        out_specs=pl.BlockSpec((MOE_BLOCK, D_MODEL), lambda i, tok: (i, 0)),
        scratch_shapes=[pltpu.VMEM((2, MOE_BLOCK, D_MODEL), jnp.float32), pltpu.SemaphoreType.DMA((2,))],
    )
    return pl.pallas_call(
        _dispatch_kernel,
        grid_spec=grid_spec,
        out_shape=jax.ShapeDtypeStruct((MOE_ROWS, D_MODEL), jnp.bfloat16),
        compiler_params=_params(("arbitrary",), 16 * MIB),
        name="moe_dispatch",
    )(row_tok, hs)


def _combine_kernel(pos_ref, x_ref, y_ref, g_ref, op_ref, os_ref, buf, sem):
    slot = _gather_step(pos_ref, y_ref, buf, sem, TOP_K * MIX_TILE)
    y = buf[slot]
    out = _rms(x_ref[...] + y[:MIX_TILE] + y[MIX_TILE:], g_ref[...])
    is_p = pl.program_id(0) < P_ROWS // MIX_TILE

    @pl.when(is_p)
    def _():
        op_ref[...] = out

    @pl.when(jnp.logical_not(is_p))
    def _():
        os_ref[...] = out


def _combine(pos, x, yb, g_final):
    rows = TOP_K * MIX_TILE
    n_p = P_ROWS // MIX_TILE
    grid_spec = pltpu.PrefetchScalarGridSpec(
        num_scalar_prefetch=1,
        grid=(ROWS // MIX_TILE,),
        in_specs=[pl.BlockSpec((MIX_TILE, D_MODEL), lambda i, pos: (i, 0)),
                  pl.BlockSpec(memory_space=pl.ANY),
                  pl.BlockSpec((1, D_MODEL), lambda i, pos: (0, 0))],
        out_specs=[pl.BlockSpec((MIX_TILE, D_MODEL), lambda i, pos: (jnp.minimum(i, n_p - 1), 0)),
                   pl.BlockSpec((MIX_TILE, D_MODEL), lambda i, pos: (jnp.maximum(i - n_p, 0), 0))],
        scratch_shapes=[pltpu.VMEM((2, rows, D_MODEL), jnp.float32), pltpu.SemaphoreType.DMA((2,))],
    )
    return pl.pallas_call(
        _combine_kernel,
        grid_spec=grid_spec,
        out_shape=[jax.ShapeDtypeStruct((P_ROWS, D_MODEL), jnp.float32),
                   jax.ShapeDtypeStruct((S_ROWS, D_MODEL), jnp.float32)],
        compiler_params=_params(("arbitrary",), 32 * MIB),
        name="moe_combine",
    )(pos, x, yb, g_final.reshape(1, D_MODEL))


def _dispatch_plan(route):
    top_idx = route[:, 0:TOP_K].astype(jnp.int32)
    gates = route[:, TOP_K:2 * TOP_K]
    flat_e = top_idx.reshape(N_ASSIGN)
    order = jnp.argsort(flat_e).astype(jnp.int32)
    rank = jnp.argsort(order).astype(jnp.int32)
    counts = jnp.sum((flat_e[:, None] == jnp.arange(N_EXPERTS)[None, :]).astype(jnp.int32), axis=0)
    padded = (counts + MOE_BLOCK - 1) // MOE_BLOCK * MOE_BLOCK
    start = jnp.cumsum(counts) - counts
    ends_p = jnp.cumsum(padded)
    pstart = ends_p - padded
    block_e = jnp.minimum(jnp.searchsorted(ends_p, jnp.arange(MOE_BLOCKS) * MOE_BLOCK, side='right'),
                          N_EXPERTS - 1).astype(jnp.int32)
    row = jnp.arange(MOE_ROWS, dtype=jnp.int32)
    row_e = jnp.repeat(block_e, MOE_BLOCK)
    off = row - pstart[row_e]
    real = off < counts[row_e]
    assign = order[jnp.clip(start[row_e] + off, 0, N_ASSIGN - 1)]
    row_tok = jnp.where(real, assign // TOP_K, 0).astype(jnp.int32)
    row_gate = jnp.where(real, gates.reshape(N_ASSIGN)[assign], 0.0)
    pos = (pstart[flat_e] + rank - start[flat_e]).astype(jnp.int32)
    pos = jnp.transpose(pos.reshape(ROWS // MIX_TILE, MIX_TILE, TOP_K), (0, 2, 1)).reshape(N_ASSIGN)
    return row_tok, row_gate.reshape(MOE_ROWS, 1), block_e, pos


def kernel(x_prompt, x_sample, cache_kv_w128, cache_kv_w512, cache_kv_w2048, state_conv, state_ssm, norm_mix, w_in, gate_bias, conv_w, ssm_lam_re, ssm_lam_im, ssm_log_dt, ssm_b_re, ssm_b_im, ssm_c_re, ssm_c_im, ssm_d, ssm_w_glu, ssm_b_glu, proj_conv, proj_attn, proj_ssm, w_out, norm_ffn, ffn_w1, ffn_w3, ffn_w2, router_w, moe_w1, moe_w3, moe_w2, norm_final):
    bf = jnp.bfloat16
    caches = tuple(jnp.transpose(c, (0, 1, 3, 4, 5, 2)) for c in (cache_kv_w128, cache_kv_w512, cache_kv_w2048))
    tables = _rope_tables()
    x, h = _norm(x_prompt.reshape(P_ROWS, D_MODEL),
                 jnp.transpose(x_sample, (1, 0, 2)).reshape(S_ROWS, D_MODEL), norm_mix[0])
    kv_p = [[] for _ in range(N_ATTN_GROUPS)]
    kv_s = [[] for _ in range(N_ATTN_GROUPS)]
    conv_p_all, conv_s_all, ssm_p_all, ssm_s_all = [], [], [], []
    y = None
    for l in range(DEPTH):
        za = _linear(h, w_in, l, 0, ZA_WIDTH, jnp.float32, IN_TILE, f"in_proj_{l}")
        gates = _linear(h, w_in, l, ZA_WIDTH, GATE_WIDTH, bf, IN_TILE, f"gate_proj_{l}")
        qk = _rope(za, tables)
        conv_p, cst_p = _conv_prompt(za, conv_w[l])
        conv_s, cst_s = _conv_sample(za, conv_w[l], jnp.transpose(state_conv[l], (1, 0, 2)))
        by_seq = lambda a: jnp.transpose(a.reshape(DEC_SEQ, DEC_BATCH, a.shape[-1]), (1, 0, 2))
        qk_s = by_seq(qk[P_ROWS:])
        v_s = by_seq(za[P_ROWS:, OFF_V:OFF_V + ATTN_QKV])
        attn_p, attn_s = [], []
        for g in range(N_ATTN_GROUPS):
            attn_p += list(_attn_prompt(qk, za, g))
            attn_s += list(_attn_sample(qk_s, v_s, caches[g], l, g))
        tabs = _ssm_tables(ssm_lam_re[l], ssm_lam_im[l], ssm_log_dt[l], ssm_b_re[l], ssm_b_im[l],
                           ssm_c_re[l], ssm_c_im[l], ssm_d[l])
        ssm_p, ssm_s, sst_p, sst_s = _ssm(za, state_ssm[l], tabs)
        moe_layer = l % 2 == 1
        lp = dict(gate_bias=gate_bias[l].reshape(1, GATE_WIDTH), w_glu=ssm_w_glu[l].astype(bf),
                  b_glu=ssm_b_glu[l].reshape(1, SSM_WIDTH), proj_conv=proj_conv[l].astype(bf),
                  proj_attn=proj_attn[l].astype(bf), proj_ssm=proj_ssm[l].astype(bf),
                  w_out=w_out[l].astype(bf), norm_next=norm_ffn[l].reshape(1, D_MODEL))
        x, h = _mix(x, gates, conv_p, conv_s, attn_p, attn_s, ssm_p, ssm_s, lp, bf)
        i = l // 2
        if not moe_layer:
            zero_e = jnp.zeros((ROWS // ROW_TILE,), jnp.int32)
            hid = _ffn_up(zero_e, h, ffn_w1[i:i + 1], ffn_w3[i:i + 1], ROW_TILE, 512, "ffn_up")
            x, h = _ffn_down_dense(hid, ffn_w2[i].astype(bf), x, norm_mix[l + 1], D_FF // 4)
        else:
            hs, route = _router(x, norm_ffn[l], router_w[i])
            row_tok, row_gate, block_e, pos = _dispatch_plan(route)
            xs = _dispatch(row_tok, hs)
            hid = _ffn_up(block_e, xs, moe_w1[i].astype(bf), moe_w3[i].astype(bf),
                          MOE_BLOCK, D_FF_EXPERT // 2, "moe_up")
            yb = _moe_down(block_e, hid, moe_w2[i].astype(bf), row_gate)
            y = _combine(pos, x, yb, norm_final)
        for g, (kp, ks) in enumerate(zip(_kv_prompt(qk, za), _kv_sample(qk, za))):
            kp = kp.reshape(BATCH, 2, HEADS_PER_GROUP, HEAD_DIM, kp.shape[-1])
            kv_p[g].append(jnp.transpose(kp, (0, 4, 1, 2, 3)))
            ks = ks.reshape(DEC_SEQ, 2, HEADS_PER_GROUP, HEAD_DIM, DEC_BATCH)
            kv_s[g].append(jnp.transpose(ks, (4, 0, 1, 2, 3)))
        conv_p_all.append(cst_p)
        conv_s_all.append(jnp.transpose(cst_s, (1, 0, 2)))
        ssm_p_all.append(sst_p)
        ssm_s_all.append(sst_s)
    y_prompt = y[0].reshape(BATCH, SEQ, D_MODEL)
    y_sample = jnp.transpose(y[1].reshape(DEC_SEQ, DEC_BATCH, D_MODEL), (1, 0, 2))
    return (y_prompt, y_sample, jnp.stack(kv_p[0]), jnp.stack(kv_p[1]), jnp.stack(kv_p[2]),
            jnp.stack(conv_p_all), jnp.stack(ssm_p_all), jnp.stack(kv_s[0]), jnp.stack(kv_s[1]),
            jnp.stack(kv_s[2]), jnp.stack(conv_s_all), jnp.stack(ssm_s_all))
```

```python
import functools
import math

import jax
import jax.numpy as jnp
import numpy as np
from jax import lax
from jax.experimental import pallas as pl
from jax.experimental.pallas import tpu as pltpu

D_MODEL = 2048
BATCH = 4
SEQ = 2048
DEPTH = 2
DEC_BATCH = 128
DEC_SEQ = 4
PAST_LEN = 2048
CONV_WIDTH = 512
CONV_K = 3
HEAD_DIM = 64
HEADS_PER_GROUP = 4
ATTN_GROUPS = ((128, 1), (512, 4), (2048, 16))
N_ATTN_GROUPS = 3
ATTN_QKV = N_ATTN_GROUPS * HEADS_PER_GROUP * HEAD_DIM
ATTN_OUT = HEADS_PER_GROUP * HEAD_DIM
ROT_DIM = HEAD_DIM // 4
ROPE_THETA = 500000.0
SSM_WIDTH = 768
SSM_CH_PER_GROUP = 16
SSM_GROUPS = SSM_WIDTH // SSM_CH_PER_GROUP
SSM_STATE = 64
N_BRANCH = 3
D_FF = 5632
N_EXPERTS = 8
TOP_K = 2
D_FF_EXPERT = D_FF // TOP_K
EPS = 1e-6

V7X_LANES = 128
V7X_VMEM_BYTES = 64 * 1024 * 1024
MIB = 1024 * 1024

P_ROWS = BATCH * SEQ
S_ROWS = DEC_SEQ * DEC_BATCH
ROWS = P_ROWS + S_ROWS
ZA_WIDTH = 3 * CONV_WIDTH + 3 * ATTN_QKV + SSM_WIDTH
GATE_WIDTH = N_BRANCH * D_MODEL
QK_WIDTH = 2 * ATTN_QKV
OFF_Q = 3 * CONV_WIDTH
OFF_V = OFF_Q + 2 * ATTN_QKV
OFF_U = OFF_V + ATTN_QKV
BAND = 128
ATTN_DENSE_BLOCKS = 4
ATTN_RESIDUE_UNROLL = 4
SSM_CHUNK = 16
SSM_GB = 16
N_GB = SSM_GROUPS // SSM_GB
GB_CH = SSM_GB * SSM_CH_PER_GROUP
GB_ST = SSM_GB * SSM_STATE
ST_WIDTH = N_GB * 2 * GB_ST
ROW_TILE = 512
MIX_TILE = 256
IN_TILE = 1536
MOE_BLOCK = 256
N_ASSIGN = ROWS * TOP_K
MOE_BLOCKS = -(-(N_ASSIGN + N_EXPERTS * (MOE_BLOCK - 1)) // MOE_BLOCK)
MOE_ROWS = MOE_BLOCKS * MOE_BLOCK
ROUTER_PAD = V7X_LANES
SAMPLE_SEQ_TILE = 8
SAMPLE_CACHE_TILE_BYTES = 8 * MIB
SSM_BATCH_SPLIT = 2
HALVES = ATTN_OUT // V7X_LANES
ROW_CHUNKS = D_MODEL // V7X_LANES
assert DEPTH == 2 and HALVES == 2 and GB_CH == HALVES * V7X_LANES

assert ROWS % ROW_TILE == 0 and P_ROWS % ROW_TILE == 0
assert ROWS % MIX_TILE == 0 and P_ROWS % MIX_TILE == 0


def _params(semantics, vmem_bytes):
    return pltpu.CompilerParams(dimension_semantics=semantics,
                                vmem_limit_bytes=min(int(vmem_bytes), V7X_VMEM_BYTES - 4 * MIB))


def _rms(x, g):
    y = x * lax.rsqrt(jnp.mean(x * x, axis=-1, keepdims=True) + EPS)
    return y * g


def _bdot(a, b):
    return jnp.dot(a.astype(jnp.bfloat16), b.astype(jnp.bfloat16), preferred_element_type=jnp.float32)


def _store_chunk_rows(ref, x):
    n = x.shape[0]
    for k in range(ROW_CHUNKS):
        ref[pl.ds(k, n, stride=ROW_CHUNKS), :] = x[:, k * V7X_LANES:(k + 1) * V7X_LANES]


def _load_chunk_rows(ref, first, n):
    return jnp.concatenate([ref[pl.ds(first * ROW_CHUNKS + k, n, stride=ROW_CHUNKS), :]
                            for k in range(ROW_CHUNKS)], axis=-1)


def _sigmoid(x):
    return 0.5 * jnp.tanh(0.5 * x) + 0.5


def _norm_kernel(xp_ref, xs_ref, g_ref, x_ref, h_ref):
    x = jnp.where(pl.program_id(0) < P_ROWS // ROW_TILE, xp_ref[...], xs_ref[...])
    x_ref[...] = x
    h_ref[...] = _rms(x, g_ref[...]).astype(h_ref.dtype)


def _norm(x_p, x_s, g):
    n_p = P_ROWS // ROW_TILE
    row = pl.BlockSpec((ROW_TILE, D_MODEL), lambda i: (i, 0))
    return pl.pallas_call(
        _norm_kernel,
        grid=(ROWS // ROW_TILE,),
        in_specs=[pl.BlockSpec((ROW_TILE, D_MODEL), lambda i: (jnp.minimum(i, n_p - 1), 0)),
                  pl.BlockSpec((ROW_TILE, D_MODEL), lambda i: (jnp.maximum(i - n_p, 0), 0)),
                  pl.BlockSpec((1, D_MODEL), lambda i: (0, 0))],
        out_specs=[row, row],
        out_shape=[jax.ShapeDtypeStruct((ROWS, D_MODEL), jnp.float32),
                   jax.ShapeDtypeStruct((ROWS, D_MODEL), jnp.bfloat16)],
        compiler_params=_params(("parallel",), 40 * MIB),
        name="norm",
    )(x_p, x_s, g.reshape(1, D_MODEL))


def _linear_kernel(a_ref, w_ref, o_ref, wb_ref):
    @pl.when(pl.program_id(1) == 0)
    def _():
        wb_ref[...] = w_ref[...].astype(wb_ref.dtype)

    o_ref[...] = jnp.dot(a_ref[...], wb_ref[...], preferred_element_type=jnp.float32).astype(o_ref.dtype)


def _linear(a, w, layer, col0, n, out_dtype, bn, name):
    m, k = a.shape
    jb = col0 // bn
    return pl.pallas_call(
        _linear_kernel,
        grid=(n // bn, m // ROW_TILE),
        in_specs=[pl.BlockSpec((ROW_TILE, k), lambda j, i: (i, 0)),
                  pl.BlockSpec((None, k, bn), lambda j, i: (layer, 0, jb + j))],
        out_specs=pl.BlockSpec((ROW_TILE, bn), lambda j, i: (i, j)),
        out_shape=jax.ShapeDtypeStruct((m, n), out_dtype),
        scratch_shapes=[pltpu.VMEM((k, bn), jnp.bfloat16)],
        compiler_params=_params(("parallel", "arbitrary"), 56 * MIB),
        name=name,
    )(a, w)


def _rope_tables():
    half = ROT_DIM // 2
    inv = ROPE_THETA ** (-jnp.arange(half, dtype=jnp.float32) * 2.0 / ROT_DIM)
    pos_p = jnp.arange(SEQ, dtype=jnp.int32)
    pos_s = jnp.repeat(PAST_LEN + jnp.arange(DEC_SEQ, dtype=jnp.int32), DEC_BATCH)
    pos = jnp.concatenate([pos_p, pos_s]).astype(jnp.float32)
    ang = pos[:, None] * inv[None, :]
    cos, sin = jnp.cos(ang), jnp.sin(ang)
    ones = jnp.ones((pos.shape[0], HEAD_DIM - ROT_DIM), jnp.float32)
    zeros = jnp.zeros_like(ones)
    zh = jnp.zeros_like(sin)
    cos_t = jnp.concatenate([cos, cos, ones], axis=1)
    sin_hi = jnp.concatenate([-sin, zh, zeros], axis=1)
    sin_lo = jnp.concatenate([zh, sin, zeros], axis=1)
    reps = V7X_LANES // HEAD_DIM
    return tuple(jnp.tile(t, (1, reps)) for t in (cos_t, sin_hi, sin_lo))


def _rope_kernel(x_ref, c_ref, sh_ref, sl_ref, o_ref):
    x = x_ref[...]
    half = ROT_DIM // 2
    reps = QK_WIDTH // V7X_LANES
    c = jnp.tile(c_ref[...], (1, reps))
    sh = jnp.tile(sh_ref[...], (1, reps))
    sl = jnp.tile(sl_ref[...], (1, reps))
    x_up = pltpu.roll(x, QK_WIDTH - half, axis=1)
    x_dn = pltpu.roll(x, half, axis=1)
    o_ref[...] = x * c + x_up * sh + x_dn * sl


def _rope(za, tables):
    n_p = P_ROWS // ROW_TILE
    per_seq = SEQ // ROW_TILE
    tmap = lambda i: (jnp.where(i < n_p, i % per_seq, per_seq), 0)
    tspec = pl.BlockSpec((ROW_TILE, V7X_LANES), tmap)
    return pl.pallas_call(
        _rope_kernel,
        grid=(ROWS // ROW_TILE,),
        in_specs=[pl.BlockSpec((ROW_TILE, QK_WIDTH), lambda i: (i, OFF_Q // QK_WIDTH)), tspec, tspec, tspec],
        out_specs=pl.BlockSpec((ROW_TILE, QK_WIDTH), lambda i: (i, 0)),
        out_shape=jax.ShapeDtypeStruct((ROWS, QK_WIDTH), jnp.float32),
        compiler_params=_params(("parallel",), 40 * MIB),
        name="rope",
    )(za, *tables)


def _kv_prompt_kernel(k_ref, v_ref, *o_refs, keeps):
    last = pl.program_id(1) == SEQ // ROW_TILE - 1
    k, v = k_ref[...], v_ref[...]

    def write(g, lo):
        cols = slice(g * ATTN_OUT, (g + 1) * ATTN_OUT)
        o_refs[g][0] = k[lo:, cols].T
        o_refs[g][1] = v[lo:, cols].T

    for g, keep in enumerate(keeps):
        if keep == SEQ:
            write(g, 0)
        else:
            pl.when(last)(functools.partial(write, g, ROW_TILE - min(keep, ROW_TILE)))


def _kv_prompt(qk, za):
    keeps = tuple(min(w, SEQ) for w, _ in ATTN_GROUPS)
    assert all(k == SEQ or k <= ROW_TILE for k in keeps)
    n_t = SEQ // ROW_TILE
    out_specs, out_shape = [], []
    for keep in keeps:
        blk = min(keep, ROW_TILE)
        imap = (lambda b, j: (b, 0, 0, j)) if keep == SEQ else (lambda b, j: (b, 0, 0, 0))
        out_specs.append(pl.BlockSpec((None, 2, ATTN_OUT, blk), imap))
        out_shape.append(jax.ShapeDtypeStruct((BATCH, 2, ATTN_OUT, keep), jnp.float32))
    return pl.pallas_call(
        functools.partial(_kv_prompt_kernel, keeps=keeps),
        grid=(BATCH, n_t),
        in_specs=[pl.BlockSpec((ROW_TILE, ATTN_QKV), lambda b, j: (b * n_t + j, 1)),
                  pl.BlockSpec((ROW_TILE, ATTN_QKV), lambda b, j: (b * n_t + j, OFF_V // ATTN_QKV))],
        out_specs=out_specs,
        out_shape=out_shape,
        compiler_params=_params(("parallel", "arbitrary"), 32 * MIB),
        name="kv_prompt",
    )(qk, za)


def _kv_sample_kernel(k_ref, v_ref, *o_refs):
    for t in range(DEC_SEQ):
        rows = slice(t * DEC_BATCH, (t + 1) * DEC_BATCH)
        for g in range(N_ATTN_GROUPS):
            cols = slice(g * ATTN_OUT, (g + 1) * ATTN_OUT)
            o_refs[g][t, 0] = k_ref[rows, cols].T
            o_refs[g][t, 1] = v_ref[rows, cols].T


def _kv_sample(qk, za):
    blk = P_ROWS // S_ROWS
    shape = (DEC_SEQ, 2, ATTN_OUT, DEC_BATCH)
    return pl.pallas_call(
        _kv_sample_kernel,
        grid=(1,),
        in_specs=[pl.BlockSpec((S_ROWS, ATTN_QKV), lambda i: (blk, 1)),
                  pl.BlockSpec((S_ROWS, ATTN_QKV), lambda i: (blk, OFF_V // ATTN_QKV))],
        out_specs=[pl.BlockSpec(shape, lambda i: (0, 0, 0, 0))] * N_ATTN_GROUPS,
        out_shape=[jax.ShapeDtypeStruct(shape, jnp.float32)] * N_ATTN_GROUPS,
        compiler_params=_params(("arbitrary",), 32 * MIB),
        name="kv_sample",
    )(qk, za)


def _conv_prompt_kernel(h_ref, b_ref, c_ref, w_ref, y_ref, st_ref):
    u = c_ref[...] * h_ref[...]
    w = w_ref[...]
    row = lax.broadcasted_iota(jnp.int32, u.shape, 0)
    u1 = jnp.where(row >= 1, pltpu.roll(u, 1, axis=0), 0.0)
    u2 = jnp.where(row >= 2, pltpu.roll(u, 2, axis=0), 0.0)
    y = w[0:1] * u2 + w[1:2] * u1 + w[2:3] * u
    y_ref[...] = (b_ref[...] * y).astype(y_ref.dtype)
    st_ref[...] = u[SEQ - (CONV_K - 1):, :]


def _conv_prompt(za, w):
    col = lambda c: pl.BlockSpec((SEQ, CONV_WIDTH), lambda b, c=c: (b, c))
    return pl.pallas_call(
        _conv_prompt_kernel,
        grid=(BATCH,),
        in_specs=[col(0), col(1), col(2), pl.BlockSpec((CONV_K, CONV_WIDTH), lambda b: (0, 0))],
        out_specs=[pl.BlockSpec((SEQ, CONV_WIDTH), lambda b: (b, 0)),
                   pl.BlockSpec((None, CONV_K - 1, CONV_WIDTH), lambda b: (b, 0, 0))],
        out_shape=[jax.ShapeDtypeStruct((P_ROWS, CONV_WIDTH), jnp.bfloat16),
                   jax.ShapeDtypeStruct((BATCH, CONV_K - 1, CONV_WIDTH), jnp.float32)],
        compiler_params=_params(("parallel",), 48 * MIB),
        name="conv_prompt",
    )(za, za, za, w)


def _conv_sample_kernel(h_ref, b_ref, c_ref, w_ref, s_ref, y_ref, st_ref):
    u = c_ref[...] * h_ref[...]
    w = w_ref[...]
    s0, s1 = s_ref[0], s_ref[1]
    keep = S_ROWS - DEC_BATCH
    u1 = jnp.concatenate([s1, u[:keep]], axis=0)
    u2 = jnp.concatenate([s0, s1, u[:keep - DEC_BATCH]], axis=0)
    y = w[0:1] * u2 + w[1:2] * u1 + w[2:3] * u
    y_ref[...] = (b_ref[...] * y).astype(y_ref.dtype)
    st_ref[0] = u[S_ROWS - 2 * DEC_BATCH:S_ROWS - DEC_BATCH]
    st_ref[1] = u[S_ROWS - DEC_BATCH:]


def _conv_sample(za, w, state):
    blk = P_ROWS // S_ROWS
    col = lambda c: pl.BlockSpec((S_ROWS, CONV_WIDTH), lambda i, c=c: (blk, c))
    st_spec = pl.BlockSpec((CONV_K - 1, DEC_BATCH, CONV_WIDTH), lambda i: (0, 0, 0))
    return pl.pallas_call(
        _conv_sample_kernel,
        grid=(1,),
        in_specs=[col(0), col(1), col(2), pl.BlockSpec((CONV_K, CONV_WIDTH), lambda i: (0, 0)), st_spec],
        out_specs=[pl.BlockSpec((S_ROWS, CONV_WIDTH), lambda i: (0, 0)), st_spec],
        out_shape=[jax.ShapeDtypeStruct((S_ROWS, CONV_WIDTH), jnp.bfloat16),
                   jax.ShapeDtypeStruct((CONV_K - 1, DEC_BATCH, CONV_WIDTH), jnp.float32)],
        compiler_params=_params(("arbitrary",), 24 * MIB),
        name="conv_sample",
    )(za, za, za, w, state)


def _attn_prompt_kernel(*refs, dil, has_prev, sub):
    n_in = (5 if has_prev else 3) * HALVES
    q_refs, kc_refs, vc_refs = (refs[i * HALVES:(i + 1) * HALVES] for i in range(3))
    if has_prev:
        kp_refs, vp_refs = (refs[i * HALVES:(i + 1) * HALVES] for i in range(3, 5))
    o_ref, l_ref = refs[n_in:]
    n_keys = (2 if has_prev else 1) * BAND
    qi = lax.broadcasted_iota(jnp.int32, (BAND, n_keys), 0)
    kj = lax.broadcasted_iota(jnp.int32, (BAND, n_keys), 1)
    if has_prev:
        dist = qi + BAND - kj
        inner_ok = (dist >= 0) & (dist <= BAND)
        first_ok = inner_ok & ((kj >= BAND) | (pl.program_id(1) > 0))
    else:
        first_ok = kj <= qi
    heads_per_half = V7X_LANES // HEAD_DIM

    def attend(q, k, v, valid):
        q, k, v = (t.astype(jnp.bfloat16) for t in (q, k, v))
        outs, lses = [], []
        for h in range(heads_per_half):
            sl = slice(h * HEAD_DIM, (h + 1) * HEAD_DIM)
            s = lax.dot_general(q[:, sl], k[:, sl], (((1,), (1,)), ((), ())),
                                preferred_element_type=jnp.float32) * (HEAD_DIM ** -0.5)
            s = jnp.where(valid, s, -jnp.inf)
            m = jnp.max(s, axis=-1, keepdims=True)
            p = jnp.exp(s - m)
            den = jnp.sum(p, axis=-1, keepdims=True)
            outs.append(jnp.dot(p.astype(jnp.bfloat16), v[:, sl], preferred_element_type=jnp.float32) / den)
            lses.append(jnp.broadcast_to(m + jnp.log(den), (BAND, HEAD_DIM)))
        return jnp.concatenate(outs, axis=-1), jnp.concatenate(lses, axis=-1)

    if dil == 1:
        for half in range(HALVES):
            q, kc, vc = q_refs[half][...], kc_refs[half][...], vc_refs[half][...]
            kp, vp = kp_refs[half][...], vp_refs[half][...]
            for s in range(sub):
                lo, hi = (s - 1) * BAND, (s + 1) * BAND
                k = jnp.concatenate([kp, kc[:BAND]], axis=0) if s == 0 else kc[lo:hi]
                v = jnp.concatenate([vp, vc[:BAND]], axis=0) if s == 0 else vc[lo:hi]
                o, l = attend(q[s * BAND:hi], k, v, first_ok if s == 0 else inner_ok)
                o_ref[half, s * BAND:hi, :] = o
                l_ref[half, s * BAND:hi, :] = l
        return

    def residue(r, carry):
        rows = pl.ds(r, BAND, stride=dil)
        for half in range(HALVES):
            k = kc_refs[half][rows, :]
            v = vc_refs[half][rows, :]
            if has_prev:
                k = jnp.concatenate([kp_refs[half][rows, :], k], axis=0)
                v = jnp.concatenate([vp_refs[half][rows, :], v], axis=0)
            o, l = attend(q_refs[half][rows, :], k, v, first_ok)
            o_ref[half, rows, :] = o
            l_ref[half, rows, :] = l
        return carry

    lax.fori_loop(0, dil, residue, 0, unroll=min(dil, ATTN_RESIDUE_UNROLL))


def _attn_prompt(qk, za, g):
    d = ATTN_GROUPS[g][1]
    sub = ATTN_DENSE_BLOCKS if d == 1 else 1
    rows = BAND * d * sub
    nb = SEQ // rows
    has_prev = nb > 1
    prev_rows = BAND if d == 1 else rows
    qcol, kcol, vcol = (off // V7X_LANES + g * HALVES for off in (0, ATTN_QKV, OFF_V))
    cur = lambda b, j: b * nb + j
    prev = lambda b, j: b * (SEQ // prev_rows) + jnp.maximum(j * (rows // prev_rows) - 1, 0)
    spec = lambda n, row, col: [pl.BlockSpec((n, V7X_LANES), lambda b, j, h=h: (row(b, j), col + h))
                                for h in range(HALVES)]
    in_specs = spec(rows, cur, qcol) + spec(rows, cur, kcol) + spec(rows, cur, vcol)
    args = [qk] * (2 * HALVES) + [za] * HALVES
    if has_prev:
        in_specs += spec(prev_rows, prev, kcol) + spec(prev_rows, prev, vcol)
        args += [qk] * HALVES + [za] * HALVES
    out_spec = pl.BlockSpec((HALVES, rows, V7X_LANES), lambda b, j: (0, cur(b, j), 0))
    shape = jax.ShapeDtypeStruct((HALVES, P_ROWS, V7X_LANES), jnp.float32)
    return pl.pallas_call(
        functools.partial(_attn_prompt_kernel, dil=d, has_prev=has_prev, sub=sub),
        grid=(BATCH, nb),
        in_specs=in_specs,
        out_specs=[out_spec, out_spec],
        out_shape=[shape, shape],
        compiler_params=_params(("parallel", "parallel"), 40 * MIB),
        name=f"attn_prompt_g{g}",
    )(*args)


def _attn_sample_kernel(q_ref, kn_ref, vn_ref, c_ref, o_ref, l_ref, *, dil, nseq):
    width = c_ref.shape[-1]
    scale = HEAD_DIM ** -0.5
    nt = (((1,), (1,)), ((), ()))
    n_rows = HEADS_PER_GROUP * DEC_SEQ
    row_l = lax.broadcasted_iota(jnp.int32, (n_rows, ATTN_OUT), 0)
    lane = lax.broadcasted_iota(jnp.int32, (n_rows, ATTN_OUT), 1)
    own_head = lane // HEAD_DIM == row_l // DEC_SEQ
    t_c = lax.broadcasted_iota(jnp.int32, (n_rows, width), 0) % DEC_SEQ
    w_c = lax.broadcasted_iota(jnp.int32, (n_rows, width), 1)
    ok_c = (w_c >= t_c) & (w_c % dil == t_c % dil)
    r_n = lax.broadcasted_iota(jnp.int32, (n_rows, n_rows), 0)
    c_n = lax.broadcasted_iota(jnp.int32, (n_rows, n_rows), 1)
    t_n, s_n = r_n % DEC_SEQ, c_n % DEC_SEQ
    ok_n = (r_n // DEC_SEQ == c_n // DEC_SEQ) & (s_n <= t_n) & (s_n % dil == t_n % dil)

    def per_head(x):
        return jnp.where(own_head, jnp.tile(x, (HEADS_PER_GROUP, 1)), 0.0).astype(jnp.bfloat16)

    def one_seq(s, carry):
        q, kn, vn = per_head(q_ref[s]), per_head(kn_ref[s]), per_head(vn_ref[s])
        k_t = c_ref[s, 0].reshape(ATTN_OUT, width).astype(jnp.bfloat16)
        v_t = c_ref[s, 1].reshape(ATTN_OUT, width).astype(jnp.bfloat16)
        sc = jnp.dot(q, k_t, preferred_element_type=jnp.float32) * scale
        sc = jnp.where(ok_c, sc, -jnp.inf)
        sn = lax.dot_general(q, kn, nt, preferred_element_type=jnp.float32) * scale
        sn = jnp.where(ok_n, sn, -jnp.inf)
        m = jnp.maximum(jnp.max(sc, axis=-1, keepdims=True), jnp.max(sn, axis=-1, keepdims=True))
        pc = jnp.exp(sc - m)
        pn = jnp.exp(sn - m)
        den = jnp.sum(pc, axis=-1, keepdims=True) + jnp.sum(pn, axis=-1, keepdims=True)
        o = (lax.dot_general(pc.astype(jnp.bfloat16), v_t, nt, preferred_element_type=jnp.float32)
             + jnp.dot(pn.astype(jnp.bfloat16), vn, preferred_element_type=jnp.float32))
        o = jnp.where(own_head, o / den, 0.0)
        lse = jnp.where(own_head, m + jnp.log(den), 0.0)
        fold = lambda a: sum(a[h * DEC_SEQ:(h + 1) * DEC_SEQ] for h in range(HEADS_PER_GROUP))
        o_ref[s] = fold(o)
        l_ref[s] = fold(lse)
        return carry

    lax.fori_loop(0, nseq, one_seq, 0, unroll=True)


def _attn_sample(qk_s, v_s, cache_t, layer, g):
    d = ATTN_GROUPS[g][1]
    width = cache_t.shape[-1]
    seq_bytes = 2 * ATTN_OUT * width * 4
    nseq = max(1, min(SAMPLE_SEQ_TILE, SAMPLE_CACHE_TILE_BYTES // seq_bytes))
    blk = (nseq, DEC_SEQ, ATTN_OUT)
    koff = ATTN_QKV // ATTN_OUT
    out_spec = pl.BlockSpec(blk, lambda i: (i, 0, 0))
    shape = jax.ShapeDtypeStruct((DEC_BATCH, DEC_SEQ, ATTN_OUT), jnp.float32)
    o, l = pl.pallas_call(
        functools.partial(_attn_sample_kernel, dil=d, nseq=nseq),
        grid=(DEC_BATCH // nseq,),
        in_specs=[pl.BlockSpec(blk, lambda i: (i, 0, g)),
                  pl.BlockSpec(blk, lambda i: (i, 0, koff + g)),
                  pl.BlockSpec(blk, lambda i: (i, 0, g)),
                  pl.BlockSpec((None, nseq, 2, HEADS_PER_GROUP, HEAD_DIM, width),
                               lambda i: (layer, i, 0, 0, 0, 0))],
        out_specs=[out_spec, out_spec],
        out_shape=[shape, shape],
        compiler_params=_params(("parallel",), 4 * SAMPLE_CACHE_TILE_BYTES),
        name=f"attn_sample_g{g}",
    )(qk_s, qk_s, v_s, cache_t)
    to_rows = lambda a: jnp.transpose(a, (1, 0, 2)).reshape(S_ROWS, ATTN_OUT)
    return to_rows(o), to_rows(l)


def _ssm_tables(lam_re, lam_im, log_dt, b_re, b_im, c_re, c_im, d_skip):
    dt = jnp.exp(log_dt)[:, None]
    er = jnp.exp(lam_re * dt)
    lbr, lbi = er * jnp.cos(lam_im * dt), er * jnp.sin(lam_im * dt)
    den = lam_re * lam_re + lam_im * lam_im
    nr, ni = lbr - 1.0, lbi
    qr = (nr * lam_re + ni * lam_im) / den
    qi = (ni * lam_re - nr * lam_im) / den
    bbr = qr[:, :, None] * b_re - qi[:, :, None] * b_im
    bbi = qr[:, :, None] * b_im + qi[:, :, None] * b_re
    eye = jnp.eye(SSM_GB, dtype=jnp.float32)

    def in_block(b):
        b = b.reshape(N_GB, SSM_GB, SSM_STATE, SSM_CH_PER_GROUP)
        return jnp.einsum('ngpc,gh->ngchp', b, eye).reshape(N_GB, GB_CH, GB_ST)

    def out_block(c):
        c = c.reshape(N_GB, SSM_GB, SSM_CH_PER_GROUP, SSM_STATE)
        return jnp.einsum('ngcp,gh->ngphc', c, eye).reshape(N_GB, GB_ST, GB_CH)

    b_blk = jnp.concatenate([in_block(bbr), in_block(bbi)], axis=2).astype(jnp.bfloat16)
    c_blk = jnp.concatenate([out_block(c_re), out_block(-c_im)], axis=1).astype(jnp.bfloat16)

    def pack(re, im):
        re = re.reshape(N_GB, 1, GB_ST)
        im = im.reshape(N_GB, 1, GB_ST)
        return jnp.concatenate([re, re], axis=2), jnp.concatenate([-im, im], axis=2)

    pr, pi = lbr, lbi
    for _ in range(int(math.log2(SSM_CHUNK))):
        pr, pi = pr * pr - pi * pi, 2.0 * pr * pi
    return dict(b=b_blk, c=c_blk, lam=pack(lbr, lbi), lam_chunk=pack(pr, pi), d=d_skip.reshape(1, SSM_WIDTH))


def _cstep(s, a, bsw):
    swapped = jnp.concatenate([s[:, GB_ST:], s[:, :GB_ST]], axis=1)
    return a * s + bsw * swapped


def _ssm_kernel(u_ref, s0_ref, b_ref, c_ref, a_ref, bsw_ref, d_ref, *refs, steps, emit_y):
    if emit_y:
        y_ref, sout_ref, s_ref = refs
    else:
        sout_ref, s_ref = refs
    t = pl.program_id(1)

    @pl.when(t == 0)
    def _():
        s_ref[...] = s0_ref[...]

    u = u_ref[...]
    s = _cstep(s_ref[...], a_ref[...], bsw_ref[...]) + _bdot(u, b_ref[...])
    s_ref[...] = s
    if emit_y:
        y_ref[...] = _bdot(s, c_ref[...]) + d_ref[...] * u

    @pl.when(t == steps - 1)
    def _():
        sout_ref[...] = s


def _ssm_pass(u_arr, u_map, y_map, rows, steps, s0, tabs, y_shape, name):
    emit_y = y_shape is not None
    st_spec = pl.BlockSpec((rows, 2 * GB_ST), lambda gb, t: (0, gb))
    coef = pl.BlockSpec((None, 1, 2 * GB_ST), lambda gb, t: (gb, 0, 0))
    in_specs = [pl.BlockSpec((rows, GB_CH), u_map), st_spec,
                pl.BlockSpec((None, GB_CH, 2 * GB_ST), lambda gb, t: (gb, 0, 0)),
                pl.BlockSpec((None, 2 * GB_ST, GB_CH), lambda gb, t: (gb, 0, 0)),
                coef, coef,
                pl.BlockSpec((1, GB_CH), lambda gb, t: (0, gb))]
    st_shape = jax.ShapeDtypeStruct((rows, ST_WIDTH), jnp.float32)
    out_specs, out_shape = [st_spec], [st_shape]
    if emit_y:
        out_specs, out_shape = [pl.BlockSpec((rows, GB_CH), y_map), st_spec], [y_shape, st_shape]
    return pl.pallas_call(
        functools.partial(_ssm_kernel, steps=steps, emit_y=emit_y),
        grid=(N_GB, steps),
        in_specs=in_specs,
        out_specs=out_specs,
        out_shape=out_shape,
        scratch_shapes=[pltpu.VMEM((rows, 2 * GB_ST), jnp.float32)],
        compiler_params=_params(("parallel", "arbitrary"), 48 * MIB),
        name=name,
    )(u_arr, s0, tabs['b'], tabs['c'], tabs['lam'][0], tabs['lam'][1], tabs['d'])


def _ssm_prompt_kernel(*refs, emit_y):
    u_refs = refs[:HALVES]
    s0_ref, b_ref, c_ref, a_ref, bsw_ref, d_ref = refs[HALVES:HALVES + 6]
    if emit_y:
        y_ref, sout_ref, s_ref = refs[HALVES + 6:]
    else:
        sout_ref, s_ref = refs[HALVES + 6:]
    t = pl.program_id(2)
    rows = pl.ds(t, s_ref.shape[0], stride=SSM_CHUNK)

    @pl.when(t == 0)
    def _():
        s_ref[...] = s0_ref[...]

    u = jnp.concatenate([r[rows, :] for r in u_refs], axis=-1)
    s = _cstep(s_ref[...], a_ref[...], bsw_ref[...]) + _bdot(u, b_ref[...])
    s_ref[...] = s
    if emit_y:
        y = _bdot(s, c_ref[...]) + d_ref[...] * u
        for half in range(HALVES):
            y_ref[half, rows, :] = y[:, half * V7X_LANES:(half + 1) * V7X_LANES]

    @pl.when(t == SSM_CHUNK - 1)
    def _():
        sout_ref[...] = s


def _ssm_prompt_pass(za, s0, tabs, emit_y, name):
    n_rows = P_ROWS // SSM_CHUNK // SSM_BATCH_SPLIT
    nat_rows = P_ROWS // SSM_BATCH_SPLIT
    ucol = OFF_U // V7X_LANES
    u_specs = [pl.BlockSpec((nat_rows, V7X_LANES), lambda gb, p, t, h=h: (p, ucol + gb * HALVES + h))
               for h in range(HALVES)]
    st_spec = pl.BlockSpec((n_rows, 2 * GB_ST), lambda gb, p, t: (p, gb))
    coef = pl.BlockSpec((None, 1, 2 * GB_ST), lambda gb, p, t: (gb, 0, 0))
    in_specs = u_specs + [st_spec,
                          pl.BlockSpec((None, GB_CH, 2 * GB_ST), lambda gb, p, t: (gb, 0, 0)),
                          pl.BlockSpec((None, 2 * GB_ST, GB_CH), lambda gb, p, t: (gb, 0, 0)),
                          coef, coef,
                          pl.BlockSpec((1, GB_CH), lambda gb, p, t: (0, gb))]
    st_shape = jax.ShapeDtypeStruct((P_ROWS // SSM_CHUNK, ST_WIDTH), jnp.float32)
    out_specs, out_shape = [st_spec], [st_shape]
    if emit_y:
        out_specs = [pl.BlockSpec((HALVES, nat_rows, V7X_LANES), lambda gb, p, t: (gb, p, 0)), st_spec]
        out_shape = [jax.ShapeDtypeStruct((N_GB * HALVES, P_ROWS, V7X_LANES), jnp.float32), st_shape]
    return pl.pallas_call(
        functools.partial(_ssm_prompt_kernel, emit_y=emit_y),
        grid=(N_GB, SSM_BATCH_SPLIT, SSM_CHUNK),
        in_specs=in_specs,
        out_specs=out_specs,
        out_shape=out_shape,
        scratch_shapes=[pltpu.VMEM((n_rows, 2 * GB_ST), jnp.float32)],
        compiler_params=_params(("parallel", "parallel", "arbitrary"), 48 * MIB),
        name=name,
    )(*([za] * HALVES), s0, tabs['b'], tabs['c'], tabs['lam'][0], tabs['lam'][1], tabs['d'])


def _ssm_carry_kernel(e_ref, a_ref, bsw_ref, sin_ref, fin_ref):
    a = a_ref[...]
    bsw = bsw_ref[...]
    n_chunks = SEQ // SSM_CHUNK

    def body(n, states):
        new = []
        for b in range(BATCH):
            row = b * n_chunks + n
            sin_ref[pl.ds(row, 1), :] = states[b]
            new.append(_cstep(states[b], a, bsw) + e_ref[pl.ds(row, 1), :])
        return tuple(new)

    zero = jnp.zeros((1, 2 * GB_ST), jnp.float32)
    final = lax.fori_loop(0, n_chunks, body, (zero,) * BATCH)
    for b in range(BATCH):
        fin_ref[pl.ds(b, 1), :] = final[b]


def _ssm_carry(e, tabs):
    n_rows = P_ROWS // SSM_CHUNK
    st_spec = lambda rows: pl.BlockSpec((rows, 2 * GB_ST), lambda gb: (0, gb))
    coef = pl.BlockSpec((None, 1, 2 * GB_ST), lambda gb: (gb, 0, 0))
    return pl.pallas_call(
        _ssm_carry_kernel,
        grid=(N_GB,),
        in_specs=[st_spec(n_rows), coef, coef],
        out_specs=[st_spec(n_rows), st_spec(BATCH)],
        out_shape=[jax.ShapeDtypeStruct((n_rows, ST_WIDTH), jnp.float32),
                   jax.ShapeDtypeStruct((BATCH, ST_WIDTH), jnp.float32)],
        compiler_params=_params(("parallel",), 32 * MIB),
        name="ssm_carry",
    )(e, tabs['lam_chunk'][0], tabs['lam_chunk'][1])


def _pack_state(s):
    b = s.shape[0]
    s = s.reshape(b, N_GB, SSM_GB * SSM_STATE, 2)
    return jnp.transpose(s, (0, 1, 3, 2)).reshape(b, ST_WIDTH)


def _unpack_state(s):
    b = s.shape[0]
    s = s.reshape(b, N_GB, 2, SSM_GB * SSM_STATE)
    return jnp.transpose(s, (0, 1, 3, 2)).reshape(b, SSM_GROUPS, SSM_STATE, 2)


def _ssm(za, state_s, tabs):
    ucol = OFF_U // GB_CH
    zero = jnp.zeros((P_ROWS // SSM_CHUNK, ST_WIDTH), jnp.float32)
    (e,) = _ssm_prompt_pass(za, zero, tabs, False, "ssm_prompt_local")
    s_in, s_fin = _ssm_carry(e, tabs)
    y_p, _ = _ssm_prompt_pass(za, s_in, tabs, True, "ssm_prompt")
    tblk = P_ROWS // DEC_BATCH
    ys_shape = jax.ShapeDtypeStruct((S_ROWS, SSM_WIDTH), jnp.float32)
    y_s, s_new = _ssm_pass(za, lambda gb, t: (tblk + t, ucol + gb), lambda gb, t: (t, gb), DEC_BATCH, DEC_SEQ,
                           _pack_state(state_s), tabs, ys_shape, "ssm_sample")
    return y_p, y_s, _unpack_state(s_fin), _unpack_state(s_new)


def _mix_kernel(x_ref, gate_ref, gb_ref, cp_ref, cs_ref, ap_refs, as_refs, sp_ref, ss_ref,
                wglu_ref, bglu_ref, pc_ref, pa_ref, ps_ref, wo_ref, gn_ref, xo_ref, ho_ref):
    is_p = pl.program_id(0) < P_ROWS // MIX_TILE

    def pick(p_ref, s_ref):
        p = p_ref[...]
        if p.ndim == 3:
            p = jnp.concatenate([p[i] for i in range(p.shape[0])], axis=-1)
        return jnp.where(is_p, p, s_ref[...])

    y_conv = pick(cp_ref, cs_ref)
    o = [pick(ap_refs[2 * g], as_refs[2 * g]) for g in range(N_ATTN_GROUPS)]
    lse = [pick(ap_refs[2 * g + 1], as_refs[2 * g + 1]) for g in range(N_ATTN_GROUPS)]
    m = jnp.maximum(jnp.maximum(lse[0], lse[1]), lse[2])
    w = [jnp.exp(l - m) for l in lse]
    y_attn = (w[0] * o[0] + w[1] * o[1] + w[2] * o[2]) / (w[0] + w[1] + w[2])
    zg = jax.nn.gelu(pick(sp_ref, ss_ref))
    y_ssm = zg * _sigmoid(_bdot(zg, wglu_ref[...]) + bglu_ref[...])
    gt = _sigmoid(gate_ref[...].astype(jnp.float32) + gb_ref[...])
    merged = (gt[:, :D_MODEL] * _bdot(y_conv, pc_ref[...])
              + gt[:, D_MODEL:2 * D_MODEL] * _bdot(y_attn, pa_ref[...])
              + gt[:, 2 * D_MODEL:] * _bdot(y_ssm, ps_ref[...]))
    x = x_ref[...] + _bdot(merged, wo_ref[...])
    xo_ref[...] = x
    ho_ref[...] = _rms(x, gn_ref[...]).astype(ho_ref.dtype)


def _mix(x, gates, conv_p, conv_s, attn_p, attn_s, ssm_p, ssm_s, lp, h_dtype):
    n_p = P_ROWS // MIX_TILE
    p_map = lambda i: (jnp.minimum(i, n_p - 1), 0)
    s_map = lambda i: (jnp.maximum(i - n_p, 0), 0)
    row_map = lambda i: (i, 0)
    const = lambda i: (0, 0)
    rows = lambda width, imap: pl.BlockSpec((MIX_TILE, width), imap)
    slabs = lambda n: pl.BlockSpec((n, MIX_TILE, V7X_LANES), lambda i: (0, jnp.minimum(i, n_p - 1), 0))
    full = lambda a: pl.BlockSpec(a.shape, const, pipeline_mode=pl.Buffered(1))
    n_attn = 2 * N_ATTN_GROUPS
    weights = [lp['w_glu'], lp['b_glu'], lp['proj_conv'], lp['proj_attn'], lp['proj_ssm'], lp['w_out'],
               lp['norm_next']]
    in_specs = ([rows(D_MODEL, row_map), rows(GATE_WIDTH, row_map), full(lp['gate_bias']),
                 rows(CONV_WIDTH, p_map), rows(CONV_WIDTH, s_map)]
                + [slabs(HALVES)] * n_attn + [rows(ATTN_OUT, s_map)] * n_attn
                + [slabs(N_GB * HALVES), rows(SSM_WIDTH, s_map)]
                + [full(w) for w in weights])

    def body(*refs):
        x_ref, gate_ref, gb_ref, cp_ref, cs_ref = refs[:5]
        ap_refs = refs[5:5 + n_attn]
        as_refs = refs[5 + n_attn:5 + 2 * n_attn]
        rest = refs[5 + 2 * n_attn:]
        _mix_kernel(x_ref, gate_ref, gb_ref, cp_ref, cs_ref, ap_refs, as_refs, *rest)

    return pl.pallas_call(
        body,
        grid=(ROWS // MIX_TILE,),
        in_specs=in_specs,
        out_specs=[rows(D_MODEL, row_map), rows(D_MODEL, row_map)],
        out_shape=[jax.ShapeDtypeStruct((ROWS, D_MODEL), jnp.float32),
                   jax.ShapeDtypeStruct((ROWS, D_MODEL), h_dtype)],
        compiler_params=_params(("parallel",), 56 * MIB),
        name="mix",
    )(x, gates, lp['gate_bias'], conv_p, conv_s, *attn_p, *attn_s, ssm_p, ssm_s, *weights)


def _ffn_up_kernel(be_ref, x_ref, w1_ref, w3_ref, h_ref, *scratch):
    if scratch:
        w1b_ref, w3b_ref = scratch

        @pl.when(pl.program_id(1) == 0)
        def _():
            w1b_ref[...] = w1_ref[...].astype(w1b_ref.dtype)
            w3b_ref[...] = w3_ref[...].astype(w3b_ref.dtype)
    else:
        w1b_ref, w3b_ref = w1_ref, w3_ref
    x = x_ref[...].astype(jnp.bfloat16)
    a = jnp.dot(x, w1b_ref[...], preferred_element_type=jnp.float32)
    b = jnp.dot(x, w3b_ref[...], preferred_element_type=jnp.float32)
    h_ref[...] = (a * _sigmoid(a) * b).astype(h_ref.dtype)


def _ffn_up(block_e, x, w1, w3, bm, bf, name):
    m = x.shape[0]
    e, _, f = w1.shape
    cast = w1.dtype != jnp.bfloat16
    assert not cast or e == 1
    wspec = pl.BlockSpec((None, D_MODEL, bf), lambda j, i, be: (be[i], 0, j))
    grid_spec = pltpu.PrefetchScalarGridSpec(
        num_scalar_prefetch=1,
        grid=(f // bf, m // bm),
        in_specs=[pl.BlockSpec((bm, D_MODEL), lambda j, i, be: (i, 0)), wspec, wspec],
        out_specs=pl.BlockSpec((bm, bf), lambda j, i, be: (i, j)),
        scratch_shapes=[pltpu.VMEM((D_MODEL, bf), jnp.bfloat16)] * 2 if cast else [],
    )
    return pl.pallas_call(
        _ffn_up_kernel,
        grid_spec=grid_spec,
        out_shape=jax.ShapeDtypeStruct((m, f), jnp.bfloat16),
        compiler_params=_params(("parallel", "arbitrary"), 48 * MIB),
        name=name,
    )(block_e, x, w1, w3)


def _ffn_down_dense_kernel(h_ref, w_ref, x_ref, g_ref, xo_ref, ho_ref, acc_ref, *, n_k):
    k = pl.program_id(1)

    @pl.when(k == 0)
    def _():
        acc_ref[...] = x_ref[...]

    acc_ref[...] += jnp.dot(h_ref[...], w_ref[...], preferred_element_type=jnp.float32)

    @pl.when(k == n_k - 1)
    def _():
        x = acc_ref[...]
        xo_ref[...] = x
        ho_ref[...] = _rms(x, g_ref[...]).astype(ho_ref.dtype)


def _ffn_down_dense(h, w2, x, g_next, bk):
    n_k = D_FF // bk
    row = pl.BlockSpec((ROW_TILE, D_MODEL), lambda i, k: (i, 0))
    return pl.pallas_call(
        functools.partial(_ffn_down_dense_kernel, n_k=n_k),
        grid=(ROWS // ROW_TILE, n_k),
        in_specs=[pl.BlockSpec((ROW_TILE, bk), lambda i, k: (i, k)),
                  pl.BlockSpec((bk, D_MODEL), lambda i, k: (k, 0)),
                  row, pl.BlockSpec((1, D_MODEL), lambda i, k: (0, 0))],
        out_specs=[row, row],
        out_shape=[jax.ShapeDtypeStruct((ROWS, D_MODEL), jnp.float32),
                   jax.ShapeDtypeStruct((ROWS, D_MODEL), jnp.bfloat16)],
        scratch_shapes=[pltpu.VMEM((ROW_TILE, D_MODEL), jnp.float32)],
        compiler_params=_params(("parallel", "arbitrary"), 58 * MIB),
        name="ffn_down",
    )(h, w2, x, g_next.reshape(1, D_MODEL))


def _moe_down_kernel(be_ref, h_ref, w_ref, g_ref, y_ref):
    y = jnp.dot(h_ref[...], w_ref[...], preferred_element_type=jnp.float32)
    _store_chunk_rows(y_ref, y * g_ref[...])


def _moe_down(block_e, h, w2, row_gate):
    grid_spec = pltpu.PrefetchScalarGridSpec(
        num_scalar_prefetch=1,
        grid=(MOE_BLOCKS,),
        in_specs=[pl.BlockSpec((MOE_BLOCK, D_FF_EXPERT), lambda i, be: (i, 0)),
                  pl.BlockSpec((None, D_FF_EXPERT, D_MODEL), lambda i, be: (be[i], 0, 0)),
                  pl.BlockSpec((MOE_BLOCK, 1), lambda i, be: (i, 0))],
        out_specs=pl.BlockSpec((MOE_BLOCK * ROW_CHUNKS, V7X_LANES), lambda i, be: (i, 0)),
    )
    return pl.pallas_call(
        _moe_down_kernel,
        grid_spec=grid_spec,
        out_shape=jax.ShapeDtypeStruct((MOE_ROWS * ROW_CHUNKS, V7X_LANES), jnp.float32),
        compiler_params=_params(("arbitrary",), 48 * MIB),
        name="moe_down",
    )(block_e, h, w2, row_gate)


def _router_kernel(x_ref, g_ref, w_ref, hs_ref, r_ref):
    hs = _rms(x_ref[...], g_ref[...])
    _store_chunk_rows(hs_ref, hs)
    logits = jnp.dot(hs, w_ref[...], preferred_element_type=jnp.float32, precision=lax.Precision.HIGHEST)
    lane = lax.broadcasted_iota(jnp.int32, logits.shape, 1)
    logits = jnp.where(lane < N_EXPERTS, logits, -jnp.inf)
    m1 = jnp.max(logits, axis=-1, keepdims=True)
    i1 = jnp.min(jnp.where(logits == m1, lane, ROUTER_PAD), axis=-1, keepdims=True)
    rest = jnp.where(lane == i1, -jnp.inf, logits)
    m2 = jnp.max(rest, axis=-1, keepdims=True)
    i2 = jnp.min(jnp.where(rest == m2, lane, ROUTER_PAD), axis=-1, keepdims=True)
    e2 = jnp.exp(m2 - m1)
    g1 = 1.0 / (1.0 + e2)
    g2 = e2 / (1.0 + e2)
    out = jnp.where(lane == 0, i1.astype(jnp.float32),
                    jnp.where(lane == 1, i2.astype(jnp.float32),
                              jnp.where(lane == 2, g1, jnp.where(lane == 3, g2, 0.0))))
    r_ref[...] = out


def _router(x, g, w_router):
    w = jnp.zeros((D_MODEL, ROUTER_PAD), jnp.float32).at[:, :N_EXPERTS].set(w_router)
    row = pl.BlockSpec((ROW_TILE, D_MODEL), lambda i: (i, 0))
    return pl.pallas_call(
        _router_kernel,
        grid=(ROWS // ROW_TILE,),
        in_specs=[row, pl.BlockSpec((1, D_MODEL), lambda i: (0, 0)),
                  pl.BlockSpec((D_MODEL, ROUTER_PAD), lambda i: (0, 0))],
        out_specs=[pl.BlockSpec((ROW_TILE * ROW_CHUNKS, V7X_LANES), lambda i: (i, 0)),
                   pl.BlockSpec((ROW_TILE, ROUTER_PAD), lambda i: (i, 0))],
        out_shape=[jax.ShapeDtypeStruct((ROWS * ROW_CHUNKS, V7X_LANES), jnp.float32),
                   jax.ShapeDtypeStruct((ROWS, ROUTER_PAD), jnp.float32)],
        compiler_params=_params(("parallel",), 32 * MIB),
        name="router",
    )(x, g.reshape(1, D_MODEL), w)


def _gather_step(idx_ref, src_ref, buf, sem, rows):
    i = pl.program_id(0)
    slot = i % 2

    def copy(step, slot_, r):
        return pltpu.make_async_copy(src_ref.at[idx_ref[step * rows + r]],
                                     buf.at[slot_, pl.ds(r * ROW_CHUNKS, ROW_CHUNKS), :], sem.at[slot_])

    def start_all(step, slot_):
        def body(r, c):
            copy(step, slot_, r).start()
            return c
        lax.fori_loop(0, rows, body, 0, unroll=8)

    @pl.when(i == 0)
    def _():
        start_all(0, 0)

    @pl.when(i + 1 < pl.num_programs(0))
    def _():
        start_all(i + 1, 1 - slot)

    def wait_body(r, c):
        copy(i, slot, r).wait()
        return c
    lax.fori_loop(0, rows, wait_body, 0, unroll=8)
    return slot


def _dispatch_kernel(tok_ref, hs_ref, o_ref, buf, sem):
    slot = _gather_step(tok_ref, hs_ref, buf, sem, MOE_BLOCK)
    o_ref[...] = _load_chunk_rows(buf.at[slot], 0, MOE_BLOCK).astype(o_ref.dtype)


def _dispatch(row_tok, hs):
    grid_spec = pltpu.PrefetchScalarGridSpec(
        num_scalar_prefetch=1,
        grid=(MOE_BLOCKS,),
        in_specs=[pl.BlockSpec(memory_space=pl.ANY)],
        out_specs=pl.BlockSpec((MOE_BLOCK, D_MODEL), lambda i, tok: (i, 0)),
        scratch_shapes=[pltpu.VMEM((2, MOE_BLOCK * ROW_CHUNKS, V7X_LANES), jnp.float32),
                        pltpu.SemaphoreType.DMA((2,))],
    )
    return pl.pallas_call(
        _dispatch_kernel,
        grid_spec=grid_spec,
        out_shape=jax.ShapeDtypeStruct((MOE_ROWS, D_MODEL), jnp.bfloat16),
        compiler_params=_params(("arbitrary",), 16 * MIB),
        name="moe_dispatch",
    )(row_tok, hs)


def _combine_kernel(pos_ref, x_ref, y_ref, g_ref, op_ref, os_ref, buf, sem):
    slot = _gather_step(pos_ref, y_ref, buf, sem, TOP_K * MIX_TILE)
    y = sum(_load_chunk_rows(buf.at[slot], k * MIX_TILE, MIX_TILE) for k in range(TOP_K))
    out = _rms(x_ref[...] + y, g_ref[...])
    is_p = pl.program_id(0) < P_ROWS // MIX_TILE

    @pl.when(is_p)
    def _():
        op_ref[...] = out

    @pl.when(jnp.logical_not(is_p))
    def _():
        os_ref[...] = out


def _combine(pos, x, yb, g_final):
    rows = TOP_K * MIX_TILE
    n_p = P_ROWS // MIX_TILE
    grid_spec = pltpu.PrefetchScalarGridSpec(
        num_scalar_prefetch=1,
        grid=(ROWS // MIX_TILE,),
        in_specs=[pl.BlockSpec((MIX_TILE, D_MODEL), lambda i, pos: (i, 0)),
                  pl.BlockSpec(memory_space=pl.ANY),
                  pl.BlockSpec((1, D_MODEL), lambda i, pos: (0, 0))],
        out_specs=[pl.BlockSpec((MIX_TILE, D_MODEL), lambda i, pos: (jnp.minimum(i, n_p - 1), 0)),
                   pl.BlockSpec((MIX_TILE, D_MODEL), lambda i, pos: (jnp.maximum(i - n_p, 0), 0))],
        scratch_shapes=[pltpu.VMEM((2, rows * ROW_CHUNKS, V7X_LANES), jnp.float32),
                        pltpu.SemaphoreType.DMA((2,))],
    )
    return pl.pallas_call(
        _combine_kernel,
        grid_spec=grid_spec,
        out_shape=[jax.ShapeDtypeStruct((P_ROWS, D_MODEL), jnp.float32),
                   jax.ShapeDtypeStruct((S_ROWS, D_MODEL), jnp.float32)],
        compiler_params=_params(("arbitrary",), 32 * MIB),
        name="moe_combine",
    )(pos, x, yb, g_final.reshape(1, D_MODEL))


def _dispatch_plan(route):
    top_idx = route[:, 0:TOP_K].astype(jnp.int32)
    gates = route[:, TOP_K:2 * TOP_K]
    flat_e = top_idx.reshape(N_ASSIGN)
    order = jnp.argsort(flat_e).astype(jnp.int32)
    rank = jnp.argsort(order).astype(jnp.int32)
    counts = jnp.sum((flat_e[:, None] == jnp.arange(N_EXPERTS)[None, :]).astype(jnp.int32), axis=0)
    padded = (counts + MOE_BLOCK - 1) // MOE_BLOCK * MOE_BLOCK
    start = jnp.cumsum(counts) - counts
    ends_p = jnp.cumsum(padded)
    pstart = ends_p - padded
    block_e = jnp.minimum(jnp.searchsorted(ends_p, jnp.arange(MOE_BLOCKS) * MOE_BLOCK, side='right'),
                          N_EXPERTS - 1).astype(jnp.int32)
    row = jnp.arange(MOE_ROWS, dtype=jnp.int32)
    row_e = jnp.repeat(block_e, MOE_BLOCK)
    off = row - pstart[row_e]
    real = off < counts[row_e]
    assign = order[jnp.clip(start[row_e] + off, 0, N_ASSIGN - 1)]
    row_tok = jnp.where(real, assign // TOP_K, 0).astype(jnp.int32)
    row_gate = jnp.where(real, gates.reshape(N_ASSIGN)[assign], 0.0)
    pos = (pstart[flat_e] + rank - start[flat_e]).astype(jnp.int32)
    pos = jnp.transpose(pos.reshape(ROWS // MIX_TILE, MIX_TILE, TOP_K), (0, 2, 1)).reshape(N_ASSIGN)
    return row_tok, row_gate.reshape(MOE_ROWS, 1), block_e, pos


def kernel(x_prompt, x_sample, cache_kv_w128, cache_kv_w512, cache_kv_w2048, state_conv, state_ssm, norm_mix, w_in, gate_bias, conv_w, ssm_lam_re, ssm_lam_im, ssm_log_dt, ssm_b_re, ssm_b_im, ssm_c_re, ssm_c_im, ssm_d, ssm_w_glu, ssm_b_glu, proj_conv, proj_attn, proj_ssm, w_out, norm_ffn, ffn_w1, ffn_w3, ffn_w2, router_w, moe_w1, moe_w3, moe_w2, norm_final):
    bf = jnp.bfloat16
    caches = tuple(jnp.transpose(c, (0, 1, 3, 4, 5, 2)) for c in (cache_kv_w128, cache_kv_w512, cache_kv_w2048))
    tables = _rope_tables()
    x, h = _norm(x_prompt.reshape(P_ROWS, D_MODEL),
                 jnp.transpose(x_sample, (1, 0, 2)).reshape(S_ROWS, D_MODEL), norm_mix[0])
    kv_p = [[] for _ in range(N_ATTN_GROUPS)]
    kv_s = [[] for _ in range(N_ATTN_GROUPS)]
    conv_p_all, conv_s_all, ssm_p_all, ssm_s_all = [], [], [], []
    y = None
    for l in range(DEPTH):
        za = _linear(h, w_in, l, 0, ZA_WIDTH, jnp.float32, IN_TILE, f"in_proj_{l}")
        gates = _linear(h, w_in, l, ZA_WIDTH, GATE_WIDTH, bf, IN_TILE, f"gate_proj_{l}")
        qk = _rope(za, tables)
        conv_p, cst_p = _conv_prompt(za, conv_w[l])
        conv_s, cst_s = _conv_sample(za, conv_w[l], jnp.transpose(state_conv[l], (1, 0, 2)))
        by_seq = lambda a: jnp.transpose(a.reshape(DEC_SEQ, DEC_BATCH, a.shape[-1]), (1, 0, 2))
        qk_s = by_seq(qk[P_ROWS:])
        v_s = by_seq(za[P_ROWS:, OFF_V:OFF_V + ATTN_QKV])
        attn_p, attn_s = [], []
        for g in range(N_ATTN_GROUPS):
            attn_p += list(_attn_prompt(qk, za, g))
            attn_s += list(_attn_sample(qk_s, v_s, caches[g], l, g))
        tabs = _ssm_tables(ssm_lam_re[l], ssm_lam_im[l], ssm_log_dt[l], ssm_b_re[l], ssm_b_im[l],
                           ssm_c_re[l], ssm_c_im[l], ssm_d[l])
        ssm_p, ssm_s, sst_p, sst_s = _ssm(za, state_ssm[l], tabs)
        moe_layer = l % 2 == 1
        lp = dict(gate_bias=gate_bias[l].reshape(1, GATE_WIDTH), w_glu=ssm_w_glu[l].astype(bf),
                  b_glu=ssm_b_glu[l].reshape(1, SSM_WIDTH), proj_conv=proj_conv[l].astype(bf),
                  proj_attn=proj_attn[l].astype(bf), proj_ssm=proj_ssm[l].astype(bf),
                  w_out=w_out[l].astype(bf), norm_next=norm_ffn[l].reshape(1, D_MODEL))
        x, h = _mix(x, gates, conv_p, conv_s, attn_p, attn_s, ssm_p, ssm_s, lp, bf)
        i = l // 2
        if not moe_layer:
            zero_e = jnp.zeros((ROWS // ROW_TILE,), jnp.int32)
            hid = _ffn_up(zero_e, h, ffn_w1[i:i + 1], ffn_w3[i:i + 1], ROW_TILE, 512, "ffn_up")
            x, h = _ffn_down_dense(hid, ffn_w2[i].astype(bf), x, norm_mix[l + 1], D_FF // 2)
        else:
            hs, route = _router(x, norm_ffn[l], router_w[i])
            row_tok, row_gate, block_e, pos = _dispatch_plan(route)
            xs = _dispatch(row_tok, hs.reshape(ROWS, ROW_CHUNKS, V7X_LANES))
            hid = _ffn_up(block_e, xs, moe_w1[i].astype(bf), moe_w3[i].astype(bf),
                          MOE_BLOCK, D_FF_EXPERT // 2, "moe_up")
            yb = _moe_down(block_e, hid, moe_w2[i].astype(bf), row_gate)
            y = _combine(pos, x, yb.reshape(MOE_ROWS, ROW_CHUNKS, V7X_LANES), norm_final)
        for g, (kp, ks) in enumerate(zip(_kv_prompt(qk, za), _kv_sample(qk, za))):
            kp = kp.reshape(BATCH, 2, HEADS_PER_GROUP, HEAD_DIM, kp.shape[-1])
            kv_p[g].append(jnp.transpose(kp, (0, 4, 1, 2, 3)))
            ks = ks.reshape(DEC_SEQ, 2, HEADS_PER_GROUP, HEAD_DIM, DEC_BATCH)
            kv_s[g].append(jnp.transpose(ks, (4, 0, 1, 2, 3)))
        conv_p_all.append(cst_p)
        conv_s_all.append(jnp.transpose(cst_s, (1, 0, 2)))
        ssm_p_all.append(sst_p)
        ssm_s_all.append(sst_s)
    y_prompt = y[0].reshape(BATCH, SEQ, D_MODEL)
    y_sample = jnp.transpose(y[1].reshape(DEC_SEQ, DEC_BATCH, D_MODEL), (1, 0, 2))
    return (y_prompt, y_sample, jnp.stack(kv_p[0]), jnp.stack(kv_p[1]), jnp.stack(kv_p[2]),
            jnp.stack(conv_p_all), jnp.stack(ssm_p_all), jnp.stack(kv_s[0]), jnp.stack(kv_s[1]),
            jnp.stack(kv_s[2]), jnp.stack(conv_s_all), jnp.stack(ssm_s_all))
```

```python
import functools
import math

import jax
import jax.numpy as jnp
import numpy as np
from jax import lax
from jax.experimental import pallas as pl
from jax.experimental.pallas import tpu as pltpu

D_MODEL = 2048
BATCH = 4
SEQ = 2048
DEPTH = 2
DEC_BATCH = 128
DEC_SEQ = 4
PAST_LEN = 2048
CONV_WIDTH = 512
CONV_K = 3
HEAD_DIM = 64
HEADS_PER_GROUP = 4
ATTN_GROUPS = ((128, 1), (512, 4), (2048, 16))
N_ATTN_GROUPS = 3
ATTN_QKV = N_ATTN_GROUPS * HEADS_PER_GROUP * HEAD_DIM
ATTN_OUT = HEADS_PER_GROUP * HEAD_DIM
ROT_DIM = HEAD_DIM // 4
ROPE_THETA = 500000.0
SSM_WIDTH = 768
SSM_CH_PER_GROUP = 16
SSM_GROUPS = SSM_WIDTH // SSM_CH_PER_GROUP
SSM_STATE = 64
N_BRANCH = 3
D_FF = 5632
N_EXPERTS = 8
TOP_K = 2
D_FF_EXPERT = D_FF // TOP_K
EPS = 1e-6

V7X_LANES = 128
V7X_VMEM_BYTES = 64 * 1024 * 1024
MIB = 1024 * 1024

P_ROWS = BATCH * SEQ
S_ROWS = DEC_SEQ * DEC_BATCH
ROWS = P_ROWS + S_ROWS
ZA_WIDTH = 3 * CONV_WIDTH + 3 * ATTN_QKV + SSM_WIDTH
GATE_WIDTH = N_BRANCH * D_MODEL
QK_WIDTH = 2 * ATTN_QKV
OFF_Q = 3 * CONV_WIDTH
OFF_V = OFF_Q + 2 * ATTN_QKV
OFF_U = OFF_V + ATTN_QKV
BAND = 128
ATTN_DENSE_BLOCKS = 4
ATTN_RESIDUE_UNROLL = 4
SSM_CHUNK = 16
SSM_GB = 16
N_GB = SSM_GROUPS // SSM_GB
GB_CH = SSM_GB * SSM_CH_PER_GROUP
GB_ST = SSM_GB * SSM_STATE
ST_WIDTH = N_GB * 2 * GB_ST
ROW_TILE = 512
MIX_TILE = 256
IN_TILE = 1536
MOE_BLOCK = 256
N_ASSIGN = ROWS * TOP_K
MOE_BLOCKS = -(-(N_ASSIGN + N_EXPERTS * (MOE_BLOCK - 1)) // MOE_BLOCK)
MOE_ROWS = MOE_BLOCKS * MOE_BLOCK
ROUTER_PAD = V7X_LANES
SAMPLE_SEQ_TILE = 8
SAMPLE_CACHE_TILE_BYTES = 8 * MIB
SSM_BATCH_SPLIT = 1
HALVES = ATTN_OUT // V7X_LANES
ROW_CHUNKS = D_MODEL // V7X_LANES
SIDE_CAST_ROWS_W13 = 128
SIDE_CAST_ROWS_W2 = 352
assert DEPTH == 2 and HALVES == 2 and GB_CH == HALVES * V7X_LANES

assert ROWS % ROW_TILE == 0 and P_ROWS % ROW_TILE == 0
assert ROWS % MIX_TILE == 0 and P_ROWS % MIX_TILE == 0


def _params(semantics, vmem_bytes):
    return pltpu.CompilerParams(dimension_semantics=semantics,
                                vmem_limit_bytes=min(int(vmem_bytes), V7X_VMEM_BYTES - 4 * MIB))


def _rms(x, g):
    y = x * lax.rsqrt(jnp.mean(x * x, axis=-1, keepdims=True) + EPS)
    return y * g


def _bdot(a, b):
    return jnp.dot(a.astype(jnp.bfloat16), b.astype(jnp.bfloat16), preferred_element_type=jnp.float32)


def _store_chunk_rows(ref, x):
    n = x.shape[0]
    for k in range(ROW_CHUNKS):
        ref[pl.ds(k, n, stride=ROW_CHUNKS), :] = x[:, k * V7X_LANES:(k + 1) * V7X_LANES]


def _load_chunk_rows(ref, first, n):
    return jnp.concatenate([ref[pl.ds(first * ROW_CHUNKS + k, n, stride=ROW_CHUNKS), :]
                            for k in range(ROW_CHUNKS)], axis=-1)


def _sigmoid(x):
    return 0.5 * jnp.tanh(0.5 * x) + 0.5


def _norm_kernel(xp_ref, xs_ref, g_ref, x_ref, h_ref):
    x = jnp.where(pl.program_id(0) < P_ROWS // ROW_TILE, xp_ref[...], xs_ref[...])
    x_ref[...] = x
    h_ref[...] = _rms(x, g_ref[...]).astype(h_ref.dtype)


def _norm(x_p, x_s, g):
    n_p = P_ROWS // ROW_TILE
    row = pl.BlockSpec((ROW_TILE, D_MODEL), lambda i: (i, 0))
    return pl.pallas_call(
        _norm_kernel,
        grid=(ROWS // ROW_TILE,),
        in_specs=[pl.BlockSpec((ROW_TILE, D_MODEL), lambda i: (jnp.minimum(i, n_p - 1), 0)),
                  pl.BlockSpec((ROW_TILE, D_MODEL), lambda i: (jnp.maximum(i - n_p, 0), 0)),
                  pl.BlockSpec((1, D_MODEL), lambda i: (0, 0))],
        out_specs=[row, row],
        out_shape=[jax.ShapeDtypeStruct((ROWS, D_MODEL), jnp.float32),
                   jax.ShapeDtypeStruct((ROWS, D_MODEL), jnp.bfloat16)],
        compiler_params=_params(("parallel",), 40 * MIB),
        name="norm",
    )(x_p, x_s, g.reshape(1, D_MODEL))


def _side_cast_specs(side, n_inner, extra_args=0):
    in_specs, out_specs, out_shape = [], [], []
    for arr, rows in side:
        n_blk = arr.shape[0] // rows
        assert n_blk * rows == arr.shape[0]

        def imap(j, i, *_, n_blk=n_blk):
            return (jnp.minimum(j * n_inner + i, n_blk - 1), 0)
        spec = pl.BlockSpec((rows, arr.shape[1]), imap)
        in_specs.append(spec)
        out_specs.append(spec)
        out_shape.append(jax.ShapeDtypeStruct(arr.shape, jnp.bfloat16))
    return in_specs, out_specs, out_shape


def _side_cast(in_refs, out_refs):
    for src, dst in zip(in_refs, out_refs):
        dst[...] = src[...].astype(dst.dtype)


def _linear_kernel(*refs, n_side):
    a_ref, w_ref = refs[:2]
    o_ref = refs[2 + n_side]
    wb_ref = refs[-1]
    _side_cast(refs[2:2 + n_side], refs[3 + n_side:3 + 2 * n_side])

    @pl.when(pl.program_id(1) == 0)
    def _():
        wb_ref[...] = w_ref[...].astype(wb_ref.dtype)

    o_ref[...] = jnp.dot(a_ref[...], wb_ref[...], preferred_element_type=jnp.float32).astype(o_ref.dtype)


def _linear(a, w, layer, col0, n, out_dtype, bn, name, side=()):
    m, k = a.shape
    jb = col0 // bn
    grid = (n // bn, m // ROW_TILE)
    assert all(arr.shape[0] // rows <= grid[0] * grid[1] for arr, rows in side)
    s_in, s_out, s_shape = _side_cast_specs(side, grid[1])
    out = pl.pallas_call(
        functools.partial(_linear_kernel, n_side=len(side)),
        grid=grid,
        in_specs=[pl.BlockSpec((ROW_TILE, k), lambda j, i: (i, 0)),
                  pl.BlockSpec((None, k, bn), lambda j, i: (layer, 0, jb + j))] + s_in,
        out_specs=[pl.BlockSpec((ROW_TILE, bn), lambda j, i: (i, j))] + s_out,
        out_shape=[jax.ShapeDtypeStruct((m, n), out_dtype)] + s_shape,
        scratch_shapes=[pltpu.VMEM((k, bn), jnp.bfloat16)],
        compiler_params=_params(("arbitrary", "arbitrary"), 56 * MIB),
        name=name,
    )(a, w, *[arr for arr, _ in side])
    return out if side else out[0]


def _rope_tables():
    half = ROT_DIM // 2
    inv = ROPE_THETA ** (-jnp.arange(half, dtype=jnp.float32) * 2.0 / ROT_DIM)
    pos_p = jnp.arange(SEQ, dtype=jnp.int32)
    pos_s = jnp.repeat(PAST_LEN + jnp.arange(DEC_SEQ, dtype=jnp.int32), DEC_BATCH)
    pos = jnp.concatenate([pos_p, pos_s]).astype(jnp.float32)
    ang = pos[:, None] * inv[None, :]
    cos, sin = jnp.cos(ang), jnp.sin(ang)
    ones = jnp.ones((pos.shape[0], HEAD_DIM - ROT_DIM), jnp.float32)
    zeros = jnp.zeros_like(ones)
    zh = jnp.zeros_like(sin)
    cos_t = jnp.concatenate([cos, cos, ones], axis=1)
    sin_hi = jnp.concatenate([-sin, zh, zeros], axis=1)
    sin_lo = jnp.concatenate([zh, sin, zeros], axis=1)
    reps = V7X_LANES // HEAD_DIM
    return tuple(jnp.tile(t, (1, reps)) for t in (cos_t, sin_hi, sin_lo))


def _rope_kernel(x_ref, c_ref, sh_ref, sl_ref, o_ref):
    x = x_ref[...]
    half = ROT_DIM // 2
    reps = QK_WIDTH // V7X_LANES
    c = jnp.tile(c_ref[...], (1, reps))
    sh = jnp.tile(sh_ref[...], (1, reps))
    sl = jnp.tile(sl_ref[...], (1, reps))
    x_up = pltpu.roll(x, QK_WIDTH - half, axis=1)
    x_dn = pltpu.roll(x, half, axis=1)
    o_ref[...] = x * c + x_up * sh + x_dn * sl


def _rope(za, tables):
    n_p = P_ROWS // ROW_TILE
    per_seq = SEQ // ROW_TILE
    tmap = lambda i: (jnp.where(i < n_p, i % per_seq, per_seq), 0)
    tspec = pl.BlockSpec((ROW_TILE, V7X_LANES), tmap)
    return pl.pallas_call(
        _rope_kernel,
        grid=(ROWS // ROW_TILE,),
        in_specs=[pl.BlockSpec((ROW_TILE, QK_WIDTH), lambda i: (i, OFF_Q // QK_WIDTH)), tspec, tspec, tspec],
        out_specs=pl.BlockSpec((ROW_TILE, QK_WIDTH), lambda i: (i, 0)),
        out_shape=jax.ShapeDtypeStruct((ROWS, QK_WIDTH), jnp.float32),
        compiler_params=_params(("parallel",), 40 * MIB),
        name="rope",
    )(za, *tables)


def _kv_prompt_kernel(k_ref, v_ref, *o_refs, keeps):
    last = pl.program_id(1) == SEQ // ROW_TILE - 1
    k, v = k_ref[...], v_ref[...]

    def write(g, lo):
        cols = slice(g * ATTN_OUT, (g + 1) * ATTN_OUT)
        o_refs[g][0] = k[lo:, cols].T
        o_refs[g][1] = v[lo:, cols].T

    for g, keep in enumerate(keeps):
        if keep == SEQ:
            write(g, 0)
        else:
            pl.when(last)(functools.partial(write, g, ROW_TILE - min(keep, ROW_TILE)))


def _kv_prompt(qk, za):
    keeps = tuple(min(w, SEQ) for w, _ in ATTN_GROUPS)
    assert all(k == SEQ or k <= ROW_TILE for k in keeps)
    n_t = SEQ // ROW_TILE
    out_specs, out_shape = [], []
    for keep in keeps:
        blk = min(keep, ROW_TILE)
        imap = (lambda b, j: (b, 0, 0, j)) if keep == SEQ else (lambda b, j: (b, 0, 0, 0))
        out_specs.append(pl.BlockSpec((None, 2, ATTN_OUT, blk), imap))
        out_shape.append(jax.ShapeDtypeStruct((BATCH, 2, ATTN_OUT, keep), jnp.float32))
    return pl.pallas_call(
        functools.partial(_kv_prompt_kernel, keeps=keeps),
        grid=(BATCH, n_t),
        in_specs=[pl.BlockSpec((ROW_TILE, ATTN_QKV), lambda b, j: (b * n_t + j, 1)),
                  pl.BlockSpec((ROW_TILE, ATTN_QKV), lambda b, j: (b * n_t + j, OFF_V // ATTN_QKV))],
        out_specs=out_specs,
        out_shape=out_shape,
        compiler_params=_params(("parallel", "arbitrary"), 32 * MIB),
        name="kv_prompt",
    )(qk, za)


def _kv_sample_kernel(k_ref, v_ref, *o_refs):
    for t in range(DEC_SEQ):
        rows = slice(t * DEC_BATCH, (t + 1) * DEC_BATCH)
        for g in range(N_ATTN_GROUPS):
            cols = slice(g * ATTN_OUT, (g + 1) * ATTN_OUT)
            o_refs[g][t, 0] = k_ref[rows, cols].T
            o_refs[g][t, 1] = v_ref[rows, cols].T


def _kv_sample(qk, za):
    blk = P_ROWS // S_ROWS
    shape = (DEC_SEQ, 2, ATTN_OUT, DEC_BATCH)
    return pl.pallas_call(
        _kv_sample_kernel,
        grid=(1,),
        in_specs=[pl.BlockSpec((S_ROWS, ATTN_QKV), lambda i: (blk, 1)),
                  pl.BlockSpec((S_ROWS, ATTN_QKV), lambda i: (blk, OFF_V // ATTN_QKV))],
        out_specs=[pl.BlockSpec(shape, lambda i: (0, 0, 0, 0))] * N_ATTN_GROUPS,
        out_shape=[jax.ShapeDtypeStruct(shape, jnp.float32)] * N_ATTN_GROUPS,
        compiler_params=_params(("arbitrary",), 32 * MIB),
        name="kv_sample",
    )(qk, za)


def _conv_prompt_kernel(h_ref, b_ref, c_ref, w_ref, y_ref, st_ref):
    u = c_ref[...] * h_ref[...]
    w = w_ref[...]
    row = lax.broadcasted_iota(jnp.int32, u.shape, 0)
    u1 = jnp.where(row >= 1, pltpu.roll(u, 1, axis=0), 0.0)
    u2 = jnp.where(row >= 2, pltpu.roll(u, 2, axis=0), 0.0)
    y = w[0:1] * u2 + w[1:2] * u1 + w[2:3] * u
    y_ref[...] = (b_ref[...] * y).astype(y_ref.dtype)
    st_ref[...] = u[SEQ - (CONV_K - 1):, :]


def _conv_prompt(za, w):
    col = lambda c: pl.BlockSpec((SEQ, CONV_WIDTH), lambda b, c=c: (b, c))
    return pl.pallas_call(
        _conv_prompt_kernel,
        grid=(BATCH,),
        in_specs=[col(0), col(1), col(2), pl.BlockSpec((CONV_K, CONV_WIDTH), lambda b: (0, 0))],
        out_specs=[pl.BlockSpec((SEQ, CONV_WIDTH), lambda b: (b, 0)),
                   pl.BlockSpec((None, CONV_K - 1, CONV_WIDTH), lambda b: (b, 0, 0))],
        out_shape=[jax.ShapeDtypeStruct((P_ROWS, CONV_WIDTH), jnp.bfloat16),
                   jax.ShapeDtypeStruct((BATCH, CONV_K - 1, CONV_WIDTH), jnp.float32)],
        compiler_params=_params(("parallel",), 48 * MIB),
        name="conv_prompt",
    )(za, za, za, w)


def _conv_sample_kernel(h_ref, b_ref, c_ref, w_ref, s_ref, y_ref, st_ref):
    u = c_ref[...] * h_ref[...]
    w = w_ref[...]
    s0, s1 = s_ref[0], s_ref[1]
    keep = S_ROWS - DEC_BATCH
    u1 = jnp.concatenate([s1, u[:keep]], axis=0)
    u2 = jnp.concatenate([s0, s1, u[:keep - DEC_BATCH]], axis=0)
    y = w[0:1] * u2 + w[1:2] * u1 + w[2:3] * u
    y_ref[...] = (b_ref[...] * y).astype(y_ref.dtype)
    st_ref[0] = u[S_ROWS - 2 * DEC_BATCH:S_ROWS - DEC_BATCH]
    st_ref[1] = u[S_ROWS - DEC_BATCH:]


def _conv_sample(za, w, state):
    blk = P_ROWS // S_ROWS
    col = lambda c: pl.BlockSpec((S_ROWS, CONV_WIDTH), lambda i, c=c: (blk, c))
    st_spec = pl.BlockSpec((CONV_K - 1, DEC_BATCH, CONV_WIDTH), lambda i: (0, 0, 0))
    return pl.pallas_call(
        _conv_sample_kernel,
        grid=(1,),
        in_specs=[col(0), col(1), col(2), pl.BlockSpec((CONV_K, CONV_WIDTH), lambda i: (0, 0)), st_spec],
        out_specs=[pl.BlockSpec((S_ROWS, CONV_WIDTH), lambda i: (0, 0)), st_spec],
        out_shape=[jax.ShapeDtypeStruct((S_ROWS, CONV_WIDTH), jnp.bfloat16),
                   jax.ShapeDtypeStruct((CONV_K - 1, DEC_BATCH, CONV_WIDTH), jnp.float32)],
        compiler_params=_params(("arbitrary",), 24 * MIB),
        name="conv_sample",
    )(za, za, za, w, state)


def _attn_prompt_kernel(*refs, dil, has_prev, sub):
    n_in = (5 if has_prev else 3) * HALVES
    q_refs, kc_refs, vc_refs = (refs[i * HALVES:(i + 1) * HALVES] for i in range(3))
    if has_prev:
        kp_refs, vp_refs = (refs[i * HALVES:(i + 1) * HALVES] for i in range(3, 5))
    o_ref, l_ref = refs[n_in:]
    n_keys = (2 if has_prev else 1) * BAND
    qi = lax.broadcasted_iota(jnp.int32, (BAND, n_keys), 0)
    kj = lax.broadcasted_iota(jnp.int32, (BAND, n_keys), 1)
    if has_prev:
        dist = qi + BAND - kj
        inner_ok = (dist >= 0) & (dist <= BAND)
        first_ok = inner_ok & ((kj >= BAND) | (pl.program_id(1) > 0))
    else:
        first_ok = kj <= qi
    heads_per_half = V7X_LANES // HEAD_DIM

    def attend(q, k, v, valid):
        q, k, v = (t.astype(jnp.bfloat16) for t in (q, k, v))
        outs, lses = [], []
        for h in range(heads_per_half):
            sl = slice(h * HEAD_DIM, (h + 1) * HEAD_DIM)
            s = lax.dot_general(q[:, sl], k[:, sl], (((1,), (1,)), ((), ())),
                                preferred_element_type=jnp.float32) * (HEAD_DIM ** -0.5)
            s = jnp.where(valid, s, -jnp.inf)
            m = jnp.max(s, axis=-1, keepdims=True)
            p = jnp.exp(s - m)
            den = jnp.sum(p, axis=-1, keepdims=True)
            outs.append(jnp.dot(p.astype(jnp.bfloat16), v[:, sl], preferred_element_type=jnp.float32) / den)
            lses.append(jnp.broadcast_to(m + jnp.log(den), (BAND, HEAD_DIM)))
        return jnp.concatenate(outs, axis=-1), jnp.concatenate(lses, axis=-1)

    if dil == 1:
        for half in range(HALVES):
            q, kc, vc = q_refs[half][...], kc_refs[half][...], vc_refs[half][...]
            kp, vp = kp_refs[half][...], vp_refs[half][...]
            for s in range(sub):
                lo, hi = (s - 1) * BAND, (s + 1) * BAND
                k = jnp.concatenate([kp, kc[:BAND]], axis=0) if s == 0 else kc[lo:hi]
                v = jnp.concatenate([vp, vc[:BAND]], axis=0) if s == 0 else vc[lo:hi]
                o, l = attend(q[s * BAND:hi], k, v, first_ok if s == 0 else inner_ok)
                o_ref[half, s * BAND:hi, :] = o
                l_ref[half, s * BAND:hi, :] = l
        return

    def residue(r, carry):
        rows = pl.ds(r, BAND, stride=dil)
        for half in range(HALVES):
            k = kc_refs[half][rows, :]
            v = vc_refs[half][rows, :]
            if has_prev:
                k = jnp.concatenate([kp_refs[half][rows, :], k], axis=0)
                v = jnp.concatenate([vp_refs[half][rows, :], v], axis=0)
            o, l = attend(q_refs[half][rows, :], k, v, first_ok)
            o_ref[half, rows, :] = o
            l_ref[half, rows, :] = l
        return carry

    lax.fori_loop(0, dil, residue, 0, unroll=min(dil, ATTN_RESIDUE_UNROLL))


def _attn_prompt(qk, za, g):
    d = ATTN_GROUPS[g][1]
    sub = ATTN_DENSE_BLOCKS if d == 1 else 1
    rows = BAND * d * sub
    nb = SEQ // rows
    has_prev = nb > 1
    prev_rows = BAND if d == 1 else rows
    qcol, kcol, vcol = (off // V7X_LANES + g * HALVES for off in (0, ATTN_QKV, OFF_V))
    cur = lambda b, j: b * nb + j
    prev = lambda b, j: b * (SEQ // prev_rows) + jnp.maximum(j * (rows // prev_rows) - 1, 0)
    spec = lambda n, row, col: [pl.BlockSpec((n, V7X_LANES), lambda b, j, h=h: (row(b, j), col + h))
                                for h in range(HALVES)]
    in_specs = spec(rows, cur, qcol) + spec(rows, cur, kcol) + spec(rows, cur, vcol)
    args = [qk] * (2 * HALVES) + [za] * HALVES
    if has_prev:
        in_specs += spec(prev_rows, prev, kcol) + spec(prev_rows, prev, vcol)
        args += [qk] * HALVES + [za] * HALVES
    out_spec = pl.BlockSpec((HALVES, rows, V7X_LANES), lambda b, j: (0, cur(b, j), 0))
    shape = jax.ShapeDtypeStruct((HALVES, P_ROWS, V7X_LANES), jnp.float32)
    return pl.pallas_call(
        functools.partial(_attn_prompt_kernel, dil=d, has_prev=has_prev, sub=sub),
        grid=(BATCH, nb),
        in_specs=in_specs,
        out_specs=[out_spec, out_spec],
        out_shape=[shape, shape],
        compiler_params=_params(("parallel", "parallel"), 40 * MIB),
        name=f"attn_prompt_g{g}",
    )(*args)


def _attn_sample_kernel(q_ref, kn_ref, vn_ref, c_ref, o_ref, l_ref, *, dil, nseq):
    width = c_ref.shape[-1]
    scale = HEAD_DIM ** -0.5
    nt = (((1,), (1,)), ((), ()))
    n_rows = HEADS_PER_GROUP * DEC_SEQ
    row_l = lax.broadcasted_iota(jnp.int32, (n_rows, ATTN_OUT), 0)
    lane = lax.broadcasted_iota(jnp.int32, (n_rows, ATTN_OUT), 1)
    own_head = lane // HEAD_DIM == row_l // DEC_SEQ
    t_c = lax.broadcasted_iota(jnp.int32, (n_rows, width), 0) % DEC_SEQ
    w_c = lax.broadcasted_iota(jnp.int32, (n_rows, width), 1)
    ok_c = (w_c >= t_c) & (w_c % dil == t_c % dil)
    r_n = lax.broadcasted_iota(jnp.int32, (n_rows, n_rows), 0)
    c_n = lax.broadcasted_iota(jnp.int32, (n_rows, n_rows), 1)
    t_n, s_n = r_n % DEC_SEQ, c_n % DEC_SEQ
    ok_n = (r_n // DEC_SEQ == c_n // DEC_SEQ) & (s_n <= t_n) & (s_n % dil == t_n % dil)

    def per_head(x):
        return jnp.where(own_head, jnp.tile(x, (HEADS_PER_GROUP, 1)), 0.0).astype(jnp.bfloat16)

    def one_seq(s, carry):
        q, kn, vn = per_head(q_ref[s]), per_head(kn_ref[s]), per_head(vn_ref[s])
        k_t = c_ref[s, 0].reshape(ATTN_OUT, width).astype(jnp.bfloat16)
        v_t = c_ref[s, 1].reshape(ATTN_OUT, width).astype(jnp.bfloat16)
        sc = jnp.dot(q, k_t, preferred_element_type=jnp.float32) * scale
        sc = jnp.where(ok_c, sc, -jnp.inf)
        sn = lax.dot_general(q, kn, nt, preferred_element_type=jnp.float32) * scale
        sn = jnp.where(ok_n, sn, -jnp.inf)
        m = jnp.maximum(jnp.max(sc, axis=-1, keepdims=True), jnp.max(sn, axis=-1, keepdims=True))
        pc = jnp.exp(sc - m)
        pn = jnp.exp(sn - m)
        den = jnp.sum(pc, axis=-1, keepdims=True) + jnp.sum(pn, axis=-1, keepdims=True)
        o = (lax.dot_general(pc.astype(jnp.bfloat16), v_t, nt, preferred_element_type=jnp.float32)
             + jnp.dot(pn.astype(jnp.bfloat16), vn, preferred_element_type=jnp.float32))
        o = jnp.where(own_head, o / den, 0.0)
        lse = jnp.where(own_head, m + jnp.log(den), 0.0)
        fold = lambda a: sum(a[h * DEC_SEQ:(h + 1) * DEC_SEQ] for h in range(HEADS_PER_GROUP))
        o_ref[s] = fold(o)
        l_ref[s] = fold(lse)
        return carry

    lax.fori_loop(0, nseq, one_seq, 0, unroll=True)


def _attn_sample(qk_s, v_s, cache_t, layer, g):
    d = ATTN_GROUPS[g][1]
    width = cache_t.shape[-1]
    seq_bytes = 2 * ATTN_OUT * width * 4
    nseq = max(1, min(SAMPLE_SEQ_TILE, SAMPLE_CACHE_TILE_BYTES // seq_bytes))
    blk = (nseq, DEC_SEQ, ATTN_OUT)
    koff = ATTN_QKV // ATTN_OUT
    out_spec = pl.BlockSpec(blk, lambda i: (i, 0, 0))
    shape = jax.ShapeDtypeStruct((DEC_BATCH, DEC_SEQ, ATTN_OUT), jnp.float32)
    o, l = pl.pallas_call(
        functools.partial(_attn_sample_kernel, dil=d, nseq=nseq),
        grid=(DEC_BATCH // nseq,),
        in_specs=[pl.BlockSpec(blk, lambda i: (i, 0, g)),
                  pl.BlockSpec(blk, lambda i: (i, 0, koff + g)),
                  pl.BlockSpec(blk, lambda i: (i, 0, g)),
                  pl.BlockSpec((None, nseq, 2, HEADS_PER_GROUP, HEAD_DIM, width),
                               lambda i: (layer, i, 0, 0, 0, 0))],
        out_specs=[out_spec, out_spec],
        out_shape=[shape, shape],
        compiler_params=_params(("parallel",), 4 * SAMPLE_CACHE_TILE_BYTES),
        name=f"attn_sample_g{g}",
    )(qk_s, qk_s, v_s, cache_t)
    to_rows = lambda a: jnp.transpose(a, (1, 0, 2)).reshape(S_ROWS, ATTN_OUT)
    return to_rows(o), to_rows(l)


def _ssm_tables(lam_re, lam_im, log_dt, b_re, b_im, c_re, c_im, d_skip):
    dt = jnp.exp(log_dt)[:, None]
    er = jnp.exp(lam_re * dt)
    lbr, lbi = er * jnp.cos(lam_im * dt), er * jnp.sin(lam_im * dt)
    den = lam_re * lam_re + lam_im * lam_im
    nr, ni = lbr - 1.0, lbi
    qr = (nr * lam_re + ni * lam_im) / den
    qi = (ni * lam_re - nr * lam_im) / den
    bbr = qr[:, :, None] * b_re - qi[:, :, None] * b_im
    bbi = qr[:, :, None] * b_im + qi[:, :, None] * b_re
    eye = jnp.eye(SSM_GB, dtype=jnp.float32)

    def in_block(b):
        b = b.reshape(N_GB, SSM_GB, SSM_STATE, SSM_CH_PER_GROUP)
        return jnp.einsum('ngpc,gh->ngchp', b, eye).reshape(N_GB, GB_CH, GB_ST)

    def out_block(c):
        c = c.reshape(N_GB, SSM_GB, SSM_CH_PER_GROUP, SSM_STATE)
        return jnp.einsum('ngcp,gh->ngphc', c, eye).reshape(N_GB, GB_ST, GB_CH)

    b_blk = jnp.concatenate([in_block(bbr), in_block(bbi)], axis=2).astype(jnp.bfloat16)
    c_blk = jnp.concatenate([out_block(c_re), out_block(-c_im)], axis=1).astype(jnp.bfloat16)

    def pack(re, im):
        re = re.reshape(N_GB, 1, GB_ST)
        im = im.reshape(N_GB, 1, GB_ST)
        return jnp.concatenate([re, re], axis=2), jnp.concatenate([-im, im], axis=2)

    pr, pi = lbr, lbi
    for _ in range(int(math.log2(SSM_CHUNK))):
        pr, pi = pr * pr - pi * pi, 2.0 * pr * pi
    return dict(b=b_blk, c=c_blk, lam=pack(lbr, lbi), lam_chunk=pack(pr, pi), d=d_skip.reshape(1, SSM_WIDTH))


def _cstep(s, a, bsw):
    swapped = jnp.concatenate([s[:, GB_ST:], s[:, :GB_ST]], axis=1)
    return a * s + bsw * swapped


def _ssm_kernel(u_ref, s0_ref, b_ref, c_ref, a_ref, bsw_ref, d_ref, *refs, steps, emit_y):
    if emit_y:
        y_ref, sout_ref, s_ref = refs
    else:
        sout_ref, s_ref = refs
    t = pl.program_id(1)

    @pl.when(t == 0)
    def _():
        s_ref[...] = s0_ref[...]

    u = u_ref[...]
    s = _cstep(s_ref[...], a_ref[...], bsw_ref[...]) + _bdot(u, b_ref[...])
    s_ref[...] = s
    if emit_y:
        y_ref[...] = _bdot(s, c_ref[...]) + d_ref[...] * u

    @pl.when(t == steps - 1)
    def _():
        sout_ref[...] = s


def _ssm_pass(u_arr, u_map, y_map, rows, steps, s0, tabs, y_shape, name):
    emit_y = y_shape is not None
    st_spec = pl.BlockSpec((rows, 2 * GB_ST), lambda gb, t: (0, gb))
    coef = pl.BlockSpec((None, 1, 2 * GB_ST), lambda gb, t: (gb, 0, 0))
    in_specs = [pl.BlockSpec((rows, GB_CH), u_map), st_spec,
                pl.BlockSpec((None, GB_CH, 2 * GB_ST), lambda gb, t: (gb, 0, 0)),
                pl.BlockSpec((None, 2 * GB_ST, GB_CH), lambda gb, t: (gb, 0, 0)),
                coef, coef,
                pl.BlockSpec((1, GB_CH), lambda gb, t: (0, gb))]
    st_shape = jax.ShapeDtypeStruct((rows, ST_WIDTH), jnp.float32)
    out_specs, out_shape = [st_spec], [st_shape]
    if emit_y:
        out_specs, out_shape = [pl.BlockSpec((rows, GB_CH), y_map), st_spec], [y_shape, st_shape]
    return pl.pallas_call(
        functools.partial(_ssm_kernel, steps=steps, emit_y=emit_y),
        grid=(N_GB, steps),
        in_specs=in_specs,
        out_specs=out_specs,
        out_shape=out_shape,
        scratch_shapes=[pltpu.VMEM((rows, 2 * GB_ST), jnp.float32)],
        compiler_params=_params(("parallel", "arbitrary"), 48 * MIB),
        name=name,
    )(u_arr, s0, tabs['b'], tabs['c'], tabs['lam'][0], tabs['lam'][1], tabs['d'])


def _ssm_prompt_kernel(*refs, emit_y):
    u_refs = refs[:HALVES]
    s0_ref, b_ref, c_ref, a_ref, bsw_ref, d_ref = refs[HALVES:HALVES + 6]
    if emit_y:
        y_ref, sout_ref, s_ref = refs[HALVES + 6:]
    else:
        sout_ref, s_ref = refs[HALVES + 6:]
    t = pl.program_id(2)
    rows = pl.ds(t, s_ref.shape[0], stride=SSM_CHUNK)

    @pl.when(t == 0)
    def _():
        s_ref[...] = s0_ref[...]

    u = jnp.concatenate([r[rows, :] for r in u_refs], axis=-1)
    s = _cstep(s_ref[...], a_ref[...], bsw_ref[...]) + _bdot(u, b_ref[...])
    s_ref[...] = s
    if emit_y:
        y = _bdot(s, c_ref[...]) + d_ref[...] * u
        for half in range(HALVES):
            y_ref[half, rows, :] = y[:, half * V7X_LANES:(half + 1) * V7X_LANES]

    @pl.when(t == SSM_CHUNK - 1)
    def _():
        sout_ref[...] = s


def _ssm_prompt_pass(za, s0, tabs, emit_y, name):
    n_rows = P_ROWS // SSM_CHUNK // SSM_BATCH_SPLIT
    nat_rows = P_ROWS // SSM_BATCH_SPLIT
    ucol = OFF_U // V7X_LANES
    once = pl.Buffered(1)
    u_specs = [pl.BlockSpec((nat_rows, V7X_LANES), lambda gb, p, t, h=h: (p, ucol + gb * HALVES + h),
                            pipeline_mode=once) for h in range(HALVES)]
    st_spec = pl.BlockSpec((n_rows, 2 * GB_ST), lambda gb, p, t: (p, gb))
    s0_spec = pl.BlockSpec((n_rows, 2 * GB_ST), lambda gb, p, t: (p, gb), pipeline_mode=once)
    coef = pl.BlockSpec((None, 1, 2 * GB_ST), lambda gb, p, t: (gb, 0, 0))
    in_specs = u_specs + [s0_spec,
                          pl.BlockSpec((None, GB_CH, 2 * GB_ST), lambda gb, p, t: (gb, 0, 0)),
                          pl.BlockSpec((None, 2 * GB_ST, GB_CH), lambda gb, p, t: (gb, 0, 0)),
                          coef, coef,
                          pl.BlockSpec((1, GB_CH), lambda gb, p, t: (0, gb))]
    st_shape = jax.ShapeDtypeStruct((P_ROWS // SSM_CHUNK, ST_WIDTH), jnp.float32)
    out_specs, out_shape = [st_spec], [st_shape]
    if emit_y:
        out_specs = [pl.BlockSpec((HALVES, nat_rows, V7X_LANES), lambda gb, p, t: (gb, p, 0)), st_spec]
        out_shape = [jax.ShapeDtypeStruct((N_GB * HALVES, P_ROWS, V7X_LANES), jnp.float32), st_shape]
    return pl.pallas_call(
        functools.partial(_ssm_prompt_kernel, emit_y=emit_y),
        grid=(N_GB, SSM_BATCH_SPLIT, SSM_CHUNK),
        in_specs=in_specs,
        out_specs=out_specs,
        out_shape=out_shape,
        scratch_shapes=[pltpu.VMEM((n_rows, 2 * GB_ST), jnp.float32)],
        compiler_params=_params(("parallel", "parallel", "arbitrary"), 58 * MIB),
        name=name,
    )(*([za] * HALVES), s0, tabs['b'], tabs['c'], tabs['lam'][0], tabs['lam'][1], tabs['d'])


def _ssm_carry_kernel(e_ref, a_ref, bsw_ref, sin_ref, fin_ref):
    a = a_ref[...]
    bsw = bsw_ref[...]
    n_chunks = SEQ // SSM_CHUNK

    def body(n, states):
        new = []
        for b in range(BATCH):
            row = b * n_chunks + n
            sin_ref[pl.ds(row, 1), :] = states[b]
            new.append(_cstep(states[b], a, bsw) + e_ref[pl.ds(row, 1), :])
        return tuple(new)

    zero = jnp.zeros((1, 2 * GB_ST), jnp.float32)
    final = lax.fori_loop(0, n_chunks, body, (zero,) * BATCH)
    for b in range(BATCH):
        fin_ref[pl.ds(b, 1), :] = final[b]


def _ssm_carry(e, tabs):
    n_rows = P_ROWS // SSM_CHUNK
    st_spec = lambda rows: pl.BlockSpec((rows, 2 * GB_ST), lambda gb: (0, gb))
    coef = pl.BlockSpec((None, 1, 2 * GB_ST), lambda gb: (gb, 0, 0))
    return pl.pallas_call(
        _ssm_carry_kernel,
        grid=(N_GB,),
        in_specs=[st_spec(n_rows), coef, coef],
        out_specs=[st_spec(n_rows), st_spec(BATCH)],
        out_shape=[jax.ShapeDtypeStruct((n_rows, ST_WIDTH), jnp.float32),
                   jax.ShapeDtypeStruct((BATCH, ST_WIDTH), jnp.float32)],
        compiler_params=_params(("parallel",), 32 * MIB),
        name="ssm_carry",
    )(e, tabs['lam_chunk'][0], tabs['lam_chunk'][1])


def _pack_state(s):
    b = s.shape[0]
    s = s.reshape(b, N_GB, SSM_GB * SSM_STATE, 2)
    return jnp.transpose(s, (0, 1, 3, 2)).reshape(b, ST_WIDTH)


def _unpack_state(s):
    b = s.shape[0]
    s = s.reshape(b, N_GB, 2, SSM_GB * SSM_STATE)
    return jnp.transpose(s, (0, 1, 3, 2)).reshape(b, SSM_GROUPS, SSM_STATE, 2)


def _ssm(za, state_s, tabs):
    ucol = OFF_U // GB_CH
    zero = jnp.zeros((P_ROWS // SSM_CHUNK, ST_WIDTH), jnp.float32)
    (e,) = _ssm_prompt_pass(za, zero, tabs, False, "ssm_prompt_local")
    s_in, s_fin = _ssm_carry(e, tabs)
    y_p, _ = _ssm_prompt_pass(za, s_in, tabs, True, "ssm_prompt")
    tblk = P_ROWS // DEC_BATCH
    ys_shape = jax.ShapeDtypeStruct((S_ROWS, SSM_WIDTH), jnp.float32)
    y_s, s_new = _ssm_pass(za, lambda gb, t: (tblk + t, ucol + gb), lambda gb, t: (t, gb), DEC_BATCH, DEC_SEQ,
                           _pack_state(state_s), tabs, ys_shape, "ssm_sample")
    return y_p, y_s, _unpack_state(s_fin), _unpack_state(s_new)


def _mix_kernel(x_ref, gate_ref, gb_ref, cp_ref, cs_ref, ap_refs, as_refs, sp_ref, ss_ref,
                wglu_ref, bglu_ref, pc_ref, pa_ref, ps_ref, wo_ref, gn_ref, xo_ref, ho_ref):
    is_p = pl.program_id(0) < P_ROWS // MIX_TILE

    def pick(p_ref, s_ref):
        p = p_ref[...]
        if p.ndim == 3:
            p = jnp.concatenate([p[i] for i in range(p.shape[0])], axis=-1)
        return jnp.where(is_p, p, s_ref[...])

    y_conv = pick(cp_ref, cs_ref)
    o = [pick(ap_refs[2 * g], as_refs[2 * g]) for g in range(N_ATTN_GROUPS)]
    lse = [pick(ap_refs[2 * g + 1], as_refs[2 * g + 1]) for g in range(N_ATTN_GROUPS)]
    m = jnp.maximum(jnp.maximum(lse[0], lse[1]), lse[2])
    w = [jnp.exp(l - m) for l in lse]
    y_attn = (w[0] * o[0] + w[1] * o[1] + w[2] * o[2]) / (w[0] + w[1] + w[2])
    zg = jax.nn.gelu(pick(sp_ref, ss_ref))
    y_ssm = zg * _sigmoid(_bdot(zg, wglu_ref[...]) + bglu_ref[...])
    gt = _sigmoid(gate_ref[...].astype(jnp.float32) + gb_ref[...])
    merged = (gt[:, :D_MODEL] * _bdot(y_conv, pc_ref[...])
              + gt[:, D_MODEL:2 * D_MODEL] * _bdot(y_attn, pa_ref[...])
              + gt[:, 2 * D_MODEL:] * _bdot(y_ssm, ps_ref[...]))
    x = x_ref[...] + _bdot(merged, wo_ref[...])
    xo_ref[...] = x
    ho_ref[...] = _rms(x, gn_ref[...]).astype(ho_ref.dtype)


def _mix(x, gates, conv_p, conv_s, attn_p, attn_s, ssm_p, ssm_s, lp, h_dtype):
    n_p = P_ROWS // MIX_TILE
    p_map = lambda i: (jnp.minimum(i, n_p - 1), 0)
    s_map = lambda i: (jnp.maximum(i - n_p, 0), 0)
    row_map = lambda i: (i, 0)
    const = lambda i: (0, 0)
    rows = lambda width, imap: pl.BlockSpec((MIX_TILE, width), imap)
    slabs = lambda n: pl.BlockSpec((n, MIX_TILE, V7X_LANES), lambda i: (0, jnp.minimum(i, n_p - 1), 0))
    full = lambda a: pl.BlockSpec(a.shape, const, pipeline_mode=pl.Buffered(1))
    n_attn = 2 * N_ATTN_GROUPS
    weights = [lp['w_glu'], lp['b_glu'], lp['proj_conv'], lp['proj_attn'], lp['proj_ssm'], lp['w_out'],
               lp['norm_next']]
    in_specs = ([rows(D_MODEL, row_map), rows(GATE_WIDTH, row_map), full(lp['gate_bias']),
                 rows(CONV_WIDTH, p_map), rows(CONV_WIDTH, s_map)]
                + [slabs(HALVES)] * n_attn + [rows(ATTN_OUT, s_map)] * n_attn
                + [slabs(N_GB * HALVES), rows(SSM_WIDTH, s_map)]
                + [full(w) for w in weights])

    def body(*refs):
        x_ref, gate_ref, gb_ref, cp_ref, cs_ref = refs[:5]
        ap_refs = refs[5:5 + n_attn]
        as_refs = refs[5 + n_attn:5 + 2 * n_attn]
        rest = refs[5 + 2 * n_attn:]
        _mix_kernel(x_ref, gate_ref, gb_ref, cp_ref, cs_ref, ap_refs, as_refs, *rest)

    return pl.pallas_call(
        body,
        grid=(ROWS // MIX_TILE,),
        in_specs=in_specs,
        out_specs=[rows(D_MODEL, row_map), rows(D_MODEL, row_map)],
        out_shape=[jax.ShapeDtypeStruct((ROWS, D_MODEL), jnp.float32),
                   jax.ShapeDtypeStruct((ROWS, D_MODEL), h_dtype)],
        compiler_params=_params(("parallel",), 56 * MIB),
        name="mix",
    )(x, gates, lp['gate_bias'], conv_p, conv_s, *attn_p, *attn_s, ssm_p, ssm_s, *weights)


def _ffn_up_kernel(be_ref, *refs, n_side, cast):
    x_ref, w1_ref, w3_ref = refs[:3]
    h_ref = refs[3 + n_side]
    _side_cast(refs[3:3 + n_side], refs[4 + n_side:4 + 2 * n_side])
    if cast:
        w1b_ref, w3b_ref = refs[-2:]

        @pl.when(pl.program_id(1) == 0)
        def _():
            w1b_ref[...] = w1_ref[...].astype(w1b_ref.dtype)
            w3b_ref[...] = w3_ref[...].astype(w3b_ref.dtype)
    else:
        w1b_ref, w3b_ref = w1_ref, w3_ref
    x = x_ref[...].astype(jnp.bfloat16)
    a = jnp.dot(x, w1b_ref[...], preferred_element_type=jnp.float32)
    b = jnp.dot(x, w3b_ref[...], preferred_element_type=jnp.float32)
    h_ref[...] = (a * _sigmoid(a) * b).astype(h_ref.dtype)


def _ffn_up(block_e, x, w1, w3, bm, bf, name, side=()):
    m = x.shape[0]
    e, _, f = w1.shape
    cast = w1.dtype != jnp.bfloat16
    assert not cast or e == 1
    grid = (f // bf, m // bm)
    assert all(arr.shape[0] // rows <= grid[0] * grid[1] for arr, rows in side)
    s_in, s_out, s_shape = _side_cast_specs(side, grid[1])
    wspec = pl.BlockSpec((None, D_MODEL, bf), lambda j, i, be: (be[i], 0, j))
    grid_spec = pltpu.PrefetchScalarGridSpec(
        num_scalar_prefetch=1,
        grid=grid,
        in_specs=[pl.BlockSpec((bm, D_MODEL), lambda j, i, be: (i, 0)), wspec, wspec] + s_in,
        out_specs=[pl.BlockSpec((bm, bf), lambda j, i, be: (i, j))] + s_out,
        scratch_shapes=[pltpu.VMEM((D_MODEL, bf), jnp.bfloat16)] * 2 if cast else [],
    )
    out = pl.pallas_call(
        functools.partial(_ffn_up_kernel, n_side=len(side), cast=cast),
        grid_spec=grid_spec,
        out_shape=[jax.ShapeDtypeStruct((m, f), jnp.bfloat16)] + s_shape,
        compiler_params=_params(("arbitrary", "arbitrary"), 56 * MIB),
        name=name,
    )(block_e, x, w1, w3, *[arr for arr, _ in side])
    return out if side else out[0]


def _ffn_down_dense_kernel(h_ref, w_ref, x_ref, g_ref, xo_ref, ho_ref, acc_ref, *, n_k):
    k = pl.program_id(1)

    @pl.when(k == 0)
    def _():
        acc_ref[...] = x_ref[...]

    acc_ref[...] += jnp.dot(h_ref[...], w_ref[...], preferred_element_type=jnp.float32)

    @pl.when(k == n_k - 1)
    def _():
        x = acc_ref[...]
        xo_ref[...] = x
        ho_ref[...] = _rms(x, g_ref[...]).astype(ho_ref.dtype)


def _ffn_down_dense(h, w2, x, g_next, bk):
    n_k = D_FF // bk
    row = pl.BlockSpec((ROW_TILE, D_MODEL), lambda i, k: (i, 0))
    return pl.pallas_call(
        functools.partial(_ffn_down_dense_kernel, n_k=n_k),
        grid=(ROWS // ROW_TILE, n_k),
        in_specs=[pl.BlockSpec((ROW_TILE, bk), lambda i, k: (i, k)),
                  pl.BlockSpec((bk, D_MODEL), lambda i, k: (k, 0)),
                  row, pl.BlockSpec((1, D_MODEL), lambda i, k: (0, 0))],
        out_specs=[row, row],
        out_shape=[jax.ShapeDtypeStruct((ROWS, D_MODEL), jnp.float32),
                   jax.ShapeDtypeStruct((ROWS, D_MODEL), jnp.bfloat16)],
        scratch_shapes=[pltpu.VMEM((ROW_TILE, D_MODEL), jnp.float32)],
        compiler_params=_params(("parallel", "arbitrary"), 58 * MIB),
        name="ffn_down",
    )(h, w2, x, g_next.reshape(1, D_MODEL))


def _moe_down_kernel(be_ref, h_ref, w_ref, g_ref, y_ref):
    y = jnp.dot(h_ref[...], w_ref[...], preferred_element_type=jnp.float32)
    _store_chunk_rows(y_ref, y * g_ref[...])


def _moe_down(block_e, h, w2, row_gate):
    grid_spec = pltpu.PrefetchScalarGridSpec(
        num_scalar_prefetch=1,
        grid=(MOE_BLOCKS,),
        in_specs=[pl.BlockSpec((MOE_BLOCK, D_FF_EXPERT), lambda i, be: (i, 0)),
                  pl.BlockSpec((None, D_FF_EXPERT, D_MODEL), lambda i, be: (be[i], 0, 0)),
                  pl.BlockSpec((MOE_BLOCK, 1), lambda i, be: (i, 0))],
        out_specs=pl.BlockSpec((MOE_BLOCK * ROW_CHUNKS, V7X_LANES), lambda i, be: (i, 0)),
    )
    return pl.pallas_call(
        _moe_down_kernel,
        grid_spec=grid_spec,
        out_shape=jax.ShapeDtypeStruct((MOE_ROWS * ROW_CHUNKS, V7X_LANES), jnp.float32),
        compiler_params=_params(("arbitrary",), 48 * MIB),
        name="moe_down",
    )(block_e, h, w2, row_gate)


def _router_kernel(x_ref, g_ref, w_ref, hs_ref, r_ref):
    hs = _rms(x_ref[...], g_ref[...])
    _store_chunk_rows(hs_ref, hs)
    logits = jnp.dot(hs, w_ref[...], preferred_element_type=jnp.float32, precision=lax.Precision.HIGHEST)
    lane = lax.broadcasted_iota(jnp.int32, logits.shape, 1)
    logits = jnp.where(lane < N_EXPERTS, logits, -jnp.inf)
    m1 = jnp.max(logits, axis=-1, keepdims=True)
    i1 = jnp.min(jnp.where(logits == m1, lane, ROUTER_PAD), axis=-1, keepdims=True)
    rest = jnp.where(lane == i1, -jnp.inf, logits)
    m2 = jnp.max(rest, axis=-1, keepdims=True)
    i2 = jnp.min(jnp.where(rest == m2, lane, ROUTER_PAD), axis=-1, keepdims=True)
    e2 = jnp.exp(m2 - m1)
    g1 = 1.0 / (1.0 + e2)
    g2 = e2 / (1.0 + e2)
    out = jnp.where(lane == 0, i1.astype(jnp.float32),
                    jnp.where(lane == 1, i2.astype(jnp.float32),
                              jnp.where(lane == 2, g1, jnp.where(lane == 3, g2, 0.0))))
    r_ref[...] = out


def _router(x, g, w_router):
    w = jnp.zeros((D_MODEL, ROUTER_PAD), jnp.float32).at[:, :N_EXPERTS].set(w_router)
    row = pl.BlockSpec((ROW_TILE, D_MODEL), lambda i: (i, 0))
    return pl.pallas_call(
        _router_kernel,
        grid=(ROWS // ROW_TILE,),
        in_specs=[row, pl.BlockSpec((1, D_MODEL), lambda i: (0, 0)),
                  pl.BlockSpec((D_MODEL, ROUTER_PAD), lambda i: (0, 0))],
        out_specs=[pl.BlockSpec((ROW_TILE * ROW_CHUNKS, V7X_LANES), lambda i: (i, 0)),
                   pl.BlockSpec((ROW_TILE, ROUTER_PAD), lambda i: (i, 0))],
        out_shape=[jax.ShapeDtypeStruct((ROWS * ROW_CHUNKS, V7X_LANES), jnp.float32),
                   jax.ShapeDtypeStruct((ROWS, ROUTER_PAD), jnp.float32)],
        compiler_params=_params(("parallel",), 32 * MIB),
        name="router",
    )(x, g.reshape(1, D_MODEL), w)


def _gather_step(idx_ref, src_ref, buf, sem, rows):
    i = pl.program_id(0)
    slot = i % 2

    def copy(step, slot_, r):
        return pltpu.make_async_copy(src_ref.at[idx_ref[step * rows + r]],
                                     buf.at[slot_, pl.ds(r * ROW_CHUNKS, ROW_CHUNKS), :], sem.at[slot_])

    def start_all(step, slot_):
        def body(r, c):
            copy(step, slot_, r).start()
            return c
        lax.fori_loop(0, rows, body, 0, unroll=8)

    @pl.when(i == 0)
    def _():
        start_all(0, 0)

    @pl.when(i + 1 < pl.num_programs(0))
    def _():
        start_all(i + 1, 1 - slot)

    def wait_body(r, c):
        copy(i, slot, r).wait()
        return c
    lax.fori_loop(0, rows, wait_body, 0, unroll=8)
    return slot


def _dispatch_kernel(tok_ref, hs_ref, o_ref, buf, sem):
    slot = _gather_step(tok_ref, hs_ref, buf, sem, MOE_BLOCK)
    o_ref[...] = _load_chunk_rows(buf.at[slot], 0, MOE_BLOCK).astype(o_ref.dtype)


def _dispatch(row_tok, hs):
    grid_spec = pltpu.PrefetchScalarGridSpec(
        num_scalar_prefetch=1,
        grid=(MOE_BLOCKS,),
        in_specs=[pl.BlockSpec(memory_space=pl.ANY)],
        out_specs=pl.BlockSpec((MOE_BLOCK, D_MODEL), lambda i, tok: (i, 0)),
        scratch_shapes=[pltpu.VMEM((2, MOE_BLOCK * ROW_CHUNKS, V7X_LANES), jnp.float32),
                        pltpu.SemaphoreType.DMA((2,))],
    )
    return pl.pallas_call(
        _dispatch_kernel,
        grid_spec=grid_spec,
        out_shape=jax.ShapeDtypeStruct((MOE_ROWS, D_MODEL), jnp.bfloat16),
        compiler_params=_params(("arbitrary",), 16 * MIB),
        name="moe_dispatch",
    )(row_tok, hs)


def _combine_kernel(pos_ref, x_ref, y_ref, g_ref, op_ref, os_ref, buf, sem):
    slot = _gather_step(pos_ref, y_ref, buf, sem, TOP_K * MIX_TILE)
    y = sum(_load_chunk_rows(buf.at[slot], k * MIX_TILE, MIX_TILE) for k in range(TOP_K))
    out = _rms(x_ref[...] + y, g_ref[...])
    is_p = pl.program_id(0) < P_ROWS // MIX_TILE

    @pl.when(is_p)
    def _():
        op_ref[...] = out

    @pl.when(jnp.logical_not(is_p))
    def _():
        os_ref[...] = out


def _combine(pos, x, yb, g_final):
    rows = TOP_K * MIX_TILE
    n_p = P_ROWS // MIX_TILE
    grid_spec = pltpu.PrefetchScalarGridSpec(
        num_scalar_prefetch=1,
        grid=(ROWS // MIX_TILE,),
        in_specs=[pl.BlockSpec((MIX_TILE, D_MODEL), lambda i, pos: (i, 0)),
                  pl.BlockSpec(memory_space=pl.ANY),
                  pl.BlockSpec((1, D_MODEL), lambda i, pos: (0, 0))],
        out_specs=[pl.BlockSpec((MIX_TILE, D_MODEL), lambda i, pos: (jnp.minimum(i, n_p - 1), 0)),
                   pl.BlockSpec((MIX_TILE, D_MODEL), lambda i, pos: (jnp.maximum(i - n_p, 0), 0))],
        scratch_shapes=[pltpu.VMEM((2, rows * ROW_CHUNKS, V7X_LANES), jnp.float32),
                        pltpu.SemaphoreType.DMA((2,))],
    )
    return pl.pallas_call(
        _combine_kernel,
        grid_spec=grid_spec,
        out_shape=[jax.ShapeDtypeStruct((P_ROWS, D_MODEL), jnp.float32),
                   jax.ShapeDtypeStruct((S_ROWS, D_MODEL), jnp.float32)],
        compiler_params=_params(("arbitrary",), 32 * MIB),
        name="moe_combine",
    )(pos, x, yb, g_final.reshape(1, D_MODEL))


def _dispatch_plan(route):
    top_idx = route[:, 0:TOP_K].astype(jnp.int32)
    gates = route[:, TOP_K:2 * TOP_K]
    flat_e = top_idx.reshape(N_ASSIGN)
    order = jnp.argsort(flat_e).astype(jnp.int32)
    rank = jnp.argsort(order).astype(jnp.int32)
    counts = jnp.sum((flat_e[:, None] == jnp.arange(N_EXPERTS)[None, :]).astype(jnp.int32), axis=0)
    padded = (counts + MOE_BLOCK - 1) // MOE_BLOCK * MOE_BLOCK
    start = jnp.cumsum(counts) - counts
    ends_p = jnp.cumsum(padded)
    pstart = ends_p - padded
    block_e = jnp.minimum(jnp.searchsorted(ends_p, jnp.arange(MOE_BLOCKS) * MOE_BLOCK, side='right'),
                          N_EXPERTS - 1).astype(jnp.int32)
    row = jnp.arange(MOE_ROWS, dtype=jnp.int32)
    row_e = jnp.repeat(block_e, MOE_BLOCK)
    off = row - pstart[row_e]
    real = off < counts[row_e]
    assign = order[jnp.clip(start[row_e] + off, 0, N_ASSIGN - 1)]
    row_tok = jnp.where(real, assign // TOP_K, 0).astype(jnp.int32)
    row_gate = jnp.where(real, gates.reshape(N_ASSIGN)[assign], 0.0)
    pos = (pstart[flat_e] + rank - start[flat_e]).astype(jnp.int32)
    pos = jnp.transpose(pos.reshape(ROWS // MIX_TILE, MIX_TILE, TOP_K), (0, 2, 1)).reshape(N_ASSIGN)
    return row_tok, row_gate.reshape(MOE_ROWS, 1), block_e, pos


def kernel(x_prompt, x_sample, cache_kv_w128, cache_kv_w512, cache_kv_w2048, state_conv, state_ssm, norm_mix, w_in, gate_bias, conv_w, ssm_lam_re, ssm_lam_im, ssm_log_dt, ssm_b_re, ssm_b_im, ssm_c_re, ssm_c_im, ssm_d, ssm_w_glu, ssm_b_glu, proj_conv, proj_attn, proj_ssm, w_out, norm_ffn, ffn_w1, ffn_w3, ffn_w2, router_w, moe_w1, moe_w3, moe_w2, norm_final):
    bf = jnp.bfloat16
    caches = tuple(jnp.transpose(c, (0, 1, 3, 4, 5, 2)) for c in (cache_kv_w128, cache_kv_w512, cache_kv_w2048))
    tables = _rope_tables()
    x, h = _norm(x_prompt.reshape(P_ROWS, D_MODEL),
                 jnp.transpose(x_sample, (1, 0, 2)).reshape(S_ROWS, D_MODEL), norm_mix[0])
    kv_p = [[] for _ in range(N_ATTN_GROUPS)]
    kv_s = [[] for _ in range(N_ATTN_GROUPS)]
    conv_p_all, conv_s_all, ssm_p_all, ssm_s_all = [], [], [], []
    y = None
    for l in range(DEPTH):
        moe_layer = l % 2 == 1
        za = _linear(h, w_in, l, 0, ZA_WIDTH, jnp.float32, IN_TILE, f"in_proj_{l}")
        if moe_layer:
            w2_flat = moe_w2[l // 2].reshape(N_EXPERTS * D_FF_EXPERT, D_MODEL)
            gates, moe_w2_bf = _linear(h, w_in, l, ZA_WIDTH, GATE_WIDTH, bf, IN_TILE, f"gate_proj_{l}",
                                       side=[(w2_flat, SIDE_CAST_ROWS_W2)])
        else:
            gates = _linear(h, w_in, l, ZA_WIDTH, GATE_WIDTH, bf, IN_TILE, f"gate_proj_{l}")
        qk = _rope(za, tables)
        conv_p, cst_p = _conv_prompt(za, conv_w[l])
        conv_s, cst_s = _conv_sample(za, conv_w[l], jnp.transpose(state_conv[l], (1, 0, 2)))
        by_seq = lambda a: jnp.transpose(a.reshape(DEC_SEQ, DEC_BATCH, a.shape[-1]), (1, 0, 2))
        qk_s = by_seq(qk[P_ROWS:])
        v_s = by_seq(za[P_ROWS:, OFF_V:OFF_V + ATTN_QKV])
        attn_p, attn_s = [], []
        for g in range(N_ATTN_GROUPS):
            attn_p += list(_attn_prompt(qk, za, g))
            attn_s += list(_attn_sample(qk_s, v_s, caches[g], l, g))
        tabs = _ssm_tables(ssm_lam_re[l], ssm_lam_im[l], ssm_log_dt[l], ssm_b_re[l], ssm_b_im[l],
                           ssm_c_re[l], ssm_c_im[l], ssm_d[l])
        ssm_p, ssm_s, sst_p, sst_s = _ssm(za, state_ssm[l], tabs)
        moe_layer = l % 2 == 1
        lp = dict(gate_bias=gate_bias[l].reshape(1, GATE_WIDTH), w_glu=ssm_w_glu[l].astype(bf),
                  b_glu=ssm_b_glu[l].reshape(1, SSM_WIDTH), proj_conv=proj_conv[l].astype(bf),
                  proj_attn=proj_attn[l].astype(bf), proj_ssm=proj_ssm[l].astype(bf),
                  w_out=w_out[l].astype(bf), norm_next=norm_ffn[l].reshape(1, D_MODEL))
        x, h = _mix(x, gates, conv_p, conv_s, attn_p, attn_s, ssm_p, ssm_s, lp, bf)
        i = l // 2
        if not moe_layer:
            zero_e = jnp.zeros((ROWS // ROW_TILE,), jnp.int32)
            flat = lambda w: w[(l + 1) // 2].reshape(N_EXPERTS * D_MODEL, D_FF_EXPERT)
            hid, moe_w1_bf, moe_w3_bf = _ffn_up(
                zero_e, h, ffn_w1[i:i + 1], ffn_w3[i:i + 1], ROW_TILE, 512, "ffn_up",
                side=[(flat(moe_w1), SIDE_CAST_ROWS_W13), (flat(moe_w3), SIDE_CAST_ROWS_W13)])
            x, h = _ffn_down_dense(hid, ffn_w2[i].astype(bf), x, norm_mix[l + 1], D_FF // 2)
        else:
            hs, route = _router(x, norm_ffn[l], router_w[i])
            row_tok, row_gate, block_e, pos = _dispatch_plan(route)
            xs = _dispatch(row_tok, hs.reshape(ROWS, ROW_CHUNKS, V7X_LANES))
            expert = lambda w, k: w.reshape(N_EXPERTS, k, w.shape[-1])
            hid = _ffn_up(block_e, xs, expert(moe_w1_bf, D_MODEL), expert(moe_w3_bf, D_MODEL),
                          MOE_BLOCK, D_FF_EXPERT // 2, "moe_up")
            yb = _moe_down(block_e, hid, expert(moe_w2_bf, D_FF_EXPERT), row_gate)
            y = _combine(pos, x, yb.reshape(MOE_ROWS, ROW_CHUNKS, V7X_LANES), norm_final)
        for g, (kp, ks) in enumerate(zip(_kv_prompt(qk, za), _kv_sample(qk, za))):
            kp = kp.reshape(BATCH, 2, HEADS_PER_GROUP, HEAD_DIM, kp.shape[-1])
            kv_p[g].append(jnp.transpose(kp, (0, 4, 1, 2, 3)))
            ks = ks.reshape(DEC_SEQ, 2, HEADS_PER_GROUP, HEAD_DIM, DEC_BATCH)
            kv_s[g].append(jnp.transpose(ks, (4, 0, 1, 2, 3)))
        conv_p_all.append(cst_p)
        conv_s_all.append(jnp.transpose(cst_s, (1, 0, 2)))
        ssm_p_all.append(sst_p)
        ssm_s_all.append(sst_s)
    y_prompt = y[0].reshape(BATCH, SEQ, D_MODEL)
    y_sample = jnp.transpose(y[1].reshape(DEC_SEQ, DEC_BATCH, D_MODEL), (1, 0, 2))
    return (y_prompt, y_sample, jnp.stack(kv_p[0]), jnp.stack(kv_p[1]), jnp.stack(kv_p[2]),
            jnp.stack(conv_p_all), jnp.stack(ssm_p_all), jnp.stack(kv_s[0]), jnp.stack(kv_s[1]),
            jnp.stack(kv_s[2]), jnp.stack(conv_s_all), jnp.stack(ssm_s_all))
```

```python
import functools
import math

import jax
import jax.numpy as jnp
import numpy as np
from jax import lax
from jax.experimental import pallas as pl
from jax.experimental.pallas import tpu as pltpu

D_MODEL = 2048
BATCH = 4
SEQ = 2048
DEPTH = 2
DEC_BATCH = 128
DEC_SEQ = 4
PAST_LEN = 2048
CONV_WIDTH = 512
CONV_K = 3
HEAD_DIM = 64
HEADS_PER_GROUP = 4
ATTN_GROUPS = ((128, 1), (512, 4), (2048, 16))
N_ATTN_GROUPS = 3
ATTN_QKV = N_ATTN_GROUPS * HEADS_PER_GROUP * HEAD_DIM
ATTN_OUT = HEADS_PER_GROUP * HEAD_DIM
ROT_DIM = HEAD_DIM // 4
ROPE_THETA = 500000.0
SSM_WIDTH = 768
SSM_CH_PER_GROUP = 16
SSM_GROUPS = SSM_WIDTH // SSM_CH_PER_GROUP
SSM_STATE = 64
N_BRANCH = 3
D_FF = 5632
N_EXPERTS = 8
TOP_K = 2
D_FF_EXPERT = D_FF // TOP_K
EPS = 1e-6

V7X_LANES = 128
V7X_VMEM_BYTES = 64 * 1024 * 1024
MIB = 1024 * 1024

P_ROWS = BATCH * SEQ
S_ROWS = DEC_SEQ * DEC_BATCH
ROWS = P_ROWS + S_ROWS
ZA_WIDTH = 3 * CONV_WIDTH + 3 * ATTN_QKV + SSM_WIDTH
GATE_WIDTH = N_BRANCH * D_MODEL
QK_WIDTH = 2 * ATTN_QKV
OFF_Q = 3 * CONV_WIDTH
OFF_V = OFF_Q + 2 * ATTN_QKV
OFF_U = OFF_V + ATTN_QKV
BAND = 128
ATTN_DENSE_BLOCKS = 4
ATTN_RESIDUE_UNROLL = 4
SSM_CHUNK = 16
SSM_GB = 16
N_GB = SSM_GROUPS // SSM_GB
GB_CH = SSM_GB * SSM_CH_PER_GROUP
GB_ST = SSM_GB * SSM_STATE
ST_WIDTH = N_GB * 2 * GB_ST
ROW_TILE = 512
MIX_TILE = 256
IN_TILE = 1536
MOE_BLOCK = 256
N_ASSIGN = ROWS * TOP_K
MOE_BLOCKS = -(-(N_ASSIGN + N_EXPERTS * (MOE_BLOCK - 1)) // MOE_BLOCK)
MOE_ROWS = MOE_BLOCKS * MOE_BLOCK
ROUTER_PAD = V7X_LANES
SAMPLE_SEQ_TILE = 8
SAMPLE_CACHE_TILE_BYTES = 8 * MIB
SSM_BATCH_SPLIT = 1
HALVES = ATTN_OUT // V7X_LANES
ROW_CHUNKS = D_MODEL // V7X_LANES
SIDE_CAST_ROWS_W1 = 128
SIDE_CAST_ROWS_W3 = 256
SIDE_CAST_ROWS_W2 = 352
SIDE_CAST_ROWS_FFN = 128
assert DEPTH == 2 and HALVES == 2 and GB_CH == HALVES * V7X_LANES

assert ROWS % ROW_TILE == 0 and P_ROWS % ROW_TILE == 0
assert ROWS % MIX_TILE == 0 and P_ROWS % MIX_TILE == 0


def _params(semantics, vmem_bytes):
    return pltpu.CompilerParams(dimension_semantics=semantics,
                                vmem_limit_bytes=min(int(vmem_bytes), V7X_VMEM_BYTES - 4 * MIB))


def _rms(x, g):
    y = x * lax.rsqrt(jnp.mean(x * x, axis=-1, keepdims=True) + EPS)
    return y * g


def _bdot(a, b):
    return jnp.dot(a.astype(jnp.bfloat16), b.astype(jnp.bfloat16), preferred_element_type=jnp.float32)


def _store_chunk_rows(ref, x):
    n = x.shape[0]
    for k in range(ROW_CHUNKS):
        ref[pl.ds(k, n, stride=ROW_CHUNKS), :] = x[:, k * V7X_LANES:(k + 1) * V7X_LANES]


def _load_chunk_rows(ref, first, n):
    return jnp.concatenate([ref[pl.ds(first * ROW_CHUNKS + k, n, stride=ROW_CHUNKS), :]
                            for k in range(ROW_CHUNKS)], axis=-1)


def _sigmoid(x):
    return 0.5 * jnp.tanh(0.5 * x) + 0.5


def _norm_kernel(xp_ref, xs_ref, g_ref, x_ref, h_ref):
    x = jnp.where(pl.program_id(0) < P_ROWS // ROW_TILE, xp_ref[...], xs_ref[...])
    x_ref[...] = x
    h_ref[...] = _rms(x, g_ref[...]).astype(h_ref.dtype)


def _norm(x_p, x_s, g):
    n_p = P_ROWS // ROW_TILE
    row = pl.BlockSpec((ROW_TILE, D_MODEL), lambda i: (i, 0))
    return pl.pallas_call(
        _norm_kernel,
        grid=(ROWS // ROW_TILE,),
        in_specs=[pl.BlockSpec((ROW_TILE, D_MODEL), lambda i: (jnp.minimum(i, n_p - 1), 0)),
                  pl.BlockSpec((ROW_TILE, D_MODEL), lambda i: (jnp.maximum(i - n_p, 0), 0)),
                  pl.BlockSpec((1, D_MODEL), lambda i: (0, 0))],
        out_specs=[row, row],
        out_shape=[jax.ShapeDtypeStruct((ROWS, D_MODEL), jnp.float32),
                   jax.ShapeDtypeStruct((ROWS, D_MODEL), jnp.bfloat16)],
        compiler_params=_params(("parallel",), 40 * MIB),
        name="norm",
    )(x_p, x_s, g.reshape(1, D_MODEL))


def _side_cast_specs(side, n_inner, extra_args=0):
    in_specs, out_specs, out_shape = [], [], []
    for arr, rows in side:
        n_blk = arr.shape[0] // rows
        assert n_blk * rows == arr.shape[0]

        def imap(j, i, *_, n_blk=n_blk):
            return (jnp.minimum(j * n_inner + i, n_blk - 1), 0)
        spec = pl.BlockSpec((rows, arr.shape[1]), imap)
        in_specs.append(spec)
        out_specs.append(spec)
        out_shape.append(jax.ShapeDtypeStruct(arr.shape, jnp.bfloat16))
    return in_specs, out_specs, out_shape


def _side_cast(in_refs, out_refs):
    for src, dst in zip(in_refs, out_refs):
        dst[...] = src[...].astype(dst.dtype)


def _linear_kernel(*refs, n_side):
    a_ref, w_ref = refs[:2]
    o_ref = refs[2 + n_side]
    wb_ref = refs[-1]
    _side_cast(refs[2:2 + n_side], refs[3 + n_side:3 + 2 * n_side])

    @pl.when(pl.program_id(1) == 0)
    def _():
        wb_ref[...] = w_ref[...].astype(wb_ref.dtype)

    o_ref[...] = jnp.dot(a_ref[...], wb_ref[...], preferred_element_type=jnp.float32).astype(o_ref.dtype)


def _linear(a, w, layer, col0, n, out_dtype, bn, name, side=()):
    m, k = a.shape
    jb = col0 // bn
    grid = (n // bn, m // ROW_TILE)
    assert all(arr.shape[0] // rows <= grid[0] * grid[1] for arr, rows in side)
    s_in, s_out, s_shape = _side_cast_specs(side, grid[1])
    out = pl.pallas_call(
        functools.partial(_linear_kernel, n_side=len(side)),
        grid=grid,
        in_specs=[pl.BlockSpec((ROW_TILE, k), lambda j, i: (i, 0)),
                  pl.BlockSpec((None, k, bn), lambda j, i: (layer, 0, jb + j))] + s_in,
        out_specs=[pl.BlockSpec((ROW_TILE, bn), lambda j, i: (i, j))] + s_out,
        out_shape=[jax.ShapeDtypeStruct((m, n), out_dtype)] + s_shape,
        scratch_shapes=[pltpu.VMEM((k, bn), jnp.bfloat16)],
        compiler_params=_params(("arbitrary", "arbitrary"), 56 * MIB),
        name=name,
    )(a, w, *[arr for arr, _ in side])
    return out if side else out[0]


def _rope_tables():
    half = ROT_DIM // 2
    inv = ROPE_THETA ** (-jnp.arange(half, dtype=jnp.float32) * 2.0 / ROT_DIM)
    pos_p = jnp.arange(SEQ, dtype=jnp.int32)
    pos_s = jnp.repeat(PAST_LEN + jnp.arange(DEC_SEQ, dtype=jnp.int32), DEC_BATCH)
    pos = jnp.concatenate([pos_p, pos_s]).astype(jnp.float32)
    ang = pos[:, None] * inv[None, :]
    cos, sin = jnp.cos(ang), jnp.sin(ang)
    ones = jnp.ones((pos.shape[0], HEAD_DIM - ROT_DIM), jnp.float32)
    zeros = jnp.zeros_like(ones)
    zh = jnp.zeros_like(sin)
    cos_t = jnp.concatenate([cos, cos, ones], axis=1)
    sin_hi = jnp.concatenate([-sin, zh, zeros], axis=1)
    sin_lo = jnp.concatenate([zh, sin, zeros], axis=1)
    reps = V7X_LANES // HEAD_DIM
    return tuple(jnp.tile(t, (1, reps)) for t in (cos_t, sin_hi, sin_lo))


def _rope_kernel(x_ref, c_ref, sh_ref, sl_ref, o_ref):
    x = x_ref[...]
    half = ROT_DIM // 2
    reps = QK_WIDTH // V7X_LANES
    c = jnp.tile(c_ref[...], (1, reps))
    sh = jnp.tile(sh_ref[...], (1, reps))
    sl = jnp.tile(sl_ref[...], (1, reps))
    x_up = pltpu.roll(x, QK_WIDTH - half, axis=1)
    x_dn = pltpu.roll(x, half, axis=1)
    o_ref[...] = x * c + x_up * sh + x_dn * sl


def _rope(za, tables):
    n_p = P_ROWS // ROW_TILE
    per_seq = SEQ // ROW_TILE
    tmap = lambda i: (jnp.where(i < n_p, i % per_seq, per_seq), 0)
    tspec = pl.BlockSpec((ROW_TILE, V7X_LANES), tmap)
    return pl.pallas_call(
        _rope_kernel,
        grid=(ROWS // ROW_TILE,),
        in_specs=[pl.BlockSpec((ROW_TILE, QK_WIDTH), lambda i: (i, OFF_Q // QK_WIDTH)), tspec, tspec, tspec],
        out_specs=pl.BlockSpec((ROW_TILE, QK_WIDTH), lambda i: (i, 0)),
        out_shape=jax.ShapeDtypeStruct((ROWS, QK_WIDTH), jnp.float32),
        compiler_params=_params(("parallel",), 40 * MIB),
        name="rope",
    )(za, *tables)


def _kv_prompt_kernel(k_ref, v_ref, *o_refs, keeps):
    last = pl.program_id(1) == SEQ // ROW_TILE - 1
    k, v = k_ref[...], v_ref[...]

    def write(g, lo):
        cols = slice(g * ATTN_OUT, (g + 1) * ATTN_OUT)
        o_refs[g][0] = k[lo:, cols].T
        o_refs[g][1] = v[lo:, cols].T

    for g, keep in enumerate(keeps):
        if keep == SEQ:
            write(g, 0)
        else:
            pl.when(last)(functools.partial(write, g, ROW_TILE - min(keep, ROW_TILE)))


def _kv_prompt(qk, za):
    keeps = tuple(min(w, SEQ) for w, _ in ATTN_GROUPS)
    assert all(k == SEQ or k <= ROW_TILE for k in keeps)
    n_t = SEQ // ROW_TILE
    out_specs, out_shape = [], []
    for keep in keeps:
        blk = min(keep, ROW_TILE)
        imap = (lambda b, j: (b, 0, 0, j)) if keep == SEQ else (lambda b, j: (b, 0, 0, 0))
        out_specs.append(pl.BlockSpec((None, 2, ATTN_OUT, blk), imap))
        out_shape.append(jax.ShapeDtypeStruct((BATCH, 2, ATTN_OUT, keep), jnp.float32))
    return pl.pallas_call(
        functools.partial(_kv_prompt_kernel, keeps=keeps),
        grid=(BATCH, n_t),
        in_specs=[pl.BlockSpec((ROW_TILE, ATTN_QKV), lambda b, j: (b * n_t + j, 1)),
                  pl.BlockSpec((ROW_TILE, ATTN_QKV), lambda b, j: (b * n_t + j, OFF_V // ATTN_QKV))],
        out_specs=out_specs,
        out_shape=out_shape,
        compiler_params=_params(("parallel", "arbitrary"), 32 * MIB),
        name="kv_prompt",
    )(qk, za)


def _kv_sample_kernel(k_ref, v_ref, *o_refs):
    for t in range(DEC_SEQ):
        rows = slice(t * DEC_BATCH, (t + 1) * DEC_BATCH)
        for g in range(N_ATTN_GROUPS):
            cols = slice(g * ATTN_OUT, (g + 1) * ATTN_OUT)
            o_refs[g][t, 0] = k_ref[rows, cols].T
            o_refs[g][t, 1] = v_ref[rows, cols].T


def _kv_sample(qk, za):
    blk = P_ROWS // S_ROWS
    shape = (DEC_SEQ, 2, ATTN_OUT, DEC_BATCH)
    return pl.pallas_call(
        _kv_sample_kernel,
        grid=(1,),
        in_specs=[pl.BlockSpec((S_ROWS, ATTN_QKV), lambda i: (blk, 1)),
                  pl.BlockSpec((S_ROWS, ATTN_QKV), lambda i: (blk, OFF_V // ATTN_QKV))],
        out_specs=[pl.BlockSpec(shape, lambda i: (0, 0, 0, 0))] * N_ATTN_GROUPS,
        out_shape=[jax.ShapeDtypeStruct(shape, jnp.float32)] * N_ATTN_GROUPS,
        compiler_params=_params(("arbitrary",), 32 * MIB),
        name="kv_sample",
    )(qk, za)


def _conv_prompt_kernel(h_ref, b_ref, c_ref, w_ref, y_ref, st_ref):
    u = c_ref[...] * h_ref[...]
    w = w_ref[...]
    row = lax.broadcasted_iota(jnp.int32, u.shape, 0)
    u1 = jnp.where(row >= 1, pltpu.roll(u, 1, axis=0), 0.0)
    u2 = jnp.where(row >= 2, pltpu.roll(u, 2, axis=0), 0.0)
    y = w[0:1] * u2 + w[1:2] * u1 + w[2:3] * u
    y_ref[...] = (b_ref[...] * y).astype(y_ref.dtype)
    st_ref[...] = u[SEQ - (CONV_K - 1):, :]


def _conv_prompt(za, w):
    col = lambda c: pl.BlockSpec((SEQ, CONV_WIDTH), lambda b, c=c: (b, c))
    return pl.pallas_call(
        _conv_prompt_kernel,
        grid=(BATCH,),
        in_specs=[col(0), col(1), col(2), pl.BlockSpec((CONV_K, CONV_WIDTH), lambda b: (0, 0))],
        out_specs=[pl.BlockSpec((SEQ, CONV_WIDTH), lambda b: (b, 0)),
                   pl.BlockSpec((None, CONV_K - 1, CONV_WIDTH), lambda b: (b, 0, 0))],
        out_shape=[jax.ShapeDtypeStruct((P_ROWS, CONV_WIDTH), jnp.bfloat16),
                   jax.ShapeDtypeStruct((BATCH, CONV_K - 1, CONV_WIDTH), jnp.float32)],
        compiler_params=_params(("parallel",), 48 * MIB),
        name="conv_prompt",
    )(za, za, za, w)


def _conv_sample_kernel(h_ref, b_ref, c_ref, w_ref, s_ref, y_ref, st_ref):
    u = c_ref[...] * h_ref[...]
    w = w_ref[...]
    s0, s1 = s_ref[0], s_ref[1]
    keep = S_ROWS - DEC_BATCH
    u1 = jnp.concatenate([s1, u[:keep]], axis=0)
    u2 = jnp.concatenate([s0, s1, u[:keep - DEC_BATCH]], axis=0)
    y = w[0:1] * u2 + w[1:2] * u1 + w[2:3] * u
    y_ref[...] = (b_ref[...] * y).astype(y_ref.dtype)
    st_ref[0] = u[S_ROWS - 2 * DEC_BATCH:S_ROWS - DEC_BATCH]
    st_ref[1] = u[S_ROWS - DEC_BATCH:]


def _conv_sample(za, w, state):
    blk = P_ROWS // S_ROWS
    col = lambda c: pl.BlockSpec((S_ROWS, CONV_WIDTH), lambda i, c=c: (blk, c))
    st_spec = pl.BlockSpec((CONV_K - 1, DEC_BATCH, CONV_WIDTH), lambda i: (0, 0, 0))
    return pl.pallas_call(
        _conv_sample_kernel,
        grid=(1,),
        in_specs=[col(0), col(1), col(2), pl.BlockSpec((CONV_K, CONV_WIDTH), lambda i: (0, 0)), st_spec],
        out_specs=[pl.BlockSpec((S_ROWS, CONV_WIDTH), lambda i: (0, 0)), st_spec],
        out_shape=[jax.ShapeDtypeStruct((S_ROWS, CONV_WIDTH), jnp.bfloat16),
                   jax.ShapeDtypeStruct((CONV_K - 1, DEC_BATCH, CONV_WIDTH), jnp.float32)],
        compiler_params=_params(("arbitrary",), 24 * MIB),
        name="conv_sample",
    )(za, za, za, w, state)


def _attn_prompt_kernel(*refs, dil, has_prev, sub):
    n_in = (5 if has_prev else 3) * HALVES
    q_refs, kc_refs, vc_refs = (refs[i * HALVES:(i + 1) * HALVES] for i in range(3))
    if has_prev:
        kp_refs, vp_refs = (refs[i * HALVES:(i + 1) * HALVES] for i in range(3, 5))
    o_ref, l_ref = refs[n_in:]
    n_keys = (2 if has_prev else 1) * BAND
    qi = lax.broadcasted_iota(jnp.int32, (BAND, n_keys), 0)
    kj = lax.broadcasted_iota(jnp.int32, (BAND, n_keys), 1)
    if has_prev:
        dist = qi + BAND - kj
        inner_ok = (dist >= 0) & (dist <= BAND)
        first_ok = inner_ok & ((kj >= BAND) | (pl.program_id(1) > 0))
    else:
        first_ok = kj <= qi
    heads_per_half = V7X_LANES // HEAD_DIM

    def attend(q, k, v, valid):
        q, k, v = (t.astype(jnp.bfloat16) for t in (q, k, v))
        outs, lses = [], []
        for h in range(heads_per_half):
            sl = slice(h * HEAD_DIM, (h + 1) * HEAD_DIM)
            s = lax.dot_general(q[:, sl], k[:, sl], (((1,), (1,)), ((), ())),
                                preferred_element_type=jnp.float32) * (HEAD_DIM ** -0.5)
            s = jnp.where(valid, s, -jnp.inf)
            m = jnp.max(s, axis=-1, keepdims=True)
            p = jnp.exp(s - m)
            den = jnp.sum(p, axis=-1, keepdims=True)
            outs.append(jnp.dot(p.astype(jnp.bfloat16), v[:, sl], preferred_element_type=jnp.float32) / den)
            lses.append(jnp.broadcast_to(m + jnp.log(den), (BAND, HEAD_DIM)))
        return jnp.concatenate(outs, axis=-1), jnp.concatenate(lses, axis=-1)

    if dil == 1:
        for half in range(HALVES):
            q, kc, vc = q_refs[half][...], kc_refs[half][...], vc_refs[half][...]
            kp, vp = kp_refs[half][...], vp_refs[half][...]
            for s in range(sub):
                lo, hi = (s - 1) * BAND, (s + 1) * BAND
                k = jnp.concatenate([kp, kc[:BAND]], axis=0) if s == 0 else kc[lo:hi]
                v = jnp.concatenate([vp, vc[:BAND]], axis=0) if s == 0 else vc[lo:hi]
                o, l = attend(q[s * BAND:hi], k, v, first_ok if s == 0 else inner_ok)
                o_ref[half, s * BAND:hi, :] = o
                l_ref[half, s * BAND:hi, :] = l
        return

    def residue(r, carry):
        rows = pl.ds(r, BAND, stride=dil)
        for half in range(HALVES):
            k = kc_refs[half][rows, :]
            v = vc_refs[half][rows, :]
            if has_prev:
                k = jnp.concatenate([kp_refs[half][rows, :], k], axis=0)
                v = jnp.concatenate([vp_refs[half][rows, :], v], axis=0)
            o, l = attend(q_refs[half][rows, :], k, v, first_ok)
            o_ref[half, rows, :] = o
            l_ref[half, rows, :] = l
        return carry

    lax.fori_loop(0, dil, residue, 0, unroll=min(dil, ATTN_RESIDUE_UNROLL))


def _attn_prompt(qk, za, g):
    d = ATTN_GROUPS[g][1]
    sub = ATTN_DENSE_BLOCKS if d == 1 else 1
    rows = BAND * d * sub
    nb = SEQ // rows
    has_prev = nb > 1
    prev_rows = BAND if d == 1 else rows
    qcol, kcol, vcol = (off // V7X_LANES + g * HALVES for off in (0, ATTN_QKV, OFF_V))
    cur = lambda b, j: b * nb + j
    prev = lambda b, j: b * (SEQ // prev_rows) + jnp.maximum(j * (rows // prev_rows) - 1, 0)
    spec = lambda n, row, col: [pl.BlockSpec((n, V7X_LANES), lambda b, j, h=h: (row(b, j), col + h))
                                for h in range(HALVES)]
    in_specs = spec(rows, cur, qcol) + spec(rows, cur, kcol) + spec(rows, cur, vcol)
    args = [qk] * (2 * HALVES) + [za] * HALVES
    if has_prev:
        in_specs += spec(prev_rows, prev, kcol) + spec(prev_rows, prev, vcol)
        args += [qk] * HALVES + [za] * HALVES
    out_spec = pl.BlockSpec((HALVES, rows, V7X_LANES), lambda b, j: (0, cur(b, j), 0))
    shape = jax.ShapeDtypeStruct((HALVES, P_ROWS, V7X_LANES), jnp.float32)
    return pl.pallas_call(
        functools.partial(_attn_prompt_kernel, dil=d, has_prev=has_prev, sub=sub),
        grid=(BATCH, nb),
        in_specs=in_specs,
        out_specs=[out_spec, out_spec],
        out_shape=[shape, shape],
        compiler_params=_params(("parallel", "parallel"), 40 * MIB),
        name=f"attn_prompt_g{g}",
    )(*args)


def _attn_sample_kernel(q_ref, kn_ref, vn_ref, c_ref, o_ref, l_ref, *, dil, nseq):
    width = c_ref.shape[-1]
    scale = HEAD_DIM ** -0.5
    nt = (((1,), (1,)), ((), ()))
    n_rows = HEADS_PER_GROUP * DEC_SEQ
    row_l = lax.broadcasted_iota(jnp.int32, (n_rows, ATTN_OUT), 0)
    lane = lax.broadcasted_iota(jnp.int32, (n_rows, ATTN_OUT), 1)
    own_head = lane // HEAD_DIM == row_l // DEC_SEQ
    t_c = lax.broadcasted_iota(jnp.int32, (n_rows, width), 0) % DEC_SEQ
    w_c = lax.broadcasted_iota(jnp.int32, (n_rows, width), 1)
    ok_c = (w_c >= t_c) & (w_c % dil == t_c % dil)
    r_n = lax.broadcasted_iota(jnp.int32, (n_rows, n_rows), 0)
    c_n = lax.broadcasted_iota(jnp.int32, (n_rows, n_rows), 1)
    t_n, s_n = r_n % DEC_SEQ, c_n % DEC_SEQ
    ok_n = (r_n // DEC_SEQ == c_n // DEC_SEQ) & (s_n <= t_n) & (s_n % dil == t_n % dil)

    def per_head(x):
        return jnp.where(own_head, jnp.tile(x, (HEADS_PER_GROUP, 1)), 0.0).astype(jnp.bfloat16)

    def one_seq(s, carry):
        q, kn, vn = per_head(q_ref[s]), per_head(kn_ref[s]), per_head(vn_ref[s])
        k_t = c_ref[s, 0].reshape(ATTN_OUT, width).astype(jnp.bfloat16)
        v_t = c_ref[s, 1].reshape(ATTN_OUT, width).astype(jnp.bfloat16)
        sc = jnp.dot(q, k_t, preferred_element_type=jnp.float32) * scale
        sc = jnp.where(ok_c, sc, -jnp.inf)
        sn = lax.dot_general(q, kn, nt, preferred_element_type=jnp.float32) * scale
        sn = jnp.where(ok_n, sn, -jnp.inf)
        m = jnp.maximum(jnp.max(sc, axis=-1, keepdims=True), jnp.max(sn, axis=-1, keepdims=True))
        pc = jnp.exp(sc - m)
        pn = jnp.exp(sn - m)
        den = jnp.sum(pc, axis=-1, keepdims=True) + jnp.sum(pn, axis=-1, keepdims=True)
        o = (lax.dot_general(pc.astype(jnp.bfloat16), v_t, nt, preferred_element_type=jnp.float32)
             + jnp.dot(pn.astype(jnp.bfloat16), vn, preferred_element_type=jnp.float32))
        o = jnp.where(own_head, o / den, 0.0)
        lse = jnp.where(own_head, m + jnp.log(den), 0.0)
        fold = lambda a: sum(a[h * DEC_SEQ:(h + 1) * DEC_SEQ] for h in range(HEADS_PER_GROUP))
        o_ref[s] = fold(o)
        l_ref[s] = fold(lse)
        return carry

    lax.fori_loop(0, nseq, one_seq, 0, unroll=True)


def _attn_sample(qk_s, v_s, cache_t, layer, g):
    d = ATTN_GROUPS[g][1]
    width = cache_t.shape[-1]
    seq_bytes = 2 * ATTN_OUT * width * 4
    nseq = max(1, min(SAMPLE_SEQ_TILE, SAMPLE_CACHE_TILE_BYTES // seq_bytes))
    blk = (nseq, DEC_SEQ, ATTN_OUT)
    koff = ATTN_QKV // ATTN_OUT
    out_spec = pl.BlockSpec(blk, lambda i: (i, 0, 0))
    shape = jax.ShapeDtypeStruct((DEC_BATCH, DEC_SEQ, ATTN_OUT), jnp.float32)
    o, l = pl.pallas_call(
        functools.partial(_attn_sample_kernel, dil=d, nseq=nseq),
        grid=(DEC_BATCH // nseq,),
        in_specs=[pl.BlockSpec(blk, lambda i: (i, 0, g)),
                  pl.BlockSpec(blk, lambda i: (i, 0, koff + g)),
                  pl.BlockSpec(blk, lambda i: (i, 0, g)),
                  pl.BlockSpec((None, nseq, 2, HEADS_PER_GROUP, HEAD_DIM, width),
                               lambda i: (layer, i, 0, 0, 0, 0))],
        out_specs=[out_spec, out_spec],
        out_shape=[shape, shape],
        compiler_params=_params(("parallel",), 4 * SAMPLE_CACHE_TILE_BYTES),
        name=f"attn_sample_g{g}",
    )(qk_s, qk_s, v_s, cache_t)
    to_rows = lambda a: jnp.transpose(a, (1, 0, 2)).reshape(S_ROWS, ATTN_OUT)
    return to_rows(o), to_rows(l)


def _ssm_tables(lam_re, lam_im, log_dt, b_re, b_im, c_re, c_im, d_skip):
    dt = jnp.exp(log_dt)[:, None]
    er = jnp.exp(lam_re * dt)
    lbr, lbi = er * jnp.cos(lam_im * dt), er * jnp.sin(lam_im * dt)
    den = lam_re * lam_re + lam_im * lam_im
    nr, ni = lbr - 1.0, lbi
    qr = (nr * lam_re + ni * lam_im) / den
    qi = (ni * lam_re - nr * lam_im) / den
    bbr = qr[:, :, None] * b_re - qi[:, :, None] * b_im
    bbi = qr[:, :, None] * b_im + qi[:, :, None] * b_re
    eye = jnp.eye(SSM_GB, dtype=jnp.float32)

    def in_block(b):
        b = b.reshape(N_GB, SSM_GB, SSM_STATE, SSM_CH_PER_GROUP)
        return jnp.einsum('ngpc,gh->ngchp', b, eye).reshape(N_GB, GB_CH, GB_ST)

    def out_block(c):
        c = c.reshape(N_GB, SSM_GB, SSM_CH_PER_GROUP, SSM_STATE)
        return jnp.einsum('ngcp,gh->ngphc', c, eye).reshape(N_GB, GB_ST, GB_CH)

    b_blk = jnp.concatenate([in_block(bbr), in_block(bbi)], axis=2).astype(jnp.bfloat16)
    c_blk = jnp.concatenate([out_block(c_re), out_block(-c_im)], axis=1).astype(jnp.bfloat16)

    def pack(re, im):
        re = re.reshape(N_GB, 1, GB_ST)
        im = im.reshape(N_GB, 1, GB_ST)
        return jnp.concatenate([re, re], axis=2), jnp.concatenate([-im, im], axis=2)

    pr, pi = lbr, lbi
    for _ in range(int(math.log2(SSM_CHUNK))):
        pr, pi = pr * pr - pi * pi, 2.0 * pr * pi
    return dict(b=b_blk, c=c_blk, lam=pack(lbr, lbi), lam_chunk=pack(pr, pi), d=d_skip.reshape(1, SSM_WIDTH))


def _cstep(s, a, bsw):
    swapped = jnp.concatenate([s[:, GB_ST:], s[:, :GB_ST]], axis=1)
    return a * s + bsw * swapped


def _ssm_kernel(u_ref, s0_ref, b_ref, c_ref, a_ref, bsw_ref, d_ref, *refs, steps, emit_y):
    if emit_y:
        y_ref, sout_ref, s_ref = refs
    else:
        sout_ref, s_ref = refs
    t = pl.program_id(1)

    @pl.when(t == 0)
    def _():
        s_ref[...] = s0_ref[...]

    u = u_ref[...]
    s = _cstep(s_ref[...], a_ref[...], bsw_ref[...]) + _bdot(u, b_ref[...])
    s_ref[...] = s
    if emit_y:
        y_ref[...] = _bdot(s, c_ref[...]) + d_ref[...] * u

    @pl.when(t == steps - 1)
    def _():
        sout_ref[...] = s


def _ssm_pass(u_arr, u_map, y_map, rows, steps, s0, tabs, y_shape, name):
    emit_y = y_shape is not None
    st_spec = pl.BlockSpec((rows, 2 * GB_ST), lambda gb, t: (0, gb))
    coef = pl.BlockSpec((None, 1, 2 * GB_ST), lambda gb, t: (gb, 0, 0))
    in_specs = [pl.BlockSpec((rows, GB_CH), u_map), st_spec,
                pl.BlockSpec((None, GB_CH, 2 * GB_ST), lambda gb, t: (gb, 0, 0)),
                pl.BlockSpec((None, 2 * GB_ST, GB_CH), lambda gb, t: (gb, 0, 0)),
                coef, coef,
                pl.BlockSpec((1, GB_CH), lambda gb, t: (0, gb))]
    st_shape = jax.ShapeDtypeStruct((rows, ST_WIDTH), jnp.float32)
    out_specs, out_shape = [st_spec], [st_shape]
    if emit_y:
        out_specs, out_shape = [pl.BlockSpec((rows, GB_CH), y_map), st_spec], [y_shape, st_shape]
    return pl.pallas_call(
        functools.partial(_ssm_kernel, steps=steps, emit_y=emit_y),
        grid=(N_GB, steps),
        in_specs=in_specs,
        out_specs=out_specs,
        out_shape=out_shape,
        scratch_shapes=[pltpu.VMEM((rows, 2 * GB_ST), jnp.float32)],
        compiler_params=_params(("parallel", "arbitrary"), 48 * MIB),
        name=name,
    )(u_arr, s0, tabs['b'], tabs['c'], tabs['lam'][0], tabs['lam'][1], tabs['d'])


def _ssm_prompt_kernel(*refs, emit_y):
    u_refs = refs[:HALVES]
    s0_ref, b_ref, c_ref, a_ref, bsw_ref, d_ref = refs[HALVES:HALVES + 6]
    if emit_y:
        y_ref, sout_ref, s_ref = refs[HALVES + 6:]
    else:
        sout_ref, s_ref = refs[HALVES + 6:]
    t = pl.program_id(2)
    rows = pl.ds(t, s_ref.shape[0], stride=SSM_CHUNK)

    @pl.when(t == 0)
    def _():
        s_ref[...] = s0_ref[...]

    u = jnp.concatenate([r[rows, :] for r in u_refs], axis=-1)
    s = _cstep(s_ref[...], a_ref[...], bsw_ref[...]) + _bdot(u, b_ref[...])
    s_ref[...] = s
    if emit_y:
        y = _bdot(s, c_ref[...]) + d_ref[...] * u
        for half in range(HALVES):
            y_ref[half, rows, :] = y[:, half * V7X_LANES:(half + 1) * V7X_LANES]

    @pl.when(t == SSM_CHUNK - 1)
    def _():
        sout_ref[...] = s


def _ssm_prompt_pass(za, s0, tabs, emit_y, name):
    n_rows = P_ROWS // SSM_CHUNK // SSM_BATCH_SPLIT
    nat_rows = P_ROWS // SSM_BATCH_SPLIT
    ucol = OFF_U // V7X_LANES
    once = pl.Buffered(1)
    u_specs = [pl.BlockSpec((nat_rows, V7X_LANES), lambda gb, p, t, h=h: (p, ucol + gb * HALVES + h),
                            pipeline_mode=once) for h in range(HALVES)]
    st_spec = pl.BlockSpec((n_rows, 2 * GB_ST), lambda gb, p, t: (p, gb))
    s0_spec = pl.BlockSpec((n_rows, 2 * GB_ST), lambda gb, p, t: (p, gb), pipeline_mode=once)
    coef = pl.BlockSpec((None, 1, 2 * GB_ST), lambda gb, p, t: (gb, 0, 0))
    in_specs = u_specs + [s0_spec,
                          pl.BlockSpec((None, GB_CH, 2 * GB_ST), lambda gb, p, t: (gb, 0, 0)),
                          pl.BlockSpec((None, 2 * GB_ST, GB_CH), lambda gb, p, t: (gb, 0, 0)),
                          coef, coef,
                          pl.BlockSpec((1, GB_CH), lambda gb, p, t: (0, gb))]
    st_shape = jax.ShapeDtypeStruct((P_ROWS // SSM_CHUNK, ST_WIDTH), jnp.float32)
    out_specs, out_shape = [st_spec], [st_shape]
    if emit_y:
        out_specs = [pl.BlockSpec((HALVES, nat_rows, V7X_LANES), lambda gb, p, t: (gb, p, 0)), st_spec]
        out_shape = [jax.ShapeDtypeStruct((N_GB * HALVES, P_ROWS, V7X_LANES), jnp.float32), st_shape]
    return pl.pallas_call(
        functools.partial(_ssm_prompt_kernel, emit_y=emit_y),
        grid=(N_GB, SSM_BATCH_SPLIT, SSM_CHUNK),
        in_specs=in_specs,
        out_specs=out_specs,
        out_shape=out_shape,
        scratch_shapes=[pltpu.VMEM((n_rows, 2 * GB_ST), jnp.float32)],
        compiler_params=_params(("parallel", "parallel", "arbitrary"), 58 * MIB),
        name=name,
    )(*([za] * HALVES), s0, tabs['b'], tabs['c'], tabs['lam'][0], tabs['lam'][1], tabs['d'])


def _ssm_carry_kernel(e_ref, a_ref, bsw_ref, sin_ref, fin_ref):
    a = a_ref[...]
    bsw = bsw_ref[...]
    n_chunks = SEQ // SSM_CHUNK

    def body(n, states):
        new = []
        for b in range(BATCH):
            row = b * n_chunks + n
            sin_ref[pl.ds(row, 1), :] = states[b]
            new.append(_cstep(states[b], a, bsw) + e_ref[pl.ds(row, 1), :])
        return tuple(new)

    zero = jnp.zeros((1, 2 * GB_ST), jnp.float32)
    final = lax.fori_loop(0, n_chunks, body, (zero,) * BATCH)
    for b in range(BATCH):
        fin_ref[pl.ds(b, 1), :] = final[b]


def _ssm_carry(e, tabs):
    n_rows = P_ROWS // SSM_CHUNK
    st_spec = lambda rows: pl.BlockSpec((rows, 2 * GB_ST), lambda gb: (0, gb))
    coef = pl.BlockSpec((None, 1, 2 * GB_ST), lambda gb: (gb, 0, 0))
    return pl.pallas_call(
        _ssm_carry_kernel,
        grid=(N_GB,),
        in_specs=[st_spec(n_rows), coef, coef],
        out_specs=[st_spec(n_rows), st_spec(BATCH)],
        out_shape=[jax.ShapeDtypeStruct((n_rows, ST_WIDTH), jnp.float32),
                   jax.ShapeDtypeStruct((BATCH, ST_WIDTH), jnp.float32)],
        compiler_params=_params(("parallel",), 32 * MIB),
        name="ssm_carry",
    )(e, tabs['lam_chunk'][0], tabs['lam_chunk'][1])


def _pack_state(s):
    b = s.shape[0]
    s = s.reshape(b, N_GB, SSM_GB * SSM_STATE, 2)
    return jnp.transpose(s, (0, 1, 3, 2)).reshape(b, ST_WIDTH)


def _unpack_state(s):
    b = s.shape[0]
    s = s.reshape(b, N_GB, 2, SSM_GB * SSM_STATE)
    return jnp.transpose(s, (0, 1, 3, 2)).reshape(b, SSM_GROUPS, SSM_STATE, 2)


def _ssm(za, state_s, tabs):
    ucol = OFF_U // GB_CH
    zero = jnp.zeros((P_ROWS // SSM_CHUNK, ST_WIDTH), jnp.float32)
    (e,) = _ssm_prompt_pass(za, zero, tabs, False, "ssm_prompt_local")
    s_in, s_fin = _ssm_carry(e, tabs)
    y_p, _ = _ssm_prompt_pass(za, s_in, tabs, True, "ssm_prompt")
    tblk = P_ROWS // DEC_BATCH
    ys_shape = jax.ShapeDtypeStruct((S_ROWS, SSM_WIDTH), jnp.float32)
    y_s, s_new = _ssm_pass(za, lambda gb, t: (tblk + t, ucol + gb), lambda gb, t: (t, gb), DEC_BATCH, DEC_SEQ,
                           _pack_state(state_s), tabs, ys_shape, "ssm_sample")
    return y_p, y_s, _unpack_state(s_fin), _unpack_state(s_new)


def _mix_kernel(x_ref, gate_ref, gb_ref, cp_ref, cs_ref, ap_refs, as_refs, sp_ref, ss_ref,
                wglu_ref, bglu_ref, pc_ref, pa_ref, ps_ref, wo_ref, gn_ref, xo_ref, ho_ref):
    is_p = pl.program_id(0) < P_ROWS // MIX_TILE

    def pick(p_ref, s_ref):
        p = p_ref[...]
        if p.ndim == 3:
            p = jnp.concatenate([p[i] for i in range(p.shape[0])], axis=-1)
        return jnp.where(is_p, p, s_ref[...])

    y_conv = pick(cp_ref, cs_ref)
    o = [pick(ap_refs[2 * g], as_refs[2 * g]) for g in range(N_ATTN_GROUPS)]
    lse = [pick(ap_refs[2 * g + 1], as_refs[2 * g + 1]) for g in range(N_ATTN_GROUPS)]
    m = jnp.maximum(jnp.maximum(lse[0], lse[1]), lse[2])
    w = [jnp.exp(l - m) for l in lse]
    y_attn = (w[0] * o[0] + w[1] * o[1] + w[2] * o[2]) / (w[0] + w[1] + w[2])
    zg = jax.nn.gelu(pick(sp_ref, ss_ref))
    y_ssm = zg * _sigmoid(_bdot(zg, wglu_ref[...]) + bglu_ref[...])
    gt = _sigmoid(gate_ref[...].astype(jnp.float32) + gb_ref[...])
    merged = (gt[:, :D_MODEL] * _bdot(y_conv, pc_ref[...])
              + gt[:, D_MODEL:2 * D_MODEL] * _bdot(y_attn, pa_ref[...])
              + gt[:, 2 * D_MODEL:] * _bdot(y_ssm, ps_ref[...]))
    x = x_ref[...] + _bdot(merged, wo_ref[...])
    xo_ref[...] = x
    ho_ref[...] = _rms(x, gn_ref[...]).astype(ho_ref.dtype)


def _mix(x, gates, conv_p, conv_s, attn_p, attn_s, ssm_p, ssm_s, lp, h_dtype):
    n_p = P_ROWS // MIX_TILE
    p_map = lambda i: (jnp.minimum(i, n_p - 1), 0)
    s_map = lambda i: (jnp.maximum(i - n_p, 0), 0)
    row_map = lambda i: (i, 0)
    const = lambda i: (0, 0)
    rows = lambda width, imap: pl.BlockSpec((MIX_TILE, width), imap)
    slabs = lambda n: pl.BlockSpec((n, MIX_TILE, V7X_LANES), lambda i: (0, jnp.minimum(i, n_p - 1), 0))
    full = lambda a: pl.BlockSpec(a.shape, const, pipeline_mode=pl.Buffered(1))
    n_attn = 2 * N_ATTN_GROUPS
    weights = [lp['w_glu'], lp['b_glu'], lp['proj_conv'], lp['proj_attn'], lp['proj_ssm'], lp['w_out'],
               lp['norm_next']]
    in_specs = ([rows(D_MODEL, row_map), rows(GATE_WIDTH, row_map), full(lp['gate_bias']),
                 rows(CONV_WIDTH, p_map), rows(CONV_WIDTH, s_map)]
                + [slabs(HALVES)] * n_attn + [rows(ATTN_OUT, s_map)] * n_attn
                + [slabs(N_GB * HALVES), rows(SSM_WIDTH, s_map)]
                + [full(w) for w in weights])

    def body(*refs):
        x_ref, gate_ref, gb_ref, cp_ref, cs_ref = refs[:5]
        ap_refs = refs[5:5 + n_attn]
        as_refs = refs[5 + n_attn:5 + 2 * n_attn]
        rest = refs[5 + 2 * n_attn:]
        _mix_kernel(x_ref, gate_ref, gb_ref, cp_ref, cs_ref, ap_refs, as_refs, *rest)

    return pl.pallas_call(
        body,
        grid=(ROWS // MIX_TILE,),
        in_specs=in_specs,
        out_specs=[rows(D_MODEL, row_map), rows(D_MODEL, row_map)],
        out_shape=[jax.ShapeDtypeStruct((ROWS, D_MODEL), jnp.float32),
                   jax.ShapeDtypeStruct((ROWS, D_MODEL), h_dtype)],
        compiler_params=_params(("parallel",), 56 * MIB),
        name="mix",
    )(x, gates, lp['gate_bias'], conv_p, conv_s, *attn_p, *attn_s, ssm_p, ssm_s, *weights)


def _ffn_up_kernel(be_ref, *refs, n_side, cast):
    x_ref, w1_ref, w3_ref = refs[:3]
    h_ref = refs[3 + n_side]
    _side_cast(refs[3:3 + n_side], refs[4 + n_side:4 + 2 * n_side])
    if cast:
        w1b_ref, w3b_ref = refs[-2:]

        @pl.when(pl.program_id(1) == 0)
        def _():
            w1b_ref[...] = w1_ref[...].astype(w1b_ref.dtype)
            w3b_ref[...] = w3_ref[...].astype(w3b_ref.dtype)
    else:
        w1b_ref, w3b_ref = w1_ref, w3_ref
    x = x_ref[...].astype(jnp.bfloat16)
    a = jnp.dot(x, w1b_ref[...], preferred_element_type=jnp.float32)
    b = jnp.dot(x, w3b_ref[...], preferred_element_type=jnp.float32)
    h_ref[...] = (a * _sigmoid(a) * b).astype(h_ref.dtype)


def _ffn_up(block_e, x, w1, w3, bm, bf, name, side=()):
    m = x.shape[0]
    e, _, f = w1.shape
    cast = w1.dtype != jnp.bfloat16
    assert not cast or e == 1
    grid = (f // bf, m // bm)
    assert all(arr.shape[0] // rows <= grid[0] * grid[1] for arr, rows in side)
    s_in, s_out, s_shape = _side_cast_specs(side, grid[1])
    wspec = pl.BlockSpec((None, D_MODEL, bf), lambda j, i, be: (be[i], 0, j))
    grid_spec = pltpu.PrefetchScalarGridSpec(
        num_scalar_prefetch=1,
        grid=grid,
        in_specs=[pl.BlockSpec((bm, D_MODEL), lambda j, i, be: (i, 0)), wspec, wspec] + s_in,
        out_specs=[pl.BlockSpec((bm, bf), lambda j, i, be: (i, j))] + s_out,
        scratch_shapes=[pltpu.VMEM((D_MODEL, bf), jnp.bfloat16)] * 2 if cast else [],
    )
    out = pl.pallas_call(
        functools.partial(_ffn_up_kernel, n_side=len(side), cast=cast),
        grid_spec=grid_spec,
        out_shape=[jax.ShapeDtypeStruct((m, f), jnp.bfloat16)] + s_shape,
        compiler_params=_params(("arbitrary", "arbitrary"), 56 * MIB),
        name=name,
    )(block_e, x, w1, w3, *[arr for arr, _ in side])
    return out if side else out[0]


def _ffn_down_dense_kernel(h_ref, w_ref, x_ref, g_ref, xo_ref, ho_ref, acc_ref, *, n_k):
    k = pl.program_id(1)

    @pl.when(k == 0)
    def _():
        acc_ref[...] = x_ref[...]

    acc_ref[...] += jnp.dot(h_ref[...], w_ref[...], preferred_element_type=jnp.float32)

    @pl.when(k == n_k - 1)
    def _():
        x = acc_ref[...]
        xo_ref[...] = x
        ho_ref[...] = _rms(x, g_ref[...]).astype(ho_ref.dtype)


def _ffn_down_dense(h, w2, x, g_next, bk):
    n_k = D_FF // bk
    row = pl.BlockSpec((ROW_TILE, D_MODEL), lambda i, k: (i, 0))
    return pl.pallas_call(
        functools.partial(_ffn_down_dense_kernel, n_k=n_k),
        grid=(ROWS // ROW_TILE, n_k),
        in_specs=[pl.BlockSpec((ROW_TILE, bk), lambda i, k: (i, k)),
                  pl.BlockSpec((bk, D_MODEL), lambda i, k: (k, 0)),
                  row, pl.BlockSpec((1, D_MODEL), lambda i, k: (0, 0))],
        out_specs=[row, row],
        out_shape=[jax.ShapeDtypeStruct((ROWS, D_MODEL), jnp.float32),
                   jax.ShapeDtypeStruct((ROWS, D_MODEL), jnp.bfloat16)],
        scratch_shapes=[pltpu.VMEM((ROW_TILE, D_MODEL), jnp.float32)],
        compiler_params=_params(("parallel", "arbitrary"), 58 * MIB),
        name="ffn_down",
    )(h, w2, x, g_next.reshape(1, D_MODEL))


def _moe_down_kernel(be_ref, h_ref, w_ref, g_ref, y_ref):
    y = jnp.dot(h_ref[...], w_ref[...], preferred_element_type=jnp.float32)
    _store_chunk_rows(y_ref, y * g_ref[...])


def _moe_down(block_e, h, w2, row_gate):
    grid_spec = pltpu.PrefetchScalarGridSpec(
        num_scalar_prefetch=1,
        grid=(MOE_BLOCKS,),
        in_specs=[pl.BlockSpec((MOE_BLOCK, D_FF_EXPERT), lambda i, be: (i, 0)),
                  pl.BlockSpec((None, D_FF_EXPERT, D_MODEL), lambda i, be: (be[i], 0, 0)),
                  pl.BlockSpec((MOE_BLOCK, 1), lambda i, be: (i, 0))],
        out_specs=pl.BlockSpec((MOE_BLOCK * ROW_CHUNKS, V7X_LANES), lambda i, be: (i, 0)),
    )
    return pl.pallas_call(
        _moe_down_kernel,
        grid_spec=grid_spec,
        out_shape=jax.ShapeDtypeStruct((MOE_ROWS * ROW_CHUNKS, V7X_LANES), jnp.float32),
        compiler_params=_params(("arbitrary",), 48 * MIB),
        name="moe_down",
    )(block_e, h, w2, row_gate)


def _router_kernel(x_ref, g_ref, w_ref, hs_ref, r_ref):
    hs = _rms(x_ref[...], g_ref[...])
    _store_chunk_rows(hs_ref, hs)
    logits = jnp.dot(hs, w_ref[...], preferred_element_type=jnp.float32, precision=lax.Precision.HIGHEST)
    lane = lax.broadcasted_iota(jnp.int32, logits.shape, 1)
    logits = jnp.where(lane < N_EXPERTS, logits, -jnp.inf)
    m1 = jnp.max(logits, axis=-1, keepdims=True)
    i1 = jnp.min(jnp.where(logits == m1, lane, ROUTER_PAD), axis=-1, keepdims=True)
    rest = jnp.where(lane == i1, -jnp.inf, logits)
    m2 = jnp.max(rest, axis=-1, keepdims=True)
    i2 = jnp.min(jnp.where(rest == m2, lane, ROUTER_PAD), axis=-1, keepdims=True)
    e2 = jnp.exp(m2 - m1)
    g1 = 1.0 / (1.0 + e2)
    g2 = e2 / (1.0 + e2)
    out = jnp.where(lane == 0, i1.astype(jnp.float32),
                    jnp.where(lane == 1, i2.astype(jnp.float32),
                              jnp.where(lane == 2, g1, jnp.where(lane == 3, g2, 0.0))))
    r_ref[...] = out


def _router(x, g, w_router):
    w = jnp.zeros((D_MODEL, ROUTER_PAD), jnp.float32).at[:, :N_EXPERTS].set(w_router)
    row = pl.BlockSpec((ROW_TILE, D_MODEL), lambda i: (i, 0))
    return pl.pallas_call(
        _router_kernel,
        grid=(ROWS // ROW_TILE,),
        in_specs=[row, pl.BlockSpec((1, D_MODEL), lambda i: (0, 0)),
                  pl.BlockSpec((D_MODEL, ROUTER_PAD), lambda i: (0, 0))],
        out_specs=[pl.BlockSpec((ROW_TILE * ROW_CHUNKS, V7X_LANES), lambda i: (i, 0)),
                   pl.BlockSpec((ROW_TILE, ROUTER_PAD), lambda i: (i, 0))],
        out_shape=[jax.ShapeDtypeStruct((ROWS * ROW_CHUNKS, V7X_LANES), jnp.float32),
                   jax.ShapeDtypeStruct((ROWS, ROUTER_PAD), jnp.float32)],
        compiler_params=_params(("parallel",), 32 * MIB),
        name="router",
    )(x, g.reshape(1, D_MODEL), w)


def _gather_step(idx_ref, src_ref, buf, sem, rows):
    i = pl.program_id(0)
    slot = i % 2

    def copy(step, slot_, r):
        return pltpu.make_async_copy(src_ref.at[idx_ref[step * rows + r]],
                                     buf.at[slot_, pl.ds(r * ROW_CHUNKS, ROW_CHUNKS), :], sem.at[slot_])

    def start_all(step, slot_):
        def body(r, c):
            copy(step, slot_, r).start()
            return c
        lax.fori_loop(0, rows, body, 0, unroll=8)

    @pl.when(i == 0)
    def _():
        start_all(0, 0)

    @pl.when(i + 1 < pl.num_programs(0))
    def _():
        start_all(i + 1, 1 - slot)

    def wait_body(r, c):
        copy(i, slot, r).wait()
        return c
    lax.fori_loop(0, rows, wait_body, 0, unroll=8)
    return slot


def _dispatch_kernel(tok_ref, hs_ref, o_ref, buf, sem):
    slot = _gather_step(tok_ref, hs_ref, buf, sem, MOE_BLOCK)
    o_ref[...] = _load_chunk_rows(buf.at[slot], 0, MOE_BLOCK).astype(o_ref.dtype)


def _dispatch(row_tok, hs):
    grid_spec = pltpu.PrefetchScalarGridSpec(
        num_scalar_prefetch=1,
        grid=(MOE_BLOCKS,),
        in_specs=[pl.BlockSpec(memory_space=pl.ANY)],
        out_specs=pl.BlockSpec((MOE_BLOCK, D_MODEL), lambda i, tok: (i, 0)),
        scratch_shapes=[pltpu.VMEM((2, MOE_BLOCK * ROW_CHUNKS, V7X_LANES), jnp.float32),
                        pltpu.SemaphoreType.DMA((2,))],
    )
    return pl.pallas_call(
        _dispatch_kernel,
        grid_spec=grid_spec,
        out_shape=jax.ShapeDtypeStruct((MOE_ROWS, D_MODEL), jnp.bfloat16),
        compiler_params=_params(("arbitrary",), 16 * MIB),
        name="moe_dispatch",
    )(row_tok, hs)


def _combine_kernel(pos_ref, x_ref, y_ref, g_ref, op_ref, os_ref, buf, sem):
    slot = _gather_step(pos_ref, y_ref, buf, sem, TOP_K * MIX_TILE)
    y = sum(_load_chunk_rows(buf.at[slot], k * MIX_TILE, MIX_TILE) for k in range(TOP_K))
    out = _rms(x_ref[...] + y, g_ref[...])
    is_p = pl.program_id(0) < P_ROWS // MIX_TILE

    @pl.when(is_p)
    def _():
        op_ref[...] = out

    @pl.when(jnp.logical_not(is_p))
    def _():
        os_ref[...] = out


def _combine(pos, x, yb, g_final):
    rows = TOP_K * MIX_TILE
    n_p = P_ROWS // MIX_TILE
    grid_spec = pltpu.PrefetchScalarGridSpec(
        num_scalar_prefetch=1,
        grid=(ROWS // MIX_TILE,),
        in_specs=[pl.BlockSpec((MIX_TILE, D_MODEL), lambda i, pos: (i, 0)),
                  pl.BlockSpec(memory_space=pl.ANY),
                  pl.BlockSpec((1, D_MODEL), lambda i, pos: (0, 0))],
        out_specs=[pl.BlockSpec((MIX_TILE, D_MODEL), lambda i, pos: (jnp.minimum(i, n_p - 1), 0)),
                   pl.BlockSpec((MIX_TILE, D_MODEL), lambda i, pos: (jnp.maximum(i - n_p, 0), 0))],
        scratch_shapes=[pltpu.VMEM((2, rows * ROW_CHUNKS, V7X_LANES), jnp.float32),
                        pltpu.SemaphoreType.DMA((2,))],
    )
    return pl.pallas_call(
        _combine_kernel,
        grid_spec=grid_spec,
        out_shape=[jax.ShapeDtypeStruct((P_ROWS, D_MODEL), jnp.float32),
                   jax.ShapeDtypeStruct((S_ROWS, D_MODEL), jnp.float32)],
        compiler_params=_params(("arbitrary",), 32 * MIB),
        name="moe_combine",
    )(pos, x, yb, g_final.reshape(1, D_MODEL))


def _dispatch_plan(route):
    top_idx = route[:, 0:TOP_K].astype(jnp.int32)
    gates = route[:, TOP_K:2 * TOP_K]
    flat_e = top_idx.reshape(N_ASSIGN)
    order = jnp.argsort(flat_e).astype(jnp.int32)
    rank = jnp.argsort(order).astype(jnp.int32)
    counts = jnp.sum((flat_e[:, None] == jnp.arange(N_EXPERTS)[None, :]).astype(jnp.int32), axis=0)
    padded = (counts + MOE_BLOCK - 1) // MOE_BLOCK * MOE_BLOCK
    start = jnp.cumsum(counts) - counts
    ends_p = jnp.cumsum(padded)
    pstart = ends_p - padded
    first_row = jnp.arange(MOE_BLOCKS, dtype=jnp.int32) * MOE_BLOCK
    block_e = jnp.minimum(jnp.sum((ends_p[None, :] <= first_row[:, None]).astype(jnp.int32), axis=1),
                          N_EXPERTS - 1).astype(jnp.int32)
    row = jnp.arange(MOE_ROWS, dtype=jnp.int32)
    row_e = jnp.repeat(block_e, MOE_BLOCK)
    off = row - pstart[row_e]
    real = off < counts[row_e]
    assign = order[jnp.clip(start[row_e] + off, 0, N_ASSIGN - 1)]
    row_tok = jnp.where(real, assign // TOP_K, 0).astype(jnp.int32)
    row_gate = jnp.where(real, gates.reshape(N_ASSIGN)[assign], 0.0)
    pos = (pstart[flat_e] + rank - start[flat_e]).astype(jnp.int32)
    pos = jnp.transpose(pos.reshape(ROWS // MIX_TILE, MIX_TILE, TOP_K), (0, 2, 1)).reshape(N_ASSIGN)
    return row_tok, row_gate.reshape(MOE_ROWS, 1), block_e, pos


def kernel(x_prompt, x_sample, cache_kv_w128, cache_kv_w512, cache_kv_w2048, state_conv, state_ssm, norm_mix, w_in, gate_bias, conv_w, ssm_lam_re, ssm_lam_im, ssm_log_dt, ssm_b_re, ssm_b_im, ssm_c_re, ssm_c_im, ssm_d, ssm_w_glu, ssm_b_glu, proj_conv, proj_attn, proj_ssm, w_out, norm_ffn, ffn_w1, ffn_w3, ffn_w2, router_w, moe_w1, moe_w3, moe_w2, norm_final):
    bf = jnp.bfloat16
    caches = tuple(jnp.transpose(c, (0, 1, 3, 4, 5, 2)) for c in (cache_kv_w128, cache_kv_w512, cache_kv_w2048))
    tables = _rope_tables()
    x, h = _norm(x_prompt.reshape(P_ROWS, D_MODEL),
                 jnp.transpose(x_sample, (1, 0, 2)).reshape(S_ROWS, D_MODEL), norm_mix[0])
    kv_p = [[] for _ in range(N_ATTN_GROUPS)]
    kv_s = [[] for _ in range(N_ATTN_GROUPS)]
    conv_p_all, conv_s_all, ssm_p_all, ssm_s_all = [], [], [], []
    y = None
    for l in range(DEPTH):
        moe_layer = l % 2 == 1
        if moe_layer:
            za = _linear(h, w_in, l, 0, ZA_WIDTH, jnp.float32, IN_TILE, f"in_proj_{l}")
            w2_flat = moe_w2[l // 2].reshape(N_EXPERTS * D_FF_EXPERT, D_MODEL)
            gates, moe_w2_bf = _linear(h, w_in, l, ZA_WIDTH, GATE_WIDTH, bf, IN_TILE, f"gate_proj_{l}",
                                       side=[(w2_flat, SIDE_CAST_ROWS_W2)])
        else:
            za, ffn_w2_bf = _linear(h, w_in, l, 0, ZA_WIDTH, jnp.float32, IN_TILE, f"in_proj_{l}",
                                    side=[(ffn_w2[l // 2], SIDE_CAST_ROWS_FFN)])
            w3_flat = moe_w3[(l + 1) // 2].reshape(N_EXPERTS * D_MODEL, D_FF_EXPERT)
            gates, moe_w3_bf = _linear(h, w_in, l, ZA_WIDTH, GATE_WIDTH, bf, IN_TILE, f"gate_proj_{l}",
                                       side=[(w3_flat, SIDE_CAST_ROWS_W3)])
        qk = _rope(za, tables)
        conv_p, cst_p = _conv_prompt(za, conv_w[l])
        conv_s, cst_s = _conv_sample(za, conv_w[l], jnp.transpose(state_conv[l], (1, 0, 2)))
        by_seq = lambda a: jnp.transpose(a.reshape(DEC_SEQ, DEC_BATCH, a.shape[-1]), (1, 0, 2))
        qk_s = by_seq(qk[P_ROWS:])
        v_s = by_seq(za[P_ROWS:, OFF_V:OFF_V + ATTN_QKV])
        attn_p, attn_s = [], []
        for g in range(N_ATTN_GROUPS):
            attn_p += list(_attn_prompt(qk, za, g))
            attn_s += list(_attn_sample(qk_s, v_s, caches[g], l, g))
        tabs = _ssm_tables(ssm_lam_re[l], ssm_lam_im[l], ssm_log_dt[l], ssm_b_re[l], ssm_b_im[l],
                           ssm_c_re[l], ssm_c_im[l], ssm_d[l])
        ssm_p, ssm_s, sst_p, sst_s = _ssm(za, state_ssm[l], tabs)
        moe_layer = l % 2 == 1
        lp = dict(gate_bias=gate_bias[l].reshape(1, GATE_WIDTH), w_glu=ssm_w_glu[l].astype(bf),
                  b_glu=ssm_b_glu[l].reshape(1, SSM_WIDTH), proj_conv=proj_conv[l].astype(bf),
                  proj_attn=proj_attn[l].astype(bf), proj_ssm=proj_ssm[l].astype(bf),
                  w_out=w_out[l].astype(bf), norm_next=norm_ffn[l].reshape(1, D_MODEL))
        x, h = _mix(x, gates, conv_p, conv_s, attn_p, attn_s, ssm_p, ssm_s, lp, bf)
        i = l // 2
        if not moe_layer:
            zero_e = jnp.zeros((ROWS // ROW_TILE,), jnp.int32)
            w1_flat = moe_w1[(l + 1) // 2].reshape(N_EXPERTS * D_MODEL, D_FF_EXPERT)
            hid, moe_w1_bf = _ffn_up(zero_e, h, ffn_w1[i:i + 1], ffn_w3[i:i + 1], ROW_TILE, 512, "ffn_up",
                                     side=[(w1_flat, SIDE_CAST_ROWS_W1)])
            x, h = _ffn_down_dense(hid, ffn_w2_bf, x, norm_mix[l + 1], D_FF // 2)
        else:
            hs, route = _router(x, norm_ffn[l], router_w[i])
            row_tok, row_gate, block_e, pos = _dispatch_plan(route)
            xs = _dispatch(row_tok, hs.reshape(ROWS, ROW_CHUNKS, V7X_LANES))
            expert = lambda w, k: w.reshape(N_EXPERTS, k, w.shape[-1])
            hid = _ffn_up(block_e, xs, expert(moe_w1_bf, D_MODEL), expert(moe_w3_bf, D_MODEL),
                          MOE_BLOCK, D_FF_EXPERT // 2, "moe_up")
            yb = _moe_down(block_e, hid, expert(moe_w2_bf, D_FF_EXPERT), row_gate)
            y = _combine(pos, x, yb.reshape(MOE_ROWS, ROW_CHUNKS, V7X_LANES), norm_final)
        for g, (kp, ks) in enumerate(zip(_kv_prompt(qk, za), _kv_sample(qk, za))):
            kp = kp.reshape(BATCH, 2, HEADS_PER_GROUP, HEAD_DIM, kp.shape[-1])
            kv_p[g].append(jnp.transpose(kp, (0, 4, 1, 2, 3)))
            ks = ks.reshape(DEC_SEQ, 2, HEADS_PER_GROUP, HEAD_DIM, DEC_BATCH)
            kv_s[g].append(jnp.transpose(ks, (4, 0, 1, 2, 3)))
        conv_p_all.append(cst_p)
        conv_s_all.append(jnp.transpose(cst_s, (1, 0, 2)))
        ssm_p_all.append(sst_p)
        ssm_s_all.append(sst_s)
    y_prompt = y[0].reshape(BATCH, SEQ, D_MODEL)
    y_sample = jnp.transpose(y[1].reshape(DEC_SEQ, DEC_BATCH, D_MODEL), (1, 0, 2))
    return (y_prompt, y_sample, jnp.stack(kv_p[0]), jnp.stack(kv_p[1]), jnp.stack(kv_p[2]),
            jnp.stack(conv_p_all), jnp.stack(ssm_p_all), jnp.stack(kv_s[0]), jnp.stack(kv_s[1]),
            jnp.stack(kv_s[2]), jnp.stack(conv_s_all), jnp.stack(ssm_s_all))
```

```python
import functools
import math

import jax
import jax.numpy as jnp
import numpy as np
from jax import lax
from jax.experimental import pallas as pl
from jax.experimental.pallas import tpu as pltpu

D_MODEL = 2048
BATCH = 4
SEQ = 2048
DEPTH = 2
DEC_BATCH = 128
DEC_SEQ = 4
PAST_LEN = 2048
CONV_WIDTH = 512
CONV_K = 3
HEAD_DIM = 64
HEADS_PER_GROUP = 4
ATTN_GROUPS = ((128, 1), (512, 4), (2048, 16))
N_ATTN_GROUPS = 3
ATTN_QKV = N_ATTN_GROUPS * HEADS_PER_GROUP * HEAD_DIM
ATTN_OUT = HEADS_PER_GROUP * HEAD_DIM
ROT_DIM = HEAD_DIM // 4
ROPE_THETA = 500000.0
SSM_WIDTH = 768
SSM_CH_PER_GROUP = 16
SSM_GROUPS = SSM_WIDTH // SSM_CH_PER_GROUP
SSM_STATE = 64
N_BRANCH = 3
D_FF = 5632
N_EXPERTS = 8
TOP_K = 2
D_FF_EXPERT = D_FF // TOP_K
EPS = 1e-6

V7X_LANES = 128
V7X_VMEM_BYTES = 64 * 1024 * 1024
MIB = 1024 * 1024

P_ROWS = BATCH * SEQ
S_ROWS = DEC_SEQ * DEC_BATCH
ROWS = P_ROWS + S_ROWS
ZA_WIDTH = 3 * CONV_WIDTH + 3 * ATTN_QKV + SSM_WIDTH
GATE_WIDTH = N_BRANCH * D_MODEL
QK_WIDTH = 2 * ATTN_QKV
OFF_Q = 3 * CONV_WIDTH
OFF_K = OFF_Q + ATTN_QKV
OFF_V = OFF_Q + 2 * ATTN_QKV
OFF_U = OFF_V + ATTN_QKV
BAND = 128
ATTN_DENSE_BLOCKS = 4
ATTN_RESIDUE_UNROLL = 4
SSM_CHUNK = 16
SSM_GB = 16
N_GB = SSM_GROUPS // SSM_GB
GB_CH = SSM_GB * SSM_CH_PER_GROUP
GB_ST = SSM_GB * SSM_STATE
ST_WIDTH = N_GB * 2 * GB_ST
ROW_TILE = 512
MIX_TILE = 256
IN_TILE = 1536
MOE_BLOCK = 256
N_ASSIGN = ROWS * TOP_K
MOE_BLOCKS = -(-(N_ASSIGN + N_EXPERTS * (MOE_BLOCK - 1)) // MOE_BLOCK)
MOE_ROWS = MOE_BLOCKS * MOE_BLOCK
ROUTER_PAD = V7X_LANES
SAMPLE_SEQ_TILE = 8
SAMPLE_CACHE_TILE_BYTES = 8 * MIB
SSM_BATCH_SPLIT = 1
HALVES = ATTN_OUT // V7X_LANES
ROW_CHUNKS = D_MODEL // V7X_LANES
SIDE_CAST_ROWS_W1 = 128
SIDE_CAST_ROWS_W3 = 256
SIDE_CAST_ROWS_W2 = 352
SIDE_CAST_ROWS_FFN = 128
assert DEPTH == 2 and HALVES == 2 and GB_CH == HALVES * V7X_LANES

assert ROWS % ROW_TILE == 0 and P_ROWS % ROW_TILE == 0
assert ROWS % MIX_TILE == 0 and P_ROWS % MIX_TILE == 0


def _params(semantics, vmem_bytes):
    return pltpu.CompilerParams(dimension_semantics=semantics,
                                vmem_limit_bytes=min(int(vmem_bytes), V7X_VMEM_BYTES - 4 * MIB))


def _rms(x, g):
    y = x * lax.rsqrt(jnp.mean(x * x, axis=-1, keepdims=True) + EPS)
    return y * g


def _bdot(a, b):
    return jnp.dot(a.astype(jnp.bfloat16), b.astype(jnp.bfloat16), preferred_element_type=jnp.float32)


def _store_chunk_rows(ref, x):
    n = x.shape[0]
    for k in range(ROW_CHUNKS):
        ref[pl.ds(k, n, stride=ROW_CHUNKS), :] = x[:, k * V7X_LANES:(k + 1) * V7X_LANES]


def _load_chunk_rows(ref, first, n):
    return jnp.concatenate([ref[pl.ds(first * ROW_CHUNKS + k, n, stride=ROW_CHUNKS), :]
                            for k in range(ROW_CHUNKS)], axis=-1)


def _sigmoid(x):
    return 0.5 * jnp.tanh(0.5 * x) + 0.5


def _norm_kernel(xp_ref, xs_ref, g_ref, x_ref, h_ref):
    x = jnp.where(pl.program_id(0) < P_ROWS // ROW_TILE, xp_ref[...], xs_ref[...])
    x_ref[...] = x
    h_ref[...] = _rms(x, g_ref[...]).astype(h_ref.dtype)


def _norm(x_p, x_s, g):
    n_p = P_ROWS // ROW_TILE
    row = pl.BlockSpec((ROW_TILE, D_MODEL), lambda i: (i, 0))
    return pl.pallas_call(
        _norm_kernel,
        grid=(ROWS // ROW_TILE,),
        in_specs=[pl.BlockSpec((ROW_TILE, D_MODEL), lambda i: (jnp.minimum(i, n_p - 1), 0)),
                  pl.BlockSpec((ROW_TILE, D_MODEL), lambda i: (jnp.maximum(i - n_p, 0), 0)),
                  pl.BlockSpec((1, D_MODEL), lambda i: (0, 0))],
        out_specs=[row, row],
        out_shape=[jax.ShapeDtypeStruct((ROWS, D_MODEL), jnp.float32),
                   jax.ShapeDtypeStruct((ROWS, D_MODEL), jnp.bfloat16)],
        compiler_params=_params(("parallel",), 40 * MIB),
        name="norm",
    )(x_p, x_s, g.reshape(1, D_MODEL))


def _side_cast_specs(side, n_inner, extra_args=0):
    in_specs, out_specs, out_shape = [], [], []
    for arr, rows in side:
        n_blk = arr.shape[0] // rows
        assert n_blk * rows == arr.shape[0]

        def imap(j, i, *_, n_blk=n_blk):
            return (jnp.minimum(j * n_inner + i, n_blk - 1), 0)
        spec = pl.BlockSpec((rows, arr.shape[1]), imap)
        in_specs.append(spec)
        out_specs.append(spec)
        out_shape.append(jax.ShapeDtypeStruct(arr.shape, jnp.bfloat16))
    return in_specs, out_specs, out_shape


def _side_cast(in_refs, out_refs):
    for src, dst in zip(in_refs, out_refs):
        dst[...] = src[...].astype(dst.dtype)


def _linear_kernel(*refs, n_side, rope_tile):
    a_ref, w_ref = refs[:2]
    n_tab = 0 if rope_tile is None else 3
    tabs = refs[2 + n_side:2 + n_side + n_tab]
    o_ref = refs[2 + n_side + n_tab]
    wb_ref = refs[-1]
    _side_cast(refs[2:2 + n_side], refs[3 + n_side + n_tab:3 + 2 * n_side + n_tab])

    @pl.when(pl.program_id(1) == 0)
    def _():
        wb_ref[...] = w_ref[...].astype(wb_ref.dtype)

    y = jnp.dot(a_ref[...], wb_ref[...], preferred_element_type=jnp.float32)
    if rope_tile is None:
        o_ref[...] = y.astype(o_ref.dtype)
    else:
        is_qk = pl.program_id(0) == rope_tile

        @pl.when(is_qk)
        def _():
            o_ref[...] = _rotate(y, *tabs).astype(o_ref.dtype)

        @pl.when(jnp.logical_not(is_qk))
        def _():
            o_ref[...] = y.astype(o_ref.dtype)


def _linear(a, w, layer, col0, n, out_dtype, bn, name, side=(), rope=None):
    m, k = a.shape
    jb = col0 // bn
    grid = (n // bn, m // ROW_TILE)
    assert all(arr.shape[0] // rows <= grid[0] * grid[1] for arr, rows in side)
    s_in, s_out, s_shape = _side_cast_specs(side, grid[1])
    rope_tile, tables = rope if rope else (None, ())
    assert rope is None or bn == QK_WIDTH
    out = pl.pallas_call(
        functools.partial(_linear_kernel, n_side=len(side), rope_tile=rope_tile),
        grid=grid,
        in_specs=[pl.BlockSpec((ROW_TILE, k), lambda j, i: (i, 0)),
                  pl.BlockSpec((None, k, bn), lambda j, i: (layer, 0, jb + j))] + s_in
                 + (_rope_table_specs() if rope else []),
        out_specs=[pl.BlockSpec((ROW_TILE, bn), lambda j, i: (i, j))] + s_out,
        out_shape=[jax.ShapeDtypeStruct((m, n), out_dtype)] + s_shape,
        scratch_shapes=[pltpu.VMEM((k, bn), jnp.bfloat16)],
        compiler_params=_params(("arbitrary", "arbitrary"), 56 * MIB),
        name=name,
    )(a, w, *[arr for arr, _ in side], *tables)
    return out if side else out[0]


def _rope_tables():
    half = ROT_DIM // 2
    inv = ROPE_THETA ** (-jnp.arange(half, dtype=jnp.float32) * 2.0 / ROT_DIM)
    pos_p = jnp.arange(SEQ, dtype=jnp.int32)
    pos_s = jnp.repeat(PAST_LEN + jnp.arange(DEC_SEQ, dtype=jnp.int32), DEC_BATCH)
    pos = jnp.concatenate([pos_p, pos_s]).astype(jnp.float32)
    ang = pos[:, None] * inv[None, :]
    cos, sin = jnp.cos(ang), jnp.sin(ang)
    ones = jnp.ones((pos.shape[0], HEAD_DIM - ROT_DIM), jnp.float32)
    zeros = jnp.zeros_like(ones)
    zh = jnp.zeros_like(sin)
    cos_t = jnp.concatenate([cos, cos, ones], axis=1)
    sin_hi = jnp.concatenate([-sin, zh, zeros], axis=1)
    sin_lo = jnp.concatenate([zh, sin, zeros], axis=1)
    reps = V7X_LANES // HEAD_DIM
    return tuple(jnp.tile(t, (1, reps)) for t in (cos_t, sin_hi, sin_lo))


def _rotate(x, c_ref, sh_ref, sl_ref):
    half = ROT_DIM // 2
    reps = QK_WIDTH // V7X_LANES
    c = jnp.tile(c_ref[...], (1, reps))
    sh = jnp.tile(sh_ref[...], (1, reps))
    sl = jnp.tile(sl_ref[...], (1, reps))
    x_up = pltpu.roll(x, QK_WIDTH - half, axis=1)
    x_dn = pltpu.roll(x, half, axis=1)
    return x * c + x_up * sh + x_dn * sl


def _rope_table_specs():
    n_p = P_ROWS // ROW_TILE
    per_seq = SEQ // ROW_TILE
    tmap = lambda j, i: (jnp.where(i < n_p, i % per_seq, per_seq), 0)
    return [pl.BlockSpec((ROW_TILE, V7X_LANES), tmap)] * 3


def _kv_prompt_kernel(k_ref, v_ref, *o_refs, keeps):
    last = pl.program_id(1) == SEQ // ROW_TILE - 1
    k, v = k_ref[...], v_ref[...]

    def write(g, lo):
        cols = slice(g * ATTN_OUT, (g + 1) * ATTN_OUT)
        o_refs[g][0] = k[lo:, cols].T
        o_refs[g][1] = v[lo:, cols].T

    for g, keep in enumerate(keeps):
        if keep == SEQ:
            write(g, 0)
        else:
            pl.when(last)(functools.partial(write, g, ROW_TILE - min(keep, ROW_TILE)))


def _kv_prompt(za):
    keeps = tuple(min(w, SEQ) for w, _ in ATTN_GROUPS)
    assert all(k == SEQ or k <= ROW_TILE for k in keeps)
    n_t = SEQ // ROW_TILE
    out_specs, out_shape = [], []
    for keep in keeps:
        blk = min(keep, ROW_TILE)
        imap = (lambda b, j: (b, 0, 0, j)) if keep == SEQ else (lambda b, j: (b, 0, 0, 0))
        out_specs.append(pl.BlockSpec((None, 2, ATTN_OUT, blk), imap))
        out_shape.append(jax.ShapeDtypeStruct((BATCH, 2, ATTN_OUT, keep), jnp.float32))
    return pl.pallas_call(
        functools.partial(_kv_prompt_kernel, keeps=keeps),
        grid=(BATCH, n_t),
        in_specs=[pl.BlockSpec((ROW_TILE, ATTN_QKV), lambda b, j: (b * n_t + j, OFF_K // ATTN_QKV)),
                  pl.BlockSpec((ROW_TILE, ATTN_QKV), lambda b, j: (b * n_t + j, OFF_V // ATTN_QKV))],
        out_specs=out_specs,
        out_shape=out_shape,
        compiler_params=_params(("parallel", "arbitrary"), 32 * MIB),
        name="kv_prompt",
    )(za, za)


def _kv_sample_kernel(k_ref, v_ref, *o_refs):
    for t in range(DEC_SEQ):
        rows = slice(t * DEC_BATCH, (t + 1) * DEC_BATCH)
        for g in range(N_ATTN_GROUPS):
            cols = slice(g * ATTN_OUT, (g + 1) * ATTN_OUT)
            o_refs[g][t, 0] = k_ref[rows, cols].T
            o_refs[g][t, 1] = v_ref[rows, cols].T


def _kv_sample(za):
    blk = P_ROWS // S_ROWS
    shape = (DEC_SEQ, 2, ATTN_OUT, DEC_BATCH)
    return pl.pallas_call(
        _kv_sample_kernel,
        grid=(1,),
        in_specs=[pl.BlockSpec((S_ROWS, ATTN_QKV), lambda i: (blk, OFF_K // ATTN_QKV)),
                  pl.BlockSpec((S_ROWS, ATTN_QKV), lambda i: (blk, OFF_V // ATTN_QKV))],
        out_specs=[pl.BlockSpec(shape, lambda i: (0, 0, 0, 0))] * N_ATTN_GROUPS,
        out_shape=[jax.ShapeDtypeStruct(shape, jnp.float32)] * N_ATTN_GROUPS,
        compiler_params=_params(("arbitrary",), 32 * MIB),
        name="kv_sample",
    )(za, za)


def _conv_prompt_kernel(h_ref, b_ref, c_ref, w_ref, y_ref, st_ref):
    u = c_ref[...] * h_ref[...]
    w = w_ref[...]
    row = lax.broadcasted_iota(jnp.int32, u.shape, 0)
    u1 = jnp.where(row >= 1, pltpu.roll(u, 1, axis=0), 0.0)
    u2 = jnp.where(row >= 2, pltpu.roll(u, 2, axis=0), 0.0)
    y = w[0:1] * u2 + w[1:2] * u1 + w[2:3] * u
    y_ref[...] = (b_ref[...] * y).astype(y_ref.dtype)
    st_ref[...] = u[SEQ - (CONV_K - 1):, :]


def _conv_prompt(za, w):
    col = lambda c: pl.BlockSpec((SEQ, CONV_WIDTH), lambda b, c=c: (b, c))
    return pl.pallas_call(
        _conv_prompt_kernel,
        grid=(BATCH,),
        in_specs=[col(0), col(1), col(2), pl.BlockSpec((CONV_K, CONV_WIDTH), lambda b: (0, 0))],
        out_specs=[pl.BlockSpec((SEQ, CONV_WIDTH), lambda b: (b, 0)),
                   pl.BlockSpec((None, CONV_K - 1, CONV_WIDTH), lambda b: (b, 0, 0))],
        out_shape=[jax.ShapeDtypeStruct((P_ROWS, CONV_WIDTH), jnp.bfloat16),
                   jax.ShapeDtypeStruct((BATCH, CONV_K - 1, CONV_WIDTH), jnp.float32)],
        compiler_params=_params(("parallel",), 48 * MIB),
        name="conv_prompt",
    )(za, za, za, w)


def _conv_sample_kernel(h_ref, b_ref, c_ref, w_ref, s_ref, y_ref, st_ref):
    u = c_ref[...] * h_ref[...]
    w = w_ref[...]
    s0, s1 = s_ref[0], s_ref[1]
    keep = S_ROWS - DEC_BATCH
    u1 = jnp.concatenate([s1, u[:keep]], axis=0)
    u2 = jnp.concatenate([s0, s1, u[:keep - DEC_BATCH]], axis=0)
    y = w[0:1] * u2 + w[1:2] * u1 + w[2:3] * u
    y_ref[...] = (b_ref[...] * y).astype(y_ref.dtype)
    st_ref[0] = u[S_ROWS - 2 * DEC_BATCH:S_ROWS - DEC_BATCH]
    st_ref[1] = u[S_ROWS - DEC_BATCH:]


def _conv_sample(za, w, state):
    blk = P_ROWS // S_ROWS
    col = lambda c: pl.BlockSpec((S_ROWS, CONV_WIDTH), lambda i, c=c: (blk, c))
    st_spec = pl.BlockSpec((CONV_K - 1, DEC_BATCH, CONV_WIDTH), lambda i: (0, 0, 0))
    return pl.pallas_call(
        _conv_sample_kernel,
        grid=(1,),
        in_specs=[col(0), col(1), col(2), pl.BlockSpec((CONV_K, CONV_WIDTH), lambda i: (0, 0)), st_spec],
        out_specs=[pl.BlockSpec((S_ROWS, CONV_WIDTH), lambda i: (0, 0)), st_spec],
        out_shape=[jax.ShapeDtypeStruct((S_ROWS, CONV_WIDTH), jnp.bfloat16),
                   jax.ShapeDtypeStruct((CONV_K - 1, DEC_BATCH, CONV_WIDTH), jnp.float32)],
        compiler_params=_params(("arbitrary",), 24 * MIB),
        name="conv_sample",
    )(za, za, za, w, state)


def _attn_prompt_kernel(*refs, dil, has_prev, sub):
    n_in = (5 if has_prev else 3) * HALVES
    q_refs, kc_refs, vc_refs = (refs[i * HALVES:(i + 1) * HALVES] for i in range(3))
    if has_prev:
        kp_refs, vp_refs = (refs[i * HALVES:(i + 1) * HALVES] for i in range(3, 5))
    o_ref, l_ref = refs[n_in:]
    n_keys = (2 if has_prev else 1) * BAND
    qi = lax.broadcasted_iota(jnp.int32, (BAND, n_keys), 0)
    kj = lax.broadcasted_iota(jnp.int32, (BAND, n_keys), 1)
    if has_prev:
        dist = qi + BAND - kj
        inner_ok = (dist >= 0) & (dist <= BAND)
        first_ok = inner_ok & ((kj >= BAND) | (pl.program_id(1) > 0))
    else:
        first_ok = kj <= qi
    heads_per_half = V7X_LANES // HEAD_DIM

    def attend(q, k, v, valid):
        q, k, v = (t.astype(jnp.bfloat16) for t in (q, k, v))
        outs, lses = [], []
        for h in range(heads_per_half):
            sl = slice(h * HEAD_DIM, (h + 1) * HEAD_DIM)
            s = lax.dot_general(q[:, sl], k[:, sl], (((1,), (1,)), ((), ())),
                                preferred_element_type=jnp.float32) * (HEAD_DIM ** -0.5)
            s = jnp.where(valid, s, -jnp.inf)
            m = jnp.max(s, axis=-1, keepdims=True)
            p = jnp.exp(s - m)
            den = jnp.sum(p, axis=-1, keepdims=True)
            outs.append(jnp.dot(p.astype(jnp.bfloat16), v[:, sl], preferred_element_type=jnp.float32) / den)
            lses.append(jnp.broadcast_to(m + jnp.log(den), (BAND, HEAD_DIM)))
        return jnp.concatenate(outs, axis=-1), jnp.concatenate(lses, axis=-1)

    if dil == 1:
        for half in range(HALVES):
            q, kc, vc = q_refs[half][...], kc_refs[half][...], vc_refs[half][...]
            kp, vp = kp_refs[half][...], vp_refs[half][...]
            for s in range(sub):
                lo, hi = (s - 1) * BAND, (s + 1) * BAND
                k = jnp.concatenate([kp, kc[:BAND]], axis=0) if s == 0 else kc[lo:hi]
                v = jnp.concatenate([vp, vc[:BAND]], axis=0) if s == 0 else vc[lo:hi]
                o, l = attend(q[s * BAND:hi], k, v, first_ok if s == 0 else inner_ok)
                o_ref[half, s * BAND:hi, :] = o
                l_ref[half, s * BAND:hi, :] = l
        return

    def residue(r, carry):
        rows = pl.ds(r, BAND, stride=dil)
        for half in range(HALVES):
            k = kc_refs[half][rows, :]
            v = vc_refs[half][rows, :]
            if has_prev:
                k = jnp.concatenate([kp_refs[half][rows, :], k], axis=0)
                v = jnp.concatenate([vp_refs[half][rows, :], v], axis=0)
            o, l = attend(q_refs[half][rows, :], k, v, first_ok)
            o_ref[half, rows, :] = o
            l_ref[half, rows, :] = l
        return carry

    lax.fori_loop(0, dil, residue, 0, unroll=min(dil, ATTN_RESIDUE_UNROLL))


def _attn_prompt(za, g):
    d = ATTN_GROUPS[g][1]
    sub = ATTN_DENSE_BLOCKS if d == 1 else 1
    rows = BAND * d * sub
    nb = SEQ // rows
    has_prev = nb > 1
    prev_rows = BAND if d == 1 else rows
    qcol, kcol, vcol = (off // V7X_LANES + g * HALVES for off in (OFF_Q, OFF_K, OFF_V))
    cur = lambda b, j: b * nb + j
    prev = lambda b, j: b * (SEQ // prev_rows) + jnp.maximum(j * (rows // prev_rows) - 1, 0)
    spec = lambda n, row, col: [pl.BlockSpec((n, V7X_LANES), lambda b, j, h=h: (row(b, j), col + h))
                                for h in range(HALVES)]
    in_specs = spec(rows, cur, qcol) + spec(rows, cur, kcol) + spec(rows, cur, vcol)
    args = [za] * (3 * HALVES)
    if has_prev:
        in_specs += spec(prev_rows, prev, kcol) + spec(prev_rows, prev, vcol)
        args += [za] * (2 * HALVES)
    out_spec = pl.BlockSpec((HALVES, rows, V7X_LANES), lambda b, j: (0, cur(b, j), 0))
    shape = jax.ShapeDtypeStruct((HALVES, P_ROWS, V7X_LANES), jnp.float32)
    return pl.pallas_call(
        functools.partial(_attn_prompt_kernel, dil=d, has_prev=has_prev, sub=sub),
        grid=(BATCH, nb),
        in_specs=in_specs,
        out_specs=[out_spec, out_spec],
        out_shape=[shape, shape],
        compiler_params=_params(("parallel", "parallel"), 40 * MIB),
        name=f"attn_prompt_g{g}",
    )(*args)


def _attn_sample_kernel(q_ref, kn_ref, vn_ref, c_ref, o_ref, l_ref, *, dil, nseq):
    width = c_ref.shape[-1]
    scale = HEAD_DIM ** -0.5
    nt = (((1,), (1,)), ((), ()))
    n_rows = HEADS_PER_GROUP * DEC_SEQ
    row_l = lax.broadcasted_iota(jnp.int32, (n_rows, ATTN_OUT), 0)
    lane = lax.broadcasted_iota(jnp.int32, (n_rows, ATTN_OUT), 1)
    own_head = lane // HEAD_DIM == row_l // DEC_SEQ
    t_c = lax.broadcasted_iota(jnp.int32, (n_rows, width), 0) % DEC_SEQ
    w_c = lax.broadcasted_iota(jnp.int32, (n_rows, width), 1)
    ok_c = (w_c >= t_c) & (w_c % dil == t_c % dil)
    r_n = lax.broadcasted_iota(jnp.int32, (n_rows, n_rows), 0)
    c_n = lax.broadcasted_iota(jnp.int32, (n_rows, n_rows), 1)
    t_n, s_n = r_n % DEC_SEQ, c_n % DEC_SEQ
    ok_n = (r_n // DEC_SEQ == c_n // DEC_SEQ) & (s_n <= t_n) & (s_n % dil == t_n % dil)

    def per_head(x):
        return jnp.where(own_head, jnp.tile(x, (HEADS_PER_GROUP, 1)), 0.0).astype(jnp.bfloat16)

    def one_seq(s, carry):
        q, kn, vn = per_head(q_ref[s]), per_head(kn_ref[s]), per_head(vn_ref[s])
        k_t = c_ref[s, 0].reshape(ATTN_OUT, width).astype(jnp.bfloat16)
        v_t = c_ref[s, 1].reshape(ATTN_OUT, width).astype(jnp.bfloat16)
        sc = jnp.dot(q, k_t, preferred_element_type=jnp.float32) * scale
        sc = jnp.where(ok_c, sc, -jnp.inf)
        sn = lax.dot_general(q, kn, nt, preferred_element_type=jnp.float32) * scale
        sn = jnp.where(ok_n, sn, -jnp.inf)
        m = jnp.maximum(jnp.max(sc, axis=-1, keepdims=True), jnp.max(sn, axis=-1, keepdims=True))
        pc = jnp.exp(sc - m)
        pn = jnp.exp(sn - m)
        den = jnp.sum(pc, axis=-1, keepdims=True) + jnp.sum(pn, axis=-1, keepdims=True)
        o = (lax.dot_general(pc.astype(jnp.bfloat16), v_t, nt, preferred_element_type=jnp.float32)
             + jnp.dot(pn.astype(jnp.bfloat16), vn, preferred_element_type=jnp.float32))
        o = jnp.where(own_head, o / den, 0.0)
        lse = jnp.where(own_head, m + jnp.log(den), 0.0)
        fold = lambda a: sum(a[h * DEC_SEQ:(h + 1) * DEC_SEQ] for h in range(HEADS_PER_GROUP))
        o_ref[s] = fold(o)
        l_ref[s] = fold(lse)
        return carry

    lax.fori_loop(0, nseq, one_seq, 0, unroll=True)


def _attn_sample(qk_s, v_s, cache_t, layer, g):
    d = ATTN_GROUPS[g][1]
    width = cache_t.shape[-1]
    seq_bytes = 2 * ATTN_OUT * width * 4
    nseq = max(1, min(SAMPLE_SEQ_TILE, SAMPLE_CACHE_TILE_BYTES // seq_bytes))
    blk = (nseq, DEC_SEQ, ATTN_OUT)
    koff = ATTN_QKV // ATTN_OUT
    out_spec = pl.BlockSpec(blk, lambda i: (i, 0, 0))
    shape = jax.ShapeDtypeStruct((DEC_BATCH, DEC_SEQ, ATTN_OUT), jnp.float32)
    o, l = pl.pallas_call(
        functools.partial(_attn_sample_kernel, dil=d, nseq=nseq),
        grid=(DEC_BATCH // nseq,),
        in_specs=[pl.BlockSpec(blk, lambda i: (i, 0, g)),
                  pl.BlockSpec(blk, lambda i: (i, 0, koff + g)),
                  pl.BlockSpec(blk, lambda i: (i, 0, g)),
                  pl.BlockSpec((None, nseq, 2, HEADS_PER_GROUP, HEAD_DIM, width),
                               lambda i: (layer, i, 0, 0, 0, 0))],
        out_specs=[out_spec, out_spec],
        out_shape=[shape, shape],
        compiler_params=_params(("parallel",), 4 * SAMPLE_CACHE_TILE_BYTES),
        name=f"attn_sample_g{g}",
    )(qk_s, qk_s, v_s, cache_t)
    to_rows = lambda a: jnp.transpose(a, (1, 0, 2)).reshape(S_ROWS, ATTN_OUT)
    return to_rows(o), to_rows(l)


def _ssm_tables(lam_re, lam_im, log_dt, b_re, b_im, c_re, c_im, d_skip):
    dt = jnp.exp(log_dt)[:, None]
    er = jnp.exp(lam_re * dt)
    lbr, lbi = er * jnp.cos(lam_im * dt), er * jnp.sin(lam_im * dt)
    den = lam_re * lam_re + lam_im * lam_im
    nr, ni = lbr - 1.0, lbi
    qr = (nr * lam_re + ni * lam_im) / den
    qi = (ni * lam_re - nr * lam_im) / den
    bbr = qr[:, :, None] * b_re - qi[:, :, None] * b_im
    bbi = qr[:, :, None] * b_im + qi[:, :, None] * b_re
    eye = jnp.eye(SSM_GB, dtype=jnp.float32)

    def in_block(b):
        b = b.reshape(N_GB, SSM_GB, SSM_STATE, SSM_CH_PER_GROUP)
        return jnp.einsum('ngpc,gh->ngchp', b, eye).reshape(N_GB, GB_CH, GB_ST)

    def out_block(c):
        c = c.reshape(N_GB, SSM_GB, SSM_CH_PER_GROUP, SSM_STATE)
        return jnp.einsum('ngcp,gh->ngphc', c, eye).reshape(N_GB, GB_ST, GB_CH)

    b_blk = jnp.concatenate([in_block(bbr), in_block(bbi)], axis=2).astype(jnp.bfloat16)
    c_blk = jnp.concatenate([out_block(c_re), out_block(-c_im)], axis=1).astype(jnp.bfloat16)

    def pack(re, im):
        re = re.reshape(N_GB, 1, GB_ST)
        im = im.reshape(N_GB, 1, GB_ST)
        return jnp.concatenate([re, re], axis=2), jnp.concatenate([-im, im], axis=2)

    pr, pi = lbr, lbi
    for _ in range(int(math.log2(SSM_CHUNK))):
        pr, pi = pr * pr - pi * pi, 2.0 * pr * pi
    return dict(b=b_blk, c=c_blk, lam=pack(lbr, lbi), lam_chunk=pack(pr, pi), d=d_skip.reshape(1, SSM_WIDTH))


def _cstep(s, a, bsw):
    swapped = jnp.concatenate([s[:, GB_ST:], s[:, :GB_ST]], axis=1)
    return a * s + bsw * swapped


def _ssm_kernel(u_ref, s0_ref, b_ref, c_ref, a_ref, bsw_ref, d_ref, *refs, steps, emit_y):
    if emit_y:
        y_ref, sout_ref, s_ref = refs
    else:
        sout_ref, s_ref = refs
    t = pl.program_id(1)

    @pl.when(t == 0)
    def _():
        s_ref[...] = s0_ref[...]

    u = u_ref[...]
    s = _cstep(s_ref[...], a_ref[...], bsw_ref[...]) + _bdot(u, b_ref[...])
    s_ref[...] = s
    if emit_y:
        y_ref[...] = _bdot(s, c_ref[...]) + d_ref[...] * u

    @pl.when(t == steps - 1)
    def _():
        sout_ref[...] = s


def _ssm_pass(u_arr, u_map, y_map, rows, steps, s0, tabs, y_shape, name):
    emit_y = y_shape is not None
    st_spec = pl.BlockSpec((rows, 2 * GB_ST), lambda gb, t: (0, gb))
    coef = pl.BlockSpec((None, 1, 2 * GB_ST), lambda gb, t: (gb, 0, 0))
    in_specs = [pl.BlockSpec((rows, GB_CH), u_map), st_spec,
                pl.BlockSpec((None, GB_CH, 2 * GB_ST), lambda gb, t: (gb, 0, 0)),
                pl.BlockSpec((None, 2 * GB_ST, GB_CH), lambda gb, t: (gb, 0, 0)),
                coef, coef,
                pl.BlockSpec((1, GB_CH), lambda gb, t: (0, gb))]
    st_shape = jax.ShapeDtypeStruct((rows, ST_WIDTH), jnp.float32)
    out_specs, out_shape = [st_spec], [st_shape]
    if emit_y:
        out_specs, out_shape = [pl.BlockSpec((rows, GB_CH), y_map), st_spec], [y_shape, st_shape]
    return pl.pallas_call(
        functools.partial(_ssm_kernel, steps=steps, emit_y=emit_y),
        grid=(N_GB, steps),
        in_specs=in_specs,
        out_specs=out_specs,
        out_shape=out_shape,
        scratch_shapes=[pltpu.VMEM((rows, 2 * GB_ST), jnp.float32)],
        compiler_params=_params(("parallel", "arbitrary"), 48 * MIB),
        name=name,
    )(u_arr, s0, tabs['b'], tabs['c'], tabs['lam'][0], tabs['lam'][1], tabs['d'])


def _ssm_prompt_kernel(*refs, emit_y):
    u_refs = refs[:HALVES]
    s0_ref, b_ref, c_ref, a_ref, bsw_ref, d_ref = refs[HALVES:HALVES + 6]
    if emit_y:
        y_ref, sout_ref, s_ref = refs[HALVES + 6:]
    else:
        sout_ref, s_ref = refs[HALVES + 6:]
    t = pl.program_id(2)
    rows = pl.ds(t, s_ref.shape[0], stride=SSM_CHUNK)

    @pl.when(t == 0)
    def _():
        s_ref[...] = s0_ref[...]

    u = jnp.concatenate([r[rows, :] for r in u_refs], axis=-1)
    s = _cstep(s_ref[...], a_ref[...], bsw_ref[...]) + _bdot(u, b_ref[...])
    s_ref[...] = s
    if emit_y:
        y = _bdot(s, c_ref[...]) + d_ref[...] * u
        for half in range(HALVES):
            y_ref[half, rows, :] = y[:, half * V7X_LANES:(half + 1) * V7X_LANES]

    @pl.when(t == SSM_CHUNK - 1)
    def _():
        sout_ref[...] = s


def _ssm_prompt_pass(za, s0, tabs, emit_y, name):
    n_rows = P_ROWS // SSM_CHUNK // SSM_BATCH_SPLIT
    nat_rows = P_ROWS // SSM_BATCH_SPLIT
    ucol = OFF_U // V7X_LANES
    once = pl.Buffered(1)
    u_specs = [pl.BlockSpec((nat_rows, V7X_LANES), lambda gb, p, t, h=h: (p, ucol + gb * HALVES + h),
                            pipeline_mode=once) for h in range(HALVES)]
    st_spec = pl.BlockSpec((n_rows, 2 * GB_ST), lambda gb, p, t: (p, gb))
    s0_spec = pl.BlockSpec((n_rows, 2 * GB_ST), lambda gb, p, t: (p, gb), pipeline_mode=once)
    coef = pl.BlockSpec((None, 1, 2 * GB_ST), lambda gb, p, t: (gb, 0, 0))
    in_specs = u_specs + [s0_spec,
                          pl.BlockSpec((None, GB_CH, 2 * GB_ST), lambda gb, p, t: (gb, 0, 0)),
                          pl.BlockSpec((None, 2 * GB_ST, GB_CH), lambda gb, p, t: (gb, 0, 0)),
                          coef, coef,
                          pl.BlockSpec((1, GB_CH), lambda gb, p, t: (0, gb))]
    st_shape = jax.ShapeDtypeStruct((P_ROWS // SSM_CHUNK, ST_WIDTH), jnp.float32)
    out_specs, out_shape = [st_spec], [st_shape]
    if emit_y:
        out_specs = [pl.BlockSpec((HALVES, nat_rows, V7X_LANES), lambda gb, p, t: (gb, p, 0)), st_spec]
        out_shape = [jax.ShapeDtypeStruct((N_GB * HALVES, P_ROWS, V7X_LANES), jnp.float32), st_shape]
    return pl.pallas_call(
        functools.partial(_ssm_prompt_kernel, emit_y=emit_y),
        grid=(N_GB, SSM_BATCH_SPLIT, SSM_CHUNK),
        in_specs=in_specs,
        out_specs=out_specs,
        out_shape=out_shape,
        scratch_shapes=[pltpu.VMEM((n_rows, 2 * GB_ST), jnp.float32)],
        compiler_params=_params(("parallel", "parallel", "arbitrary"), 58 * MIB),
        name=name,
    )(*([za] * HALVES), s0, tabs['b'], tabs['c'], tabs['lam'][0], tabs['lam'][1], tabs['d'])


def _ssm_carry_kernel(e_ref, a_ref, bsw_ref, sin_ref, fin_ref):
    a = a_ref[...]
    bsw = bsw_ref[...]
    n_chunks = SEQ // SSM_CHUNK

    def body(n, states):
        new = []
        for b in range(BATCH):
            row = b * n_chunks + n
            sin_ref[pl.ds(row, 1), :] = states[b]
            new.append(_cstep(states[b], a, bsw) + e_ref[pl.ds(row, 1), :])
        return tuple(new)

    zero = jnp.zeros((1, 2 * GB_ST), jnp.float32)
    final = lax.fori_loop(0, n_chunks, body, (zero,) * BATCH)
    for b in range(BATCH):
        fin_ref[pl.ds(b, 1), :] = final[b]


def _ssm_carry(e, tabs):
    n_rows = P_ROWS // SSM_CHUNK
    st_spec = lambda rows: pl.BlockSpec((rows, 2 * GB_ST), lambda gb: (0, gb))
    coef = pl.BlockSpec((None, 1, 2 * GB_ST), lambda gb: (gb, 0, 0))
    return pl.pallas_call(
        _ssm_carry_kernel,
        grid=(N_GB,),
        in_specs=[st_spec(n_rows), coef, coef],
        out_specs=[st_spec(n_rows), st_spec(BATCH)],
        out_shape=[jax.ShapeDtypeStruct((n_rows, ST_WIDTH), jnp.float32),
                   jax.ShapeDtypeStruct((BATCH, ST_WIDTH), jnp.float32)],
        compiler_params=_params(("parallel",), 32 * MIB),
        name="ssm_carry",
    )(e, tabs['lam_chunk'][0], tabs['lam_chunk'][1])


def _pack_state(s):
    b = s.shape[0]
    s = s.reshape(b, N_GB, SSM_GB * SSM_STATE, 2)
    return jnp.transpose(s, (0, 1, 3, 2)).reshape(b, ST_WIDTH)


def _unpack_state(s):
    b = s.shape[0]
    s = s.reshape(b, N_GB, 2, SSM_GB * SSM_STATE)
    return jnp.transpose(s, (0, 1, 3, 2)).reshape(b, SSM_GROUPS, SSM_STATE, 2)


def _ssm(za, state_s, tabs):
    ucol = OFF_U // GB_CH
    zero = jnp.zeros((P_ROWS // SSM_CHUNK, ST_WIDTH), jnp.float32)
    (e,) = _ssm_prompt_pass(za, zero, tabs, False, "ssm_prompt_local")
    s_in, s_fin = _ssm_carry(e, tabs)
    y_p, _ = _ssm_prompt_pass(za, s_in, tabs, True, "ssm_prompt")
    tblk = P_ROWS // DEC_BATCH
    ys_shape = jax.ShapeDtypeStruct((S_ROWS, SSM_WIDTH), jnp.float32)
    y_s, s_new = _ssm_pass(za, lambda gb, t: (tblk + t, ucol + gb), lambda gb, t: (t, gb), DEC_BATCH, DEC_SEQ,
                           _pack_state(state_s), tabs, ys_shape, "ssm_sample")
    return y_p, y_s, _unpack_state(s_fin), _unpack_state(s_new)


def _mix_kernel(x_ref, gate_ref, gb_ref, cp_ref, cs_ref, ap_refs, as_refs, sp_ref, ss_ref,
                wglu_ref, bglu_ref, pc_ref, pa_ref, ps_ref, wo_ref, gn_ref, xo_ref, ho_ref):
    is_p = pl.program_id(0) < P_ROWS // MIX_TILE

    def pick(p_ref, s_ref):
        p = p_ref[...]
        if p.ndim == 3:
            p = jnp.concatenate([p[i] for i in range(p.shape[0])], axis=-1)
        return jnp.where(is_p, p, s_ref[...])

    y_conv = pick(cp_ref, cs_ref)
    o = [pick(ap_refs[2 * g], as_refs[2 * g]) for g in range(N_ATTN_GROUPS)]
    lse = [pick(ap_refs[2 * g + 1], as_refs[2 * g + 1]) for g in range(N_ATTN_GROUPS)]
    m = jnp.maximum(jnp.maximum(lse[0], lse[1]), lse[2])
    w = [jnp.exp(l - m) for l in lse]
    y_attn = (w[0] * o[0] + w[1] * o[1] + w[2] * o[2]) / (w[0] + w[1] + w[2])
    zg = jax.nn.gelu(pick(sp_ref, ss_ref))
    y_ssm = zg * _sigmoid(_bdot(zg, wglu_ref[...]) + bglu_ref[...])
    gt = _sigmoid(gate_ref[...].astype(jnp.float32) + gb_ref[...])
    merged = (gt[:, :D_MODEL] * _bdot(y_conv, pc_ref[...])
              + gt[:, D_MODEL:2 * D_MODEL] * _bdot(y_attn, pa_ref[...])
              + gt[:, 2 * D_MODEL:] * _bdot(y_ssm, ps_ref[...]))
    x = x_ref[...] + _bdot(merged, wo_ref[...])
    xo_ref[...] = x
    ho_ref[...] = _rms(x, gn_ref[...]).astype(ho_ref.dtype)


def _mix(x, gates, conv_p, conv_s, attn_p, attn_s, ssm_p, ssm_s, lp, h_dtype):
    n_p = P_ROWS // MIX_TILE
    p_map = lambda i: (jnp.minimum(i, n_p - 1), 0)
    s_map = lambda i: (jnp.maximum(i - n_p, 0), 0)
    row_map = lambda i: (i, 0)
    const = lambda i: (0, 0)
    rows = lambda width, imap: pl.BlockSpec((MIX_TILE, width), imap)
    slabs = lambda n: pl.BlockSpec((n, MIX_TILE, V7X_LANES), lambda i: (0, jnp.minimum(i, n_p - 1), 0))
    full = lambda a: pl.BlockSpec(a.shape, const, pipeline_mode=pl.Buffered(1))
    n_attn = 2 * N_ATTN_GROUPS
    weights = [lp['w_glu'], lp['b_glu'], lp['proj_conv'], lp['proj_attn'], lp['proj_ssm'], lp['w_out'],
               lp['norm_next']]
    in_specs = ([rows(D_MODEL, row_map), rows(GATE_WIDTH, row_map), full(lp['gate_bias']),
                 rows(CONV_WIDTH, p_map), rows(CONV_WIDTH, s_map)]
                + [slabs(HALVES)] * n_attn + [rows(ATTN_OUT, s_map)] * n_attn
                + [slabs(N_GB * HALVES), rows(SSM_WIDTH, s_map)]
                + [full(w) for w in weights])

    def body(*refs):
        x_ref, gate_ref, gb_ref, cp_ref, cs_ref = refs[:5]
        ap_refs = refs[5:5 + n_attn]
        as_refs = refs[5 + n_attn:5 + 2 * n_attn]
        rest = refs[5 + 2 * n_attn:]
        _mix_kernel(x_ref, gate_ref, gb_ref, cp_ref, cs_ref, ap_refs, as_refs, *rest)

    return pl.pallas_call(
        body,
        grid=(ROWS // MIX_TILE,),
        in_specs=in_specs,
        out_specs=[rows(D_MODEL, row_map), rows(D_MODEL, row_map)],
        out_shape=[jax.ShapeDtypeStruct((ROWS, D_MODEL), jnp.float32),
                   jax.ShapeDtypeStruct((ROWS, D_MODEL), h_dtype)],
        compiler_params=_params(("parallel",), 56 * MIB),
        name="mix",
    )(x, gates, lp['gate_bias'], conv_p, conv_s, *attn_p, *attn_s, ssm_p, ssm_s, *weights)


def _ffn_up_kernel(be_ref, *refs, n_side, cast):
    x_ref, w1_ref, w3_ref = refs[:3]
    h_ref = refs[3 + n_side]
    _side_cast(refs[3:3 + n_side], refs[4 + n_side:4 + 2 * n_side])
    if cast:
        w1b_ref, w3b_ref = refs[-2:]

        @pl.when(pl.program_id(1) == 0)
        def _():
            w1b_ref[...] = w1_ref[...].astype(w1b_ref.dtype)
            w3b_ref[...] = w3_ref[...].astype(w3b_ref.dtype)
    else:
        w1b_ref, w3b_ref = w1_ref, w3_ref
    x = x_ref[...].astype(jnp.bfloat16)
    a = jnp.dot(x, w1b_ref[...], preferred_element_type=jnp.float32)
    b = jnp.dot(x, w3b_ref[...], preferred_element_type=jnp.float32)
    h_ref[...] = (a * _sigmoid(a) * b).astype(h_ref.dtype)


def _ffn_up(block_e, x, w1, w3, bm, bf, name, side=()):
    m = x.shape[0]
    e, _, f = w1.shape
    cast = w1.dtype != jnp.bfloat16
    assert not cast or e == 1
    grid = (f // bf, m // bm)
    assert all(arr.shape[0] // rows <= grid[0] * grid[1] for arr, rows in side)
    s_in, s_out, s_shape = _side_cast_specs(side, grid[1])
    wspec = pl.BlockSpec((None, D_MODEL, bf), lambda j, i, be: (be[i], 0, j))
    grid_spec = pltpu.PrefetchScalarGridSpec(
        num_scalar_prefetch=1,
        grid=grid,
        in_specs=[pl.BlockSpec((bm, D_MODEL), lambda j, i, be: (i, 0)), wspec, wspec] + s_in,
        out_specs=[pl.BlockSpec((bm, bf), lambda j, i, be: (i, j))] + s_out,
        scratch_shapes=[pltpu.VMEM((D_MODEL, bf), jnp.bfloat16)] * 2 if cast else [],
    )
    out = pl.pallas_call(
        functools.partial(_ffn_up_kernel, n_side=len(side), cast=cast),
        grid_spec=grid_spec,
        out_shape=[jax.ShapeDtypeStruct((m, f), jnp.bfloat16)] + s_shape,
        compiler_params=_params(("arbitrary", "arbitrary"), 56 * MIB),
        name=name,
    )(block_e, x, w1, w3, *[arr for arr, _ in side])
    return out if side else out[0]


def _ffn_down_dense_kernel(h_ref, w_ref, x_ref, g_ref, xo_ref, ho_ref, acc_ref, *, n_k):
    k = pl.program_id(1)

    @pl.when(k == 0)
    def _():
        acc_ref[...] = x_ref[...]

    acc_ref[...] += jnp.dot(h_ref[...], w_ref[...], preferred_element_type=jnp.float32)

    @pl.when(k == n_k - 1)
    def _():
        x = acc_ref[...]
        xo_ref[...] = x
        ho_ref[...] = _rms(x, g_ref[...]).astype(ho_ref.dtype)


def _ffn_down_dense(h, w2, x, g_next, bk):
    n_k = D_FF // bk
    row = pl.BlockSpec((ROW_TILE, D_MODEL), lambda i, k: (i, 0))
    return pl.pallas_call(
        functools.partial(_ffn_down_dense_kernel, n_k=n_k),
        grid=(ROWS // ROW_TILE, n_k),
        in_specs=[pl.BlockSpec((ROW_TILE, bk), lambda i, k: (i, k)),
                  pl.BlockSpec((bk, D_MODEL), lambda i, k: (k, 0)),
                  row, pl.BlockSpec((1, D_MODEL), lambda i, k: (0, 0))],
        out_specs=[row, row],
        out_shape=[jax.ShapeDtypeStruct((ROWS, D_MODEL), jnp.float32),
                   jax.ShapeDtypeStruct((ROWS, D_MODEL), jnp.bfloat16)],
        scratch_shapes=[pltpu.VMEM((ROW_TILE, D_MODEL), jnp.float32)],
        compiler_params=_params(("parallel", "arbitrary"), 58 * MIB),
        name="ffn_down",
    )(h, w2, x, g_next.reshape(1, D_MODEL))


def _moe_down_kernel(be_ref, h_ref, w_ref, g_ref, y_ref):
    y = jnp.dot(h_ref[...], w_ref[...], preferred_element_type=jnp.float32)
    _store_chunk_rows(y_ref, y * g_ref[...])


def _moe_down(block_e, h, w2, row_gate):
    grid_spec = pltpu.PrefetchScalarGridSpec(
        num_scalar_prefetch=1,
        grid=(MOE_BLOCKS,),
        in_specs=[pl.BlockSpec((MOE_BLOCK, D_FF_EXPERT), lambda i, be: (i, 0)),
                  pl.BlockSpec((None, D_FF_EXPERT, D_MODEL), lambda i, be: (be[i], 0, 0)),
                  pl.BlockSpec((MOE_BLOCK, 1), lambda i, be: (i, 0))],
        out_specs=pl.BlockSpec((MOE_BLOCK * ROW_CHUNKS, V7X_LANES), lambda i, be: (i, 0)),
    )
    return pl.pallas_call(
        _moe_down_kernel,
        grid_spec=grid_spec,
        out_shape=jax.ShapeDtypeStruct((MOE_ROWS * ROW_CHUNKS, V7X_LANES), jnp.float32),
        compiler_params=_params(("arbitrary",), 48 * MIB),
        name="moe_down",
    )(block_e, h, w2, row_gate)


def _router_kernel(x_ref, g_ref, w_ref, hs_ref, r_ref):
    hs = _rms(x_ref[...], g_ref[...])
    _store_chunk_rows(hs_ref, hs)
    logits = jnp.dot(hs, w_ref[...], preferred_element_type=jnp.float32, precision=lax.Precision.HIGHEST)
    lane = lax.broadcasted_iota(jnp.int32, logits.shape, 1)
    logits = jnp.where(lane < N_EXPERTS, logits, -jnp.inf)
    m1 = jnp.max(logits, axis=-1, keepdims=True)
    i1 = jnp.min(jnp.where(logits == m1, lane, ROUTER_PAD), axis=-1, keepdims=True)
    rest = jnp.where(lane == i1, -jnp.inf, logits)
    m2 = jnp.max(rest, axis=-1, keepdims=True)
    i2 = jnp.min(jnp.where(rest == m2, lane, ROUTER_PAD), axis=-1, keepdims=True)
    e2 = jnp.exp(m2 - m1)
    g1 = 1.0 / (1.0 + e2)
    g2 = e2 / (1.0 + e2)
    out = jnp.where(lane == 0, i1.astype(jnp.float32),
                    jnp.where(lane == 1, i2.astype(jnp.float32),
                              jnp.where(lane == 2, g1, jnp.where(lane == 3, g2, 0.0))))
    r_ref[...] = out


def _router(x, g, w_router):
    w = jnp.zeros((D_MODEL, ROUTER_PAD), jnp.float32).at[:, :N_EXPERTS].set(w_router)
    row = pl.BlockSpec((ROW_TILE, D_MODEL), lambda i: (i, 0))
    return pl.pallas_call(
        _router_kernel,
        grid=(ROWS // ROW_TILE,),
        in_specs=[row, pl.BlockSpec((1, D_MODEL), lambda i: (0, 0)),
                  pl.BlockSpec((D_MODEL, ROUTER_PAD), lambda i: (0, 0))],
        out_specs=[pl.BlockSpec((ROW_TILE * ROW_CHUNKS, V7X_LANES), lambda i: (i, 0)),
                   pl.BlockSpec((ROW_TILE, ROUTER_PAD), lambda i: (i, 0))],
        out_shape=[jax.ShapeDtypeStruct((ROWS * ROW_CHUNKS, V7X_LANES), jnp.float32),
                   jax.ShapeDtypeStruct((ROWS, ROUTER_PAD), jnp.float32)],
        compiler_params=_params(("parallel",), 32 * MIB),
        name="router",
    )(x, g.reshape(1, D_MODEL), w)


def _gather_step(idx_ref, src_ref, buf, sem, rows):
    i = pl.program_id(0)
    slot = i % 2

    def copy(step, slot_, r):
        return pltpu.make_async_copy(src_ref.at[idx_ref[step * rows + r]],
                                     buf.at[slot_, pl.ds(r * ROW_CHUNKS, ROW_CHUNKS), :], sem.at[slot_])

    def start_all(step, slot_):
        def body(r, c):
            copy(step, slot_, r).start()
            return c
        lax.fori_loop(0, rows, body, 0, unroll=8)

    @pl.when(i == 0)
    def _():
        start_all(0, 0)

    @pl.when(i + 1 < pl.num_programs(0))
    def _():
        start_all(i + 1, 1 - slot)

    def wait_body(r, c):
        copy(i, slot, r).wait()
        return c
    lax.fori_loop(0, rows, wait_body, 0, unroll=8)
    return slot


def _dispatch_kernel(tok_ref, hs_ref, o_ref, buf, sem):
    slot = _gather_step(tok_ref, hs_ref, buf, sem, MOE_BLOCK)
    o_ref[...] = _load_chunk_rows(buf.at[slot], 0, MOE_BLOCK).astype(o_ref.dtype)


def _dispatch(row_tok, hs):
    grid_spec = pltpu.PrefetchScalarGridSpec(
        num_scalar_prefetch=1,
        grid=(MOE_BLOCKS,),
        in_specs=[pl.BlockSpec(memory_space=pl.ANY)],
        out_specs=pl.BlockSpec((MOE_BLOCK, D_MODEL), lambda i, tok: (i, 0)),
        scratch_shapes=[pltpu.VMEM((2, MOE_BLOCK * ROW_CHUNKS, V7X_LANES), jnp.float32),
                        pltpu.SemaphoreType.DMA((2,))],
    )
    return pl.pallas_call(
        _dispatch_kernel,
        grid_spec=grid_spec,
        out_shape=jax.ShapeDtypeStruct((MOE_ROWS, D_MODEL), jnp.bfloat16),
        compiler_params=_params(("arbitrary",), 16 * MIB),
        name="moe_dispatch",
    )(row_tok, hs)


def _combine_kernel(pos_ref, x_ref, y_ref, g_ref, op_ref, os_ref, buf, sem):
    slot = _gather_step(pos_ref, y_ref, buf, sem, TOP_K * MIX_TILE)
    y = sum(_load_chunk_rows(buf.at[slot], k * MIX_TILE, MIX_TILE) for k in range(TOP_K))
    out = _rms(x_ref[...] + y, g_ref[...])
    is_p = pl.program_id(0) < P_ROWS // MIX_TILE

    @pl.when(is_p)
    def _():
        op_ref[...] = out

    @pl.when(jnp.logical_not(is_p))
    def _():
        os_ref[...] = out


def _combine(pos, x, yb, g_final):
    rows = TOP_K * MIX_TILE
    n_p = P_ROWS // MIX_TILE
    grid_spec = pltpu.PrefetchScalarGridSpec(
        num_scalar_prefetch=1,
        grid=(ROWS // MIX_TILE,),
        in_specs=[pl.BlockSpec((MIX_TILE, D_MODEL), lambda i, pos: (i, 0)),
                  pl.BlockSpec(memory_space=pl.ANY),
                  pl.BlockSpec((1, D_MODEL), lambda i, pos: (0, 0))],
        out_specs=[pl.BlockSpec((MIX_TILE, D_MODEL), lambda i, pos: (jnp.minimum(i, n_p - 1), 0)),
                   pl.BlockSpec((MIX_TILE, D_MODEL), lambda i, pos: (jnp.maximum(i - n_p, 0), 0))],
        scratch_shapes=[pltpu.VMEM((2, rows * ROW_CHUNKS, V7X_LANES), jnp.float32),
                        pltpu.SemaphoreType.DMA((2,))],
    )
    return pl.pallas_call(
        _combine_kernel,
        grid_spec=grid_spec,
        out_shape=[jax.ShapeDtypeStruct((P_ROWS, D_MODEL), jnp.float32),
                   jax.ShapeDtypeStruct((S_ROWS, D_MODEL), jnp.float32)],
        compiler_params=_params(("arbitrary",), 32 * MIB),
        name="moe_combine",
    )(pos, x, yb, g_final.reshape(1, D_MODEL))


def _dispatch_plan(route):
    top_idx = route[:, 0:TOP_K].astype(jnp.int32)
    gates = route[:, TOP_K:2 * TOP_K]
    flat_e = top_idx.reshape(N_ASSIGN)
    order = jnp.argsort(flat_e).astype(jnp.int32)
    rank = jnp.argsort(order).astype(jnp.int32)
    counts = jnp.sum((flat_e[:, None] == jnp.arange(N_EXPERTS)[None, :]).astype(jnp.int32), axis=0)
    padded = (counts + MOE_BLOCK - 1) // MOE_BLOCK * MOE_BLOCK
    start = jnp.cumsum(counts) - counts
    ends_p = jnp.cumsum(padded)
    pstart = ends_p - padded
    first_row = jnp.arange(MOE_BLOCKS, dtype=jnp.int32) * MOE_BLOCK
    block_e = jnp.minimum(jnp.sum((ends_p[None, :] <= first_row[:, None]).astype(jnp.int32), axis=1),
                          N_EXPERTS - 1).astype(jnp.int32)
    row = jnp.arange(MOE_ROWS, dtype=jnp.int32)
    row_e = jnp.repeat(block_e, MOE_BLOCK)
    off = row - pstart[row_e]
    real = off < counts[row_e]
    assign = order[jnp.clip(start[row_e] + off, 0, N_ASSIGN - 1)]
    row_tok = jnp.where(real, assign // TOP_K, 0).astype(jnp.int32)
    row_gate = jnp.where(real, gates.reshape(N_ASSIGN)[assign], 0.0)
    pos = (pstart[flat_e] + rank - start[flat_e]).astype(jnp.int32)
    pos = jnp.transpose(pos.reshape(ROWS // MIX_TILE, MIX_TILE, TOP_K), (0, 2, 1)).reshape(N_ASSIGN)
    return row_tok, row_gate.reshape(MOE_ROWS, 1), block_e, pos


def kernel(x_prompt, x_sample, cache_kv_w128, cache_kv_w512, cache_kv_w2048, state_conv, state_ssm, norm_mix, w_in, gate_bias, conv_w, ssm_lam_re, ssm_lam_im, ssm_log_dt, ssm_b_re, ssm_b_im, ssm_c_re, ssm_c_im, ssm_d, ssm_w_glu, ssm_b_glu, proj_conv, proj_attn, proj_ssm, w_out, norm_ffn, ffn_w1, ffn_w3, ffn_w2, router_w, moe_w1, moe_w3, moe_w2, norm_final):
    bf = jnp.bfloat16
    caches = tuple(jnp.transpose(c, (0, 1, 3, 4, 5, 2)) for c in (cache_kv_w128, cache_kv_w512, cache_kv_w2048))
    assert OFF_Q % IN_TILE == 0 and IN_TILE == QK_WIDTH
    rope = (OFF_Q // IN_TILE, _rope_tables())
    x, h = _norm(x_prompt.reshape(P_ROWS, D_MODEL),
                 jnp.transpose(x_sample, (1, 0, 2)).reshape(S_ROWS, D_MODEL), norm_mix[0])
    kv_p = [[] for _ in range(N_ATTN_GROUPS)]
    kv_s = [[] for _ in range(N_ATTN_GROUPS)]
    conv_p_all, conv_s_all, ssm_p_all, ssm_s_all = [], [], [], []
    y = None
    for l in range(DEPTH):
        moe_layer = l % 2 == 1
        if moe_layer:
            za = _linear(h, w_in, l, 0, ZA_WIDTH, jnp.float32, IN_TILE, f"in_proj_{l}", rope=rope)
            w2_flat = moe_w2[l // 2].reshape(N_EXPERTS * D_FF_EXPERT, D_MODEL)
            gates, moe_w2_bf = _linear(h, w_in, l, ZA_WIDTH, GATE_WIDTH, bf, IN_TILE, f"gate_proj_{l}",
                                       side=[(w2_flat, SIDE_CAST_ROWS_W2)])
        else:
            za, ffn_w2_bf = _linear(h, w_in, l, 0, ZA_WIDTH, jnp.float32, IN_TILE, f"in_proj_{l}",
                                    side=[(ffn_w2[l // 2], SIDE_CAST_ROWS_FFN)], rope=rope)
            w3_flat = moe_w3[(l + 1) // 2].reshape(N_EXPERTS * D_MODEL, D_FF_EXPERT)
            gates, moe_w3_bf = _linear(h, w_in, l, ZA_WIDTH, GATE_WIDTH, bf, IN_TILE, f"gate_proj_{l}",
                                       side=[(w3_flat, SIDE_CAST_ROWS_W3)])
        conv_p, cst_p = _conv_prompt(za, conv_w[l])
        conv_s, cst_s = _conv_sample(za, conv_w[l], jnp.transpose(state_conv[l], (1, 0, 2)))
        by_seq = lambda a: jnp.transpose(a.reshape(DEC_SEQ, DEC_BATCH, a.shape[-1]), (1, 0, 2))
        qk_s = by_seq(za[P_ROWS:, OFF_Q:OFF_V])
        v_s = by_seq(za[P_ROWS:, OFF_V:OFF_V + ATTN_QKV])
        attn_p, attn_s = [], []
        for g in range(N_ATTN_GROUPS):
            attn_p += list(_attn_prompt(za, g))
            attn_s += list(_attn_sample(qk_s, v_s, caches[g], l, g))
        tabs = _ssm_tables(ssm_lam_re[l], ssm_lam_im[l], ssm_log_dt[l], ssm_b_re[l], ssm_b_im[l],
                           ssm_c_re[l], ssm_c_im[l], ssm_d[l])
        ssm_p, ssm_s, sst_p, sst_s = _ssm(za, state_ssm[l], tabs)
        moe_layer = l % 2 == 1
        lp = dict(gate_bias=gate_bias[l].reshape(1, GATE_WIDTH), w_glu=ssm_w_glu[l].astype(bf),
                  b_glu=ssm_b_glu[l].reshape(1, SSM_WIDTH), proj_conv=proj_conv[l].astype(bf),
                  proj_attn=proj_attn[l].astype(bf), proj_ssm=proj_ssm[l].astype(bf),
                  w_out=w_out[l].astype(bf), norm_next=norm_ffn[l].reshape(1, D_MODEL))
        x, h = _mix(x, gates, conv_p, conv_s, attn_p, attn_s, ssm_p, ssm_s, lp, bf)
        i = l // 2
        if not moe_layer:
            zero_e = jnp.zeros((ROWS // ROW_TILE,), jnp.int32)
            w1_flat = moe_w1[(l + 1) // 2].reshape(N_EXPERTS * D_MODEL, D_FF_EXPERT)
            hid, moe_w1_bf = _ffn_up(zero_e, h, ffn_w1[i:i + 1], ffn_w3[i:i + 1], ROW_TILE, 512, "ffn_up",
                                     side=[(w1_flat, SIDE_CAST_ROWS_W1)])
            x, h = _ffn_down_dense(hid, ffn_w2_bf, x, norm_mix[l + 1], D_FF // 2)
        else:
            hs, route = _router(x, norm_ffn[l], router_w[i])
            row_tok, row_gate, block_e, pos = _dispatch_plan(route)
            xs = _dispatch(row_tok, hs.reshape(ROWS, ROW_CHUNKS, V7X_LANES))
            expert = lambda w, k: w.reshape(N_EXPERTS, k, w.shape[-1])
            hid = _ffn_up(block_e, xs, expert(moe_w1_bf, D_MODEL), expert(moe_w3_bf, D_MODEL),
                          MOE_BLOCK, D_FF_EXPERT // 2, "moe_up")
            yb = _moe_down(block_e, hid, expert(moe_w2_bf, D_FF_EXPERT), row_gate)
            y = _combine(pos, x, yb.reshape(MOE_ROWS, ROW_CHUNKS, V7X_LANES), norm_final)
        for g, (kp, ks) in enumerate(zip(_kv_prompt(za), _kv_sample(za))):
            kp = kp.reshape(BATCH, 2, HEADS_PER_GROUP, HEAD_DIM, kp.shape[-1])
            kv_p[g].append(jnp.transpose(kp, (0, 4, 1, 2, 3)))
            ks = ks.reshape(DEC_SEQ, 2, HEADS_PER_GROUP, HEAD_DIM, DEC_BATCH)
            kv_s[g].append(jnp.transpose(ks, (4, 0, 1, 2, 3)))
        conv_p_all.append(cst_p)
        conv_s_all.append(jnp.transpose(cst_s, (1, 0, 2)))
        ssm_p_all.append(sst_p)
        ssm_s_all.append(sst_s)
    y_prompt = y[0].reshape(BATCH, SEQ, D_MODEL)
    y_sample = jnp.transpose(y[1].reshape(DEC_SEQ, DEC_BATCH, D_MODEL), (1, 0, 2))
    return (y_prompt, y_sample, jnp.stack(kv_p[0]), jnp.stack(kv_p[1]), jnp.stack(kv_p[2]),
            jnp.stack(conv_p_all), jnp.stack(ssm_p_all), jnp.stack(kv_s[0]), jnp.stack(kv_s[1]),
            jnp.stack(kv_s[2]), jnp.stack(conv_s_all), jnp.stack(ssm_s_all))
```

```python
import functools
import math

import jax
import jax.numpy as jnp
import numpy as np
from jax import lax
from jax.experimental import pallas as pl
from jax.experimental.pallas import tpu as pltpu

D_MODEL = 2048
BATCH = 4
SEQ = 2048
DEPTH = 2
DEC_BATCH = 128
DEC_SEQ = 4
PAST_LEN = 2048
CONV_WIDTH = 512
CONV_K = 3
HEAD_DIM = 64
HEADS_PER_GROUP = 4
ATTN_GROUPS = ((128, 1), (512, 4), (2048, 16))
N_ATTN_GROUPS = 3
ATTN_QKV = N_ATTN_GROUPS * HEADS_PER_GROUP * HEAD_DIM
ATTN_OUT = HEADS_PER_GROUP * HEAD_DIM
ROT_DIM = HEAD_DIM // 4
ROPE_THETA = 500000.0
SSM_WIDTH = 768
SSM_CH_PER_GROUP = 16
SSM_GROUPS = SSM_WIDTH // SSM_CH_PER_GROUP
SSM_STATE = 64
N_BRANCH = 3
D_FF = 5632
N_EXPERTS = 8
TOP_K = 2
D_FF_EXPERT = D_FF // TOP_K
EPS = 1e-6

V7X_LANES = 128
V7X_VMEM_BYTES = 64 * 1024 * 1024
MIB = 1024 * 1024

P_ROWS = BATCH * SEQ
S_ROWS = DEC_SEQ * DEC_BATCH
ROWS = P_ROWS + S_ROWS
W_COL_QK = 3 * CONV_WIDTH
W_COL_GATES = 3 * CONV_WIDTH + 3 * ATTN_QKV + SSM_WIDTH
QK_WIDTH = 2 * ATTN_QKV
ZA_WIDTH = 3 * CONV_WIDTH + ATTN_QKV + SSM_WIDTH
GATE_WIDTH = N_BRANCH * D_MODEL
OFF_Q = 0
OFF_K = ATTN_QKV
OFF_V = 3 * CONV_WIDTH
OFF_U = OFF_V + ATTN_QKV
BAND = 128
ATTN_DENSE_BLOCKS = 4
ATTN_RESIDUE_UNROLL = 4
SSM_CHUNK = 16
SSM_GB = 16
N_GB = SSM_GROUPS // SSM_GB
GB_CH = SSM_GB * SSM_CH_PER_GROUP
GB_ST = SSM_GB * SSM_STATE
ST_WIDTH = N_GB * 2 * GB_ST
ROW_TILE = 512
MIX_TILE = 256
IN_TILE = 1536
MOE_BLOCK = 256
N_ASSIGN = ROWS * TOP_K
MOE_BLOCKS = -(-(N_ASSIGN + N_EXPERTS * (MOE_BLOCK - 1)) // MOE_BLOCK)
MOE_ROWS = MOE_BLOCKS * MOE_BLOCK
ROUTER_PAD = V7X_LANES
SAMPLE_SEQ_TILE = 8
SAMPLE_CACHE_TILE_BYTES = 8 * MIB
SSM_BATCH_SPLIT = 1
HALVES = ATTN_OUT // V7X_LANES
ROW_CHUNKS = D_MODEL // V7X_LANES
SIDE_CAST_ROWS_W1 = 128
SIDE_CAST_ROWS_W3 = 256
SIDE_CAST_ROWS_W2 = 352
SIDE_CAST_ROWS_FFN = 176
assert DEPTH == 2 and HALVES == 2 and GB_CH == HALVES * V7X_LANES

assert ROWS % ROW_TILE == 0 and P_ROWS % ROW_TILE == 0
assert ROWS % MIX_TILE == 0 and P_ROWS % MIX_TILE == 0


def _params(semantics, vmem_bytes):
    return pltpu.CompilerParams(dimension_semantics=semantics,
                                vmem_limit_bytes=min(int(vmem_bytes), V7X_VMEM_BYTES - 4 * MIB))


def _rms(x, g):
    y = x * lax.rsqrt(jnp.mean(x * x, axis=-1, keepdims=True) + EPS)
    return y * g


def _bdot(a, b):
    return jnp.dot(a.astype(jnp.bfloat16), b.astype(jnp.bfloat16), preferred_element_type=jnp.float32)


def _store_chunk_rows(ref, x):
    n = x.shape[0]
    for k in range(ROW_CHUNKS):
        ref[pl.ds(k, n, stride=ROW_CHUNKS), :] = x[:, k * V7X_LANES:(k + 1) * V7X_LANES]


def _load_chunk_rows(ref, first, n):
    return jnp.concatenate([ref[pl.ds(first * ROW_CHUNKS + k, n, stride=ROW_CHUNKS), :]
                            for k in range(ROW_CHUNKS)], axis=-1)


def _sigmoid(x):
    return 0.5 * jnp.tanh(0.5 * x) + 0.5


def _norm_kernel(xp_ref, xs_ref, g_ref, x_ref, h_ref):
    x = jnp.where(pl.program_id(0) < P_ROWS // ROW_TILE, xp_ref[...], xs_ref[...])
    x_ref[...] = x
    h_ref[...] = _rms(x, g_ref[...]).astype(h_ref.dtype)


def _norm(x_p, x_s, g):
    n_p = P_ROWS // ROW_TILE
    row = pl.BlockSpec((ROW_TILE, D_MODEL), lambda i: (i, 0))
    return pl.pallas_call(
        _norm_kernel,
        grid=(ROWS // ROW_TILE,),
        in_specs=[pl.BlockSpec((ROW_TILE, D_MODEL), lambda i: (jnp.minimum(i, n_p - 1), 0)),
                  pl.BlockSpec((ROW_TILE, D_MODEL), lambda i: (jnp.maximum(i - n_p, 0), 0)),
                  pl.BlockSpec((1, D_MODEL), lambda i: (0, 0))],
        out_specs=[row, row],
        out_shape=[jax.ShapeDtypeStruct((ROWS, D_MODEL), jnp.float32),
                   jax.ShapeDtypeStruct((ROWS, D_MODEL), jnp.bfloat16)],
        compiler_params=_params(("parallel",), 40 * MIB),
        name="norm",
    )(x_p, x_s, g.reshape(1, D_MODEL))


def _side_cast_specs(side, n_inner, extra_args=0):
    in_specs, out_specs, out_shape = [], [], []
    for arr, rows in side:
        n_blk = arr.shape[0] // rows
        assert n_blk * rows == arr.shape[0]

        def imap(j, i, *_, n_blk=n_blk):
            return (jnp.minimum(j * n_inner + i, n_blk - 1), 0)
        spec = pl.BlockSpec((rows, arr.shape[1]), imap)
        in_specs.append(spec)
        out_specs.append(spec)
        out_shape.append(jax.ShapeDtypeStruct(arr.shape, jnp.bfloat16))
    return in_specs, out_specs, out_shape


def _side_cast(in_refs, out_refs):
    for src, dst in zip(in_refs, out_refs):
        dst[...] = src[...].astype(dst.dtype)


def _linear_kernel(*refs, n_side, rotate):
    a_ref, w_ref = refs[:2]
    n_tab = 3 if rotate else 0
    tabs = refs[2 + n_side:2 + n_side + n_tab]
    o_ref = refs[2 + n_side + n_tab]
    wb_ref = refs[-1]
    _side_cast(refs[2:2 + n_side], refs[3 + n_side + n_tab:3 + 2 * n_side + n_tab])

    @pl.when(pl.program_id(1) == 0)
    def _():
        wb_ref[...] = w_ref[...].astype(wb_ref.dtype)

    y = jnp.dot(a_ref[...], wb_ref[...], preferred_element_type=jnp.float32)
    if rotate:
        y = _rotate(y, *tabs)
    o_ref[...] = y.astype(o_ref.dtype)


def _linear(a, w, layer, col0, n, out_dtype, bn, name, side=(), col_step=1, rope=None):
    m, k = a.shape
    jb = col0 // bn
    grid = (n // bn, m // ROW_TILE)
    assert all(arr.shape[0] // rows <= grid[0] * grid[1] for arr, rows in side)
    s_in, s_out, s_shape = _side_cast_specs(side, grid[1])
    tables = rope if rope else ()
    assert rope is None or bn == QK_WIDTH
    out = pl.pallas_call(
        functools.partial(_linear_kernel, n_side=len(side), rotate=rope is not None),
        grid=grid,
        in_specs=[pl.BlockSpec((ROW_TILE, k), lambda j, i: (i, 0)),
                  pl.BlockSpec((None, k, bn), lambda j, i: (layer, 0, jb + j * col_step))] + s_in
                 + (_rope_table_specs() if rope else []),
        out_specs=[pl.BlockSpec((ROW_TILE, bn), lambda j, i: (i, j))] + s_out,
        out_shape=[jax.ShapeDtypeStruct((m, n), out_dtype)] + s_shape,
        scratch_shapes=[pltpu.VMEM((k, bn), jnp.bfloat16)],
        compiler_params=_params(("arbitrary", "arbitrary"), 56 * MIB),
        name=name,
    )(a, w, *[arr for arr, _ in side], *tables)
    return out if side else out[0]


def _rope_tables():
    half = ROT_DIM // 2
    inv = ROPE_THETA ** (-jnp.arange(half, dtype=jnp.float32) * 2.0 / ROT_DIM)
    pos_p = jnp.arange(SEQ, dtype=jnp.int32)
    pos_s = jnp.repeat(PAST_LEN + jnp.arange(DEC_SEQ, dtype=jnp.int32), DEC_BATCH)
    pos = jnp.concatenate([pos_p, pos_s]).astype(jnp.float32)
    ang = pos[:, None] * inv[None, :]
    cos, sin = jnp.cos(ang), jnp.sin(ang)
    ones = jnp.ones((pos.shape[0], HEAD_DIM - ROT_DIM), jnp.float32)
    zeros = jnp.zeros_like(ones)
    zh = jnp.zeros_like(sin)
    cos_t = jnp.concatenate([cos, cos, ones], axis=1)
    sin_hi = jnp.concatenate([-sin, zh, zeros], axis=1)
    sin_lo = jnp.concatenate([zh, sin, zeros], axis=1)
    reps = V7X_LANES // HEAD_DIM
    return tuple(jnp.tile(t, (1, reps)) for t in (cos_t, sin_hi, sin_lo))


def _rotate(x, c_ref, sh_ref, sl_ref):
    half = ROT_DIM // 2
    reps = QK_WIDTH // V7X_LANES
    c = jnp.tile(c_ref[...], (1, reps))
    sh = jnp.tile(sh_ref[...], (1, reps))
    sl = jnp.tile(sl_ref[...], (1, reps))
    x_up = pltpu.roll(x, QK_WIDTH - half, axis=1)
    x_dn = pltpu.roll(x, half, axis=1)
    return x * c + x_up * sh + x_dn * sl


def _rope_table_specs():
    n_p = P_ROWS // ROW_TILE
    per_seq = SEQ // ROW_TILE
    tmap = lambda j, i: (jnp.where(i < n_p, i % per_seq, per_seq), 0)
    return [pl.BlockSpec((ROW_TILE, V7X_LANES), tmap)] * 3


def _kv_prompt_kernel(k_ref, v_ref, *o_refs, keeps):
    last = pl.program_id(1) == SEQ // ROW_TILE - 1
    k, v = k_ref[...], v_ref[...]

    def write(g, lo):
        cols = slice(g * ATTN_OUT, (g + 1) * ATTN_OUT)
        o_refs[g][0] = k[lo:, cols].T
        o_refs[g][1] = v[lo:, cols].T

    for g, keep in enumerate(keeps):
        if keep == SEQ:
            write(g, 0)
        else:
            pl.when(last)(functools.partial(write, g, ROW_TILE - min(keep, ROW_TILE)))


def _kv_prompt(qk, za):
    keeps = tuple(min(w, SEQ) for w, _ in ATTN_GROUPS)
    assert all(k == SEQ or k <= ROW_TILE for k in keeps)
    n_t = SEQ // ROW_TILE
    out_specs, out_shape = [], []
    for keep in keeps:
        blk = min(keep, ROW_TILE)
        imap = (lambda b, j: (b, 0, 0, j)) if keep == SEQ else (lambda b, j: (b, 0, 0, 0))
        out_specs.append(pl.BlockSpec((None, 2, ATTN_OUT, blk), imap))
        out_shape.append(jax.ShapeDtypeStruct((BATCH, 2, ATTN_OUT, keep), jnp.float32))
    return pl.pallas_call(
        functools.partial(_kv_prompt_kernel, keeps=keeps),
        grid=(BATCH, n_t),
        in_specs=[pl.BlockSpec((ROW_TILE, ATTN_QKV), lambda b, j: (b * n_t + j, OFF_K // ATTN_QKV)),
                  pl.BlockSpec((ROW_TILE, ATTN_QKV), lambda b, j: (b * n_t + j, OFF_V // ATTN_QKV))],
        out_specs=out_specs,
        out_shape=out_shape,
        compiler_params=_params(("parallel", "arbitrary"), 32 * MIB),
        name="kv_prompt",
    )(qk, za)


def _kv_sample_kernel(k_ref, v_ref, *o_refs):
    for t in range(DEC_SEQ):
        rows = slice(t * DEC_BATCH, (t + 1) * DEC_BATCH)
        for g in range(N_ATTN_GROUPS):
            cols = slice(g * ATTN_OUT, (g + 1) * ATTN_OUT)
            o_refs[g][t, 0] = k_ref[rows, cols].T
            o_refs[g][t, 1] = v_ref[rows, cols].T


def _kv_sample(qk, za):
    blk = P_ROWS // S_ROWS
    shape = (DEC_SEQ, 2, ATTN_OUT, DEC_BATCH)
    return pl.pallas_call(
        _kv_sample_kernel,
        grid=(1,),
        in_specs=[pl.BlockSpec((S_ROWS, ATTN_QKV), lambda i: (blk, OFF_K // ATTN_QKV)),
                  pl.BlockSpec((S_ROWS, ATTN_QKV), lambda i: (blk, OFF_V // ATTN_QKV))],
        out_specs=[pl.BlockSpec(shape, lambda i: (0, 0, 0, 0))] * N_ATTN_GROUPS,
        out_shape=[jax.ShapeDtypeStruct(shape, jnp.float32)] * N_ATTN_GROUPS,
        compiler_params=_params(("arbitrary",), 32 * MIB),
        name="kv_sample",
    )(qk, za)


def _conv_prompt_kernel(h_ref, b_ref, c_ref, w_ref, y_ref, st_ref):
    u = c_ref[...] * h_ref[...]
    w = w_ref[...]
    row = lax.broadcasted_iota(jnp.int32, u.shape, 0)
    u1 = jnp.where(row >= 1, pltpu.roll(u, 1, axis=0), 0.0)
    u2 = jnp.where(row >= 2, pltpu.roll(u, 2, axis=0), 0.0)
    y = w[0:1] * u2 + w[1:2] * u1 + w[2:3] * u
    y_ref[...] = (b_ref[...] * y).astype(y_ref.dtype)
    st_ref[...] = u[SEQ - (CONV_K - 1):, :]


def _conv_prompt(za, w):
    col = lambda c: pl.BlockSpec((SEQ, CONV_WIDTH), lambda b, c=c: (b, c))
    return pl.pallas_call(
        _conv_prompt_kernel,
        grid=(BATCH,),
        in_specs=[col(0), col(1), col(2), pl.BlockSpec((CONV_K, CONV_WIDTH), lambda b: (0, 0))],
        out_specs=[pl.BlockSpec((SEQ, CONV_WIDTH), lambda b: (b, 0)),
                   pl.BlockSpec((None, CONV_K - 1, CONV_WIDTH), lambda b: (b, 0, 0))],
        out_shape=[jax.ShapeDtypeStruct((P_ROWS, CONV_WIDTH), jnp.bfloat16),
                   jax.ShapeDtypeStruct((BATCH, CONV_K - 1, CONV_WIDTH), jnp.float32)],
        compiler_params=_params(("parallel",), 48 * MIB),
        name="conv_prompt",
    )(za, za, za, w)


def _conv_sample_kernel(h_ref, b_ref, c_ref, w_ref, s_ref, y_ref, st_ref):
    u = c_ref[...] * h_ref[...]
    w = w_ref[...]
    s0, s1 = s_ref[0], s_ref[1]
    keep = S_ROWS - DEC_BATCH
    u1 = jnp.concatenate([s1, u[:keep]], axis=0)
    u2 = jnp.concatenate([s0, s1, u[:keep - DEC_BATCH]], axis=0)
    y = w[0:1] * u2 + w[1:2] * u1 + w[2:3] * u
    y_ref[...] = (b_ref[...] * y).astype(y_ref.dtype)
    st_ref[0] = u[S_ROWS - 2 * DEC_BATCH:S_ROWS - DEC_BATCH]
    st_ref[1] = u[S_ROWS - DEC_BATCH:]


def _conv_sample(za, w, state):
    blk = P_ROWS // S_ROWS
    col = lambda c: pl.BlockSpec((S_ROWS, CONV_WIDTH), lambda i, c=c: (blk, c))
    st_spec = pl.BlockSpec((CONV_K - 1, DEC_BATCH, CONV_WIDTH), lambda i: (0, 0, 0))
    return pl.pallas_call(
        _conv_sample_kernel,
        grid=(1,),
        in_specs=[col(0), col(1), col(2), pl.BlockSpec((CONV_K, CONV_WIDTH), lambda i: (0, 0)), st_spec],
        out_specs=[pl.BlockSpec((S_ROWS, CONV_WIDTH), lambda i: (0, 0)), st_spec],
        out_shape=[jax.ShapeDtypeStruct((S_ROWS, CONV_WIDTH), jnp.bfloat16),
                   jax.ShapeDtypeStruct((CONV_K - 1, DEC_BATCH, CONV_WIDTH), jnp.float32)],
        compiler_params=_params(("arbitrary",), 24 * MIB),
        name="conv_sample",
    )(za, za, za, w, state)


def _attn_prompt_kernel(*refs, dil, has_prev, sub):
    n_in = (5 if has_prev else 3) * HALVES
    q_refs, kc_refs, vc_refs = (refs[i * HALVES:(i + 1) * HALVES] for i in range(3))
    if has_prev:
        kp_refs, vp_refs = (refs[i * HALVES:(i + 1) * HALVES] for i in range(3, 5))
    o_ref, l_ref = refs[n_in:]
    n_keys = (2 if has_prev else 1) * BAND
    qi = lax.broadcasted_iota(jnp.int32, (BAND, n_keys), 0)
    kj = lax.broadcasted_iota(jnp.int32, (BAND, n_keys), 1)
    if has_prev:
        dist = qi + BAND - kj
        inner_ok = (dist >= 0) & (dist <= BAND)
        first_ok = inner_ok & ((kj >= BAND) | (pl.program_id(1) > 0))
    else:
        first_ok = kj <= qi
    heads_per_half = V7X_LANES // HEAD_DIM

    def attend(q, k, v, valid):
        q, k, v = (t.astype(jnp.bfloat16) for t in (q, k, v))
        outs, lses = [], []
        for h in range(heads_per_half):
            sl = slice(h * HEAD_DIM, (h + 1) * HEAD_DIM)
            s = lax.dot_general(q[:, sl], k[:, sl], (((1,), (1,)), ((), ())),
                                preferred_element_type=jnp.float32) * (HEAD_DIM ** -0.5)
            s = jnp.where(valid, s, -jnp.inf)
            m = jnp.max(s, axis=-1, keepdims=True)
            p = jnp.exp(s - m)
            den = jnp.sum(p, axis=-1, keepdims=True)
            outs.append(jnp.dot(p.astype(jnp.bfloat16), v[:, sl], preferred_element_type=jnp.float32) / den)
            lses.append(jnp.broadcast_to(m + jnp.log(den), (BAND, HEAD_DIM)))
        return jnp.concatenate(outs, axis=-1), jnp.concatenate(lses, axis=-1)

    if dil == 1:
        for half in range(HALVES):
            q, kc, vc = q_refs[half][...], kc_refs[half][...], vc_refs[half][...]
            kp, vp = kp_refs[half][...], vp_refs[half][...]
            for s in range(sub):
                lo, hi = (s - 1) * BAND, (s + 1) * BAND
                k = jnp.concatenate([kp, kc[:BAND]], axis=0) if s == 0 else kc[lo:hi]
                v = jnp.concatenate([vp, vc[:BAND]], axis=0) if s == 0 else vc[lo:hi]
                o, l = attend(q[s * BAND:hi], k, v, first_ok if s == 0 else inner_ok)
                o_ref[half, s * BAND:hi, :] = o
                l_ref[half, s * BAND:hi, :] = l
        return

    def residue(r, carry):
        rows = pl.ds(r, BAND, stride=dil)
        for half in range(HALVES):
            k = kc_refs[half][rows, :]
            v = vc_refs[half][rows, :]
            if has_prev:
                k = jnp.concatenate([kp_refs[half][rows, :], k], axis=0)
                v = jnp.concatenate([vp_refs[half][rows, :], v], axis=0)
            o, l = attend(q_refs[half][rows, :], k, v, first_ok)
            o_ref[half, rows, :] = o
            l_ref[half, rows, :] = l
        return carry

    lax.fori_loop(0, dil, residue, 0, unroll=min(dil, ATTN_RESIDUE_UNROLL))


def _attn_prompt(qk, za, g):
    d = ATTN_GROUPS[g][1]
    sub = ATTN_DENSE_BLOCKS if d == 1 else 1
    rows = BAND * d * sub
    nb = SEQ // rows
    has_prev = nb > 1
    prev_rows = BAND if d == 1 else rows
    qcol, kcol, vcol = (off // V7X_LANES + g * HALVES for off in (OFF_Q, OFF_K, OFF_V))
    cur = lambda b, j: b * nb + j
    prev = lambda b, j: b * (SEQ // prev_rows) + jnp.maximum(j * (rows // prev_rows) - 1, 0)
    spec = lambda n, row, col: [pl.BlockSpec((n, V7X_LANES), lambda b, j, h=h: (row(b, j), col + h))
                                for h in range(HALVES)]
    in_specs = spec(rows, cur, qcol) + spec(rows, cur, kcol) + spec(rows, cur, vcol)
    args = [qk] * (2 * HALVES) + [za] * HALVES
    if has_prev:
        in_specs += spec(prev_rows, prev, kcol) + spec(prev_rows, prev, vcol)
        args += [qk] * HALVES + [za] * HALVES
    out_spec = pl.BlockSpec((HALVES, rows, V7X_LANES), lambda b, j: (0, cur(b, j), 0))
    shape = jax.ShapeDtypeStruct((HALVES, P_ROWS, V7X_LANES), jnp.float32)
    return pl.pallas_call(
        functools.partial(_attn_prompt_kernel, dil=d, has_prev=has_prev, sub=sub),
        grid=(BATCH, nb),
        in_specs=in_specs,
        out_specs=[out_spec, out_spec],
        out_shape=[shape, shape],
        compiler_params=_params(("parallel", "parallel"), 40 * MIB),
        name=f"attn_prompt_g{g}",
    )(*args)


def _attn_sample_kernel(q_ref, kn_ref, vn_ref, c_ref, o_ref, l_ref, *, dil, nseq):
    width = c_ref.shape[-1]
    scale = HEAD_DIM ** -0.5
    nt = (((1,), (1,)), ((), ()))
    n_rows = HEADS_PER_GROUP * DEC_SEQ
    row_l = lax.broadcasted_iota(jnp.int32, (n_rows, ATTN_OUT), 0)
    lane = lax.broadcasted_iota(jnp.int32, (n_rows, ATTN_OUT), 1)
    own_head = lane // HEAD_DIM == row_l // DEC_SEQ
    t_c = lax.broadcasted_iota(jnp.int32, (n_rows, width), 0) % DEC_SEQ
    w_c = lax.broadcasted_iota(jnp.int32, (n_rows, width), 1)
    ok_c = (w_c >= t_c) & (w_c % dil == t_c % dil)
    r_n = lax.broadcasted_iota(jnp.int32, (n_rows, n_rows), 0)
    c_n = lax.broadcasted_iota(jnp.int32, (n_rows, n_rows), 1)
    t_n, s_n = r_n % DEC_SEQ, c_n % DEC_SEQ
    ok_n = (r_n // DEC_SEQ == c_n // DEC_SEQ) & (s_n <= t_n) & (s_n % dil == t_n % dil)

    def per_head(x):
        return jnp.where(own_head, jnp.tile(x, (HEADS_PER_GROUP, 1)), 0.0).astype(jnp.bfloat16)

    def one_seq(s, carry):
        q, kn, vn = per_head(q_ref[s]), per_head(kn_ref[s]), per_head(vn_ref[s])
        k_t = c_ref[s, 0].reshape(ATTN_OUT, width).astype(jnp.bfloat16)
        v_t = c_ref[s, 1].reshape(ATTN_OUT, width).astype(jnp.bfloat16)
        sc = jnp.dot(q, k_t, preferred_element_type=jnp.float32) * scale
        sc = jnp.where(ok_c, sc, -jnp.inf)
        sn = lax.dot_general(q, kn, nt, preferred_element_type=jnp.float32) * scale
        sn = jnp.where(ok_n, sn, -jnp.inf)
        m = jnp.maximum(jnp.max(sc, axis=-1, keepdims=True), jnp.max(sn, axis=-1, keepdims=True))
        pc = jnp.exp(sc - m)
        pn = jnp.exp(sn - m)
        den = jnp.sum(pc, axis=-1, keepdims=True) + jnp.sum(pn, axis=-1, keepdims=True)
        o = (lax.dot_general(pc.astype(jnp.bfloat16), v_t, nt, preferred_element_type=jnp.float32)
             + jnp.dot(pn.astype(jnp.bfloat16), vn, preferred_element_type=jnp.float32))
        o = jnp.where(own_head, o / den, 0.0)
        lse = jnp.where(own_head, m + jnp.log(den), 0.0)
        fold = lambda a: sum(a[h * DEC_SEQ:(h + 1) * DEC_SEQ] for h in range(HEADS_PER_GROUP))
        o_ref[s] = fold(o)
        l_ref[s] = fold(lse)
        return carry

    lax.fori_loop(0, nseq, one_seq, 0, unroll=True)


def _attn_sample(qk_s, v_s, cache_t, layer, g):
    d = ATTN_GROUPS[g][1]
    width = cache_t.shape[-1]
    seq_bytes = 2 * ATTN_OUT * width * 4
    nseq = max(1, min(SAMPLE_SEQ_TILE, SAMPLE_CACHE_TILE_BYTES // seq_bytes))
    blk = (nseq, DEC_SEQ, ATTN_OUT)
    koff = ATTN_QKV // ATTN_OUT
    out_spec = pl.BlockSpec(blk, lambda i: (i, 0, 0))
    shape = jax.ShapeDtypeStruct((DEC_BATCH, DEC_SEQ, ATTN_OUT), jnp.float32)
    o, l = pl.pallas_call(
        functools.partial(_attn_sample_kernel, dil=d, nseq=nseq),
        grid=(DEC_BATCH // nseq,),
        in_specs=[pl.BlockSpec(blk, lambda i: (i, 0, g)),
                  pl.BlockSpec(blk, lambda i: (i, 0, koff + g)),
                  pl.BlockSpec(blk, lambda i: (i, 0, g)),
                  pl.BlockSpec((None, nseq, 2, HEADS_PER_GROUP, HEAD_DIM, width),
                               lambda i: (layer, i, 0, 0, 0, 0))],
        out_specs=[out_spec, out_spec],
        out_shape=[shape, shape],
        compiler_params=_params(("parallel",), 4 * SAMPLE_CACHE_TILE_BYTES),
        name=f"attn_sample_g{g}",
    )(qk_s, qk_s, v_s, cache_t)
    to_rows = lambda a: jnp.transpose(a, (1, 0, 2)).reshape(S_ROWS, ATTN_OUT)
    return to_rows(o), to_rows(l)


def _ssm_tables(lam_re, lam_im, log_dt, b_re, b_im, c_re, c_im, d_skip):
    dt = jnp.exp(log_dt)[:, None]
    er = jnp.exp(lam_re * dt)
    lbr, lbi = er * jnp.cos(lam_im * dt), er * jnp.sin(lam_im * dt)
    den = lam_re * lam_re + lam_im * lam_im
    nr, ni = lbr - 1.0, lbi
    qr = (nr * lam_re + ni * lam_im) / den
    qi = (ni * lam_re - nr * lam_im) / den
    bbr = qr[:, :, None] * b_re - qi[:, :, None] * b_im
    bbi = qr[:, :, None] * b_im + qi[:, :, None] * b_re
    eye = jnp.eye(SSM_GB, dtype=jnp.float32)

    def in_block(b):
        b = b.reshape(N_GB, SSM_GB, SSM_STATE, SSM_CH_PER_GROUP)
        return jnp.einsum('ngpc,gh->ngchp', b, eye).reshape(N_GB, GB_CH, GB_ST)

    def out_block(c):
        c = c.reshape(N_GB, SSM_GB, SSM_CH_PER_GROUP, SSM_STATE)
        return jnp.einsum('ngcp,gh->ngphc', c, eye).reshape(N_GB, GB_ST, GB_CH)

    b_blk = jnp.concatenate([in_block(bbr), in_block(bbi)], axis=2).astype(jnp.bfloat16)
    c_blk = jnp.concatenate([out_block(c_re), out_block(-c_im)], axis=1).astype(jnp.bfloat16)

    def pack(re, im):
        re = re.reshape(N_GB, 1, GB_ST)
        im = im.reshape(N_GB, 1, GB_ST)
        return jnp.concatenate([re, re], axis=2), jnp.concatenate([-im, im], axis=2)

    pr, pi = lbr, lbi
    for _ in range(int(math.log2(SSM_CHUNK))):
        pr, pi = pr * pr - pi * pi, 2.0 * pr * pi
    return dict(b=b_blk, c=c_blk, lam=pack(lbr, lbi), lam_chunk=pack(pr, pi), d=d_skip.reshape(1, SSM_WIDTH))


def _cstep(s, a, bsw):
    swapped = jnp.concatenate([s[:, GB_ST:], s[:, :GB_ST]], axis=1)
    return a * s + bsw * swapped


def _ssm_kernel(u_ref, s0_ref, b_ref, c_ref, a_ref, bsw_ref, d_ref, *refs, steps, emit_y):
    if emit_y:
        y_ref, sout_ref, s_ref = refs
    else:
        sout_ref, s_ref = refs
    t = pl.program_id(1)

    @pl.when(t == 0)
    def _():
        s_ref[...] = s0_ref[...]

    u = u_ref[...]
    s = _cstep(s_ref[...], a_ref[...], bsw_ref[...]) + _bdot(u, b_ref[...])
    s_ref[...] = s
    if emit_y:
        y_ref[...] = _bdot(s, c_ref[...]) + d_ref[...] * u

    @pl.when(t == steps - 1)
    def _():
        sout_ref[...] = s


def _ssm_pass(u_arr, u_map, y_map, rows, steps, s0, tabs, y_shape, name):
    emit_y = y_shape is not None
    st_spec = pl.BlockSpec((rows, 2 * GB_ST), lambda gb, t: (0, gb))
    coef = pl.BlockSpec((None, 1, 2 * GB_ST), lambda gb, t: (gb, 0, 0))
    in_specs = [pl.BlockSpec((rows, GB_CH), u_map), st_spec,
                pl.BlockSpec((None, GB_CH, 2 * GB_ST), lambda gb, t: (gb, 0, 0)),
                pl.BlockSpec((None, 2 * GB_ST, GB_CH), lambda gb, t: (gb, 0, 0)),
                coef, coef,
                pl.BlockSpec((1, GB_CH), lambda gb, t: (0, gb))]
    st_shape = jax.ShapeDtypeStruct((rows, ST_WIDTH), jnp.float32)
    out_specs, out_shape = [st_spec], [st_shape]
    if emit_y:
        out_specs, out_shape = [pl.BlockSpec((rows, GB_CH), y_map), st_spec], [y_shape, st_shape]
    return pl.pallas_call(
        functools.partial(_ssm_kernel, steps=steps, emit_y=emit_y),
        grid=(N_GB, steps),
        in_specs=in_specs,
        out_specs=out_specs,
        out_shape=out_shape,
        scratch_shapes=[pltpu.VMEM((rows, 2 * GB_ST), jnp.float32)],
        compiler_params=_params(("parallel", "arbitrary"), 48 * MIB),
        name=name,
    )(u_arr, s0, tabs['b'], tabs['c'], tabs['lam'][0], tabs['lam'][1], tabs['d'])


def _ssm_prompt_kernel(*refs, emit_y):
    u_refs = refs[:HALVES]
    s0_ref, b_ref, c_ref, a_ref, bsw_ref, d_ref = refs[HALVES:HALVES + 6]
    if emit_y:
        y_ref, sout_ref, s_ref = refs[HALVES + 6:]
    else:
        sout_ref, s_ref = refs[HALVES + 6:]
    t = pl.program_id(2)
    rows = pl.ds(t, s_ref.shape[0], stride=SSM_CHUNK)

    @pl.when(t == 0)
    def _():
        s_ref[...] = s0_ref[...]

    u = jnp.concatenate([r[rows, :] for r in u_refs], axis=-1)
    s = _cstep(s_ref[...], a_ref[...], bsw_ref[...]) + _bdot(u, b_ref[...])
    s_ref[...] = s
    if emit_y:
        y = _bdot(s, c_ref[...]) + d_ref[...] * u
        for half in range(HALVES):
            y_ref[half, rows, :] = y[:, half * V7X_LANES:(half + 1) * V7X_LANES]

    @pl.when(t == SSM_CHUNK - 1)
    def _():
        sout_ref[...] = s


def _ssm_prompt_pass(za, s0, tabs, emit_y, name):
    n_rows = P_ROWS // SSM_CHUNK // SSM_BATCH_SPLIT
    nat_rows = P_ROWS // SSM_BATCH_SPLIT
    ucol = OFF_U // V7X_LANES
    once = pl.Buffered(1)
    u_specs = [pl.BlockSpec((nat_rows, V7X_LANES), lambda gb, p, t, h=h: (p, ucol + gb * HALVES + h),
                            pipeline_mode=once) for h in range(HALVES)]
    st_spec = pl.BlockSpec((n_rows, 2 * GB_ST), lambda gb, p, t: (p, gb))
    s0_spec = pl.BlockSpec((n_rows, 2 * GB_ST), lambda gb, p, t: (p, gb), pipeline_mode=once)
    coef = pl.BlockSpec((None, 1, 2 * GB_ST), lambda gb, p, t: (gb, 0, 0))
    in_specs = u_specs + [s0_spec,
                          pl.BlockSpec((None, GB_CH, 2 * GB_ST), lambda gb, p, t: (gb, 0, 0)),
                          pl.BlockSpec((None, 2 * GB_ST, GB_CH), lambda gb, p, t: (gb, 0, 0)),
                          coef, coef,
                          pl.BlockSpec((1, GB_CH), lambda gb, p, t: (0, gb))]
    st_shape = jax.ShapeDtypeStruct((P_ROWS // SSM_CHUNK, ST_WIDTH), jnp.float32)
    out_specs, out_shape = [st_spec], [st_shape]
    if emit_y:
        out_specs = [pl.BlockSpec((HALVES, nat_rows, V7X_LANES), lambda gb, p, t: (gb, p, 0)), st_spec]
        out_shape = [jax.ShapeDtypeStruct((N_GB * HALVES, P_ROWS, V7X_LANES), jnp.float32), st_shape]
    return pl.pallas_call(
        functools.partial(_ssm_prompt_kernel, emit_y=emit_y),
        grid=(N_GB, SSM_BATCH_SPLIT, SSM_CHUNK),
        in_specs=in_specs,
        out_specs=out_specs,
        out_shape=out_shape,
        scratch_shapes=[pltpu.VMEM((n_rows, 2 * GB_ST), jnp.float32)],
        compiler_params=_params(("parallel", "parallel", "arbitrary"), 58 * MIB),
        name=name,
    )(*([za] * HALVES), s0, tabs['b'], tabs['c'], tabs['lam'][0], tabs['lam'][1], tabs['d'])


def _ssm_carry_kernel(e_ref, a_ref, bsw_ref, sin_ref, fin_ref):
    a = a_ref[...]
    bsw = bsw_ref[...]
    n_chunks = SEQ // SSM_CHUNK

    def body(n, states):
        new = []
        for b in range(BATCH):
            row = b * n_chunks + n
            sin_ref[pl.ds(row, 1), :] = states[b]
            new.append(_cstep(states[b], a, bsw) + e_ref[pl.ds(row, 1), :])
        return tuple(new)

    zero = jnp.zeros((1, 2 * GB_ST), jnp.float32)
    final = lax.fori_loop(0, n_chunks, body, (zero,) * BATCH)
    for b in range(BATCH):
        fin_ref[pl.ds(b, 1), :] = final[b]


def _ssm_carry(e, tabs):
    n_rows = P_ROWS // SSM_CHUNK
    st_spec = lambda rows: pl.BlockSpec((rows, 2 * GB_ST), lambda gb: (0, gb))
    coef = pl.BlockSpec((None, 1, 2 * GB_ST), lambda gb: (gb, 0, 0))
    return pl.pallas_call(
        _ssm_carry_kernel,
        grid=(N_GB,),
        in_specs=[st_spec(n_rows), coef, coef],
        out_specs=[st_spec(n_rows), st_spec(BATCH)],
        out_shape=[jax.ShapeDtypeStruct((n_rows, ST_WIDTH), jnp.float32),
                   jax.ShapeDtypeStruct((BATCH, ST_WIDTH), jnp.float32)],
        compiler_params=_params(("parallel",), 32 * MIB),
        name="ssm_carry",
    )(e, tabs['lam_chunk'][0], tabs['lam_chunk'][1])


def _pack_state(s):
    b = s.shape[0]
    s = s.reshape(b, N_GB, SSM_GB * SSM_STATE, 2)
    return jnp.transpose(s, (0, 1, 3, 2)).reshape(b, ST_WIDTH)


def _unpack_state(s):
    b = s.shape[0]
    s = s.reshape(b, N_GB, 2, SSM_GB * SSM_STATE)
    return jnp.transpose(s, (0, 1, 3, 2)).reshape(b, SSM_GROUPS, SSM_STATE, 2)


def _ssm(za, state_s, tabs):
    ucol = OFF_U // GB_CH
    zero = jnp.zeros((P_ROWS // SSM_CHUNK, ST_WIDTH), jnp.float32)
    (e,) = _ssm_prompt_pass(za, zero, tabs, False, "ssm_prompt_local")
    s_in, s_fin = _ssm_carry(e, tabs)
    y_p, _ = _ssm_prompt_pass(za, s_in, tabs, True, "ssm_prompt")
    tblk = P_ROWS // DEC_BATCH
    ys_shape = jax.ShapeDtypeStruct((S_ROWS, SSM_WIDTH), jnp.float32)
    y_s, s_new = _ssm_pass(za, lambda gb, t: (tblk + t, ucol + gb), lambda gb, t: (t, gb), DEC_BATCH, DEC_SEQ,
                           _pack_state(state_s), tabs, ys_shape, "ssm_sample")
    return y_p, y_s, _unpack_state(s_fin), _unpack_state(s_new)


def _mix_kernel(x_ref, gate_ref, gb_ref, cp_ref, cs_ref, ap_refs, as_refs, sp_ref, ss_ref,
                wglu_ref, bglu_ref, pc_ref, pa_ref, ps_ref, wo_ref, gn_ref, xo_ref, ho_ref):
    is_p = pl.program_id(0) < P_ROWS // MIX_TILE

    def pick(p_ref, s_ref):
        p = p_ref[...]
        if p.ndim == 3:
            p = jnp.concatenate([p[i] for i in range(p.shape[0])], axis=-1)
        return jnp.where(is_p, p, s_ref[...])

    y_conv = pick(cp_ref, cs_ref)
    o = [pick(ap_refs[2 * g], as_refs[2 * g]) for g in range(N_ATTN_GROUPS)]
    lse = [pick(ap_refs[2 * g + 1], as_refs[2 * g + 1]) for g in range(N_ATTN_GROUPS)]
    m = jnp.maximum(jnp.maximum(lse[0], lse[1]), lse[2])
    w = [jnp.exp(l - m) for l in lse]
    y_attn = (w[0] * o[0] + w[1] * o[1] + w[2] * o[2]) / (w[0] + w[1] + w[2])
    zg = jax.nn.gelu(pick(sp_ref, ss_ref))
    y_ssm = zg * _sigmoid(_bdot(zg, wglu_ref[...]) + bglu_ref[...])
    gt = _sigmoid(gate_ref[...].astype(jnp.float32) + gb_ref[...])
    merged = (gt[:, :D_MODEL] * _bdot(y_conv, pc_ref[...])
              + gt[:, D_MODEL:2 * D_MODEL] * _bdot(y_attn, pa_ref[...])
              + gt[:, 2 * D_MODEL:] * _bdot(y_ssm, ps_ref[...]))
    x = x_ref[...] + _bdot(merged, wo_ref[...])
    xo_ref[...] = x
    ho_ref[...] = _rms(x, gn_ref[...]).astype(ho_ref.dtype)


def _mix(x, gates, conv_p, conv_s, attn_p, attn_s, ssm_p, ssm_s, lp, h_dtype):
    n_p = P_ROWS // MIX_TILE
    p_map = lambda i: (jnp.minimum(i, n_p - 1), 0)
    s_map = lambda i: (jnp.maximum(i - n_p, 0), 0)
    row_map = lambda i: (i, 0)
    const = lambda i: (0, 0)
    rows = lambda width, imap: pl.BlockSpec((MIX_TILE, width), imap)
    slabs = lambda n: pl.BlockSpec((n, MIX_TILE, V7X_LANES), lambda i: (0, jnp.minimum(i, n_p - 1), 0))
    full = lambda a: pl.BlockSpec(a.shape, const, pipeline_mode=pl.Buffered(1))
    n_attn = 2 * N_ATTN_GROUPS
    weights = [lp['w_glu'], lp['b_glu'], lp['proj_conv'], lp['proj_attn'], lp['proj_ssm'], lp['w_out'],
               lp['norm_next']]
    in_specs = ([rows(D_MODEL, row_map), rows(GATE_WIDTH, row_map), full(lp['gate_bias']),
                 rows(CONV_WIDTH, p_map), rows(CONV_WIDTH, s_map)]
                + [slabs(HALVES)] * n_attn + [rows(ATTN_OUT, s_map)] * n_attn
                + [slabs(N_GB * HALVES), rows(SSM_WIDTH, s_map)]
                + [full(w) for w in weights])

    def body(*refs):
        x_ref, gate_ref, gb_ref, cp_ref, cs_ref = refs[:5]
        ap_refs = refs[5:5 + n_attn]
        as_refs = refs[5 + n_attn:5 + 2 * n_attn]
        rest = refs[5 + 2 * n_attn:]
        _mix_kernel(x_ref, gate_ref, gb_ref, cp_ref, cs_ref, ap_refs, as_refs, *rest)

    return pl.pallas_call(
        body,
        grid=(ROWS // MIX_TILE,),
        in_specs=in_specs,
        out_specs=[rows(D_MODEL, row_map), rows(D_MODEL, row_map)],
        out_shape=[jax.ShapeDtypeStruct((ROWS, D_MODEL), jnp.float32),
                   jax.ShapeDtypeStruct((ROWS, D_MODEL), h_dtype)],
        compiler_params=_params(("parallel",), 56 * MIB),
        name="mix",
    )(x, gates, lp['gate_bias'], conv_p, conv_s, *attn_p, *attn_s, ssm_p, ssm_s, *weights)


def _ffn_up_kernel(be_ref, *refs, n_side, cast):
    x_ref, w1_ref, w3_ref = refs[:3]
    h_ref = refs[3 + n_side]
    _side_cast(refs[3:3 + n_side], refs[4 + n_side:4 + 2 * n_side])
    if cast:
        w1b_ref, w3b_ref = refs[-2:]

        @pl.when(pl.program_id(1) == 0)
        def _():
            w1b_ref[...] = w1_ref[...].astype(w1b_ref.dtype)
            w3b_ref[...] = w3_ref[...].astype(w3b_ref.dtype)
    else:
        w1b_ref, w3b_ref = w1_ref, w3_ref
    x = x_ref[...].astype(jnp.bfloat16)
    a = jnp.dot(x, w1b_ref[...], preferred_element_type=jnp.float32)
    b = jnp.dot(x, w3b_ref[...], preferred_element_type=jnp.float32)
    h_ref[...] = (a * _sigmoid(a) * b).astype(h_ref.dtype)


def _ffn_up(block_e, x, w1, w3, bm, bf, name, side=()):
    m = x.shape[0]
    e, _, f = w1.shape
    cast = w1.dtype != jnp.bfloat16
    assert not cast or e == 1
    grid = (f // bf, m // bm)
    assert all(arr.shape[0] // rows <= grid[0] * grid[1] for arr, rows in side)
    s_in, s_out, s_shape = _side_cast_specs(side, grid[1])
    wspec = pl.BlockSpec((None, D_MODEL, bf), lambda j, i, be: (be[i], 0, j))
    grid_spec = pltpu.PrefetchScalarGridSpec(
        num_scalar_prefetch=1,
        grid=grid,
        in_specs=[pl.BlockSpec((bm, D_MODEL), lambda j, i, be: (i, 0)), wspec, wspec] + s_in,
        out_specs=[pl.BlockSpec((bm, bf), lambda j, i, be: (i, j))] + s_out,
        scratch_shapes=[pltpu.VMEM((D_MODEL, bf), jnp.bfloat16)] * 2 if cast else [],
    )
    out = pl.pallas_call(
        functools.partial(_ffn_up_kernel, n_side=len(side), cast=cast),
        grid_spec=grid_spec,
        out_shape=[jax.ShapeDtypeStruct((m, f), jnp.bfloat16)] + s_shape,
        compiler_params=_params(("arbitrary", "arbitrary"), 56 * MIB),
        name=name,
    )(block_e, x, w1, w3, *[arr for arr, _ in side])
    return out if side else out[0]


def _ffn_down_dense_kernel(h_ref, w_ref, x_ref, g_ref, xo_ref, ho_ref, acc_ref, *, n_k):
    k = pl.program_id(1)

    @pl.when(k == 0)
    def _():
        acc_ref[...] = x_ref[...]

    acc_ref[...] += jnp.dot(h_ref[...], w_ref[...], preferred_element_type=jnp.float32)

    @pl.when(k == n_k - 1)
    def _():
        x = acc_ref[...]
        xo_ref[...] = x
        ho_ref[...] = _rms(x, g_ref[...]).astype(ho_ref.dtype)


def _ffn_down_dense(h, w2, x, g_next, bk):
    n_k = D_FF // bk
    row = pl.BlockSpec((ROW_TILE, D_MODEL), lambda i, k: (i, 0))
    return pl.pallas_call(
        functools.partial(_ffn_down_dense_kernel, n_k=n_k),
        grid=(ROWS // ROW_TILE, n_k),
        in_specs=[pl.BlockSpec((ROW_TILE, bk), lambda i, k: (i, k)),
                  pl.BlockSpec((bk, D_MODEL), lambda i, k: (k, 0)),
                  row, pl.BlockSpec((1, D_MODEL), lambda i, k: (0, 0))],
        out_specs=[row, row],
        out_shape=[jax.ShapeDtypeStruct((ROWS, D_MODEL), jnp.float32),
                   jax.ShapeDtypeStruct((ROWS, D_MODEL), jnp.bfloat16)],
        scratch_shapes=[pltpu.VMEM((ROW_TILE, D_MODEL), jnp.float32)],
        compiler_params=_params(("parallel", "arbitrary"), 58 * MIB),
        name="ffn_down",
    )(h, w2, x, g_next.reshape(1, D_MODEL))


def _moe_down_kernel(be_ref, h_ref, w_ref, g_ref, y_ref):
    y = jnp.dot(h_ref[...], w_ref[...], preferred_element_type=jnp.float32)
    _store_chunk_rows(y_ref, y * g_ref[...])


def _moe_down(block_e, h, w2, row_gate):
    grid_spec = pltpu.PrefetchScalarGridSpec(
        num_scalar_prefetch=1,
        grid=(MOE_BLOCKS,),
        in_specs=[pl.BlockSpec((MOE_BLOCK, D_FF_EXPERT), lambda i, be: (i, 0)),
                  pl.BlockSpec((None, D_FF_EXPERT, D_MODEL), lambda i, be: (be[i], 0, 0)),
                  pl.BlockSpec((MOE_BLOCK, 1), lambda i, be: (i, 0))],
        out_specs=pl.BlockSpec((MOE_BLOCK * ROW_CHUNKS, V7X_LANES), lambda i, be: (i, 0)),
    )
    return pl.pallas_call(
        _moe_down_kernel,
        grid_spec=grid_spec,
        out_shape=jax.ShapeDtypeStruct((MOE_ROWS * ROW_CHUNKS, V7X_LANES), jnp.float32),
        compiler_params=_params(("arbitrary",), 48 * MIB),
        name="moe_down",
    )(block_e, h, w2, row_gate)


def _router_kernel(x_ref, g_ref, w_ref, hs_ref, r_ref):
    hs = _rms(x_ref[...], g_ref[...])
    _store_chunk_rows(hs_ref, hs)
    logits = jnp.dot(hs, w_ref[...], preferred_element_type=jnp.float32, precision=lax.Precision.HIGHEST)
    lane = lax.broadcasted_iota(jnp.int32, logits.shape, 1)
    logits = jnp.where(lane < N_EXPERTS, logits, -jnp.inf)
    m1 = jnp.max(logits, axis=-1, keepdims=True)
    i1 = jnp.min(jnp.where(logits == m1, lane, ROUTER_PAD), axis=-1, keepdims=True)
    rest = jnp.where(lane == i1, -jnp.inf, logits)
    m2 = jnp.max(rest, axis=-1, keepdims=True)
    i2 = jnp.min(jnp.where(rest == m2, lane, ROUTER_PAD), axis=-1, keepdims=True)
    e2 = jnp.exp(m2 - m1)
    g1 = 1.0 / (1.0 + e2)
    g2 = e2 / (1.0 + e2)
    out = jnp.where(lane == 0, i1.astype(jnp.float32),
                    jnp.where(lane == 1, i2.astype(jnp.float32),
                              jnp.where(lane == 2, g1, jnp.where(lane == 3, g2, 0.0))))
    r_ref[...] = out


def _router(x, g, w_router):
    w = jnp.zeros((D_MODEL, ROUTER_PAD), jnp.float32).at[:, :N_EXPERTS].set(w_router)
    row = pl.BlockSpec((ROW_TILE, D_MODEL), lambda i: (i, 0))
    return pl.pallas_call(
        _router_kernel,
        grid=(ROWS // ROW_TILE,),
        in_specs=[row, pl.BlockSpec((1, D_MODEL), lambda i: (0, 0)),
                  pl.BlockSpec((D_MODEL, ROUTER_PAD), lambda i: (0, 0))],
        out_specs=[pl.BlockSpec((ROW_TILE * ROW_CHUNKS, V7X_LANES), lambda i: (i, 0)),
                   pl.BlockSpec((ROW_TILE, ROUTER_PAD), lambda i: (i, 0))],
        out_shape=[jax.ShapeDtypeStruct((ROWS * ROW_CHUNKS, V7X_LANES), jnp.float32),
                   jax.ShapeDtypeStruct((ROWS, ROUTER_PAD), jnp.float32)],
        compiler_params=_params(("parallel",), 32 * MIB),
        name="router",
    )(x, g.reshape(1, D_MODEL), w)


def _gather_step(idx_ref, src_ref, buf, sem, rows):
    i = pl.program_id(0)
    slot = i % 2

    def copy(step, slot_, r):
        return pltpu.make_async_copy(src_ref.at[idx_ref[step * rows + r]],
                                     buf.at[slot_, pl.ds(r * ROW_CHUNKS, ROW_CHUNKS), :], sem.at[slot_])

    def start_all(step, slot_):
        def body(r, c):
            copy(step, slot_, r).start()
            return c
        lax.fori_loop(0, rows, body, 0, unroll=8)

    @pl.when(i == 0)
    def _():
        start_all(0, 0)

    @pl.when(i + 1 < pl.num_programs(0))
    def _():
        start_all(i + 1, 1 - slot)

    def wait_body(r, c):
        copy(i, slot, r).wait()
        return c
    lax.fori_loop(0, rows, wait_body, 0, unroll=8)
    return slot


def _dispatch_kernel(tok_ref, hs_ref, o_ref, buf, sem):
    slot = _gather_step(tok_ref, hs_ref, buf, sem, MOE_BLOCK)
    o_ref[...] = _load_chunk_rows(buf.at[slot], 0, MOE_BLOCK).astype(o_ref.dtype)


def _dispatch(row_tok, hs):
    grid_spec = pltpu.PrefetchScalarGridSpec(
        num_scalar_prefetch=1,
        grid=(MOE_BLOCKS,),
        in_specs=[pl.BlockSpec(memory_space=pl.ANY)],
        out_specs=pl.BlockSpec((MOE_BLOCK, D_MODEL), lambda i, tok: (i, 0)),
        scratch_shapes=[pltpu.VMEM((2, MOE_BLOCK * ROW_CHUNKS, V7X_LANES), jnp.float32),
                        pltpu.SemaphoreType.DMA((2,))],
    )
    return pl.pallas_call(
        _dispatch_kernel,
        grid_spec=grid_spec,
        out_shape=jax.ShapeDtypeStruct((MOE_ROWS, D_MODEL), jnp.bfloat16),
        compiler_params=_params(("arbitrary",), 16 * MIB),
        name="moe_dispatch",
    )(row_tok, hs)


def _combine_kernel(pos_ref, x_ref, y_ref, g_ref, op_ref, os_ref, buf, sem):
    slot = _gather_step(pos_ref, y_ref, buf, sem, TOP_K * MIX_TILE)
    y = sum(_load_chunk_rows(buf.at[slot], k * MIX_TILE, MIX_TILE) for k in range(TOP_K))
    out = _rms(x_ref[...] + y, g_ref[...])
    is_p = pl.program_id(0) < P_ROWS // MIX_TILE

    @pl.when(is_p)
    def _():
        op_ref[...] = out

    @pl.when(jnp.logical_not(is_p))
    def _():
        os_ref[...] = out


def _combine(pos, x, yb, g_final):
    rows = TOP_K * MIX_TILE
    n_p = P_ROWS // MIX_TILE
    grid_spec = pltpu.PrefetchScalarGridSpec(
        num_scalar_prefetch=1,
        grid=(ROWS // MIX_TILE,),
        in_specs=[pl.BlockSpec((MIX_TILE, D_MODEL), lambda i, pos: (i, 0)),
                  pl.BlockSpec(memory_space=pl.ANY),
                  pl.BlockSpec((1, D_MODEL), lambda i, pos: (0, 0))],
        out_specs=[pl.BlockSpec((MIX_TILE, D_MODEL), lambda i, pos: (jnp.minimum(i, n_p - 1), 0)),
                   pl.BlockSpec((MIX_TILE, D_MODEL), lambda i, pos: (jnp.maximum(i - n_p, 0), 0))],
        scratch_shapes=[pltpu.VMEM((2, rows * ROW_CHUNKS, V7X_LANES), jnp.float32),
                        pltpu.SemaphoreType.DMA((2,))],
    )
    return pl.pallas_call(
        _combine_kernel,
        grid_spec=grid_spec,
        out_shape=[jax.ShapeDtypeStruct((P_ROWS, D_MODEL), jnp.float32),
                   jax.ShapeDtypeStruct((S_ROWS, D_MODEL), jnp.float32)],
        compiler_params=_params(("arbitrary",), 32 * MIB),
        name="moe_combine",
    )(pos, x, yb, g_final.reshape(1, D_MODEL))


def _dispatch_plan(route):
    top_idx = route[:, 0:TOP_K].astype(jnp.int32)
    gates = route[:, TOP_K:2 * TOP_K]
    flat_e = top_idx.reshape(N_ASSIGN)
    order = jnp.argsort(flat_e).astype(jnp.int32)
    rank = jnp.argsort(order).astype(jnp.int32)
    counts = jnp.sum((flat_e[:, None] == jnp.arange(N_EXPERTS)[None, :]).astype(jnp.int32), axis=0)
    padded = (counts + MOE_BLOCK - 1) // MOE_BLOCK * MOE_BLOCK
    start = jnp.cumsum(counts) - counts
    ends_p = jnp.cumsum(padded)
    pstart = ends_p - padded
    first_row = jnp.arange(MOE_BLOCKS, dtype=jnp.int32) * MOE_BLOCK
    block_e = jnp.minimum(jnp.sum((ends_p[None, :] <= first_row[:, None]).astype(jnp.int32), axis=1),
                          N_EXPERTS - 1).astype(jnp.int32)
    row = jnp.arange(MOE_ROWS, dtype=jnp.int32)
    row_e = jnp.repeat(block_e, MOE_BLOCK)
    off = row - pstart[row_e]
    real = off < counts[row_e]
    assign = order[jnp.clip(start[row_e] + off, 0, N_ASSIGN - 1)]
    row_tok = jnp.where(real, assign // TOP_K, 0).astype(jnp.int32)
    row_gate = jnp.where(real, gates.reshape(N_ASSIGN)[assign], 0.0)
    pos = (pstart[flat_e] + rank - start[flat_e]).astype(jnp.int32)
    pos = jnp.transpose(pos.reshape(ROWS // MIX_TILE, MIX_TILE, TOP_K), (0, 2, 1)).reshape(N_ASSIGN)
    return row_tok, row_gate.reshape(MOE_ROWS, 1), block_e, pos


def kernel(x_prompt, x_sample, cache_kv_w128, cache_kv_w512, cache_kv_w2048, state_conv, state_ssm, norm_mix, w_in, gate_bias, conv_w, ssm_lam_re, ssm_lam_im, ssm_log_dt, ssm_b_re, ssm_b_im, ssm_c_re, ssm_c_im, ssm_d, ssm_w_glu, ssm_b_glu, proj_conv, proj_attn, proj_ssm, w_out, norm_ffn, ffn_w1, ffn_w3, ffn_w2, router_w, moe_w1, moe_w3, moe_w2, norm_final):
    bf = jnp.bfloat16
    caches = tuple(jnp.transpose(c, (0, 1, 3, 4, 5, 2)) for c in (cache_kv_w128, cache_kv_w512, cache_kv_w2048))
    assert W_COL_QK == IN_TILE == QK_WIDTH
    rope = _rope_tables()
    x, h = _norm(x_prompt.reshape(P_ROWS, D_MODEL),
                 jnp.transpose(x_sample, (1, 0, 2)).reshape(S_ROWS, D_MODEL), norm_mix[0])
    kv_p = [[] for _ in range(N_ATTN_GROUPS)]
    kv_s = [[] for _ in range(N_ATTN_GROUPS)]
    conv_p_all, conv_s_all, ssm_p_all, ssm_s_all = [], [], [], []
    y = None
    for l in range(DEPTH):
        moe_layer = l % 2 == 1
        qk = _linear(h, w_in, l, W_COL_QK, QK_WIDTH, jnp.float32, IN_TILE, f"qk_proj_{l}", rope=rope)
        if moe_layer:
            za = _linear(h, w_in, l, 0, ZA_WIDTH, jnp.float32, IN_TILE, f"in_proj_{l}", col_step=2)
            w2_flat = moe_w2[l // 2].reshape(N_EXPERTS * D_FF_EXPERT, D_MODEL)
            gates, moe_w2_bf = _linear(h, w_in, l, W_COL_GATES, GATE_WIDTH, bf, IN_TILE, f"gate_proj_{l}",
                                       side=[(w2_flat, SIDE_CAST_ROWS_W2)])
        else:
            za, ffn_w2_bf = _linear(h, w_in, l, 0, ZA_WIDTH, jnp.float32, IN_TILE, f"in_proj_{l}", col_step=2,
                                    side=[(ffn_w2[l // 2], SIDE_CAST_ROWS_FFN)])
            w3_flat = moe_w3[(l + 1) // 2].reshape(N_EXPERTS * D_MODEL, D_FF_EXPERT)
            gates, moe_w3_bf = _linear(h, w_in, l, W_COL_GATES, GATE_WIDTH, bf, IN_TILE, f"gate_proj_{l}",
                                       side=[(w3_flat, SIDE_CAST_ROWS_W3)])
        conv_p, cst_p = _conv_prompt(za, conv_w[l])
        conv_s, cst_s = _conv_sample(za, conv_w[l], jnp.transpose(state_conv[l], (1, 0, 2)))
        by_seq = lambda a: jnp.transpose(a.reshape(DEC_SEQ, DEC_BATCH, a.shape[-1]), (1, 0, 2))
        qk_s = by_seq(qk[P_ROWS:])
        v_s = by_seq(za[P_ROWS:, OFF_V:OFF_V + ATTN_QKV])
        attn_p, attn_s = [], []
        for g in range(N_ATTN_GROUPS):
            attn_p += list(_attn_prompt(qk, za, g))
            attn_s += list(_attn_sample(qk_s, v_s, caches[g], l, g))
        tabs = _ssm_tables(ssm_lam_re[l], ssm_lam_im[l], ssm_log_dt[l], ssm_b_re[l], ssm_b_im[l],
                           ssm_c_re[l], ssm_c_im[l], ssm_d[l])
        ssm_p, ssm_s, sst_p, sst_s = _ssm(za, state_ssm[l], tabs)
        moe_layer = l % 2 == 1
        lp = dict(gate_bias=gate_bias[l].reshape(1, GATE_WIDTH), w_glu=ssm_w_glu[l].astype(bf),
                  b_glu=ssm_b_glu[l].reshape(1, SSM_WIDTH), proj_conv=proj_conv[l].astype(bf),
                  proj_attn=proj_attn[l].astype(bf), proj_ssm=proj_ssm[l].astype(bf),
                  w_out=w_out[l].astype(bf), norm_next=norm_ffn[l].reshape(1, D_MODEL))
        x, h = _mix(x, gates, conv_p, conv_s, attn_p, attn_s, ssm_p, ssm_s, lp, bf)
        i = l // 2
        if not moe_layer:
            zero_e = jnp.zeros((ROWS // ROW_TILE,), jnp.int32)
            w1_flat = moe_w1[(l + 1) // 2].reshape(N_EXPERTS * D_MODEL, D_FF_EXPERT)
            hid, moe_w1_bf = _ffn_up(zero_e, h, ffn_w1[i:i + 1], ffn_w3[i:i + 1], ROW_TILE, 512, "ffn_up",
                                     side=[(w1_flat, SIDE_CAST_ROWS_W1)])
            x, h = _ffn_down_dense(hid, ffn_w2_bf, x, norm_mix[l + 1], D_FF // 2)
        else:
            hs, route = _router(x, norm_ffn[l], router_w[i])
            row_tok, row_gate, block_e, pos = _dispatch_plan(route)
            xs = _dispatch(row_tok, hs.reshape(ROWS, ROW_CHUNKS, V7X_LANES))
            expert = lambda w, k: w.reshape(N_EXPERTS, k, w.shape[-1])
            hid = _ffn_up(block_e, xs, expert(moe_w1_bf, D_MODEL), expert(moe_w3_bf, D_MODEL),
                          MOE_BLOCK, D_FF_EXPERT // 2, "moe_up")
            yb = _moe_down(block_e, hid, expert(moe_w2_bf, D_FF_EXPERT), row_gate)
            y = _combine(pos, x, yb.reshape(MOE_ROWS, ROW_CHUNKS, V7X_LANES), norm_final)
        for g, (kp, ks) in enumerate(zip(_kv_prompt(qk, za), _kv_sample(qk, za))):
            kp = kp.reshape(BATCH, 2, HEADS_PER_GROUP, HEAD_DIM, kp.shape[-1])
            kv_p[g].append(jnp.transpose(kp, (0, 4, 1, 2, 3)))
            ks = ks.reshape(DEC_SEQ, 2, HEADS_PER_GROUP, HEAD_DIM, DEC_BATCH)
            kv_s[g].append(jnp.transpose(ks, (4, 0, 1, 2, 3)))
        conv_p_all.append(cst_p)
        conv_s_all.append(jnp.transpose(cst_s, (1, 0, 2)))
        ssm_p_all.append(sst_p)
        ssm_s_all.append(sst_s)
    y_prompt = y[0].reshape(BATCH, SEQ, D_MODEL)
    y_sample = jnp.transpose(y[1].reshape(DEC_SEQ, DEC_BATCH, D_MODEL), (1, 0, 2))
    return (y_prompt, y_sample, jnp.stack(kv_p[0]), jnp.stack(kv_p[1]), jnp.stack(kv_p[2]),
            jnp.stack(conv_p_all), jnp.stack(ssm_p_all), jnp.stack(kv_s[0]), jnp.stack(kv_s[1]),
            jnp.stack(kv_s[2]), jnp.stack(conv_s_all), jnp.stack(ssm_s_all))
```

```python
import functools
import math

import jax
import jax.numpy as jnp
import numpy as np
from jax import lax
from jax.experimental import pallas as pl
from jax.experimental.pallas import tpu as pltpu

D_MODEL = 2048
BATCH = 4
SEQ = 2048
DEPTH = 2
DEC_BATCH = 128
DEC_SEQ = 4
PAST_LEN = 2048
CONV_WIDTH = 512
CONV_K = 3
HEAD_DIM = 64
HEADS_PER_GROUP = 4
ATTN_GROUPS = ((128, 1), (512, 4), (2048, 16))
N_ATTN_GROUPS = 3
ATTN_QKV = N_ATTN_GROUPS * HEADS_PER_GROUP * HEAD_DIM
ATTN_OUT = HEADS_PER_GROUP * HEAD_DIM
ROT_DIM = HEAD_DIM // 4
ROPE_THETA = 500000.0
SSM_WIDTH = 768
SSM_CH_PER_GROUP = 16
SSM_GROUPS = SSM_WIDTH // SSM_CH_PER_GROUP
SSM_STATE = 64
N_BRANCH = 3
D_FF = 5632
N_EXPERTS = 8
TOP_K = 2
D_FF_EXPERT = D_FF // TOP_K
EPS = 1e-6

V7X_LANES = 128
V7X_VMEM_BYTES = 64 * 1024 * 1024
MIB = 1024 * 1024

P_ROWS = BATCH * SEQ
S_ROWS = DEC_SEQ * DEC_BATCH
ROWS = P_ROWS + S_ROWS
W_COL_QK = 3 * CONV_WIDTH
W_COL_GATES = 3 * CONV_WIDTH + 3 * ATTN_QKV + SSM_WIDTH
QK_WIDTH = 2 * ATTN_QKV
ZA_WIDTH = 3 * CONV_WIDTH + ATTN_QKV + SSM_WIDTH
GATE_WIDTH = N_BRANCH * D_MODEL
OFF_Q = 0
OFF_K = ATTN_QKV
OFF_V = 3 * CONV_WIDTH
OFF_U = OFF_V + ATTN_QKV
BAND = 128
ATTN_DENSE_BLOCKS = 4
ATTN_RESIDUE_UNROLL = 4
SSM_CHUNK = 16
SSM_GB = 16
N_GB = SSM_GROUPS // SSM_GB
GB_CH = SSM_GB * SSM_CH_PER_GROUP
GB_ST = SSM_GB * SSM_STATE
ST_WIDTH = N_GB * 2 * GB_ST
ROW_TILE = 512
MIX_TILE = 256
IN_TILE = 1536
MOE_BLOCK = 256
N_ASSIGN = ROWS * TOP_K
MOE_BLOCKS = -(-(N_ASSIGN + N_EXPERTS * (MOE_BLOCK - 1)) // MOE_BLOCK)
MOE_ROWS = MOE_BLOCKS * MOE_BLOCK
ROUTER_PAD = V7X_LANES
SAMPLE_SEQ_TILE = 8
SAMPLE_CACHE_TILE_BYTES = 8 * MIB
SSM_BATCH_SPLIT = 1
HALVES = ATTN_OUT // V7X_LANES
ROW_CHUNKS = D_MODEL // V7X_LANES
SIDE_CAST_ROWS_W1 = 128
SIDE_CAST_ROWS_W3 = 256
SIDE_CAST_ROWS_W2 = 352
SIDE_CAST_ROWS_FFN = 176
assert DEPTH == 2 and HALVES == 2 and GB_CH == HALVES * V7X_LANES

assert ROWS % ROW_TILE == 0 and P_ROWS % ROW_TILE == 0
assert ROWS % MIX_TILE == 0 and P_ROWS % MIX_TILE == 0


def _params(semantics, vmem_bytes):
    return pltpu.CompilerParams(dimension_semantics=semantics,
                                vmem_limit_bytes=min(int(vmem_bytes), V7X_VMEM_BYTES - 4 * MIB))


def _rms(x, g):
    y = x * lax.rsqrt(jnp.mean(x * x, axis=-1, keepdims=True) + EPS)
    return y * g


def _bdot(a, b):
    return jnp.dot(a.astype(jnp.bfloat16), b.astype(jnp.bfloat16), preferred_element_type=jnp.float32)


def _store_chunk_rows(ref, x):
    n = x.shape[0]
    for k in range(ROW_CHUNKS):
        ref[pl.ds(k, n, stride=ROW_CHUNKS), :] = x[:, k * V7X_LANES:(k + 1) * V7X_LANES]


def _load_chunk_rows(ref, first, n):
    return jnp.concatenate([ref[pl.ds(first * ROW_CHUNKS + k, n, stride=ROW_CHUNKS), :]
                            for k in range(ROW_CHUNKS)], axis=-1)


def _sigmoid(x):
    return 0.5 * jnp.tanh(0.5 * x) + 0.5


def _norm_kernel(xp_ref, xs_ref, g_ref, x_ref, h_ref):
    x = jnp.where(pl.program_id(0) < P_ROWS // ROW_TILE, xp_ref[...], xs_ref[...])
    x_ref[...] = x
    h_ref[...] = _rms(x, g_ref[...]).astype(h_ref.dtype)


def _norm(x_p, x_s, g):
    n_p = P_ROWS // ROW_TILE
    row = pl.BlockSpec((ROW_TILE, D_MODEL), lambda i: (i, 0))
    return pl.pallas_call(
        _norm_kernel,
        grid=(ROWS // ROW_TILE,),
        in_specs=[pl.BlockSpec((ROW_TILE, D_MODEL), lambda i: (jnp.minimum(i, n_p - 1), 0)),
                  pl.BlockSpec((ROW_TILE, D_MODEL), lambda i: (jnp.maximum(i - n_p, 0), 0)),
                  pl.BlockSpec((1, D_MODEL), lambda i: (0, 0))],
        out_specs=[row, row],
        out_shape=[jax.ShapeDtypeStruct((ROWS, D_MODEL), jnp.float32),
                   jax.ShapeDtypeStruct((ROWS, D_MODEL), jnp.bfloat16)],
        compiler_params=_params(("parallel",), 40 * MIB),
        name="norm",
    )(x_p, x_s, g.reshape(1, D_MODEL))


def _side_cast_specs(side, n_inner, extra_args=0):
    in_specs, out_specs, out_shape = [], [], []
    for arr, rows in side:
        n_blk = arr.shape[0] // rows
        assert n_blk * rows == arr.shape[0]

        def imap(j, i, *_, n_blk=n_blk):
            return (jnp.minimum(j * n_inner + i, n_blk - 1), 0)
        spec = pl.BlockSpec((rows, arr.shape[1]), imap)
        in_specs.append(spec)
        out_specs.append(spec)
        out_shape.append(jax.ShapeDtypeStruct(arr.shape, jnp.bfloat16))
    return in_specs, out_specs, out_shape


def _side_cast(in_refs, out_refs):
    for src, dst in zip(in_refs, out_refs):
        dst[...] = src[...].astype(dst.dtype)


def _linear_kernel(*refs, n_side, rotate):
    a_ref, w_ref = refs[:2]
    n_tab = 3 if rotate else 0
    tabs = refs[2 + n_side:2 + n_side + n_tab]
    o_ref = refs[2 + n_side + n_tab]
    wb_ref = refs[-1]
    _side_cast(refs[2:2 + n_side], refs[3 + n_side + n_tab:3 + 2 * n_side + n_tab])

    @pl.when(pl.program_id(1) == 0)
    def _():
        wb_ref[...] = w_ref[...].astype(wb_ref.dtype)

    y = jnp.dot(a_ref[...], wb_ref[...], preferred_element_type=jnp.float32)
    if rotate:
        y = _rotate(y, *tabs)
    o_ref[...] = y.astype(o_ref.dtype)


def _linear(a, w, layer, col0, n, out_dtype, bn, name, side=(), col_step=1, rope=None):
    m, k = a.shape
    jb = col0 // bn
    grid = (n // bn, m // ROW_TILE)
    assert all(arr.shape[0] // rows <= grid[0] * grid[1] for arr, rows in side)
    s_in, s_out, s_shape = _side_cast_specs(side, grid[1])
    tables = rope if rope else ()
    assert rope is None or bn == QK_WIDTH
    out = pl.pallas_call(
        functools.partial(_linear_kernel, n_side=len(side), rotate=rope is not None),
        grid=grid,
        in_specs=[pl.BlockSpec((ROW_TILE, k), lambda j, i: (i, 0)),
                  pl.BlockSpec((None, k, bn), lambda j, i: (layer, 0, jb + j * col_step))] + s_in
                 + (_rope_table_specs() if rope else []),
        out_specs=[pl.BlockSpec((ROW_TILE, bn), lambda j, i: (i, j))] + s_out,
        out_shape=[jax.ShapeDtypeStruct((m, n), out_dtype)] + s_shape,
        scratch_shapes=[pltpu.VMEM((k, bn), jnp.bfloat16)],
        compiler_params=_params(("arbitrary", "arbitrary"), 56 * MIB),
        name=name,
    )(a, w, *[arr for arr, _ in side], *tables)
    return out if side else out[0]


def _rope_tables():
    half = ROT_DIM // 2
    inv = ROPE_THETA ** (-jnp.arange(half, dtype=jnp.float32) * 2.0 / ROT_DIM)
    pos_p = jnp.arange(SEQ, dtype=jnp.int32)
    pos_s = jnp.repeat(PAST_LEN + jnp.arange(DEC_SEQ, dtype=jnp.int32), DEC_BATCH)
    pos = jnp.concatenate([pos_p, pos_s]).astype(jnp.float32)
    ang = pos[:, None] * inv[None, :]
    cos, sin = jnp.cos(ang), jnp.sin(ang)
    ones = jnp.ones((pos.shape[0], HEAD_DIM - ROT_DIM), jnp.float32)
    zeros = jnp.zeros_like(ones)
    zh = jnp.zeros_like(sin)
    cos_t = jnp.concatenate([cos, cos, ones], axis=1)
    sin_hi = jnp.concatenate([-sin, zh, zeros], axis=1)
    sin_lo = jnp.concatenate([zh, sin, zeros], axis=1)
    reps = V7X_LANES // HEAD_DIM
    return tuple(jnp.tile(t, (1, reps)) for t in (cos_t, sin_hi, sin_lo))


def _rotate(x, c_ref, sh_ref, sl_ref):
    half = ROT_DIM // 2
    reps = QK_WIDTH // V7X_LANES
    c = jnp.tile(c_ref[...], (1, reps))
    sh = jnp.tile(sh_ref[...], (1, reps))
    sl = jnp.tile(sl_ref[...], (1, reps))
    x_up = pltpu.roll(x, QK_WIDTH - half, axis=1)
    x_dn = pltpu.roll(x, half, axis=1)
    return x * c + x_up * sh + x_dn * sl


def _rope_table_specs():
    n_p = P_ROWS // ROW_TILE
    per_seq = SEQ // ROW_TILE
    tmap = lambda j, i: (jnp.where(i < n_p, i % per_seq, per_seq), 0)
    return [pl.BlockSpec((ROW_TILE, V7X_LANES), tmap)] * 3


def _kv_prompt_kernel(k_ref, v_ref, *o_refs, keeps):
    last = pl.program_id(1) == SEQ // ROW_TILE - 1
    k, v = k_ref[...], v_ref[...]

    def write(g, lo):
        cols = slice(g * ATTN_OUT, (g + 1) * ATTN_OUT)
        o_refs[g][0] = k[lo:, cols].T
        o_refs[g][1] = v[lo:, cols].T

    for g, keep in enumerate(keeps):
        if keep == SEQ:
            write(g, 0)
        else:
            pl.when(last)(functools.partial(write, g, ROW_TILE - min(keep, ROW_TILE)))


def _kv_prompt(qk, za):
    keeps = tuple(min(w, SEQ) for w, _ in ATTN_GROUPS)
    assert all(k == SEQ or k <= ROW_TILE for k in keeps)
    n_t = SEQ // ROW_TILE
    out_specs, out_shape = [], []
    for keep in keeps:
        blk = min(keep, ROW_TILE)
        imap = (lambda b, j: (b, 0, 0, j)) if keep == SEQ else (lambda b, j: (b, 0, 0, 0))
        out_specs.append(pl.BlockSpec((None, 2, ATTN_OUT, blk), imap))
        out_shape.append(jax.ShapeDtypeStruct((BATCH, 2, ATTN_OUT, keep), jnp.float32))
    return pl.pallas_call(
        functools.partial(_kv_prompt_kernel, keeps=keeps),
        grid=(BATCH, n_t),
        in_specs=[pl.BlockSpec((ROW_TILE, ATTN_QKV), lambda b, j: (b * n_t + j, OFF_K // ATTN_QKV)),
                  pl.BlockSpec((ROW_TILE, ATTN_QKV), lambda b, j: (b * n_t + j, OFF_V // ATTN_QKV))],
        out_specs=out_specs,
        out_shape=out_shape,
        compiler_params=_params(("parallel", "arbitrary"), 32 * MIB),
        name="kv_prompt",
    )(qk, za)


def _kv_sample_kernel(k_ref, v_ref, *o_refs):
    for t in range(DEC_SEQ):
        rows = slice(t * DEC_BATCH, (t + 1) * DEC_BATCH)
        for g in range(N_ATTN_GROUPS):
            cols = slice(g * ATTN_OUT, (g + 1) * ATTN_OUT)
            o_refs[g][t, 0] = k_ref[rows, cols].T
            o_refs[g][t, 1] = v_ref[rows, cols].T


def _kv_sample(qk, za):
    blk = P_ROWS // S_ROWS
    shape = (DEC_SEQ, 2, ATTN_OUT, DEC_BATCH)
    return pl.pallas_call(
        _kv_sample_kernel,
        grid=(1,),
        in_specs=[pl.BlockSpec((S_ROWS, ATTN_QKV), lambda i: (blk, OFF_K // ATTN_QKV)),
                  pl.BlockSpec((S_ROWS, ATTN_QKV), lambda i: (blk, OFF_V // ATTN_QKV))],
        out_specs=[pl.BlockSpec(shape, lambda i: (0, 0, 0, 0))] * N_ATTN_GROUPS,
        out_shape=[jax.ShapeDtypeStruct(shape, jnp.float32)] * N_ATTN_GROUPS,
        compiler_params=_params(("arbitrary",), 32 * MIB),
        name="kv_sample",
    )(qk, za)


def _conv_prompt_kernel(h_ref, b_ref, c_ref, w_ref, y_ref, st_ref):
    u = c_ref[...] * h_ref[...]
    w = w_ref[...]
    row = lax.broadcasted_iota(jnp.int32, u.shape, 0)
    u1 = jnp.where(row >= 1, pltpu.roll(u, 1, axis=0), 0.0)
    u2 = jnp.where(row >= 2, pltpu.roll(u, 2, axis=0), 0.0)
    y = w[0:1] * u2 + w[1:2] * u1 + w[2:3] * u
    y_ref[...] = (b_ref[...] * y).astype(y_ref.dtype)
    st_ref[...] = u[SEQ - (CONV_K - 1):, :]


def _conv_prompt(za, w):
    col = lambda c: pl.BlockSpec((SEQ, CONV_WIDTH), lambda b, c=c: (b, c))
    return pl.pallas_call(
        _conv_prompt_kernel,
        grid=(BATCH,),
        in_specs=[col(0), col(1), col(2), pl.BlockSpec((CONV_K, CONV_WIDTH), lambda b: (0, 0))],
        out_specs=[pl.BlockSpec((SEQ, CONV_WIDTH), lambda b: (b, 0)),
                   pl.BlockSpec((None, CONV_K - 1, CONV_WIDTH), lambda b: (b, 0, 0))],
        out_shape=[jax.ShapeDtypeStruct((P_ROWS, CONV_WIDTH), jnp.bfloat16),
                   jax.ShapeDtypeStruct((BATCH, CONV_K - 1, CONV_WIDTH), jnp.float32)],
        compiler_params=_params(("parallel",), 48 * MIB),
        name="conv_prompt",
    )(za, za, za, w)


def _conv_sample_kernel(h_ref, b_ref, c_ref, w_ref, s_ref, y_ref, st_ref):
    u = c_ref[...] * h_ref[...]
    w = w_ref[...]
    s0, s1 = s_ref[0], s_ref[1]
    keep = S_ROWS - DEC_BATCH
    u1 = jnp.concatenate([s1, u[:keep]], axis=0)
    u2 = jnp.concatenate([s0, s1, u[:keep - DEC_BATCH]], axis=0)
    y = w[0:1] * u2 + w[1:2] * u1 + w[2:3] * u
    y_ref[...] = (b_ref[...] * y).astype(y_ref.dtype)
    st_ref[0] = u[S_ROWS - 2 * DEC_BATCH:S_ROWS - DEC_BATCH]
    st_ref[1] = u[S_ROWS - DEC_BATCH:]


def _conv_sample(za, w, state):
    blk = P_ROWS // S_ROWS
    col = lambda c: pl.BlockSpec((S_ROWS, CONV_WIDTH), lambda i, c=c: (blk, c))
    st_spec = pl.BlockSpec((CONV_K - 1, DEC_BATCH, CONV_WIDTH), lambda i: (0, 0, 0))
    return pl.pallas_call(
        _conv_sample_kernel,
        grid=(1,),
        in_specs=[col(0), col(1), col(2), pl.BlockSpec((CONV_K, CONV_WIDTH), lambda i: (0, 0)), st_spec],
        out_specs=[pl.BlockSpec((S_ROWS, CONV_WIDTH), lambda i: (0, 0)), st_spec],
        out_shape=[jax.ShapeDtypeStruct((S_ROWS, CONV_WIDTH), jnp.bfloat16),
                   jax.ShapeDtypeStruct((CONV_K - 1, DEC_BATCH, CONV_WIDTH), jnp.float32)],
        compiler_params=_params(("arbitrary",), 24 * MIB),
        name="conv_sample",
    )(za, za, za, w, state)


def _attn_prompt_kernel(*refs, dil, has_prev, sub):
    n_in = (5 if has_prev else 3) * HALVES
    q_refs, kc_refs, vc_refs = (refs[i * HALVES:(i + 1) * HALVES] for i in range(3))
    if has_prev:
        kp_refs, vp_refs = (refs[i * HALVES:(i + 1) * HALVES] for i in range(3, 5))
    o_ref, l_ref = refs[n_in:]
    n_keys = (2 if has_prev else 1) * BAND
    qi = lax.broadcasted_iota(jnp.int32, (BAND, n_keys), 0)
    kj = lax.broadcasted_iota(jnp.int32, (BAND, n_keys), 1)
    if has_prev:
        dist = qi + BAND - kj
        inner_ok = (dist >= 0) & (dist <= BAND)
        first_ok = inner_ok & ((kj >= BAND) | (pl.program_id(1) > 0))
    else:
        first_ok = kj <= qi
    heads_per_half = V7X_LANES // HEAD_DIM

    def attend(q, k, v, valid):
        q, k, v = (t.astype(jnp.bfloat16) for t in (q, k, v))
        outs, lses = [], []
        for h in range(heads_per_half):
            sl = slice(h * HEAD_DIM, (h + 1) * HEAD_DIM)
            s = lax.dot_general(q[:, sl], k[:, sl], (((1,), (1,)), ((), ())),
                                preferred_element_type=jnp.float32) * (HEAD_DIM ** -0.5)
            s = jnp.where(valid, s, -jnp.inf)
            m = jnp.max(s, axis=-1, keepdims=True)
            p = jnp.exp(s - m)
            den = jnp.sum(p, axis=-1, keepdims=True)
            outs.append(jnp.dot(p.astype(jnp.bfloat16), v[:, sl], preferred_element_type=jnp.float32) / den)
            lses.append(jnp.broadcast_to(m + jnp.log(den), (BAND, HEAD_DIM)))
        return jnp.concatenate(outs, axis=-1), jnp.concatenate(lses, axis=-1)

    if dil == 1:
        for half in range(HALVES):
            q, kc, vc = q_refs[half][...], kc_refs[half][...], vc_refs[half][...]
            kp, vp = kp_refs[half][...], vp_refs[half][...]
            for s in range(sub):
                lo, hi = (s - 1) * BAND, (s + 1) * BAND
                k = jnp.concatenate([kp, kc[:BAND]], axis=0) if s == 0 else kc[lo:hi]
                v = jnp.concatenate([vp, vc[:BAND]], axis=0) if s == 0 else vc[lo:hi]
                o, l = attend(q[s * BAND:hi], k, v, first_ok if s == 0 else inner_ok)
                o_ref[half, s * BAND:hi, :] = o
                l_ref[half, s * BAND:hi, :] = l
        return

    def residue(r, carry):
        rows = pl.ds(r, BAND, stride=dil)
        for half in range(HALVES):
            k = kc_refs[half][rows, :]
            v = vc_refs[half][rows, :]
            if has_prev:
                k = jnp.concatenate([kp_refs[half][rows, :], k], axis=0)
                v = jnp.concatenate([vp_refs[half][rows, :], v], axis=0)
            o, l = attend(q_refs[half][rows, :], k, v, first_ok)
            o_ref[half, rows, :] = o
            l_ref[half, rows, :] = l
        return carry

    lax.fori_loop(0, dil, residue, 0, unroll=min(dil, ATTN_RESIDUE_UNROLL))


def _attn_prompt(qk, za, g):
    d = ATTN_GROUPS[g][1]
    sub = ATTN_DENSE_BLOCKS if d == 1 else 1
    rows = BAND * d * sub
    nb = SEQ // rows
    has_prev = nb > 1
    prev_rows = BAND if d == 1 else rows
    qcol, kcol, vcol = (off // V7X_LANES + g * HALVES for off in (OFF_Q, OFF_K, OFF_V))
    cur = lambda b, j: b * nb + j
    prev = lambda b, j: b * (SEQ // prev_rows) + jnp.maximum(j * (rows // prev_rows) - 1, 0)
    spec = lambda n, row, col: [pl.BlockSpec((n, V7X_LANES), lambda b, j, h=h: (row(b, j), col + h))
                                for h in range(HALVES)]
    in_specs = spec(rows, cur, qcol) + spec(rows, cur, kcol) + spec(rows, cur, vcol)
    args = [qk] * (2 * HALVES) + [za] * HALVES
    if has_prev:
        in_specs += spec(prev_rows, prev, kcol) + spec(prev_rows, prev, vcol)
        args += [qk] * HALVES + [za] * HALVES
    out_spec = pl.BlockSpec((HALVES, rows, V7X_LANES), lambda b, j: (0, cur(b, j), 0))
    shape = jax.ShapeDtypeStruct((HALVES, P_ROWS, V7X_LANES), jnp.float32)
    return pl.pallas_call(
        functools.partial(_attn_prompt_kernel, dil=d, has_prev=has_prev, sub=sub),
        grid=(BATCH, nb),
        in_specs=in_specs,
        out_specs=[out_spec, out_spec],
        out_shape=[shape, shape],
        compiler_params=_params(("parallel", "parallel"), 40 * MIB),
        name=f"attn_prompt_g{g}",
    )(*args)


def _attn_sample_kernel(q_ref, kn_ref, vn_ref, c_ref, o_ref, l_ref, *, dil, nseq):
    width = c_ref.shape[-1]
    scale = HEAD_DIM ** -0.5
    nt = (((1,), (1,)), ((), ()))
    n_rows = HEADS_PER_GROUP * DEC_SEQ
    row_l = lax.broadcasted_iota(jnp.int32, (n_rows, ATTN_OUT), 0)
    lane = lax.broadcasted_iota(jnp.int32, (n_rows, ATTN_OUT), 1)
    own_head = lane // HEAD_DIM == row_l // DEC_SEQ
    t_c = lax.broadcasted_iota(jnp.int32, (n_rows, width), 0) % DEC_SEQ
    w_c = lax.broadcasted_iota(jnp.int32, (n_rows, width), 1)
    ok_c = (w_c >= t_c) & (w_c % dil == t_c % dil)
    r_n = lax.broadcasted_iota(jnp.int32, (n_rows, n_rows), 0)
    c_n = lax.broadcasted_iota(jnp.int32, (n_rows, n_rows), 1)
    t_n, s_n = r_n % DEC_SEQ, c_n % DEC_SEQ
    ok_n = (r_n // DEC_SEQ == c_n // DEC_SEQ) & (s_n <= t_n) & (s_n % dil == t_n % dil)

    def per_head(x):
        return jnp.where(own_head, jnp.tile(x, (HEADS_PER_GROUP, 1)), 0.0).astype(jnp.bfloat16)

    def one_seq(s, carry):
        q, kn, vn = per_head(q_ref[s]), per_head(kn_ref[s]), per_head(vn_ref[s])
        k_t = c_ref[s, 0].reshape(ATTN_OUT, width).astype(jnp.bfloat16)
        v_t = c_ref[s, 1].reshape(ATTN_OUT, width).astype(jnp.bfloat16)
        sc = jnp.dot(q, k_t, preferred_element_type=jnp.float32) * scale
        sc = jnp.where(ok_c, sc, -jnp.inf)
        sn = lax.dot_general(q, kn, nt, preferred_element_type=jnp.float32) * scale
        sn = jnp.where(ok_n, sn, -jnp.inf)
        m = jnp.maximum(jnp.max(sc, axis=-1, keepdims=True), jnp.max(sn, axis=-1, keepdims=True))
        pc = jnp.exp(sc - m)
        pn = jnp.exp(sn - m)
        den = jnp.sum(pc, axis=-1, keepdims=True) + jnp.sum(pn, axis=-1, keepdims=True)
        o = (lax.dot_general(pc.astype(jnp.bfloat16), v_t, nt, preferred_element_type=jnp.float32)
             + jnp.dot(pn.astype(jnp.bfloat16), vn, preferred_element_type=jnp.float32))
        o = jnp.where(own_head, o / den, 0.0)
        lse = jnp.where(own_head, m + jnp.log(den), 0.0)
        fold = lambda a: sum(a[h * DEC_SEQ:(h + 1) * DEC_SEQ] for h in range(HEADS_PER_GROUP))
        o_ref[s] = fold(o)
        l_ref[s] = fold(lse)
        return carry

    lax.fori_loop(0, nseq, one_seq, 0, unroll=True)


def _attn_sample(qk_s, v_s, cache_t, layer, g):
    d = ATTN_GROUPS[g][1]
    width = cache_t.shape[-1]
    seq_bytes = 2 * ATTN_OUT * width * 4
    nseq = max(1, min(SAMPLE_SEQ_TILE, SAMPLE_CACHE_TILE_BYTES // seq_bytes))
    blk = (nseq, DEC_SEQ, ATTN_OUT)
    koff = ATTN_QKV // ATTN_OUT
    out_spec = pl.BlockSpec(blk, lambda i: (i, 0, 0))
    shape = jax.ShapeDtypeStruct((DEC_BATCH, DEC_SEQ, ATTN_OUT), jnp.float32)
    o, l = pl.pallas_call(
        functools.partial(_attn_sample_kernel, dil=d, nseq=nseq),
        grid=(DEC_BATCH // nseq,),
        in_specs=[pl.BlockSpec(blk, lambda i: (i, 0, g)),
                  pl.BlockSpec(blk, lambda i: (i, 0, koff + g)),
                  pl.BlockSpec(blk, lambda i: (i, 0, g)),
                  pl.BlockSpec((None, nseq, 2, HEADS_PER_GROUP, HEAD_DIM, width),
                               lambda i: (layer, i, 0, 0, 0, 0))],
        out_specs=[out_spec, out_spec],
        out_shape=[shape, shape],
        compiler_params=_params(("parallel",), 4 * SAMPLE_CACHE_TILE_BYTES),
        name=f"attn_sample_g{g}",
    )(qk_s, qk_s, v_s, cache_t)
    to_rows = lambda a: jnp.transpose(a, (1, 0, 2)).reshape(S_ROWS, ATTN_OUT)
    return to_rows(o), to_rows(l)


def _ssm_tables(lam_re, lam_im, log_dt, b_re, b_im, c_re, c_im, d_skip):
    dt = jnp.exp(log_dt)[:, None]
    er = jnp.exp(lam_re * dt)
    lbr, lbi = er * jnp.cos(lam_im * dt), er * jnp.sin(lam_im * dt)
    den = lam_re * lam_re + lam_im * lam_im
    nr, ni = lbr - 1.0, lbi
    qr = (nr * lam_re + ni * lam_im) / den
    qi = (ni * lam_re - nr * lam_im) / den
    bbr = qr[:, :, None] * b_re - qi[:, :, None] * b_im
    bbi = qr[:, :, None] * b_im + qi[:, :, None] * b_re
    eye = jnp.eye(SSM_GB, dtype=jnp.float32)

    def in_block(b):
        b = b.reshape(N_GB, SSM_GB, SSM_STATE, SSM_CH_PER_GROUP)
        return jnp.einsum('ngpc,gh->ngchp', b, eye).reshape(N_GB, GB_CH, GB_ST)

    def out_block(c):
        c = c.reshape(N_GB, SSM_GB, SSM_CH_PER_GROUP, SSM_STATE)
        return jnp.einsum('ngcp,gh->ngphc', c, eye).reshape(N_GB, GB_ST, GB_CH)

    b_blk = jnp.concatenate([in_block(bbr), in_block(bbi)], axis=2).astype(jnp.bfloat16)
    c_blk = jnp.concatenate([out_block(c_re), out_block(-c_im)], axis=1).astype(jnp.bfloat16)

    def pack(re, im):
        re = re.reshape(N_GB, 1, GB_ST)
        im = im.reshape(N_GB, 1, GB_ST)
        return jnp.concatenate([re, re], axis=2), jnp.concatenate([-im, im], axis=2)

    pr, pi = lbr, lbi
    for _ in range(int(math.log2(SSM_CHUNK))):
        pr, pi = pr * pr - pi * pi, 2.0 * pr * pi
    bre, bim = in_block(bbr), in_block(bbi)
    kr, ki = jnp.ones_like(lbr), jnp.zeros_like(lbi)
    steps = []
    for _ in range(SSM_CHUNK):
        wr, wi = kr.reshape(N_GB, 1, GB_ST), ki.reshape(N_GB, 1, GB_ST)
        steps.append(jnp.concatenate([bre * wr - bim * wi, bre * wi + bim * wr], axis=2))
        kr, ki = kr * lbr - ki * lbi, kr * lbi + ki * lbr
    b_chunk = jnp.concatenate(steps[::-1], axis=1).astype(jnp.bfloat16)
    return dict(b=b_blk, c=c_blk, lam=pack(lbr, lbi), lam_chunk=pack(pr, pi), d=d_skip.reshape(1, SSM_WIDTH),
                b_chunk=b_chunk)


def _cstep(s, a, bsw):
    swapped = jnp.concatenate([s[:, GB_ST:], s[:, :GB_ST]], axis=1)
    return a * s + bsw * swapped


def _ssm_kernel(u_ref, s0_ref, b_ref, c_ref, a_ref, bsw_ref, d_ref, *refs, steps, emit_y):
    if emit_y:
        y_ref, sout_ref, s_ref = refs
    else:
        sout_ref, s_ref = refs
    t = pl.program_id(1)

    @pl.when(t == 0)
    def _():
        s_ref[...] = s0_ref[...]

    u = u_ref[...]
    s = _cstep(s_ref[...], a_ref[...], bsw_ref[...]) + _bdot(u, b_ref[...])
    s_ref[...] = s
    if emit_y:
        y_ref[...] = _bdot(s, c_ref[...]) + d_ref[...] * u

    @pl.when(t == steps - 1)
    def _():
        sout_ref[...] = s


def _ssm_pass(u_arr, u_map, y_map, rows, steps, s0, tabs, y_shape, name):
    emit_y = y_shape is not None
    st_spec = pl.BlockSpec((rows, 2 * GB_ST), lambda gb, t: (0, gb))
    coef = pl.BlockSpec((None, 1, 2 * GB_ST), lambda gb, t: (gb, 0, 0))
    in_specs = [pl.BlockSpec((rows, GB_CH), u_map), st_spec,
                pl.BlockSpec((None, GB_CH, 2 * GB_ST), lambda gb, t: (gb, 0, 0)),
                pl.BlockSpec((None, 2 * GB_ST, GB_CH), lambda gb, t: (gb, 0, 0)),
                coef, coef,
                pl.BlockSpec((1, GB_CH), lambda gb, t: (0, gb))]
    st_shape = jax.ShapeDtypeStruct((rows, ST_WIDTH), jnp.float32)
    out_specs, out_shape = [st_spec], [st_shape]
    if emit_y:
        out_specs, out_shape = [pl.BlockSpec((rows, GB_CH), y_map), st_spec], [y_shape, st_shape]
    return pl.pallas_call(
        functools.partial(_ssm_kernel, steps=steps, emit_y=emit_y),
        grid=(N_GB, steps),
        in_specs=in_specs,
        out_specs=out_specs,
        out_shape=out_shape,
        scratch_shapes=[pltpu.VMEM((rows, 2 * GB_ST), jnp.float32)],
        compiler_params=_params(("parallel", "arbitrary"), 48 * MIB),
        name=name,
    )(u_arr, s0, tabs['b'], tabs['c'], tabs['lam'][0], tabs['lam'][1], tabs['d'])


def _ssm_prompt_kernel(*refs, emit_y):
    u_refs = refs[:HALVES]
    s0_ref, b_ref, c_ref, a_ref, bsw_ref, d_ref = refs[HALVES:HALVES + 6]
    if emit_y:
        y_ref, sout_ref, s_ref = refs[HALVES + 6:]
    else:
        sout_ref, s_ref = refs[HALVES + 6:]
    t = pl.program_id(2)
    rows = pl.ds(t, s_ref.shape[0], stride=SSM_CHUNK)

    @pl.when(t == 0)
    def _():
        s_ref[...] = s0_ref[...]

    u = jnp.concatenate([r[rows, :] for r in u_refs], axis=-1)
    s = _cstep(s_ref[...], a_ref[...], bsw_ref[...]) + _bdot(u, b_ref[...])
    s_ref[...] = s
    if emit_y:
        y = _bdot(s, c_ref[...]) + d_ref[...] * u
        for half in range(HALVES):
            y_ref[half, rows, :] = y[:, half * V7X_LANES:(half + 1) * V7X_LANES]

    @pl.when(t == SSM_CHUNK - 1)
    def _():
        sout_ref[...] = s


def _ssm_prompt_pass(za, s0, tabs, emit_y, name):
    n_rows = P_ROWS // SSM_CHUNK // SSM_BATCH_SPLIT
    nat_rows = P_ROWS // SSM_BATCH_SPLIT
    ucol = OFF_U // V7X_LANES
    once = pl.Buffered(1)
    u_specs = [pl.BlockSpec((nat_rows, V7X_LANES), lambda gb, p, t, h=h: (p, ucol + gb * HALVES + h),
                            pipeline_mode=once) for h in range(HALVES)]
    st_spec = pl.BlockSpec((n_rows, 2 * GB_ST), lambda gb, p, t: (p, gb))
    s0_spec = pl.BlockSpec((n_rows, 2 * GB_ST), lambda gb, p, t: (p, gb), pipeline_mode=once)
    coef = pl.BlockSpec((None, 1, 2 * GB_ST), lambda gb, p, t: (gb, 0, 0))
    in_specs = u_specs + [s0_spec,
                          pl.BlockSpec((None, GB_CH, 2 * GB_ST), lambda gb, p, t: (gb, 0, 0)),
                          pl.BlockSpec((None, 2 * GB_ST, GB_CH), lambda gb, p, t: (gb, 0, 0)),
                          coef, coef,
                          pl.BlockSpec((1, GB_CH), lambda gb, p, t: (0, gb))]
    st_shape = jax.ShapeDtypeStruct((P_ROWS // SSM_CHUNK, ST_WIDTH), jnp.float32)
    out_specs, out_shape = [st_spec], [st_shape]
    if emit_y:
        out_specs = [pl.BlockSpec((HALVES, nat_rows, V7X_LANES), lambda gb, p, t: (gb, p, 0)), st_spec]
        out_shape = [jax.ShapeDtypeStruct((N_GB * HALVES, P_ROWS, V7X_LANES), jnp.float32), st_shape]
    return pl.pallas_call(
        functools.partial(_ssm_prompt_kernel, emit_y=emit_y),
        grid=(N_GB, SSM_BATCH_SPLIT, SSM_CHUNK),
        in_specs=in_specs,
        out_specs=out_specs,
        out_shape=out_shape,
        scratch_shapes=[pltpu.VMEM((n_rows, 2 * GB_ST), jnp.float32)],
        compiler_params=_params(("parallel", "parallel", "arbitrary"), 58 * MIB),
        name=name,
    )(*([za] * HALVES), s0, tabs['b'], tabs['c'], tabs['lam'][0], tabs['lam'][1], tabs['d'])


def _ssm_local_kernel(*refs):
    u_refs, (b_ref, e_ref) = refs[:HALVES], refs[HALVES:]
    n = e_ref.shape[0]
    parts = []
    for t in range(SSM_CHUNK):
        rows = pl.ds(t, n, stride=SSM_CHUNK)
        parts += [r[rows, :].astype(jnp.bfloat16) for r in u_refs]
    e_ref[...] = jnp.dot(jnp.concatenate(parts, axis=-1), b_ref[...], preferred_element_type=jnp.float32)


def _ssm_local(za, tabs):
    n_rows = P_ROWS // SSM_CHUNK
    ucol = OFF_U // V7X_LANES
    once = pl.Buffered(1)
    u_specs = [pl.BlockSpec((P_ROWS, V7X_LANES), lambda gb, h=h: (0, ucol + gb * HALVES + h), pipeline_mode=once)
               for h in range(HALVES)]
    return pl.pallas_call(
        _ssm_local_kernel,
        grid=(N_GB,),
        in_specs=u_specs + [pl.BlockSpec((None, SSM_CHUNK * GB_CH, 2 * GB_ST), lambda gb: (gb, 0, 0),
                                         pipeline_mode=once)],
        out_specs=pl.BlockSpec((n_rows, 2 * GB_ST), lambda gb: (0, gb)),
        out_shape=jax.ShapeDtypeStruct((n_rows, ST_WIDTH), jnp.float32),
        compiler_params=_params(("parallel",), 58 * MIB),
        name="ssm_prompt_local",
    )(*([za] * HALVES), tabs['b_chunk'])


def _ssm_carry_kernel(e_ref, a_ref, bsw_ref, sin_ref, fin_ref):
    a = a_ref[...]
    bsw = bsw_ref[...]
    n_chunks = SEQ // SSM_CHUNK

    def body(n, states):
        new = []
        for b in range(BATCH):
            row = b * n_chunks + n
            sin_ref[pl.ds(row, 1), :] = states[b]
            new.append(_cstep(states[b], a, bsw) + e_ref[pl.ds(row, 1), :])
        return tuple(new)

    zero = jnp.zeros((1, 2 * GB_ST), jnp.float32)
    final = lax.fori_loop(0, n_chunks, body, (zero,) * BATCH)
    for b in range(BATCH):
        fin_ref[pl.ds(b, 1), :] = final[b]


def _ssm_carry(e, tabs):
    n_rows = P_ROWS // SSM_CHUNK
    st_spec = lambda rows: pl.BlockSpec((rows, 2 * GB_ST), lambda gb: (0, gb))
    coef = pl.BlockSpec((None, 1, 2 * GB_ST), lambda gb: (gb, 0, 0))
    return pl.pallas_call(
        _ssm_carry_kernel,
        grid=(N_GB,),
        in_specs=[st_spec(n_rows), coef, coef],
        out_specs=[st_spec(n_rows), st_spec(BATCH)],
        out_shape=[jax.ShapeDtypeStruct((n_rows, ST_WIDTH), jnp.float32),
                   jax.ShapeDtypeStruct((BATCH, ST_WIDTH), jnp.float32)],
        compiler_params=_params(("parallel",), 32 * MIB),
        name="ssm_carry",
    )(e, tabs['lam_chunk'][0], tabs['lam_chunk'][1])


def _pack_state(s):
    b = s.shape[0]
    s = s.reshape(b, N_GB, SSM_GB * SSM_STATE, 2)
    return jnp.transpose(s, (0, 1, 3, 2)).reshape(b, ST_WIDTH)


def _unpack_state(s):
    b = s.shape[0]
    s = s.reshape(b, N_GB, 2, SSM_GB * SSM_STATE)
    return jnp.transpose(s, (0, 1, 3, 2)).reshape(b, SSM_GROUPS, SSM_STATE, 2)


def _ssm(za, state_s, tabs):
    ucol = OFF_U // GB_CH
    e = _ssm_local(za, tabs)
    s_in, s_fin = _ssm_carry(e, tabs)
    y_p, _ = _ssm_prompt_pass(za, s_in, tabs, True, "ssm_prompt")
    tblk = P_ROWS // DEC_BATCH
    ys_shape = jax.ShapeDtypeStruct((S_ROWS, SSM_WIDTH), jnp.float32)
    y_s, s_new = _ssm_pass(za, lambda gb, t: (tblk + t, ucol + gb), lambda gb, t: (t, gb), DEC_BATCH, DEC_SEQ,
                           _pack_state(state_s), tabs, ys_shape, "ssm_sample")
    return y_p, y_s, _unpack_state(s_fin), _unpack_state(s_new)


def _mix_kernel(x_ref, gate_ref, gb_ref, cp_ref, cs_ref, ap_refs, as_refs, sp_ref, ss_ref,
                wglu_ref, bglu_ref, pc_ref, pa_ref, ps_ref, wo_ref, gn_ref, xo_ref, ho_ref):
    is_p = pl.program_id(0) < P_ROWS // MIX_TILE

    def pick(p_ref, s_ref):
        p = p_ref[...]
        if p.ndim == 3:
            p = jnp.concatenate([p[i] for i in range(p.shape[0])], axis=-1)
        return jnp.where(is_p, p, s_ref[...])

    y_conv = pick(cp_ref, cs_ref)
    o = [pick(ap_refs[2 * g], as_refs[2 * g]) for g in range(N_ATTN_GROUPS)]
    lse = [pick(ap_refs[2 * g + 1], as_refs[2 * g + 1]) for g in range(N_ATTN_GROUPS)]
    m = jnp.maximum(jnp.maximum(lse[0], lse[1]), lse[2])
    w = [jnp.exp(l - m) for l in lse]
    y_attn = (w[0] * o[0] + w[1] * o[1] + w[2] * o[2]) / (w[0] + w[1] + w[2])
    zg = jax.nn.gelu(pick(sp_ref, ss_ref))
    y_ssm = zg * _sigmoid(_bdot(zg, wglu_ref[...]) + bglu_ref[...])
    gt = _sigmoid(gate_ref[...].astype(jnp.float32) + gb_ref[...])
    merged = (gt[:, :D_MODEL] * _bdot(y_conv, pc_ref[...])
              + gt[:, D_MODEL:2 * D_MODEL] * _bdot(y_attn, pa_ref[...])
              + gt[:, 2 * D_MODEL:] * _bdot(y_ssm, ps_ref[...]))
    x = x_ref[...] + _bdot(merged, wo_ref[...])
    xo_ref[...] = x
    ho_ref[...] = _rms(x, gn_ref[...]).astype(ho_ref.dtype)


def _mix(x, gates, conv_p, conv_s, attn_p, attn_s, ssm_p, ssm_s, lp, h_dtype):
    n_p = P_ROWS // MIX_TILE
    p_map = lambda i: (jnp.minimum(i, n_p - 1), 0)
    s_map = lambda i: (jnp.maximum(i - n_p, 0), 0)
    row_map = lambda i: (i, 0)
    const = lambda i: (0, 0)
    rows = lambda width, imap: pl.BlockSpec((MIX_TILE, width), imap)
    slabs = lambda n: pl.BlockSpec((n, MIX_TILE, V7X_LANES), lambda i: (0, jnp.minimum(i, n_p - 1), 0))
    full = lambda a: pl.BlockSpec(a.shape, const, pipeline_mode=pl.Buffered(1))
    n_attn = 2 * N_ATTN_GROUPS
    weights = [lp['w_glu'], lp['b_glu'], lp['proj_conv'], lp['proj_attn'], lp['proj_ssm'], lp['w_out'],
               lp['norm_next']]
    in_specs = ([rows(D_MODEL, row_map), rows(GATE_WIDTH, row_map), full(lp['gate_bias']),
                 rows(CONV_WIDTH, p_map), rows(CONV_WIDTH, s_map)]
                + [slabs(HALVES)] * n_attn + [rows(ATTN_OUT, s_map)] * n_attn
                + [slabs(N_GB * HALVES), rows(SSM_WIDTH, s_map)]
                + [full(w) for w in weights])

    def body(*refs):
        x_ref, gate_ref, gb_ref, cp_ref, cs_ref = refs[:5]
        ap_refs = refs[5:5 + n_attn]
        as_refs = refs[5 + n_attn:5 + 2 * n_attn]
        rest = refs[5 + 2 * n_attn:]
        _mix_kernel(x_ref, gate_ref, gb_ref, cp_ref, cs_ref, ap_refs, as_refs, *rest)

    return pl.pallas_call(
        body,
        grid=(ROWS // MIX_TILE,),
        in_specs=in_specs,
        out_specs=[rows(D_MODEL, row_map), rows(D_MODEL, row_map)],
        out_shape=[jax.ShapeDtypeStruct((ROWS, D_MODEL), jnp.float32),
                   jax.ShapeDtypeStruct((ROWS, D_MODEL), h_dtype)],
        compiler_params=_params(("parallel",), 56 * MIB),
        name="mix",
    )(x, gates, lp['gate_bias'], conv_p, conv_s, *attn_p, *attn_s, ssm_p, ssm_s, *weights)


def _ffn_up_kernel(be_ref, *refs, n_side, cast):
    x_ref, w1_ref, w3_ref = refs[:3]
    h_ref = refs[3 + n_side]
    _side_cast(refs[3:3 + n_side], refs[4 + n_side:4 + 2 * n_side])
    if cast:
        w1b_ref, w3b_ref = refs[-2:]

        @pl.when(pl.program_id(1) == 0)
        def _():
            w1b_ref[...] = w1_ref[...].astype(w1b_ref.dtype)
            w3b_ref[...] = w3_ref[...].astype(w3b_ref.dtype)
    else:
        w1b_ref, w3b_ref = w1_ref, w3_ref
    x = x_ref[...].astype(jnp.bfloat16)
    a = jnp.dot(x, w1b_ref[...], preferred_element_type=jnp.float32)
    b = jnp.dot(x, w3b_ref[...], preferred_element_type=jnp.float32)
    h_ref[...] = (a * _sigmoid(a) * b).astype(h_ref.dtype)


def _ffn_up(block_e, x, w1, w3, bm, bf, name, side=()):
    m = x.shape[0]
    e, _, f = w1.shape
    cast = w1.dtype != jnp.bfloat16
    assert not cast or e == 1
    grid = (f // bf, m // bm)
    assert all(arr.shape[0] // rows <= grid[0] * grid[1] for arr, rows in side)
    s_in, s_out, s_shape = _side_cast_specs(side, grid[1])
    wspec = pl.BlockSpec((None, D_MODEL, bf), lambda j, i, be: (be[i], 0, j))
    grid_spec = pltpu.PrefetchScalarGridSpec(
        num_scalar_prefetch=1,
        grid=grid,
        in_specs=[pl.BlockSpec((bm, D_MODEL), lambda j, i, be: (i, 0)), wspec, wspec] + s_in,
        out_specs=[pl.BlockSpec((bm, bf), lambda j, i, be: (i, j))] + s_out,
        scratch_shapes=[pltpu.VMEM((D_MODEL, bf), jnp.bfloat16)] * 2 if cast else [],
    )
    out = pl.pallas_call(
        functools.partial(_ffn_up_kernel, n_side=len(side), cast=cast),
        grid_spec=grid_spec,
        out_shape=[jax.ShapeDtypeStruct((m, f), jnp.bfloat16)] + s_shape,
        compiler_params=_params(("arbitrary", "arbitrary"), 56 * MIB),
        name=name,
    )(block_e, x, w1, w3, *[arr for arr, _ in side])
    return out if side else out[0]


def _ffn_down_dense_kernel(h_ref, w_ref, x_ref, g_ref, xo_ref, ho_ref, acc_ref, *, n_k):
    k = pl.program_id(1)

    @pl.when(k == 0)
    def _():
        acc_ref[...] = x_ref[...]

    acc_ref[...] += jnp.dot(h_ref[...], w_ref[...], preferred_element_type=jnp.float32)

    @pl.when(k == n_k - 1)
    def _():
        x = acc_ref[...]
        xo_ref[...] = x
        ho_ref[...] = _rms(x, g_ref[...]).astype(ho_ref.dtype)


def _ffn_down_dense(h, w2, x, g_next, bk):
    n_k = D_FF // bk
    row = pl.BlockSpec((ROW_TILE, D_MODEL), lambda i, k: (i, 0))
    return pl.pallas_call(
        functools.partial(_ffn_down_dense_kernel, n_k=n_k),
        grid=(ROWS // ROW_TILE, n_k),
        in_specs=[pl.BlockSpec((ROW_TILE, bk), lambda i, k: (i, k)),
                  pl.BlockSpec((bk, D_MODEL), lambda i, k: (k, 0)),
                  row, pl.BlockSpec((1, D_MODEL), lambda i, k: (0, 0))],
        out_specs=[row, row],
        out_shape=[jax.ShapeDtypeStruct((ROWS, D_MODEL), jnp.float32),
                   jax.ShapeDtypeStruct((ROWS, D_MODEL), jnp.bfloat16)],
        scratch_shapes=[pltpu.VMEM((ROW_TILE, D_MODEL), jnp.float32)],
        compiler_params=_params(("parallel", "arbitrary"), 58 * MIB),
        name="ffn_down",
    )(h, w2, x, g_next.reshape(1, D_MODEL))


def _moe_down_kernel(be_ref, h_ref, w_ref, g_ref, y_ref):
    y = jnp.dot(h_ref[...], w_ref[...], preferred_element_type=jnp.float32)
    _store_chunk_rows(y_ref, y * g_ref[...])


def _moe_down(block_e, h, w2, row_gate):
    grid_spec = pltpu.PrefetchScalarGridSpec(
        num_scalar_prefetch=1,
        grid=(MOE_BLOCKS,),
        in_specs=[pl.BlockSpec((MOE_BLOCK, D_FF_EXPERT), lambda i, be: (i, 0)),
                  pl.BlockSpec((None, D_FF_EXPERT, D_MODEL), lambda i, be: (be[i], 0, 0)),
                  pl.BlockSpec((MOE_BLOCK, 1), lambda i, be: (i, 0))],
        out_specs=pl.BlockSpec((MOE_BLOCK * ROW_CHUNKS, V7X_LANES), lambda i, be: (i, 0)),
    )
    return pl.pallas_call(
        _moe_down_kernel,
        grid_spec=grid_spec,
        out_shape=jax.ShapeDtypeStruct((MOE_ROWS * ROW_CHUNKS, V7X_LANES), jnp.float32),
        compiler_params=_params(("arbitrary",), 48 * MIB),
        name="moe_down",
    )(block_e, h, w2, row_gate)


def _router_kernel(x_ref, g_ref, w_ref, hs_ref, r_ref):
    hs = _rms(x_ref[...], g_ref[...])
    _store_chunk_rows(hs_ref, hs)
    logits = jnp.dot(hs, w_ref[...], preferred_element_type=jnp.float32, precision=lax.Precision.HIGHEST)
    lane = lax.broadcasted_iota(jnp.int32, logits.shape, 1)
    logits = jnp.where(lane < N_EXPERTS, logits, -jnp.inf)
    m1 = jnp.max(logits, axis=-1, keepdims=True)
    i1 = jnp.min(jnp.where(logits == m1, lane, ROUTER_PAD), axis=-1, keepdims=True)
    rest = jnp.where(lane == i1, -jnp.inf, logits)
    m2 = jnp.max(rest, axis=-1, keepdims=True)
    i2 = jnp.min(jnp.where(rest == m2, lane, ROUTER_PAD), axis=-1, keepdims=True)
    e2 = jnp.exp(m2 - m1)
    g1 = 1.0 / (1.0 + e2)
    g2 = e2 / (1.0 + e2)
    out = jnp.where(lane == 0, i1.astype(jnp.float32),
                    jnp.where(lane == 1, i2.astype(jnp.float32),
                              jnp.where(lane == 2, g1, jnp.where(lane == 3, g2, 0.0))))
    r_ref[...] = out


def _router(x, g, w_router):
    w = jnp.zeros((D_MODEL, ROUTER_PAD), jnp.float32).at[:, :N_EXPERTS].set(w_router)
    row = pl.BlockSpec((ROW_TILE, D_MODEL), lambda i: (i, 0))
    return pl.pallas_call(
        _router_kernel,
        grid=(ROWS // ROW_TILE,),
        in_specs=[row, pl.BlockSpec((1, D_MODEL), lambda i: (0, 0)),
                  pl.BlockSpec((D_MODEL, ROUTER_PAD), lambda i: (0, 0))],
        out_specs=[pl.BlockSpec((ROW_TILE * ROW_CHUNKS, V7X_LANES), lambda i: (i, 0)),
                   pl.BlockSpec((ROW_TILE, ROUTER_PAD), lambda i: (i, 0))],
        out_shape=[jax.ShapeDtypeStruct((ROWS * ROW_CHUNKS, V7X_LANES), jnp.float32),
                   jax.ShapeDtypeStruct((ROWS, ROUTER_PAD), jnp.float32)],
        compiler_params=_params(("parallel",), 32 * MIB),
        name="router",
    )(x, g.reshape(1, D_MODEL), w)


def _gather_step(idx_ref, src_ref, buf, sem, rows):
    i = pl.program_id(0)
    slot = i % 2

    def copy(step, slot_, r):
        return pltpu.make_async_copy(src_ref.at[idx_ref[step * rows + r]],
                                     buf.at[slot_, pl.ds(r * ROW_CHUNKS, ROW_CHUNKS), :], sem.at[slot_])

    def start_all(step, slot_):
        def body(r, c):
            copy(step, slot_, r).start()
            return c
        lax.fori_loop(0, rows, body, 0, unroll=8)

    @pl.when(i == 0)
    def _():
        start_all(0, 0)

    @pl.when(i + 1 < pl.num_programs(0))
    def _():
        start_all(i + 1, 1 - slot)

    def wait_body(r, c):
        copy(i, slot, r).wait()
        return c
    lax.fori_loop(0, rows, wait_body, 0, unroll=8)
    return slot


def _dispatch_kernel(tok_ref, hs_ref, o_ref, buf, sem):
    slot = _gather_step(tok_ref, hs_ref, buf, sem, MOE_BLOCK)
    o_ref[...] = _load_chunk_rows(buf.at[slot], 0, MOE_BLOCK).astype(o_ref.dtype)


def _dispatch(row_tok, hs):
    grid_spec = pltpu.PrefetchScalarGridSpec(
        num_scalar_prefetch=1,
        grid=(MOE_BLOCKS,),
        in_specs=[pl.BlockSpec(memory_space=pl.ANY)],
        out_specs=pl.BlockSpec((MOE_BLOCK, D_MODEL), lambda i, tok: (i, 0)),
        scratch_shapes=[pltpu.VMEM((2, MOE_BLOCK * ROW_CHUNKS, V7X_LANES), jnp.float32),
                        pltpu.SemaphoreType.DMA((2,))],
    )
    return pl.pallas_call(
        _dispatch_kernel,
        grid_spec=grid_spec,
        out_shape=jax.ShapeDtypeStruct((MOE_ROWS, D_MODEL), jnp.bfloat16),
        compiler_params=_params(("arbitrary",), 16 * MIB),
        name="moe_dispatch",
    )(row_tok, hs)


def _combine_kernel(pos_ref, x_ref, y_ref, g_ref, op_ref, os_ref, buf, sem):
    slot = _gather_step(pos_ref, y_ref, buf, sem, TOP_K * MIX_TILE)
    y = sum(_load_chunk_rows(buf.at[slot], k * MIX_TILE, MIX_TILE) for k in range(TOP_K))
    out = _rms(x_ref[...] + y, g_ref[...])
    is_p = pl.program_id(0) < P_ROWS // MIX_TILE

    @pl.when(is_p)
    def _():
        op_ref[...] = out

    @pl.when(jnp.logical_not(is_p))
    def _():
        os_ref[...] = out


def _combine(pos, x, yb, g_final):
    rows = TOP_K * MIX_TILE
    n_p = P_ROWS // MIX_TILE
    grid_spec = pltpu.PrefetchScalarGridSpec(
        num_scalar_prefetch=1,
        grid=(ROWS // MIX_TILE,),
        in_specs=[pl.BlockSpec((MIX_TILE, D_MODEL), lambda i, pos: (i, 0)),
                  pl.BlockSpec(memory_space=pl.ANY),
                  pl.BlockSpec((1, D_MODEL), lambda i, pos: (0, 0))],
        out_specs=[pl.BlockSpec((MIX_TILE, D_MODEL), lambda i, pos: (jnp.minimum(i, n_p - 1), 0)),
                   pl.BlockSpec((MIX_TILE, D_MODEL), lambda i, pos: (jnp.maximum(i - n_p, 0), 0))],
        scratch_shapes=[pltpu.VMEM((2, rows * ROW_CHUNKS, V7X_LANES), jnp.float32),
                        pltpu.SemaphoreType.DMA((2,))],
    )
    return pl.pallas_call(
        _combine_kernel,
        grid_spec=grid_spec,
        out_shape=[jax.ShapeDtypeStruct((P_ROWS, D_MODEL), jnp.float32),
                   jax.ShapeDtypeStruct((S_ROWS, D_MODEL), jnp.float32)],
        compiler_params=_params(("arbitrary",), 32 * MIB),
        name="moe_combine",
    )(pos, x, yb, g_final.reshape(1, D_MODEL))


def _dispatch_plan(route):
    top_idx = route[:, 0:TOP_K].astype(jnp.int32)
    gates = route[:, TOP_K:2 * TOP_K]
    flat_e = top_idx.reshape(N_ASSIGN)
    order = jnp.argsort(flat_e).astype(jnp.int32)
    rank = jnp.argsort(order).astype(jnp.int32)
    counts = jnp.sum((flat_e[:, None] == jnp.arange(N_EXPERTS)[None, :]).astype(jnp.int32), axis=0)
    padded = (counts + MOE_BLOCK - 1) // MOE_BLOCK * MOE_BLOCK
    start = jnp.cumsum(counts) - counts
    ends_p = jnp.cumsum(padded)
    pstart = ends_p - padded
    first_row = jnp.arange(MOE_BLOCKS, dtype=jnp.int32) * MOE_BLOCK
    block_e = jnp.minimum(jnp.sum((ends_p[None, :] <= first_row[:, None]).astype(jnp.int32), axis=1),
                          N_EXPERTS - 1).astype(jnp.int32)
    row = jnp.arange(MOE_ROWS, dtype=jnp.int32)
    row_e = jnp.repeat(block_e, MOE_BLOCK)
    off = row - pstart[row_e]
    real = off < counts[row_e]
    assign = order[jnp.clip(start[row_e] + off, 0, N_ASSIGN - 1)]
    row_tok = jnp.where(real, assign // TOP_K, 0).astype(jnp.int32)
    row_gate = jnp.where(real, gates.reshape(N_ASSIGN)[assign], 0.0)
    pos = (pstart[flat_e] + rank - start[flat_e]).astype(jnp.int32)
    pos = jnp.transpose(pos.reshape(ROWS // MIX_TILE, MIX_TILE, TOP_K), (0, 2, 1)).reshape(N_ASSIGN)
    return row_tok, row_gate.reshape(MOE_ROWS, 1), block_e, pos


def kernel(x_prompt, x_sample, cache_kv_w128, cache_kv_w512, cache_kv_w2048, state_conv, state_ssm, norm_mix, w_in, gate_bias, conv_w, ssm_lam_re, ssm_lam_im, ssm_log_dt, ssm_b_re, ssm_b_im, ssm_c_re, ssm_c_im, ssm_d, ssm_w_glu, ssm_b_glu, proj_conv, proj_attn, proj_ssm, w_out, norm_ffn, ffn_w1, ffn_w3, ffn_w2, router_w, moe_w1, moe_w3, moe_w2, norm_final):
    bf = jnp.bfloat16
    caches = tuple(jnp.transpose(c, (0, 1, 3, 4, 5, 2)) for c in (cache_kv_w128, cache_kv_w512, cache_kv_w2048))
    assert W_COL_QK == IN_TILE == QK_WIDTH
    rope = _rope_tables()
    x, h = _norm(x_prompt.reshape(P_ROWS, D_MODEL),
                 jnp.transpose(x_sample, (1, 0, 2)).reshape(S_ROWS, D_MODEL), norm_mix[0])
    kv_p = [[] for _ in range(N_ATTN_GROUPS)]
    kv_s = [[] for _ in range(N_ATTN_GROUPS)]
    conv_p_all, conv_s_all, ssm_p_all, ssm_s_all = [], [], [], []
    y = None
    for l in range(DEPTH):
        moe_layer = l % 2 == 1
        qk = _linear(h, w_in, l, W_COL_QK, QK_WIDTH, jnp.float32, IN_TILE, f"qk_proj_{l}", rope=rope)
        if moe_layer:
            za = _linear(h, w_in, l, 0, ZA_WIDTH, jnp.float32, IN_TILE, f"in_proj_{l}", col_step=2)
            w2_flat = moe_w2[l // 2].reshape(N_EXPERTS * D_FF_EXPERT, D_MODEL)
            gates, moe_w2_bf = _linear(h, w_in, l, W_COL_GATES, GATE_WIDTH, bf, IN_TILE, f"gate_proj_{l}",
                                       side=[(w2_flat, SIDE_CAST_ROWS_W2)])
        else:
            za, ffn_w2_bf = _linear(h, w_in, l, 0, ZA_WIDTH, jnp.float32, IN_TILE, f"in_proj_{l}", col_step=2,
                                    side=[(ffn_w2[l // 2], SIDE_CAST_ROWS_FFN)])
            w3_flat = moe_w3[(l + 1) // 2].reshape(N_EXPERTS * D_MODEL, D_FF_EXPERT)
            gates, moe_w3_bf = _linear(h, w_in, l, W_COL_GATES, GATE_WIDTH, bf, IN_TILE, f"gate_proj_{l}",
                                       side=[(w3_flat, SIDE_CAST_ROWS_W3)])
        conv_p, cst_p = _conv_prompt(za, conv_w[l])
        conv_s, cst_s = _conv_sample(za, conv_w[l], jnp.transpose(state_conv[l], (1, 0, 2)))
        by_seq = lambda a: jnp.transpose(a.reshape(DEC_SEQ, DEC_BATCH, a.shape[-1]), (1, 0, 2))
        qk_s = by_seq(qk[P_ROWS:])
        v_s = by_seq(za[P_ROWS:, OFF_V:OFF_V + ATTN_QKV])
        attn_p, attn_s = [], []
        for g in range(N_ATTN_GROUPS):
            attn_p += list(_attn_prompt(qk, za, g))
            attn_s += list(_attn_sample(qk_s, v_s, caches[g], l, g))
        tabs = _ssm_tables(ssm_lam_re[l], ssm_lam_im[l], ssm_log_dt[l], ssm_b_re[l], ssm_b_im[l],
                           ssm_c_re[l], ssm_c_im[l], ssm_d[l])
        ssm_p, ssm_s, sst_p, sst_s = _ssm(za, state_ssm[l], tabs)
        moe_layer = l % 2 == 1
        lp = dict(gate_bias=gate_bias[l].reshape(1, GATE_WIDTH), w_glu=ssm_w_glu[l].astype(bf),
                  b_glu=ssm_b_glu[l].reshape(1, SSM_WIDTH), proj_conv=proj_conv[l].astype(bf),
                  proj_attn=proj_attn[l].astype(bf), proj_ssm=proj_ssm[l].astype(bf),
                  w_out=w_out[l].astype(bf), norm_next=norm_ffn[l].reshape(1, D_MODEL))
        x, h = _mix(x, gates, conv_p, conv_s, attn_p, attn_s, ssm_p, ssm_s, lp, bf)
        i = l // 2
        if not moe_layer:
            zero_e = jnp.zeros((ROWS // ROW_TILE,), jnp.int32)
            w1_flat = moe_w1[(l + 1) // 2].reshape(N_EXPERTS * D_MODEL, D_FF_EXPERT)
            hid, moe_w1_bf = _ffn_up(zero_e, h, ffn_w1[i:i + 1], ffn_w3[i:i + 1], ROW_TILE, 512, "ffn_up",
                                     side=[(w1_flat, SIDE_CAST_ROWS_W1)])
            x, h = _ffn_down_dense(hid, ffn_w2_bf, x, norm_mix[l + 1], D_FF // 2)
        else:
            hs, route = _router(x, norm_ffn[l], router_w[i])
            row_tok, row_gate, block_e, pos = _dispatch_plan(route)
            xs = _dispatch(row_tok, hs.reshape(ROWS, ROW_CHUNKS, V7X_LANES))
            expert = lambda w, k: w.reshape(N_EXPERTS, k, w.shape[-1])
            hid = _ffn_up(block_e, xs, expert(moe_w1_bf, D_MODEL), expert(moe_w3_bf, D_MODEL),
                          MOE_BLOCK, D_FF_EXPERT // 2, "moe_up")
            yb = _moe_down(block_e, hid, expert(moe_w2_bf, D_FF_EXPERT), row_gate)
            y = _combine(pos, x, yb.reshape(MOE_ROWS, ROW_CHUNKS, V7X_LANES), norm_final)
        for g, (kp, ks) in enumerate(zip(_kv_prompt(qk, za), _kv_sample(qk, za))):
            kp = kp.reshape(BATCH, 2, HEADS_PER_GROUP, HEAD_DIM, kp.shape[-1])
            kv_p[g].append(jnp.transpose(kp, (0, 4, 1, 2, 3)))
            ks = ks.reshape(DEC_SEQ, 2, HEADS_PER_GROUP, HEAD_DIM, DEC_BATCH)
            kv_s[g].append(jnp.transpose(ks, (4, 0, 1, 2, 3)))
        conv_p_all.append(cst_p)
        conv_s_all.append(jnp.transpose(cst_s, (1, 0, 2)))
        ssm_p_all.append(sst_p)
        ssm_s_all.append(sst_s)
    y_prompt = y[0].reshape(BATCH, SEQ, D_MODEL)
    y_sample = jnp.transpose(y[1].reshape(DEC_SEQ, DEC_BATCH, D_MODEL), (1, 0, 2))
    return (y_prompt, y_sample, jnp.stack(kv_p[0]), jnp.stack(kv_p[1]), jnp.stack(kv_p[2]),
            jnp.stack(conv_p_all), jnp.stack(ssm_p_all), jnp.stack(kv_s[0]), jnp.stack(kv_s[1]),
            jnp.stack(kv_s[2]), jnp.stack(conv_s_all), jnp.stack(ssm_s_all))
```
